```python
import math, functools
import jax, jax.numpy as jnp
from jax import lax
import numpy as np

D_MODEL = 1024
BATCH = 8
SEQ = 2048
DEPTH = 2
DEC_BATCH = 128
DEC_SEQ = 1
PAST_LEN = 16384
PAGE_SIZE = 128

N_META = 16
M_INNER = D_MODEL
M_HEADDIM = 64
M_HEADS = M_INNER // M_HEADDIM
M_GROUPS = 4
M_HPG = M_HEADS // M_GROUPS
M_STATE = 128
M_CONV = 4
M_CONV_DIM = M_INNER + 2 * M_GROUPS * M_STATE
M_CHUNK = 128
H_KDIM = 128
H_HEADS = D_MODEL // H_KDIM
H_VDIM = 128
H_WIDTH = H_HEADS * H_KDIM
H_VWIDTH = H_HEADS * H_VDIM
H_CHUNK = 32
R_HEADS = 4
R_KDIM = D_MODEL // R_HEADS
R_VDIM = 2 * R_KDIM
R_QK = R_HEADS * R_KDIM
R_V = R_HEADS * R_VDIM
R_CHUNK = 128
ROPE_BASE = 10000.0
D_FF = ((8 * D_MODEL // 3 + 255) // 256) * 256
DN_ALPHA = (2 * DEPTH) ** 0.25
DN_BETA = (8 * DEPTH) ** -0.25
IN_SPLITS = (M_INNER, M_CONV_DIM, M_HEADS, H_WIDTH, H_WIDTH, H_VWIDTH, H_VWIDTH,
             R_QK, R_QK, R_V, R_V, 3 * D_MODEL)
IN_DIM = sum(IN_SPLITS)

kernel_name = "hybrid_ssd_hgrn2_retention_step"


def layer_norm(x, g, b, eps=1e-5):
    xf = x.astype(jnp.float32)
    mu = jnp.mean(xf, axis=-1, keepdims=True)
    var = jnp.mean(jnp.square(xf - mu), axis=-1, keepdims=True)
    return ((xf - mu) * lax.rsqrt(var + eps)).astype(x.dtype) * g + b


def rms_norm(x, eps=1e-6):
    xf = x.astype(jnp.float32)
    return (xf * lax.rsqrt(jnp.mean(jnp.square(xf), axis=-1, keepdims=True) + eps)).astype(x.dtype)


def causal_mask(t):
    return jnp.tril(jnp.ones((t, t), dtype=bool))


def run_chunks(step, state, xs, head_len, chunk):
    bsz, total = xs[0].shape[0], xs[0].shape[1]
    outs = []
    if head_len > 0:
        state, o = step(state, tuple(a[:, :head_len] for a in xs))
        outs.append(o)
    rest = tuple(a[:, head_len:] for a in xs)
    rest_len = total - head_len
    if rest_len <= chunk:
        state, o = step(state, rest)
        outs.append(o)
    else:
        n = rest_len // chunk
        blocks = tuple(jnp.moveaxis(a.reshape((bsz, n, chunk) + a.shape[2:]), 1, 0) for a in rest)
        state, o = lax.scan(step, state, blocks)
        outs.append(jnp.moveaxis(o, 0, 1).reshape((bsz, rest_len) + o.shape[3:]))
    y = outs[0] if len(outs) == 1 else jnp.concatenate(outs, axis=1)
    return state, y


def rotary(x, positions):
    half = x.shape[-1] // 2
    inv_freq = 1.0 / (ROPE_BASE ** jnp.linspace(0.0, 1.0, half, dtype=jnp.float32))
    ang = positions[:, None] * inv_freq[None, :]
    cos = jnp.cos(ang)[None, :, None, :]
    sin = jnp.sin(ang)[None, :, None, :]
    x1 = x[..., :half].astype(jnp.float32)
    x2 = x[..., half:].astype(jnp.float32)
    return jnp.concatenate([x1 * cos - x2 * sin, x2 * cos + x1 * sin], axis=-1).astype(x.dtype)


def ssd_step(A, state, inp):
    x, dt, Bm, Cm = inp
    T = x.shape[1]
    cum = jnp.cumsum(dt * A, axis=1)
    mask = causal_mask(T)[None, :, :, None, None]
    decay = jnp.exp(jnp.where(mask, cum[:, :, None] - cum[:, None, :], -jnp.inf))
    cb = jnp.einsum('btgn,bsgn->btsg', Cm, Bm)
    w = cb[..., None] * decay * dt[:, None]
    y = jnp.einsum('btsgh,bsghp->btghp', w, x)
    y = y + jnp.einsum('btgn,bghpn->btghp', Cm, state) * jnp.exp(cum)[..., None]
    tail = jnp.exp(cum[:, -1:] - cum) * dt
    new_state = state * jnp.exp(cum[:, -1])[..., None, None] + jnp.einsum('bsgh,bsghp,bsgn->bghpn', tail, x, Bm)
    return new_state.astype(state.dtype), y.astype(x.dtype)


def mamba_branch(z, xbc, dt_raw, conv_buf, ssm_state, p, head_len):
    b, L = xbc.shape[0], xbc.shape[1]
    xpad = jnp.concatenate([conv_buf, xbc], axis=1)
    acc = p['conv_b']
    for k in range(M_CONV):
        acc = acc + xpad[:, k:k + L] * p['conv_w'][k]
    new_buf = xpad[:, -(M_CONV - 1):]
    xbc = jax.nn.silu(acc)
    xs = xbc[..., :M_INNER].reshape(b, L, M_GROUPS, M_HPG, M_HEADDIM)
    Bm = xbc[..., M_INNER:M_INNER + M_GROUPS * M_STATE].reshape(b, L, M_GROUPS, M_STATE)
    Cm = xbc[..., M_INNER + M_GROUPS * M_STATE:].reshape(b, L, M_GROUPS, M_STATE)
    dt = jax.nn.softplus((dt_raw + p['dt_bias']).astype(jnp.float32)).reshape(b, L, M_GROUPS, M_HPG)
    A = -jnp.exp(p['a_log'].astype(jnp.float32)).reshape(M_GROUPS, M_HPG)
    state = ssm_state.reshape(b, M_GROUPS, M_HPG, M_HEADDIM, M_STATE)
    state, y = run_chunks(functools.partial(ssd_step, A), state, (xs, dt, Bm, Cm), head_len, M_CHUNK)
    y = y + p['d_skip'].reshape(M_GROUPS, M_HPG)[..., None] * xs
    y = y.reshape(b, L, M_INNER) * jax.nn.silu(z)
    y = rms_norm(y.reshape(b, L, M_GROUPS, M_INNER // M_GROUPS)).reshape(b, L, M_INNER) * p['m_norm_w']
    return y, new_buf, state.reshape(b, M_HEADS, M_HEADDIM, M_STATE)


def hgrn_step(state, inp):
    q, logf, k, v = inp
    T = q.shape[1]
    cum = jnp.cumsum(logf, axis=1)
    mask = causal_mask(T)[None, :, :, None, None]
    decay = jnp.exp(jnp.where(mask, cum[:, :, None] - cum[:, None, :], -jnp.inf))
    scores = jnp.einsum('bthk,bshk,btshk->btsh', q, k, decay)
    o = jnp.einsum('btsh,bshv->bthv', scores, v)
    o = o + jnp.einsum('bthk,bhkv->bthv', q * jnp.exp(cum), state)
    new_state = state * jnp.exp(cum[:, -1])[..., None] + jnp.einsum('bshk,bshv->bhkv', k * jnp.exp(cum[:, -1:] - cum), v)
    return new_state.astype(state.dtype), o.astype(q.dtype)


def hgrn_branch(q, f_raw, i_in, g, state, lb, p, head_len):
    b, L = q.shape[0], q.shape[1]
    q = q.reshape(b, L, H_HEADS, H_KDIM) * (H_KDIM ** -0.5)
    fz = f_raw.astype(jnp.float32).reshape(b, L, H_HEADS, H_KDIM)
    lbh = lb.astype(jnp.float32).reshape(H_HEADS, H_KDIM)
    logf = jnp.logaddexp(jnp.log(lbh), jnp.log1p(-lbh) + jax.nn.log_sigmoid(fz))
    k = ((1.0 - lbh) * jax.nn.sigmoid(-fz)).astype(q.dtype)
    v = i_in.reshape(b, L, H_HEADS, H_VDIM)
    state, o = run_chunks(hgrn_step, state, (q, logf, k, v), head_len, H_CHUNK)
    o = rms_norm(o).reshape(b, L, H_VWIDTH) * p['h_norm_w'] * jax.nn.sigmoid(g)
    return o, state


def ret_step(log_gamma, state, inp):
    q, k, v = inp
    T = q.shape[1]
    t = jnp.arange(T, dtype=jnp.float32)
    diff = (t[:, None] - t[None, :])[..., None] * log_gamma
    decay = jnp.exp(jnp.where(causal_mask(T)[..., None], diff, -jnp.inf))
    scores = jnp.einsum('bthk,bshk->bhts', q, k) * jnp.transpose(decay, (2, 0, 1))[None]
    o = jnp.einsum('bhts,bshv->bthv', scores, v)
    o = o + jnp.einsum('bthk,bhkv->bthv', q, state) * jnp.exp((t + 1.0)[:, None] * log_gamma)[None, :, :, None]
    k_dec = k * jnp.exp((T - 1.0 - t)[:, None] * log_gamma)[None, :, :, None]
    new_state = state * jnp.exp(T * log_gamma)[:, None, None] + jnp.einsum('bshk,bshv->bhkv', k_dec, v)
    return new_state.astype(state.dtype), o.astype(q.dtype)


def retention_branch(q, k, v, g, state, positions, head_len):
    b, L = q.shape[0], q.shape[1]
    q = rotary(q.reshape(b, L, R_HEADS, R_KDIM), positions)
    k = rotary(k.reshape(b, L, R_HEADS, R_KDIM), positions) * (R_KDIM ** -0.5)
    v = v.reshape(b, L, R_HEADS, R_VDIM)
    log_gamma = jnp.log(1.0 - jnp.exp2(-5.0 - jnp.arange(R_HEADS, dtype=jnp.float32)))
    state, o = run_chunks(functools.partial(ret_step, log_gamma), state, (q, k, v), head_len, R_CHUNK)
    o = rms_norm(o).reshape(b, L, R_V) * jax.nn.silu(g)
    return o, state


def trunk_layer(x, conv_buf, ssm_state, hgrn_state, ret_state, positions, head_len, lb, p):
    proj = x @ p['w_in']
    offsets = np.cumsum(IN_SPLITS)[:-1].tolist()
    (m_z, m_xbc, m_dt, h_q, h_f, h_i, h_g, r_q, r_k, r_v, r_g, gates) = jnp.split(proj, offsets, axis=-1)
    y_m, new_conv, new_ssm = mamba_branch(m_z, m_xbc, m_dt, conv_buf, ssm_state, p, head_len)
    y_h, new_hgrn = hgrn_branch(h_q, h_f, h_i, h_g, hgrn_state, lb, p, head_len)
    y_r, new_ret = retention_branch(r_q, r_k, r_v, r_g, ret_state, positions, head_len)
    g_m, g_h, g_r = jnp.split(jax.nn.sigmoid(gates), 3, axis=-1)
    mixed = g_m * (y_m @ p['w_br_m']) + g_h * (y_h @ p['w_br_h']) + g_r * (y_r @ p['w_br_r'])
    x = layer_norm(DN_ALPHA * x + mixed @ p['w_out'], p['ln1_g'], p['ln1_b'])
    hg, hu = jnp.split(x @ p['w_ffn_in'], 2, axis=-1)
    x = layer_norm(DN_ALPHA * x + (jax.nn.silu(hg) * hu) @ p['w_ffn_out'], p['ln2_g'], p['ln2_b'])
    return x, new_conv, new_ssm, new_hgrn, new_ret


def setup_inputs(seed: int = 0) -> dict:
    key = jax.random.key(seed)
    ks = jax.random.split(key, 32)

    def nrm(k, shape, scale=1.0):
        return jax.random.normal(k, shape, jnp.float32) * scale

    dt0 = jnp.exp(jax.random.uniform(ks[10], (DEPTH, M_HEADS), jnp.float32, math.log(1e-3), math.log(1e-1)))
    return {
        "x_prompt": nrm(ks[0], (BATCH, SEQ, D_MODEL)),
        "x_sample": nrm(ks[1], (DEC_BATCH, DEC_SEQ, D_MODEL)),
        "state_ssm": nrm(ks[2], (DEPTH, DEC_BATCH, M_HEADS, M_HEADDIM, M_STATE), 0.3),
        "state_conv": nrm(ks[3], (DEPTH, DEC_BATCH, M_CONV - 1, M_CONV_DIM)),
        "state_hgrn": nrm(ks[4], (DEPTH, DEC_BATCH, H_HEADS, H_KDIM, H_VDIM), 0.3),
        "state_ret": nrm(ks[5], (DEPTH, DEC_BATCH, R_HEADS, R_KDIM, R_VDIM), 0.3),
        "meta_tokens": nrm(ks[6], (N_META, D_MODEL)),
        "ln_in_g": 1.0 + nrm(ks[7], (D_MODEL,), 0.02),
        "ln_in_b": nrm(ks[8], (D_MODEL,), 0.02),
        "w_in": nrm(ks[9], (DEPTH, D_MODEL, IN_DIM), D_MODEL ** -0.5),
        "conv_w": nrm(ks[11], (DEPTH, M_CONV, M_CONV_DIM), M_CONV ** -0.5),
        "conv_b": nrm(ks[12], (DEPTH, M_CONV_DIM), 0.02),
        "dt_bias": dt0 + jnp.log(-jnp.expm1(-dt0)),
        "a_log": jnp.log(jax.random.uniform(ks[13], (DEPTH, M_HEADS), jnp.float32, 1.0, 16.0)),
        "d_skip": 1.0 + nrm(ks[14], (DEPTH, M_HEADS), 0.1),
        "m_norm_w": 1.0 + nrm(ks[15], (DEPTH, M_INNER), 0.02),
        "hgrn_lb_logits": nrm(ks[16], (DEPTH, H_WIDTH)),
        "h_norm_w": 1.0 + nrm(ks[17], (DEPTH, H_VWIDTH), 0.02),
        "w_br_m": nrm(ks[18], (DEPTH, M_INNER, D_MODEL), DN_BETA * M_INNER ** -0.5),
        "w_br_h": nrm(ks[19], (DEPTH, H_VWIDTH, D_MODEL), DN_BETA * H_VWIDTH ** -0.5),
        "w_br_r": nrm(ks[20], (DEPTH, R_V, D_MODEL), DN_BETA * R_V ** -0.5),
        "w_out": nrm(ks[21], (DEPTH, D_MODEL, D_MODEL), DN_BETA * D_MODEL ** -0.5),
        "ln1_g": 1.0 + nrm(ks[22], (DEPTH, D_MODEL), 0.02),
        "ln1_b": nrm(ks[23], (DEPTH, D_MODEL), 0.02),
        "w_ffn_in": nrm(ks[24], (DEPTH, D_MODEL, 2 * D_FF), DN_BETA * D_MODEL ** -0.5),
        "w_ffn_out": nrm(ks[25], (DEPTH, D_FF, D_MODEL), DN_BETA * D_FF ** -0.5),
        "ln2_g": 1.0 + nrm(ks[26], (DEPTH, D_MODEL), 0.02),
        "ln2_b": nrm(ks[27], (DEPTH, D_MODEL), 0.02),
    }


def reference(x_prompt, x_sample, state_ssm, state_conv, state_hgrn, state_ret, meta_tokens,
              ln_in_g, ln_in_b, w_in, conv_w, conv_b, dt_bias, a_log, d_skip, m_norm_w,
              hgrn_lb_logits, h_norm_w, w_br_m, w_br_h, w_br_r, w_out, ln1_g, ln1_b,
              w_ffn_in, w_ffn_out, ln2_g, ln2_b):
    bp, sp = x_prompt.shape[0], x_prompt.shape[1]
    dt_ = x_prompt.dtype
    lb_cum = jnp.cumsum(jax.nn.softmax(hgrn_lb_logits.astype(jnp.float32), axis=0), axis=0)
    lbs = lb_cum - lb_cum[0]

    xp = jnp.concatenate([jnp.broadcast_to(meta_tokens[None].astype(dt_), (bp, N_META, D_MODEL)), x_prompt], axis=1)
    xp = layer_norm(xp, ln_in_g, ln_in_b)
    pos_p = jnp.arange(N_META + sp, dtype=jnp.float32)
    xs = layer_norm(x_sample, ln_in_g, ln_in_b)
    pos_s = PAST_LEN + jnp.arange(x_sample.shape[1], dtype=jnp.float32)

    conv_p, ssm_p, hgrn_p, ret_p = [], [], [], []
    conv_s, ssm_s, hgrn_s, ret_s = [], [], [], []
    for l in range(DEPTH):
        p = dict(w_in=w_in[l], conv_w=conv_w[l], conv_b=conv_b[l], dt_bias=dt_bias[l], a_log=a_log[l],
                 d_skip=d_skip[l], m_norm_w=m_norm_w[l], h_norm_w=h_norm_w[l], w_br_m=w_br_m[l],
                 w_br_h=w_br_h[l], w_br_r=w_br_r[l], w_out=w_out[l], ln1_g=ln1_g[l], ln1_b=ln1_b[l],
                 w_ffn_in=w_ffn_in[l], w_ffn_out=w_ffn_out[l], ln2_g=ln2_g[l], ln2_b=ln2_b[l])
        xp, c, s, h, r = trunk_layer(
            xp,
            jnp.zeros((bp, M_CONV - 1, M_CONV_DIM), dt_),
            jnp.zeros((bp, M_HEADS, M_HEADDIM, M_STATE), dt_),
            jnp.zeros((bp, H_HEADS, H_KDIM, H_VDIM), dt_),
            jnp.zeros((bp, R_HEADS, R_KDIM, R_VDIM), dt_),
            pos_p, N_META, lbs[l], p)
        conv_p.append(c); ssm_p.append(s); hgrn_p.append(h); ret_p.append(r)
        xs, c, s, h, r = trunk_layer(xs, state_conv[l], state_ssm[l], state_hgrn[l], state_ret[l],
                                     pos_s, 0, lbs[l], p)
        conv_s.append(c); ssm_s.append(s); hgrn_s.append(h); ret_s.append(r)

    y_prompt = xp[:, N_META:]
    y_sample = xs
    return (y_prompt, y_sample,
            jnp.stack(ssm_p), jnp.stack(conv_p), jnp.stack(hgrn_p), jnp.stack(ret_p),
            jnp.stack(ssm_s), jnp.stack(conv_s), jnp.stack(hgrn_s), jnp.stack(ret_s))
```

```python
import functools
import math

import numpy as np
import jax
import jax.numpy as jnp
from jax import lax
from jax.experimental import pallas as pl
from jax.experimental.pallas import tpu as pltpu

F32 = jnp.float32
BF16 = jnp.bfloat16

D_MODEL = 1024
DEPTH = 2
N_META = 16
M_INNER = D_MODEL
M_HEADDIM = 64
M_HEADS = M_INNER // M_HEADDIM
M_GROUPS = 4
M_HPG = M_HEADS // M_GROUPS
M_STATE = 128
M_CONV = 4
M_BC = 2 * M_GROUPS * M_STATE
M_CONV_DIM = M_INNER + M_BC
M_GW = M_INNER // M_GROUPS
H_KDIM = 128
H_HEADS = D_MODEL // H_KDIM
H_VDIM = 128
H_SUB = 16
R_HEADS = 4
R_KDIM = D_MODEL // R_HEADS
R_VDIM = 2 * R_KDIM
R_HALF = R_KDIM // 2
ROPE_BASE = 10000.0
D_FF = ((8 * D_MODEL // 3 + 255) // 256) * 256
FF_CHUNK = 256
DN_ALPHA = (2 * DEPTH) ** 0.25
PAST_LEN = 16384

LANES = 128
SEQ_BLOCK = 128
RET_BLOCK = 256
SMALL_ROWS = 256
META_LEAD = SEQ_BLOCK - N_META
STEP_BT = 8
VMEM_LIMIT = 56 * 1024 * 1024

COL_RV, COL_RG = 0, 1
COL_Z, COL_XS, COL_BC, COL_HQ, COL_HF, COL_HI, COL_HG, COL_RQ, COL_RK, COL_GM, COL_GH, COL_GR = range(4, 16)
PROJ_COLS = 16 * 1024

_ORIG_SPLITS = (M_INNER, M_CONV_DIM, M_HEADS, 1024, 1024, 1024, 1024, 1024, 1024, 2048, 2048, 3072)
_ORIG_OFF = np.concatenate([[0], np.cumsum(_ORIG_SPLITS)]).tolist()


def _cparams(sem):
    return pltpu.CompilerParams(dimension_semantics=sem, vmem_limit_bytes=VMEM_LIMIT)


def _sigmoid(x):
    return 1.0 / (1.0 + jnp.exp(-x))


def _silu(x):
    return x * _sigmoid(x)


def _softplus(x):
    return jnp.maximum(x, 0.0) + jnp.log1p(jnp.exp(-jnp.abs(x)))


def _layer_norm(x, g, b):
    mu = jnp.mean(x, axis=-1, keepdims=True)
    xc = x - mu
    var = jnp.mean(xc * xc, axis=-1, keepdims=True)
    return xc * lax.rsqrt(var + 1e-5) * g + b


def _rms(x):
    return x * lax.rsqrt(jnp.mean(x * x, axis=-1, keepdims=True) + 1e-6)


def _split3(x):
    hi = x.astype(BF16)
    r = x - hi.astype(F32)
    mid = r.astype(BF16)
    lo = (r - mid.astype(F32)).astype(BF16)
    return hi, mid, lo


def _dot(a, b):
    return jnp.dot(a, b, preferred_element_type=F32)


def _dot_nt(a, b):
    return lax.dot_general(a, b, (((1,), (1,)), ((), ())), preferred_element_type=F32)


def _sel_right(x, m01):
    hi, mid, lo = _split3(x)
    return _dot(hi, m01) + _dot(mid, m01) + _dot(lo, m01)


def _sel_left(m01, x):
    hi, mid, lo = _split3(x)
    return _dot(m01, hi) + _dot(m01, mid) + _dot(m01, lo)


def _iota(shape, dim):
    return lax.broadcasted_iota(jnp.int32, shape, dim)


def _ln_kernel(x_ref, g_ref, b_ref, o_ref):
    o_ref[...] = _layer_norm(x_ref[...], g_ref[...], b_ref[...])


def _layer_norm_rows(x, g, b, tm):
    m = x.shape[0]
    tm = min(tm, m)
    return pl.pallas_call(
        _ln_kernel,
        grid=(m // tm,),
        in_specs=[pl.BlockSpec((tm, D_MODEL), lambda i: (i, 0)),
                  pl.BlockSpec((1, D_MODEL), lambda i: (0, 0)),
                  pl.BlockSpec((1, D_MODEL), lambda i: (0, 0))],
        out_specs=pl.BlockSpec((tm, D_MODEL), lambda i: (i, 0)),
        out_shape=jax.ShapeDtypeStruct((m, D_MODEL), F32),
        compiler_params=_cparams(("parallel",)),
        name="ln_in",
    )(x, g.reshape(1, -1), b.reshape(1, -1))


def _proj_kernel(x_ref, w_ref, o_ref, xb_ref):
    @pl.when(pl.program_id(1) == 0)
    def _():
        xb_ref[...] = x_ref[...].astype(BF16)

    o_ref[...] = _dot(xb_ref[...], w_ref[...])


def _proj(x, w, tm, tn, name):
    m, n = x.shape[0], w.shape[1]
    tm = min(tm, m)
    return pl.pallas_call(
        _proj_kernel,
        grid=(m // tm, n // tn),
        in_specs=[pl.BlockSpec((tm, D_MODEL), lambda i, j: (i, 0)),
                  pl.BlockSpec((D_MODEL, tn), lambda i, j: (0, j))],
        out_specs=pl.BlockSpec((tm, tn), lambda i, j: (i, j)),
        out_shape=jax.ShapeDtypeStruct((m, n), F32),
        scratch_shapes=[pltpu.VMEM((tm, D_MODEL), BF16)],
        compiler_params=_cparams(("parallel", "arbitrary")),
        name=name,
    )(x, w)


def _mix_kernel(x_ref, ym_ref, yh_ref, yr_ref, gm_ref, gh_ref, gr_ref,
                wm_ref, wh_ref, wr_ref, wo_ref, g_ref, b_ref, o_ref):
    mixed = _sigmoid(gm_ref[...]) * _dot(ym_ref[...].astype(BF16), wm_ref[...])
    mixed += _sigmoid(gh_ref[...]) * _dot(yh_ref[...].astype(BF16), wh_ref[...])
    mixed += _sigmoid(gr_ref[...]) * _dot(yr_ref[...].astype(BF16), wr_ref[...])
    h = _dot(mixed.astype(BF16), wo_ref[...])
    o_ref[...] = _layer_norm(DN_ALPHA * x_ref[...] + h, g_ref[...], b_ref[...])


def _mix(x, ym, yh, yr, proj, wm, wh, wr, wo, g, b, tm):
    m = x.shape[0]
    tm = min(tm, m)
    row = lambda w: pl.BlockSpec((tm, w), lambda i: (i, 0))
    col = lambda c: pl.BlockSpec((tm, D_MODEL), lambda i, c=c: (i, c))
    full = lambda a: pl.BlockSpec(a.shape, lambda i: (0, 0))
    g2, b2 = g.reshape(1, -1), b.reshape(1, -1)
    return pl.pallas_call(
        _mix_kernel,
        grid=(m // tm,),
        in_specs=[row(D_MODEL), row(M_INNER), row(D_MODEL), row(R_HEADS * R_VDIM),
                  col(COL_GM), col(COL_GH), col(COL_GR),
                  full(wm), full(wh), full(wr), full(wo), full(g2), full(b2)],
        out_specs=row(D_MODEL),
        out_shape=jax.ShapeDtypeStruct((m, D_MODEL), F32),
        compiler_params=_cparams(("parallel",)),
        name="mix",
    )(x, ym, yh, yr, proj, proj, proj, wm, wh, wr, wo, g2, b2)


def _ffn_kernel(x_ref, wi_ref, wo_ref, g_ref, b_ref, o_ref):
    x = x_ref[...]
    xb = x.astype(BF16)
    acc = jnp.zeros(x.shape, F32)
    for j in range(0, D_FF, FF_CHUNK):
        hg = _dot(xb, wi_ref[:, j:j + FF_CHUNK])
        hu = _dot(xb, wi_ref[:, D_FF + j:D_FF + j + FF_CHUNK])
        acc += _dot((_silu(hg) * hu).astype(BF16), wo_ref[j:j + FF_CHUNK, :])
    o_ref[...] = _layer_norm(DN_ALPHA * x + acc, g_ref[...], b_ref[...])


def _ffn(x, wi, wo, g, b, tm):
    m = x.shape[0]
    tm = min(tm, m)
    full = lambda a: pl.BlockSpec(a.shape, lambda i: (0, 0))
    g2, b2 = g.reshape(1, -1), b.reshape(1, -1)
    return pl.pallas_call(
        _ffn_kernel,
        grid=(m // tm,),
        in_specs=[pl.BlockSpec((tm, D_MODEL), lambda i: (i, 0)), full(wi), full(wo), full(g2), full(b2)],
        out_specs=pl.BlockSpec((tm, D_MODEL), lambda i: (i, 0)),
        out_shape=jax.ShapeDtypeStruct((m, D_MODEL), F32),
        compiler_params=_cparams(("parallel",)),
        name="ffn",
    )(x, wi, wo, g2, b2)


def _ssd_seq_kernel(lead, z_ref, xs_ref, bc_ref, dt_ref, cw_ref, cb_ref, dtb_ref, a_ref, dsk_ref, nw_ref,
                    ex_ref, s0_ref, c0_ref, y_ref, sout_ref, cout_ref, ext_ref, st_ref):
    t = SEQ_BLOCK
    c = pl.program_id(1)

    @pl.when(c == 0)
    def _():
        ext_ref[0:8, :] = c0_ref[...]
        for g in range(M_GROUPS):
            st_ref[g] = s0_ref[M_HPG * g:M_HPG * (g + 1)].reshape(M_GW, M_STATE).T

    rows = _iota((t, 1), 0)
    xs_raw, bc_raw = xs_ref[...], bc_ref[...]
    if lead:
        xs_raw = jnp.where(rows >= lead, xs_raw, 0.0)
        bc_raw = jnp.where(rows >= lead, bc_raw, 0.0)
    ext_ref[8:8 + t, 0:M_INNER] = xs_raw
    ext_ref[8:8 + t, M_INNER:M_CONV_DIM] = bc_raw

    acc = cb_ref[...] + cw_ref[3:4, :] * ext_ref[8:8 + t, :]
    for k in range(M_CONV - 1):
        acc += cw_ref[k:k + 1, :] * ext_ref[5 + k:5 + k + t, :]
    tail = ext_ref[t + 5:t + 8, :]
    ext_ref[5:8, :] = tail

    @pl.when(c == pl.num_programs(1) - 1)
    def _():
        cout_ref[...] = tail

    xbc = _silu(acc)
    xs = xbc[:, 0:M_INNER]
    dt = _softplus(dt_ref[...] + dtb_ref[...])
    if lead:
        dt = jnp.where(rows >= lead, dt, 0.0)
    a = dt * a_ref[...]
    ti, si = _iota((t, t), 0), _iota((t, t), 1)
    tril = si <= ti
    cum = _sel_left(jnp.where(tril, 1.0, 0.0).astype(BF16), a)
    ecum = jnp.exp(cum)
    tailw = jnp.exp(cum[t - 1:t, :] - cum) * dt
    cum_t, dt_t = cum.T, dt.T
    ex = ex_ref[...]
    ecum_full = _sel_right(ecum, ex)
    xw = (xs * _sel_right(tailw, ex)).astype(BF16)
    xs_b = xs.astype(BF16)
    head_of_lane = jnp.right_shift(_iota((1, M_GW), 1), int(math.log2(M_HEADDIM)))

    for g in range(M_GROUPS):
        gs = slice(g * M_GW, (g + 1) * M_GW)
        bm_f = xbc[:, M_INNER + g * M_STATE:M_INNER + (g + 1) * M_STATE]
        bm = bm_f.astype(BF16)
        cm = xbc[:, M_INNER + M_BC // 2 + g * M_STATE:M_INNER + M_BC // 2 + (g + 1) * M_STATE].astype(BF16)
        cb = _dot_nt(cm, bm)
        st = st_ref[g]
        y_g = _dot(cm, st.astype(BF16)) * ecum_full[:, gs]
        for hh in range(M_HPG):
            h = g * M_HPG + hh
            diff = cum[:, h:h + 1] - cum_t[h:h + 1, :]
            w = cb * jnp.where(tril, jnp.exp(jnp.minimum(diff, 0.0)), 0.0) * dt_t[h:h + 1, :]
            x_h = jnp.where(head_of_lane == hh, xs_b[:, gs], jnp.zeros((), BF16))
            y_g = y_g + _dot(w.astype(BF16), x_h)
        st_ref[g] = st * ecum_full[t - 1:t, gs] + _dot(bm_f.T.astype(BF16), xw[:, gs])
        y_g = (y_g + dsk_ref[:, gs] * xs[:, gs]) * _silu(z_ref[:, gs])
        y_ref[:, gs] = (_rms(y_g) * nw_ref[:, gs]).astype(y_ref.dtype)

    @pl.when(c == pl.num_programs(1) - 1)
    def _():
        for g in range(M_GROUPS):
            sout_ref[M_HPG * g:M_HPG * (g + 1)] = st_ref[g].T.reshape(M_HPG, M_HEADDIM, M_STATE)


def _ssd_seq(proj3, dt3, row_blk0, nblk, lead, p, s0, c0, out_dtype):
    bsz = proj3.shape[0]
    t = SEQ_BLOCK
    colspec = lambda cblk: pl.BlockSpec((None, t, D_MODEL), lambda b, c, cblk=cblk: (b, row_blk0 + c, cblk))
    full = lambda a: pl.BlockSpec(a.shape, lambda b, c: (0,) * a.ndim)
    bcast = lambda a: pl.BlockSpec((None,) + a.shape[1:], lambda b, c: (0,) * a.ndim)
    params = [p["conv_w"], p["conv_b"], p["dt_bias"], p["a_neg"], p["d_skip"], p["m_norm_w"], p["expand"]]
    return pl.pallas_call(
        functools.partial(_ssd_seq_kernel, lead),
        grid=(bsz, nblk),
        in_specs=[colspec(COL_Z), colspec(COL_XS), colspec(COL_BC),
                  pl.BlockSpec((None, t, LANES), lambda b, c: (b, row_blk0 + c, 0))]
                 + [full(a) for a in params] + [bcast(s0), bcast(c0)],
        out_specs=[pl.BlockSpec((None, t, M_INNER), lambda b, c: (b, c, 0)),
                   pl.BlockSpec((None, M_HEADS, M_HEADDIM, M_STATE), lambda b, c: (b, 0, 0, 0)),
                   pl.BlockSpec((None, M_CONV - 1, M_CONV_DIM), lambda b, c: (b, 0, 0))],
        out_shape=[jax.ShapeDtypeStruct((bsz, nblk * t, M_INNER), out_dtype),
                   jax.ShapeDtypeStruct((bsz, M_HEADS, M_HEADDIM, M_STATE), F32),
                   jax.ShapeDtypeStruct((bsz, M_CONV - 1, M_CONV_DIM), F32)],
        scratch_shapes=[pltpu.VMEM((t + 8, M_CONV_DIM), F32),
                        pltpu.VMEM((M_GROUPS, M_STATE, M_GW), F32)],
        compiler_params=_cparams(("parallel", "arbitrary")),
        name="ssd_seq",
    )(proj3, proj3, proj3, dt3, *params, s0, c0)


def _hgrn_gates(fz, lb):
    log_sig = -_softplus(-fz)
    la = jnp.log(jnp.where(lb > 0.0, lb, 1.0))
    l1m = jnp.log1p(-jnp.where(lb < 1.0, lb, 0.0))
    bterm = jnp.where(lb < 1.0, l1m + log_sig, -1e30)
    both = jnp.maximum(la, bterm) + jnp.log1p(jnp.exp(-jnp.abs(la - bterm)))
    logf = jnp.where(lb > 0.0, both, bterm)
    k = (1.0 - lb) * _sigmoid(-fz)
    return logf, k


def _hgrn_seq_kernel(lead, q_ref, f_ref, i_ref, g_ref, lb_ref, nw_ref, s0_ref, y_ref, sout_ref,
                     st_ref, cum_ref, qs_ref, k_ref, dec_ref, qd_ref, kd_ref, vt_ref):
    t = SEQ_BLOCK
    nsub = t // H_SUB
    c = pl.program_id(1)

    @pl.when(c == 0)
    def _():
        for h in range(H_HEADS):
            st_ref[h] = s0_ref[h].T

    logf, k = _hgrn_gates(f_ref[...], lb_ref[...])
    if lead:
        rows = _iota((t, 1), 0)
        logf = jnp.where(rows >= lead, logf, 0.0)
        k = jnp.where(rows >= lead, k, 0.0)
    ti, si = _iota((t, t), 0), _iota((t, t), 1)
    sub_shift = int(math.log2(H_SUB))
    same = jnp.right_shift(ti, sub_shift) == jnp.right_shift(si, sub_shift)
    one_if = lambda m: jnp.where(m, 1.0, 0.0).astype(BF16)
    cum = _sel_left(one_if(same & (si <= ti)), logf)
    tot = _sel_left(one_if(same), logf)
    q = q_ref[...] * (H_KDIM ** -0.5)
    cum_ref[...] = cum
    qs_ref[...] = q
    k_ref[...] = k
    dec_ref[...] = jnp.exp(tot)
    qd_ref[...] = (q * jnp.exp(cum)).astype(BF16)
    kd_ref[...] = (k * jnp.exp(tot - cum)).astype(BF16)
    for h in range(H_HEADS):
        vt_ref[h] = i_ref[:, h * H_VDIM:(h + 1) * H_VDIM].T.astype(BF16)

    sub_rows = _iota((H_SUB, 1), 0)
    lane_t = _iota((1, t), 1)

    def body(j, carry):
        r0 = pl.multiple_of(j * H_SUB, H_SUB)
        rs = pl.ds(r0, H_SUB)
        in_sub = (lane_t >= r0) & (lane_t < r0 + H_SUB)
        for h in range(H_HEADS):
            cs = slice(h * H_KDIM, (h + 1) * H_KDIM)
            st = st_ref[h]
            o = _dot_nt(qd_ref[rs, cs], st.astype(BF16))
            cq, qh, kh, vh = cum_ref[rs, cs], qs_ref[rs, cs], k_ref[rs, cs], i_ref[rs, cs]
            for s in range(H_SUB):
                d = jnp.exp(jnp.minimum(cq - cq[s:s + 1, :], 0.0))
                w = jnp.where(sub_rows >= s, d * qh * kh[s:s + 1, :], 0.0)
                o = o + jnp.sum(w, axis=-1, keepdims=True) * vh[s:s + 1, :]
            y = _rms(o) * nw_ref[:, cs] * _sigmoid(g_ref[rs, cs])
            y_ref[rs, cs] = y.astype(y_ref.dtype)
            vt = jnp.where(in_sub, vt_ref[h], jnp.zeros((), BF16))
            st_ref[h] = st * dec_ref[pl.ds(r0, 1), cs] + _dot(vt, kd_ref[:, cs])
        return carry

    lax.fori_loop(0, nsub, body, 0)

    @pl.when(c == pl.num_programs(1) - 1)
    def _():
        for h in range(H_HEADS):
            sout_ref[h] = st_ref[h].T


def _hgrn_seq(proj3, row_blk0, nblk, lead, lb, nw, s0, out_dtype):
    bsz = proj3.shape[0]
    t = SEQ_BLOCK
    colspec = lambda cblk: pl.BlockSpec((None, t, D_MODEL), lambda b, c, cblk=cblk: (b, row_blk0 + c, cblk))
    full = lambda a: pl.BlockSpec(a.shape, lambda b, c: (0,) * a.ndim)
    bcast = lambda a: pl.BlockSpec((None,) + a.shape[1:], lambda b, c: (0,) * a.ndim)
    return pl.pallas_call(
        functools.partial(_hgrn_seq_kernel, lead),
        grid=(bsz, nblk),
        in_specs=[colspec(COL_HQ), colspec(COL_HF), colspec(COL_HI), colspec(COL_HG),
                  full(lb), full(nw), bcast(s0)],
        out_specs=[pl.BlockSpec((None, t, D_MODEL), lambda b, c: (b, c, 0)),
                   pl.BlockSpec((None, H_HEADS, H_KDIM, H_VDIM), lambda b, c: (b, 0, 0, 0))],
        out_shape=[jax.ShapeDtypeStruct((bsz, nblk * t, D_MODEL), out_dtype),
                   jax.ShapeDtypeStruct((bsz, H_HEADS, H_KDIM, H_VDIM), F32)],
        scratch_shapes=[pltpu.VMEM((H_HEADS, H_VDIM, H_KDIM), F32),
                        pltpu.VMEM((t, D_MODEL), F32), pltpu.VMEM((t, D_MODEL), F32),
                        pltpu.VMEM((t, D_MODEL), F32), pltpu.VMEM((t, D_MODEL), F32),
                        pltpu.VMEM((t, D_MODEL), BF16), pltpu.VMEM((t, D_MODEL), BF16),
                        pltpu.VMEM((H_HEADS, H_VDIM, t), BF16)],
        compiler_params=_cparams(("parallel", "arbitrary")),
        name="hgrn_seq",
    )(proj3, proj3, proj3, proj3, lb, nw, s0)


def _log_gamma(h):
    return math.log(1.0 - 2.0 ** (-5.0 - h))


def _rotary(x, cos, sin):
    x1, x2 = x[:, :R_HALF], x[:, R_HALF:]
    return jnp.concatenate([x1 * cos - x2 * sin, x2 * cos + x1 * sin], axis=1)


def _ret_seq_kernel(lead, t, q_ref, k_ref, v_ref, g_ref, cos_ref, sin_ref, s0_ref, y_ref, sout_ref, st_ref):
    c = pl.program_id(1)

    @pl.when(c == 0)
    def _():
        st_ref[...] = s0_ref[...]

    cos, sin = cos_ref[...], sin_ref[...]
    ti, si = _iota((t, t), 0), _iota((t, t), 1)
    tril = si <= ti
    dpos = (ti - si).astype(F32)
    tcol = _iota((t, 1), 0).astype(F32)
    for h in range(R_HEADS):
        lg = _log_gamma(h)
        ks = slice(h * R_KDIM, (h + 1) * R_KDIM)
        vs = slice(h * R_VDIM, (h + 1) * R_VDIM)
        qh = _rotary(q_ref[:, ks], cos, sin)
        kh = _rotary(k_ref[:, ks], cos, sin) * (R_KDIM ** -0.5)
        if lead:
            kh = jnp.where(_iota((t, 1), 0) >= lead, kh, 0.0)
        qb, kb, vb = qh.astype(BF16), kh.astype(BF16), v_ref[:, vs].astype(BF16)
        decay = jnp.where(tril, jnp.exp(jnp.where(tril, dpos, 0.0) * lg), 0.0)
        scores = _dot_nt(qb, kb) * decay
        st = st_ref[h]
        o = _dot(scores.astype(BF16), vb) + _dot(qb, st.astype(BF16)) * jnp.exp((tcol + 1.0) * lg)
        kdec = (kh * jnp.exp((t - 1.0 - tcol) * lg)).T.astype(BF16)
        st_ref[h] = st * math.exp(t * lg) + _dot(kdec, vb)
        y_ref[:, vs] = (_rms(o) * _silu(g_ref[:, vs])).astype(y_ref.dtype)

    @pl.when(c == pl.num_programs(1) - 1)
    def _():
        sout_ref[...] = st_ref[...]


def _ret_seq(proj3, t, row_blk0, nblk, lead, cos, sin, s0, out_dtype):
    bsz = proj3.shape[0]
    wide = R_HEADS * R_VDIM
    bcast = lambda a: pl.BlockSpec((None,) + a.shape[1:], lambda b, c: (0,) * a.ndim)
    return pl.pallas_call(
        functools.partial(_ret_seq_kernel, lead, t),
        grid=(bsz, nblk),
        in_specs=[pl.BlockSpec((None, t, D_MODEL), lambda b, c: (b, row_blk0 + c, COL_RQ)),
                  pl.BlockSpec((None, t, D_MODEL), lambda b, c: (b, row_blk0 + c, COL_RK)),
                  pl.BlockSpec((None, t, wide), lambda b, c: (b, row_blk0 + c, COL_RV)),
                  pl.BlockSpec((None, t, wide), lambda b, c: (b, row_blk0 + c, COL_RG)),
                  pl.BlockSpec((t, R_HALF), lambda b, c: (c, 0)),
                  pl.BlockSpec((t, R_HALF), lambda b, c: (c, 0)),
                  bcast(s0)],
        out_specs=[pl.BlockSpec((None, t, wide), lambda b, c: (b, c, 0)),
                   pl.BlockSpec((None, R_HEADS, R_KDIM, R_VDIM), lambda b, c: (b, 0, 0, 0))],
        out_shape=[jax.ShapeDtypeStruct((bsz, nblk * t, wide), out_dtype),
                   jax.ShapeDtypeStruct((bsz, R_HEADS, R_KDIM, R_VDIM), F32)],
        scratch_shapes=[pltpu.VMEM((R_HEADS, R_KDIM, R_VDIM), F32)],
        compiler_params=_cparams(("parallel", "arbitrary")),
        name="ret_seq",
    )(proj3, proj3, proj3, proj3, cos, sin, s0)


def _step_prep_kernel(xs_ref, bc_ref, dt_ref, hq_ref, hf_ref, rq_ref, rk_ref, conv_ref,
                      cw_ref, cb_ref, dtb_ref, a_ref, ex_ref, lb_ref, cos_ref, sin_ref,
                      xs_o, bc_o, xdt_o, edec_o, conv_o, hq_o, hef_o, hk_o, rq_o, rk_o):
    raw = jnp.concatenate([xs_ref[...], bc_ref[...]], axis=1)
    buf = conv_ref[...]
    acc = cb_ref[...] + cw_ref[3:4, :] * raw
    for k in range(M_CONV - 1):
        acc += cw_ref[k:k + 1, :] * buf[:, k * M_CONV_DIM:(k + 1) * M_CONV_DIM]
    conv_o[:, 0:2 * M_CONV_DIM] = buf[:, M_CONV_DIM:]
    conv_o[:, 2 * M_CONV_DIM:] = raw
    xbc = _silu(acc)
    xs = xbc[:, :M_INNER]
    xs_o[...] = xs
    bc_o[...] = xbc[:, M_INNER:]
    dt = _softplus(dt_ref[...] + dtb_ref[...])
    ex = ex_ref[...]
    xdt_o[...] = xs * _sel_right(dt, ex)
    edec_o[...] = _sel_right(jnp.exp(dt * a_ref[...]), ex)
    logf, k = _hgrn_gates(hf_ref[...], lb_ref[...])
    hq_o[...] = hq_ref[...] * (H_KDIM ** -0.5)
    hef_o[...] = jnp.exp(logf)
    hk_o[...] = k
    cos, sin = cos_ref[...], sin_ref[...]
    for h in range(R_HEADS):
        ks = slice(h * R_KDIM, (h + 1) * R_KDIM)
        rq_o[:, ks] = _rotary(rq_ref[:, ks], cos, sin)
        rk_o[:, ks] = _rotary(rk_ref[:, ks], cos, sin) * (R_KDIM ** -0.5)


def _step_prep(proj_s, dt_s, conv_flat, p, cos, sin):
    nb = conv_flat.shape[0]
    col = lambda cblk: pl.BlockSpec((nb, D_MODEL), lambda i, cblk=cblk: (0, cblk))
    full = lambda a: pl.BlockSpec(a.shape, lambda i: (0,) * a.ndim)
    params = [p["conv_w"], p["conv_b"], p["dt_bias"], p["a_neg"], p["expand"], p["lb"], cos, sin]
    o = lambda w: jax.ShapeDtypeStruct((nb, w), F32)
    widths = [M_INNER, M_BC, M_INNER, M_INNER, (M_CONV - 1) * M_CONV_DIM, D_MODEL, D_MODEL, D_MODEL, D_MODEL, D_MODEL]
    return pl.pallas_call(
        _step_prep_kernel,
        grid=(1,),
        in_specs=[col(COL_XS), col(COL_BC), pl.BlockSpec((nb, LANES), lambda i: (0, 0)),
                  col(COL_HQ), col(COL_HF), col(COL_RQ), col(COL_RK), full(conv_flat)]
                 + [full(a) for a in params],
        out_specs=[pl.BlockSpec((nb, w), lambda i: (0, 0)) for w in widths],
        out_shape=[o(w) for w in widths],
        compiler_params=_cparams(("arbitrary",)),
        name="step_prep",
    )(proj_s, proj_s, dt_s, proj_s, proj_s, proj_s, proj_s, conv_flat, *params)


def _cols(x):
    nb, w = x.shape
    return jnp.transpose(x.reshape(nb // STEP_BT, STEP_BT, w), (0, 2, 1))


def _ssd_step_kernel(s_ref, xdt_ref, edec_ref, bc_ref, xs_ref, z_ref, dsk_ref, nw_ref, so_ref, y_ref, yt_ref):
    yt_ref[...] = jnp.zeros(yt_ref.shape, F32)
    for i in range(STEP_BT):
        for g in range(M_GROUPS):
            gs = slice(g * M_GW, (g + 1) * M_GW)
            hs = slice(M_HPG * g, M_HPG * (g + 1))
            st = s_ref[i, hs].reshape(M_GW, M_STATE)
            brow = bc_ref[i:i + 1, g * M_STATE:(g + 1) * M_STATE]
            crow = bc_ref[i:i + 1, M_BC // 2 + g * M_STATE:M_BC // 2 + (g + 1) * M_STATE]
            new = st * edec_ref[gs, i:i + 1] + xdt_ref[gs, i:i + 1] * brow
            so_ref[i, hs] = new.reshape(M_HPG, M_HEADDIM, M_STATE)
            yt_ref[gs, i:i + 1] = jnp.sum(new * crow, axis=-1, keepdims=True)
    y = yt_ref[...].T[0:STEP_BT, :]
    xs = xs_ref[...]
    y = (y + dsk_ref[...] * xs) * _silu(z_ref[...])
    for g in range(M_GROUPS):
        gs = slice(g * M_GW, (g + 1) * M_GW)
        y_ref[:, gs] = _rms(y[:, gs]) * nw_ref[:, gs]


def _ssd_step(state, layer, xdt_c, edec_c, bc, xs, proj_s, p):
    nb = xs.shape[0]
    bt = STEP_BT
    full = lambda a: pl.BlockSpec(a.shape, lambda j: (0,) * a.ndim)
    sspec = pl.BlockSpec((None, bt, M_HEADS, M_HEADDIM, M_STATE), lambda j: (layer, j, 0, 0, 0))
    return pl.pallas_call(
        _ssd_step_kernel,
        grid=(nb // bt,),
        in_specs=[sspec,
                  pl.BlockSpec((None, M_INNER, bt), lambda j: (j, 0, 0)),
                  pl.BlockSpec((None, M_INNER, bt), lambda j: (j, 0, 0)),
                  pl.BlockSpec((bt, M_BC), lambda j: (j, 0)),
                  pl.BlockSpec((bt, M_INNER), lambda j: (j, 0)),
                  pl.BlockSpec((bt, D_MODEL), lambda j: (j, COL_Z)),
                  full(p["d_skip"]), full(p["m_norm_w"])],
        out_specs=[pl.BlockSpec((bt, M_HEADS, M_HEADDIM, M_STATE), lambda j: (j, 0, 0, 0)),
                   pl.BlockSpec((bt, M_INNER), lambda j: (j, 0))],
        out_shape=[jax.ShapeDtypeStruct((nb, M_HEADS, M_HEADDIM, M_STATE), F32),
                   jax.ShapeDtypeStruct((nb, M_INNER), F32)],
        scratch_shapes=[pltpu.VMEM((M_INNER, LANES), F32)],
        compiler_params=_cparams(("parallel",)),
        name="ssd_step",
    )(state, xdt_c, edec_c, bc, xs, proj_s, p["d_skip"], p["m_norm_w"])


def _hgrn_step_kernel(s_ref, q_ref, ef_ref, k_ref, v_ref, g_ref, nw_ref, so_ref, y_ref):
    for i in range(STEP_BT):
        for h in range(H_HEADS):
            cs = slice(h * H_KDIM, (h + 1) * H_KDIM)
            new = s_ref[i, h] * ef_ref[cs, i:i + 1] + k_ref[cs, i:i + 1] * v_ref[i:i + 1, cs]
            so_ref[i, h] = new
            y_ref[i:i + 1, cs] = jnp.sum(new * q_ref[cs, i:i + 1], axis=0, keepdims=True)
    for h in range(H_HEADS):
        cs = slice(h * H_KDIM, (h + 1) * H_KDIM)
        y_ref[:, cs] = _rms(y_ref[:, cs]) * nw_ref[:, cs] * _sigmoid(g_ref[:, cs])


def _hgrn_step(state, layer, q_c, ef_c, k_c, proj_s, nw):
    nb = proj_s.shape[0] // 2
    bt = STEP_BT
    cspec = pl.BlockSpec((None, D_MODEL, bt), lambda j: (j, 0, 0))
    return pl.pallas_call(
        _hgrn_step_kernel,
        grid=(nb // bt,),
        in_specs=[pl.BlockSpec((None, bt, H_HEADS, H_KDIM, H_VDIM), lambda j: (layer, j, 0, 0, 0)),
                  cspec, cspec, cspec,
                  pl.BlockSpec((bt, D_MODEL), lambda j: (j, COL_HI)),
                  pl.BlockSpec((bt, D_MODEL), lambda j: (j, COL_HG)),
                  pl.BlockSpec(nw.shape, lambda j: (0, 0))],
        out_specs=[pl.BlockSpec((bt, H_HEADS, H_KDIM, H_VDIM), lambda j: (j, 0, 0, 0)),
                   pl.BlockSpec((bt, D_MODEL), lambda j: (j, 0))],
        out_shape=[jax.ShapeDtypeStruct((nb, H_HEADS, H_KDIM, H_VDIM), F32),
                   jax.ShapeDtypeStruct((nb, D_MODEL), F32)],
        compiler_params=_cparams(("parallel",)),
        name="hgrn_step",
    )(state, q_c, ef_c, k_c, proj_s, proj_s, nw)


def _ret_step_kernel(s_ref, q_ref, k_ref, v_ref, g_ref, gam_ref, so_ref, y_ref):
    gam = gam_ref[...]
    for i in range(STEP_BT):
        new = s_ref[i] * gam + k_ref[:, i:i + 1] * v_ref[i:i + 1, :]
        so_ref[i] = new
        y_ref[i:i + 1, :] = jnp.sum(new * q_ref[:, i:i + 1], axis=0, keepdims=True)
    y_ref[...] = _rms(y_ref[...]) * _silu(g_ref[...])


def _ret_step(state, layer, q_c, k_c, proj_s, gam):
    nb = proj_s.shape[0] // 2
    bt = STEP_BT
    cspec = pl.BlockSpec((None, R_KDIM, bt), lambda j, h: (j, h, 0))
    return pl.pallas_call(
        _ret_step_kernel,
        grid=(nb // bt, R_HEADS),
        in_specs=[pl.BlockSpec((None, bt, None, R_KDIM, R_VDIM), lambda j, h: (layer, j, h, 0, 0)),
                  cspec, cspec,
                  pl.BlockSpec((bt, R_VDIM), lambda j, h: (j, h)),
                  pl.BlockSpec((bt, R_VDIM), lambda j, h: (j, R_HEADS + h)),
                  pl.BlockSpec((None, 1, R_VDIM), lambda j, h: (h, 0, 0))],
        out_specs=[pl.BlockSpec((bt, None, R_KDIM, R_VDIM), lambda j, h: (j, h, 0, 0)),
                   pl.BlockSpec((bt, R_VDIM), lambda j, h: (j, h))],
        out_shape=[jax.ShapeDtypeStruct((nb, R_HEADS, R_KDIM, R_VDIM), F32),
                   jax.ShapeDtypeStruct((nb, R_HEADS * R_VDIM), F32)],
        compiler_params=_cparams(("parallel", "parallel")),
        name="ret_step",
    )(state, q_c, k_c, proj_s, proj_s, gam)


def _rope_tables(positions):
    inv_freq = 1.0 / (ROPE_BASE ** jnp.linspace(0.0, 1.0, R_HALF, dtype=F32))
    ang = positions[:, None] * inv_freq[None, :]
    return jnp.cos(ang), jnp.sin(ang)


def _per_channel(v):
    return jnp.repeat(v.astype(F32), M_HEADDIM).reshape(1, M_INNER)


def _pad_lanes(v):
    return jnp.pad(v.astype(F32), (0, LANES - v.shape[0])).reshape(1, LANES)


def kernel(x_prompt, x_sample, state_ssm, state_conv, state_hgrn, state_ret, meta_tokens, ln_in_g, ln_in_b,
           w_in, conv_w, conv_b, dt_bias, a_log, d_skip, m_norm_w, hgrn_lb_logits, h_norm_w, w_br_m, w_br_h,
           w_br_r, w_out, ln1_g, ln1_b, w_ffn_in, w_ffn_out, ln2_g, ln2_b):
    bp, sp = x_prompt.shape[0], x_prompt.shape[1]
    nb = x_sample.shape[0]
    assert x_sample.shape[1] == 1 and nb == SMALL_ROWS - SEQ_BLOCK and nb % STEP_BT == 0
    assert sp % RET_BLOCK == 0 and meta_tokens.shape[0] == N_META

    o = _ORIG_OFF
    w_main = jnp.concatenate([w_in[:, :, o[9]:o[11]], w_in[:, :, o[0]:o[2]], w_in[:, :, o[3]:o[9]],
                              w_in[:, :, o[11]:o[12]]], axis=2).astype(BF16)
    w_dt = jnp.pad(w_in[:, :, o[2]:o[3]], ((0, 0), (0, 0), (0, LANES - M_HEADS))).astype(BF16)
    wm_b, wh_b, wr_b, wo_b = (w.astype(BF16) for w in (w_br_m, w_br_h, w_br_r, w_out))
    wfi_b, wfo_b = w_ffn_in.astype(BF16), w_ffn_out.astype(BF16)
    lb_cum = jnp.cumsum(jax.nn.softmax(hgrn_lb_logits.astype(F32), axis=0), axis=0)
    lbs = lb_cum - lb_cum[0]
    expand = (np.arange(LANES)[:, None] == (np.arange(M_INNER)[None, :] // M_HEADDIM)).astype(np.float32)
    expand = jnp.asarray(expand, BF16)
    gam = jnp.asarray(np.broadcast_to(
        np.array([1.0 - 2.0 ** (-5.0 - h) for h in range(R_HEADS)], np.float32)[:, None, None],
        (R_HEADS, 1, R_VDIM)))

    pos_real = jnp.arange(N_META, N_META + sp, dtype=F32)
    pos_meta = jnp.maximum(jnp.arange(SEQ_BLOCK, dtype=F32) - META_LEAD, 0.0)
    pos_samp = jnp.full((nb,), float(PAST_LEN), F32)
    cos_r, sin_r = _rope_tables(pos_real)
    cos_m, sin_m = _rope_tables(pos_meta)
    cos_s, sin_s = _rope_tables(pos_samp)

    x_real = _layer_norm_rows(x_prompt.reshape(bp * sp, D_MODEL), ln_in_g, ln_in_b, 512)
    small_in = jnp.concatenate([x_sample.reshape(nb, D_MODEL),
                                jnp.zeros((META_LEAD, D_MODEL), F32), meta_tokens.astype(F32)], axis=0)
    x_small = _layer_norm_rows(small_in, ln_in_g, ln_in_b, SMALL_ROWS)

    outs = {k: [] for k in ("ssm_p", "conv_p", "hgrn_p", "ret_p", "ssm_s", "conv_s", "hgrn_s", "ret_s")}
    for l in range(DEPTH):
        p = dict(conv_w=conv_w[l], conv_b=conv_b[l].reshape(1, -1), dt_bias=_pad_lanes(dt_bias[l]),
                 a_neg=_pad_lanes(-jnp.exp(a_log[l].astype(F32))), d_skip=_per_channel(d_skip[l]),
                 m_norm_w=m_norm_w[l].reshape(1, -1), expand=expand, lb=lbs[l].reshape(1, -1))
        hnw = h_norm_w[l].reshape(1, -1)

        proj_r = _proj(x_real, w_main[l], 1024, 1024, "proj_real")
        dt_r = _proj(x_real, w_dt[l], 1024, LANES, "proj_dt_real")
        proj_s = _proj(x_small, w_main[l], SMALL_ROWS, 2048, "proj_small")
        dt_s = _proj(x_small, w_dt[l], SMALL_ROWS, LANES, "proj_dt_small")
        proj_r3 = proj_r.reshape(bp, sp, PROJ_COLS)
        dt_r3 = dt_r.reshape(bp, sp, LANES)
        proj_s3 = proj_s.reshape(1, SMALL_ROWS, PROJ_COLS)
        dt_s3 = dt_s.reshape(1, SMALL_ROWS, LANES)

        z_ssm = jnp.zeros((1, M_HEADS, M_HEADDIM, M_STATE), F32)
        z_conv = jnp.zeros((1, 8, M_CONV_DIM), F32)
        ym_m, ssm_m, conv_m = _ssd_seq(proj_s3, dt_s3, 1, 1, META_LEAD, p, z_ssm, z_conv, F32)
        yh_m, hgrn_m = _hgrn_seq(proj_s3, 1, 1, META_LEAD, p["lb"], hnw,
                                 jnp.zeros((1, H_HEADS, H_KDIM, H_VDIM), F32), F32)
        yr_m, ret_m = _ret_seq(proj_s3, SEQ_BLOCK, 1, 1, META_LEAD, cos_m, sin_m,
                               jnp.zeros((1, R_HEADS, R_KDIM, R_VDIM), F32), F32)

        conv0 = jnp.pad(conv_m, ((0, 0), (8 - (M_CONV - 1), 0), (0, 0)))
        ym_r, ssm_r, conv_r = _ssd_seq(proj_r3, dt_r3, 0, sp // SEQ_BLOCK, 0, p, ssm_m, conv0, BF16)
        yh_r, hgrn_r = _hgrn_seq(proj_r3, 0, sp // SEQ_BLOCK, 0, p["lb"], hnw, hgrn_m, BF16)
        yr_r, ret_r = _ret_seq(proj_r3, RET_BLOCK, 0, sp // RET_BLOCK, 0, cos_r, sin_r, ret_m, BF16)
        outs["ssm_p"].append(ssm_r); outs["conv_p"].append(conv_r)
        outs["hgrn_p"].append(hgrn_r); outs["ret_p"].append(ret_r)

        conv_flat = state_conv[l].reshape(nb, (M_CONV - 1) * M_CONV_DIM)
        (xs_s, bc_s, xdt_s, edec_s, conv_new, hq_s, hef_s, hk_s, rq_s, rk_s) = _step_prep(
            proj_s, dt_s, conv_flat, p, cos_s, sin_s)
        ssm_new, ym_s = _ssd_step(state_ssm, l, _cols(xdt_s), _cols(edec_s), bc_s, xs_s, proj_s, p)
        hgrn_new, yh_s = _hgrn_step(state_hgrn, l, _cols(hq_s), _cols(hef_s), _cols(hk_s), proj_s, hnw)
        ret_new, yr_s = _ret_step(state_ret, l, _cols(rq_s), _cols(rk_s), proj_s, gam)
        outs["ssm_s"].append(ssm_new); outs["conv_s"].append(conv_new.reshape(nb, M_CONV - 1, M_CONV_DIM))
        outs["hgrn_s"].append(hgrn_new); outs["ret_s"].append(ret_new)

        ym_small = jnp.concatenate([ym_s, ym_m[0]], axis=0)
        yh_small = jnp.concatenate([yh_s, yh_m[0]], axis=0)
        yr_small = jnp.concatenate([yr_s, yr_m[0]], axis=0)
        x_real = _mix(x_real, ym_r.reshape(bp * sp, -1), yh_r.reshape(bp * sp, -1), yr_r.reshape(bp * sp, -1),
                      proj_r, wm_b[l], wh_b[l], wr_b[l], wo_b[l], ln1_g[l], ln1_b[l], 256)
        x_small = _mix(x_small, ym_small, yh_small, yr_small, proj_s,
                       wm_b[l], wh_b[l], wr_b[l], wo_b[l], ln1_g[l], ln1_b[l], SMALL_ROWS)
        x_real = _ffn(x_real, wfi_b[l], wfo_b[l], ln2_g[l], ln2_b[l], 512)
        x_small = _ffn(x_small, wfi_b[l], wfo_b[l], ln2_g[l], ln2_b[l], SMALL_ROWS)

    st = {k: jnp.stack(v) for k, v in outs.items()}
    return (x_real.reshape(bp, sp, D_MODEL), x_small[:nb].reshape(nb, 1, D_MODEL),
            st["ssm_p"], st["conv_p"], st["hgrn_p"], st["ret_p"],
            st["ssm_s"], st["conv_s"], st["hgrn_s"], st["ret_s"])
```

```python
import functools
import math

import numpy as np
import jax
import jax.numpy as jnp
from jax import lax
from jax.experimental import pallas as pl
from jax.experimental.pallas import tpu as pltpu

F32 = jnp.float32
BF16 = jnp.bfloat16

D_MODEL = 1024
DEPTH = 2
N_META = 16
M_INNER = D_MODEL
M_HEADDIM = 64
M_HEADS = M_INNER // M_HEADDIM
M_GROUPS = 4
M_HPG = M_HEADS // M_GROUPS
M_STATE = 128
M_CONV = 4
M_BC = 2 * M_GROUPS * M_STATE
M_CONV_DIM = M_INNER + M_BC
M_GW = M_INNER // M_GROUPS
H_KDIM = 128
H_HEADS = D_MODEL // H_KDIM
H_VDIM = 128
H_SUB = 16
R_HEADS = 4
R_KDIM = D_MODEL // R_HEADS
R_VDIM = 2 * R_KDIM
R_HALF = R_KDIM // 2
ROPE_BASE = 10000.0
D_FF = ((8 * D_MODEL // 3 + 255) // 256) * 256
FF_CHUNK = 256
DN_ALPHA = (2 * DEPTH) ** 0.25
PAST_LEN = 16384

LANES = 128
SEQ_BLOCK = 128
RET_BLOCK = 256
SMALL_ROWS = 256
META_LEAD = SEQ_BLOCK - N_META
STEP_BT = 8
VMEM_LIMIT = 56 * 1024 * 1024

COL_RV, COL_RG = 0, 1
COL_Z, COL_XS, COL_BC, COL_HQ, COL_HF, COL_HI, COL_HG, COL_RQ, COL_RK, COL_GM, COL_GH, COL_GR = range(4, 16)
PROJ_COLS = 16 * 1024

_ORIG_SPLITS = (M_INNER, M_CONV_DIM, M_HEADS, 1024, 1024, 1024, 1024, 1024, 1024, 2048, 2048, 3072)
_ORIG_OFF = np.concatenate([[0], np.cumsum(_ORIG_SPLITS)]).tolist()


def _cparams(sem):
    return pltpu.CompilerParams(dimension_semantics=sem, vmem_limit_bytes=VMEM_LIMIT)


def _sigmoid(x):
    return 1.0 / (1.0 + jnp.exp(-x))


def _silu(x):
    return x * _sigmoid(x)


def _softplus(x):
    return jnp.maximum(x, 0.0) + jnp.log1p(jnp.exp(-jnp.abs(x)))


def _layer_norm(x, g, b):
    mu = jnp.mean(x, axis=-1, keepdims=True)
    xc = x - mu
    var = jnp.mean(xc * xc, axis=-1, keepdims=True)
    return xc * lax.rsqrt(var + 1e-5) * g + b


def _rms(x):
    return x * lax.rsqrt(jnp.mean(x * x, axis=-1, keepdims=True) + 1e-6)


def _split3(x):
    hi = x.astype(BF16)
    r = x - hi.astype(F32)
    mid = r.astype(BF16)
    lo = (r - mid.astype(F32)).astype(BF16)
    return hi, mid, lo


def _dot(a, b):
    return jnp.dot(a, b, preferred_element_type=F32)


def _dot_nt(a, b):
    return lax.dot_general(a, b, (((1,), (1,)), ((), ())), preferred_element_type=F32)


def _sel_right(x, m01):
    hi, mid, lo = _split3(x)
    return _dot(hi, m01) + _dot(mid, m01) + _dot(lo, m01)


def _sel_left(m01, x):
    hi, mid, lo = _split3(x)
    return _dot(m01, hi) + _dot(m01, mid) + _dot(m01, lo)


def _iota(shape, dim):
    return lax.broadcasted_iota(jnp.int32, shape, dim)


def _ln_kernel(x_ref, g_ref, b_ref, o_ref):
    o_ref[...] = _layer_norm(x_ref[...], g_ref[...], b_ref[...])


def _layer_norm_rows(x, g, b, tm):
    m = x.shape[0]
    tm = min(tm, m)
    return pl.pallas_call(
        _ln_kernel,
        grid=(m // tm,),
        in_specs=[pl.BlockSpec((tm, D_MODEL), lambda i: (i, 0)),
                  pl.BlockSpec((1, D_MODEL), lambda i: (0, 0)),
                  pl.BlockSpec((1, D_MODEL), lambda i: (0, 0))],
        out_specs=pl.BlockSpec((tm, D_MODEL), lambda i: (i, 0)),
        out_shape=jax.ShapeDtypeStruct((m, D_MODEL), F32),
        compiler_params=_cparams(("parallel",)),
        name="ln_in",
    )(x, g.reshape(1, -1), b.reshape(1, -1))


def _proj_kernel(x_ref, w_ref, o_ref, xb_ref):
    @pl.when(pl.program_id(1) == 0)
    def _():
        xb_ref[...] = x_ref[...].astype(BF16)

    o_ref[...] = _dot(xb_ref[...], w_ref[...])


def _layer_block(a, layer):
    return pl.BlockSpec((None,) + a.shape[1:], lambda *_: (layer,) + (0,) * (a.ndim - 1))


def _proj(x, w, layer, tm, tn, name):
    m, n = x.shape[0], w.shape[2]
    tm = min(tm, m)
    return pl.pallas_call(
        _proj_kernel,
        grid=(m // tm, n // tn),
        in_specs=[pl.BlockSpec((tm, D_MODEL), lambda i, j: (i, 0)),
                  pl.BlockSpec((None, D_MODEL, tn), lambda i, j: (layer, 0, j))],
        out_specs=pl.BlockSpec((tm, tn), lambda i, j: (i, j)),
        out_shape=jax.ShapeDtypeStruct((m, n), F32),
        scratch_shapes=[pltpu.VMEM((tm, D_MODEL), BF16)],
        compiler_params=_cparams(("parallel", "arbitrary")),
        name=name,
    )(x, w)


_W_PIECES = ((9, 11), (0, 2), (3, 9), (11, 12))


def _wprep_kernel(w_ref, o_ref, dt_ref):
    c = 0
    for a, b in _W_PIECES:
        lo, hi = _ORIG_OFF[a], _ORIG_OFF[b]
        o_ref[:, c:c + hi - lo] = w_ref[:, lo:hi].astype(BF16)
        c += hi - lo
    dt0 = _ORIG_OFF[2]
    dtw = w_ref[:, dt0:dt0 + LANES]
    dt_ref[...] = jnp.where(_iota(dtw.shape, 1) < M_HEADS, dtw, 0.0).astype(BF16)


def _wprep(w_in, tr):
    depth, k, n = w_in.shape
    return pl.pallas_call(
        _wprep_kernel,
        grid=(depth, k // tr),
        in_specs=[pl.BlockSpec((None, tr, n), lambda l, i: (l, i, 0))],
        out_specs=[pl.BlockSpec((None, tr, PROJ_COLS), lambda l, i: (l, i, 0)),
                   pl.BlockSpec((None, tr, LANES), lambda l, i: (l, i, 0))],
        out_shape=[jax.ShapeDtypeStruct((depth, k, PROJ_COLS), BF16),
                   jax.ShapeDtypeStruct((depth, k, LANES), BF16)],
        compiler_params=_cparams(("parallel", "parallel")),
        name="wprep",
    )(w_in)


def _mix_kernel(x_ref, ym_ref, yh_ref, yr_ref, gm_ref, gh_ref, gr_ref,
                wm_ref, wh_ref, wr_ref, wo_ref, g_ref, b_ref, o_ref):
    mixed = _sigmoid(gm_ref[...]) * _dot(ym_ref[...].astype(BF16), wm_ref[...])
    mixed += _sigmoid(gh_ref[...]) * _dot(yh_ref[...].astype(BF16), wh_ref[...])
    mixed += _sigmoid(gr_ref[...]) * _dot(yr_ref[...].astype(BF16), wr_ref[...])
    h = _dot(mixed.astype(BF16), wo_ref[...])
    o_ref[...] = _layer_norm(DN_ALPHA * x_ref[...] + h, g_ref[...], b_ref[...])


def _mix(x, ym, yh, yr, proj, layer, wm, wh, wr, wo, g2, b2, tm):
    m = x.shape[0]
    tm = min(tm, m)
    row = lambda w: pl.BlockSpec((tm, w), lambda i: (i, 0))
    col = lambda c: pl.BlockSpec((tm, D_MODEL), lambda i, c=c: (i, c))
    full = lambda a: _layer_block(a, layer)
    return pl.pallas_call(
        _mix_kernel,
        grid=(m // tm,),
        in_specs=[row(D_MODEL), row(M_INNER), row(D_MODEL), row(R_HEADS * R_VDIM),
                  col(COL_GM), col(COL_GH), col(COL_GR),
                  full(wm), full(wh), full(wr), full(wo), full(g2), full(b2)],
        out_specs=row(D_MODEL),
        out_shape=jax.ShapeDtypeStruct((m, D_MODEL), F32),
        compiler_params=_cparams(("parallel",)),
        name="mix",
    )(x, ym, yh, yr, proj, proj, proj, wm, wh, wr, wo, g2, b2)


def _ffn_kernel(x_ref, wi_ref, wo_ref, g_ref, b_ref, o_ref):
    x = x_ref[...]
    xb = x.astype(BF16)
    acc = jnp.zeros(x.shape, F32)
    for j in range(0, D_FF, FF_CHUNK):
        hg = _dot(xb, wi_ref[:, j:j + FF_CHUNK])
        hu = _dot(xb, wi_ref[:, D_FF + j:D_FF + j + FF_CHUNK])
        acc += _dot((_silu(hg) * hu).astype(BF16), wo_ref[j:j + FF_CHUNK, :])
    o_ref[...] = _layer_norm(DN_ALPHA * x + acc, g_ref[...], b_ref[...])


def _ffn(x, layer, wi, wo, g2, b2, tm):
    m = x.shape[0]
    tm = min(tm, m)
    full = lambda a: _layer_block(a, layer)
    return pl.pallas_call(
        _ffn_kernel,
        grid=(m // tm,),
        in_specs=[pl.BlockSpec((tm, D_MODEL), lambda i: (i, 0)), full(wi), full(wo), full(g2), full(b2)],
        out_specs=pl.BlockSpec((tm, D_MODEL), lambda i: (i, 0)),
        out_shape=jax.ShapeDtypeStruct((m, D_MODEL), F32),
        compiler_params=_cparams(("parallel",)),
        name="ffn",
    )(x, wi, wo, g2, b2)


def _ssd_seq_kernel(lead, z_ref, xs_ref, bc_ref, dt_ref, cw_ref, cb_ref, dtb_ref, a_ref, dsk_ref, nw_ref,
                    ex_ref, s0_ref, c0_ref, y_ref, sout_ref, cout_ref, ext_ref, st_ref):
    t = SEQ_BLOCK
    c = pl.program_id(1)

    @pl.when(c == 0)
    def _():
        ext_ref[0:8, :] = c0_ref[...]
        for g in range(M_GROUPS):
            st_ref[g] = s0_ref[M_HPG * g:M_HPG * (g + 1)].reshape(M_GW, M_STATE).T

    rows = _iota((t, 1), 0)
    xs_raw, bc_raw = xs_ref[...], bc_ref[...]
    if lead:
        xs_raw = jnp.where(rows >= lead, xs_raw, 0.0)
        bc_raw = jnp.where(rows >= lead, bc_raw, 0.0)
    ext_ref[8:8 + t, 0:M_INNER] = xs_raw
    ext_ref[8:8 + t, M_INNER:M_CONV_DIM] = bc_raw

    acc = cb_ref[...] + cw_ref[3:4, :] * ext_ref[8:8 + t, :]
    for k in range(M_CONV - 1):
        acc += cw_ref[k:k + 1, :] * ext_ref[5 + k:5 + k + t, :]
    tail = ext_ref[t + 5:t + 8, :]
    ext_ref[5:8, :] = tail

    @pl.when(c == pl.num_programs(1) - 1)
    def _():
        cout_ref[...] = tail

    xbc = _silu(acc)
    xs = xbc[:, 0:M_INNER]
    dt = _softplus(dt_ref[...] + dtb_ref[...])
    if lead:
        dt = jnp.where(rows >= lead, dt, 0.0)
    a = dt * a_ref[...]
    ti, si = _iota((t, t), 0), _iota((t, t), 1)
    tril = si <= ti
    cum = _sel_left(jnp.where(tril, 1.0, 0.0).astype(BF16), a)
    ecum = jnp.exp(cum)
    tailw = jnp.exp(cum[t - 1:t, :] - cum) * dt
    cum_t, dt_t = cum.T, dt.T
    ex = ex_ref[...]
    ecum_full = _sel_right(ecum, ex)
    xw = (xs * _sel_right(tailw, ex)).astype(BF16)
    xs_b = xs.astype(BF16)
    head_of_lane = jnp.right_shift(_iota((1, M_GW), 1), int(math.log2(M_HEADDIM)))

    for g in range(M_GROUPS):
        gs = slice(g * M_GW, (g + 1) * M_GW)
        bm_f = xbc[:, M_INNER + g * M_STATE:M_INNER + (g + 1) * M_STATE]
        bm = bm_f.astype(BF16)
        cm = xbc[:, M_INNER + M_BC // 2 + g * M_STATE:M_INNER + M_BC // 2 + (g + 1) * M_STATE].astype(BF16)
        cb = _dot_nt(cm, bm)
        st = st_ref[g]
        y_g = _dot(cm, st.astype(BF16)) * ecum_full[:, gs]
        for hh in range(M_HPG):
            h = g * M_HPG + hh
            diff = cum[:, h:h + 1] - cum_t[h:h + 1, :]
            w = cb * jnp.where(tril, jnp.exp(jnp.minimum(diff, 0.0)), 0.0) * dt_t[h:h + 1, :]
            x_h = jnp.where(head_of_lane == hh, xs_b[:, gs], jnp.zeros((), BF16))
            y_g = y_g + _dot(w.astype(BF16), x_h)
        st_ref[g] = st * ecum_full[t - 1:t, gs] + _dot(bm_f.T.astype(BF16), xw[:, gs])
        y_g = (y_g + dsk_ref[:, gs] * xs[:, gs]) * _silu(z_ref[:, gs])
        y_ref[:, gs] = (_rms(y_g) * nw_ref[:, gs]).astype(y_ref.dtype)

    @pl.when(c == pl.num_programs(1) - 1)
    def _():
        for g in range(M_GROUPS):
            sout_ref[M_HPG * g:M_HPG * (g + 1)] = st_ref[g].T.reshape(M_HPG, M_HEADDIM, M_STATE)


def _alias_prev(body, n_in, prevs, out_ids):
    if not prevs:
        return body, [], [], {}
    k = len(prevs)
    wrapped = lambda *refs: body(*refs[:n_in], *refs[n_in + k:])
    return (wrapped, [pl.BlockSpec(memory_space=pl.ANY)] * k, list(prevs),
            {n_in + i: o for i, o in enumerate(out_ids)})


def _seq_state_out(stack, bsz, dims):
    zeros = (0,) * len(dims)
    if stack is None:
        return (jax.ShapeDtypeStruct((bsz,) + dims, F32),
                pl.BlockSpec((None,) + dims, lambda b, c: (b,) + zeros))
    layer = stack[0]
    return (jax.ShapeDtypeStruct((DEPTH, bsz) + dims, F32),
            pl.BlockSpec((None, None) + dims, lambda b, c: (layer, b) + zeros))


def _ssd_seq(proj3, dt3, row_blk0, nblk, lead, p, s0, c0, out_dtype, stack=None):
    bsz = proj3.shape[0]
    t = SEQ_BLOCK
    colspec = lambda cblk: pl.BlockSpec((None, t, D_MODEL), lambda b, c, cblk=cblk: (b, row_blk0 + c, cblk))
    full = lambda a: pl.BlockSpec(a.shape, lambda b, c: (0,) * a.ndim)
    bcast = lambda a: pl.BlockSpec((None,) + a.shape[1:], lambda b, c: (0,) * a.ndim)
    params = [p["conv_w"], p["conv_b"], p["dt_bias"], p["a_neg"], p["d_skip"], p["m_norm_w"], p["expand"]]
    inputs = [proj3, proj3, proj3, dt3, *params, s0, c0]
    st_shape, st_spec = _seq_state_out(stack, bsz, (M_HEADS, M_HEADDIM, M_STATE))
    cv_shape, cv_spec = _seq_state_out(stack, bsz, (M_CONV - 1, M_CONV_DIM))
    body, x_specs, x_in, aliases = _alias_prev(functools.partial(_ssd_seq_kernel, lead), len(inputs),
                                               stack and stack[1], (1, 2))
    return pl.pallas_call(
        body,
        grid=(bsz, nblk),
        in_specs=[colspec(COL_Z), colspec(COL_XS), colspec(COL_BC),
                  pl.BlockSpec((None, t, LANES), lambda b, c: (b, row_blk0 + c, 0))]
                 + [full(a) for a in params] + [bcast(s0), bcast(c0)] + x_specs,
        out_specs=[pl.BlockSpec((None, t, M_INNER), lambda b, c: (b, c, 0)), st_spec, cv_spec],
        out_shape=[jax.ShapeDtypeStruct((bsz, nblk * t, M_INNER), out_dtype), st_shape, cv_shape],
        scratch_shapes=[pltpu.VMEM((t + 8, M_CONV_DIM), F32),
                        pltpu.VMEM((M_GROUPS, M_STATE, M_GW), F32)],
        input_output_aliases=aliases,
        compiler_params=_cparams(("parallel", "arbitrary")),
        name="ssd_seq",
    )(*inputs, *x_in)


def _hgrn_gates(fz, lb):
    e = jnp.exp(-jnp.abs(fz))
    r = 1.0 / (1.0 + e)
    pos = fz >= 0.0
    sig_pos = jnp.where(pos, r, e * r)
    sig_neg = jnp.where(pos, e * r, r)
    log_sig = jnp.minimum(fz, 0.0) - jnp.log(1.0 + e)
    logf = jnp.where(lb > 0.0, jnp.log(lb + (1.0 - lb) * sig_pos), log_sig)
    k = (1.0 - lb) * sig_neg
    return logf, k


def _hgrn_seq_kernel(lead, q_ref, f_ref, i_ref, g_ref, lb_ref, nw_ref, s0_ref, y_ref, sout_ref,
                     st_ref, cum_ref, qs_ref, k_ref, dec_ref, qd_ref, kd_ref, vt_ref):
    t = SEQ_BLOCK
    nsub = t // H_SUB
    c = pl.program_id(1)

    @pl.when(c == 0)
    def _():
        for h in range(H_HEADS):
            st_ref[h] = s0_ref[h].T

    logf, k = _hgrn_gates(f_ref[...], lb_ref[...])
    if lead:
        rows = _iota((t, 1), 0)
        logf = jnp.where(rows >= lead, logf, 0.0)
        k = jnp.where(rows >= lead, k, 0.0)
    ti, si = _iota((t, t), 0), _iota((t, t), 1)
    sub_shift = int(math.log2(H_SUB))
    same = jnp.right_shift(ti, sub_shift) == jnp.right_shift(si, sub_shift)
    one_if = lambda m: jnp.where(m, 1.0, 0.0).astype(BF16)
    pieces = _split3(logf)
    m_cum, m_tot = one_if(same & (si <= ti)), one_if(same)
    cum = sum(_dot(m_cum, x) for x in pieces)
    tot = sum(_dot(m_tot, x) for x in pieces)
    q = q_ref[...] * (H_KDIM ** -0.5)
    cum_ref[...] = cum
    qs_ref[...] = q
    k_ref[...] = k
    dec_ref[...] = jnp.exp(tot)
    qd_ref[...] = (q * jnp.exp(cum)).astype(BF16)
    kd_ref[...] = (k * jnp.exp(tot - cum)).astype(BF16)
    for h in range(H_HEADS):
        vt_ref[h] = i_ref[:, h * H_VDIM:(h + 1) * H_VDIM].T.astype(BF16)

    sub_rows = _iota((H_SUB, 1), 0)
    lane_t = _iota((1, t), 1)

    def body(j, carry):
        r0 = pl.multiple_of(j * H_SUB, H_SUB)
        rs = pl.ds(r0, H_SUB)
        in_sub = (lane_t >= r0) & (lane_t < r0 + H_SUB)
        for h in range(H_HEADS):
            cs = slice(h * H_KDIM, (h + 1) * H_KDIM)
            st = st_ref[h]
            o = _dot_nt(qd_ref[rs, cs], st.astype(BF16))
            cq, qh, kh, vh = cum_ref[rs, cs], qs_ref[rs, cs], k_ref[rs, cs], i_ref[rs, cs]
            halves = [o[r:r + 8] for r in range(0, H_SUB, 8)]
            for s in range(H_SUB):
                for gi in range(s // 8, H_SUB // 8):
                    r = slice(8 * gi, 8 * gi + 8)
                    ex = cq[r] - cq[s:s + 1, :]
                    if 8 * gi <= s:
                        ex = jnp.where(sub_rows[r] >= s, ex, -1e30)
                    w = jnp.exp(ex) * qh[r] * kh[s:s + 1, :]
                    halves[gi] = halves[gi] + jnp.sum(w, axis=-1, keepdims=True) * vh[s:s + 1, :]
            o = jnp.concatenate(halves, axis=0)
            y = _rms(o) * nw_ref[:, cs] * _sigmoid(g_ref[rs, cs])
            y_ref[rs, cs] = y.astype(y_ref.dtype)
            vt = jnp.where(in_sub, vt_ref[h], jnp.zeros((), BF16))
            st_ref[h] = st * dec_ref[pl.ds(r0, 1), cs] + _dot(vt, kd_ref[:, cs])
        return carry

    lax.fori_loop(0, nsub, body, 0)

    @pl.when(c == pl.num_programs(1) - 1)
    def _():
        for h in range(H_HEADS):
            sout_ref[h] = st_ref[h].T


def _hgrn_seq(proj3, row_blk0, nblk, lead, lb, nw, s0, out_dtype, stack=None):
    bsz = proj3.shape[0]
    t = SEQ_BLOCK
    colspec = lambda cblk: pl.BlockSpec((None, t, D_MODEL), lambda b, c, cblk=cblk: (b, row_blk0 + c, cblk))
    full = lambda a: pl.BlockSpec(a.shape, lambda b, c: (0,) * a.ndim)
    bcast = lambda a: pl.BlockSpec((None,) + a.shape[1:], lambda b, c: (0,) * a.ndim)
    inputs = [proj3, proj3, proj3, proj3, lb, nw, s0]
    st_shape, st_spec = _seq_state_out(stack, bsz, (H_HEADS, H_KDIM, H_VDIM))
    body, x_specs, x_in, aliases = _alias_prev(functools.partial(_hgrn_seq_kernel, lead), len(inputs),
                                               stack and stack[1], (1,))
    return pl.pallas_call(
        body,
        grid=(bsz, nblk),
        in_specs=[colspec(COL_HQ), colspec(COL_HF), colspec(COL_HI), colspec(COL_HG),
                  full(lb), full(nw), bcast(s0)] + x_specs,
        out_specs=[pl.BlockSpec((None, t, D_MODEL), lambda b, c: (b, c, 0)), st_spec],
        out_shape=[jax.ShapeDtypeStruct((bsz, nblk * t, D_MODEL), out_dtype), st_shape],
        input_output_aliases=aliases,
        scratch_shapes=[pltpu.VMEM((H_HEADS, H_VDIM, H_KDIM), F32),
                        pltpu.VMEM((t, D_MODEL), F32), pltpu.VMEM((t, D_MODEL), F32),
                        pltpu.VMEM((t, D_MODEL), F32), pltpu.VMEM((t, D_MODEL), F32),
                        pltpu.VMEM((t, D_MODEL), BF16), pltpu.VMEM((t, D_MODEL), BF16),
                        pltpu.VMEM((H_HEADS, H_VDIM, t), BF16)],
        compiler_params=_cparams(("parallel", "arbitrary")),
        name="hgrn_seq",
    )(*inputs, *x_in)


def _log_gamma(h):
    return math.log(1.0 - 2.0 ** (-5.0 - h))


def _rotary(x, cos, sin):
    x1, x2 = x[:, :R_HALF], x[:, R_HALF:]
    return jnp.concatenate([x1 * cos - x2 * sin, x2 * cos + x1 * sin], axis=1)


def _ret_seq_kernel(lead, t, q_ref, k_ref, v_ref, g_ref, cos_ref, sin_ref, s0_ref, y_ref, sout_ref, st_ref):
    c = pl.program_id(1)

    @pl.when(c == 0)
    def _():
        st_ref[...] = s0_ref[...]

    cos, sin = cos_ref[...], sin_ref[...]
    ti, si = _iota((t, t), 0), _iota((t, t), 1)
    tril = si <= ti
    dpos = (ti - si).astype(F32)
    tcol = _iota((t, 1), 0).astype(F32)
    for h in range(R_HEADS):
        lg = _log_gamma(h)
        ks = slice(h * R_KDIM, (h + 1) * R_KDIM)
        vs = slice(h * R_VDIM, (h + 1) * R_VDIM)
        qh = _rotary(q_ref[:, ks], cos, sin)
        kh = _rotary(k_ref[:, ks], cos, sin) * (R_KDIM ** -0.5)
        if lead:
            kh = jnp.where(_iota((t, 1), 0) >= lead, kh, 0.0)
        qb, kb, vb = qh.astype(BF16), kh.astype(BF16), v_ref[:, vs].astype(BF16)
        decay = jnp.where(tril, jnp.exp(jnp.where(tril, dpos, 0.0) * lg), 0.0)
        scores = _dot_nt(qb, kb) * decay
        st = st_ref[h]
        o = _dot(scores.astype(BF16), vb) + _dot(qb, st.astype(BF16)) * jnp.exp((tcol + 1.0) * lg)
        kdec = (kh * jnp.exp((t - 1.0 - tcol) * lg)).T.astype(BF16)
        st_ref[h] = st * math.exp(t * lg) + _dot(kdec, vb)
        y_ref[:, vs] = (_rms(o) * _silu(g_ref[:, vs])).astype(y_ref.dtype)

    @pl.when(c == pl.num_programs(1) - 1)
    def _():
        sout_ref[...] = st_ref[...]


def _ret_seq(proj3, t, row_blk0, nblk, lead, cos, sin, s0, out_dtype, stack=None):
    bsz = proj3.shape[0]
    wide = R_HEADS * R_VDIM
    bcast = lambda a: pl.BlockSpec((None,) + a.shape[1:], lambda b, c: (0,) * a.ndim)
    inputs = [proj3, proj3, proj3, proj3, cos, sin, s0]
    st_shape, st_spec = _seq_state_out(stack, bsz, (R_HEADS, R_KDIM, R_VDIM))
    body, x_specs, x_in, aliases = _alias_prev(functools.partial(_ret_seq_kernel, lead, t), len(inputs),
                                               stack and stack[1], (1,))
    return pl.pallas_call(
        body,
        grid=(bsz, nblk),
        in_specs=[pl.BlockSpec((None, t, D_MODEL), lambda b, c: (b, row_blk0 + c, COL_RQ)),
                  pl.BlockSpec((None, t, D_MODEL), lambda b, c: (b, row_blk0 + c, COL_RK)),
                  pl.BlockSpec((None, t, wide), lambda b, c: (b, row_blk0 + c, COL_RV)),
                  pl.BlockSpec((None, t, wide), lambda b, c: (b, row_blk0 + c, COL_RG)),
                  pl.BlockSpec((t, R_HALF), lambda b, c: (c, 0)),
                  pl.BlockSpec((t, R_HALF), lambda b, c: (c, 0)),
                  bcast(s0)] + x_specs,
        out_specs=[pl.BlockSpec((None, t, wide), lambda b, c: (b, c, 0)), st_spec],
        out_shape=[jax.ShapeDtypeStruct((bsz, nblk * t, wide), out_dtype), st_shape],
        scratch_shapes=[pltpu.VMEM((R_HEADS, R_KDIM, R_VDIM), F32)],
        input_output_aliases=aliases,
        compiler_params=_cparams(("parallel", "arbitrary")),
        name="ret_seq",
    )(*inputs, *x_in)


def _step_prep_kernel(xs_ref, bc_ref, dt_ref, hq_ref, hf_ref, rq_ref, rk_ref, conv_ref,
                      cw_ref, cb_ref, dtb_ref, a_ref, ex_ref, lb_ref, cos_ref, sin_ref,
                      xs_o, bc_o, xdt_o, edec_o, conv_o, hq_o, hef_o, hk_o, rq_o, rk_o):
    raw = jnp.concatenate([xs_ref[...], bc_ref[...]], axis=1)
    buf = conv_ref[...]
    acc = cb_ref[...] + cw_ref[3:4, :] * raw
    for k in range(M_CONV - 1):
        acc += cw_ref[k:k + 1, :] * buf[:, k * M_CONV_DIM:(k + 1) * M_CONV_DIM]
    conv_o[:, 0:2 * M_CONV_DIM] = buf[:, M_CONV_DIM:]
    conv_o[:, 2 * M_CONV_DIM:] = raw
    xbc = _silu(acc)
    xs = xbc[:, :M_INNER]
    xs_o[...] = xs
    bc_o[...] = xbc[:, M_INNER:]
    dt = _softplus(dt_ref[...] + dtb_ref[...])
    ex = ex_ref[...]
    xdt_o[...] = xs * _sel_right(dt, ex)
    edec_o[...] = _sel_right(jnp.exp(dt * a_ref[...]), ex)
    logf, k = _hgrn_gates(hf_ref[...], lb_ref[...])
    hq_o[...] = hq_ref[...] * (H_KDIM ** -0.5)
    hef_o[...] = jnp.exp(logf)
    hk_o[...] = k
    cos, sin = cos_ref[...], sin_ref[...]
    for h in range(R_HEADS):
        ks = slice(h * R_KDIM, (h + 1) * R_KDIM)
        rq_o[:, ks] = _rotary(rq_ref[:, ks], cos, sin)
        rk_o[:, ks] = _rotary(rk_ref[:, ks], cos, sin) * (R_KDIM ** -0.5)


def _step_prep(proj_s, dt_s, conv_flat, p, cos, sin):
    nb = conv_flat.shape[0]
    col = lambda cblk: pl.BlockSpec((nb, D_MODEL), lambda i, cblk=cblk: (0, cblk))
    full = lambda a: pl.BlockSpec(a.shape, lambda i: (0,) * a.ndim)
    params = [p["conv_w"], p["conv_b"], p["dt_bias"], p["a_neg"], p["expand"], p["lb"], cos, sin]
    o = lambda w: jax.ShapeDtypeStruct((nb, w), F32)
    widths = [M_INNER, M_BC, M_INNER, M_INNER, (M_CONV - 1) * M_CONV_DIM, D_MODEL, D_MODEL, D_MODEL, D_MODEL, D_MODEL]
    return pl.pallas_call(
        _step_prep_kernel,
        grid=(1,),
        in_specs=[col(COL_XS), col(COL_BC), pl.BlockSpec((nb, LANES), lambda i: (0, 0)),
                  col(COL_HQ), col(COL_HF), col(COL_RQ), col(COL_RK), full(conv_flat)]
                 + [full(a) for a in params],
        out_specs=[pl.BlockSpec((nb, w), lambda i: (0, 0)) for w in widths],
        out_shape=[o(w) for w in widths],
        compiler_params=_cparams(("arbitrary",)),
        name="step_prep",
    )(proj_s, proj_s, dt_s, proj_s, proj_s, proj_s, proj_s, conv_flat, *params)


def _cols(x):
    nb, w = x.shape
    return jnp.transpose(x.reshape(nb // STEP_BT, STEP_BT, w), (0, 2, 1))


def _ssd_step_kernel(s_ref, xdt_ref, edec_ref, bc_ref, xs_ref, z_ref, dsk_ref, nw_ref, so_ref, y_ref, yt_ref):
    yt_ref[...] = jnp.zeros(yt_ref.shape, F32)
    for i in range(STEP_BT):
        for g in range(M_GROUPS):
            gs = slice(g * M_GW, (g + 1) * M_GW)
            hs = slice(M_HPG * g, M_HPG * (g + 1))
            st = s_ref[i, hs].reshape(M_GW, M_STATE)
            brow = bc_ref[i:i + 1, g * M_STATE:(g + 1) * M_STATE]
            crow = bc_ref[i:i + 1, M_BC // 2 + g * M_STATE:M_BC // 2 + (g + 1) * M_STATE]
            new = st * edec_ref[gs, i:i + 1] + xdt_ref[gs, i:i + 1] * brow
            so_ref[i, hs] = new.reshape(M_HPG, M_HEADDIM, M_STATE)
            yt_ref[gs, i:i + 1] = jnp.sum(new * crow, axis=-1, keepdims=True)
    y = yt_ref[...].T[0:STEP_BT, :]
    xs = xs_ref[...]
    y = (y + dsk_ref[...] * xs) * _silu(z_ref[...])
    for g in range(M_GROUPS):
        gs = slice(g * M_GW, (g + 1) * M_GW)
        y_ref[:, gs] = _rms(y[:, gs]) * nw_ref[:, gs]


def _ssd_step(state, layer, prev, xdt_c, edec_c, bc, xs, proj_s, p):
    nb = xs.shape[0]
    bt = STEP_BT
    full = lambda a: pl.BlockSpec(a.shape, lambda j: (0,) * a.ndim)
    sspec = pl.BlockSpec((None, bt, M_HEADS, M_HEADDIM, M_STATE), lambda j: (layer, j, 0, 0, 0))
    inputs = [state, xdt_c, edec_c, bc, xs, proj_s, p["d_skip"], p["m_norm_w"]]
    body, x_specs, x_in, aliases = _alias_prev(_ssd_step_kernel, len(inputs),
                                               None if prev is None else [prev], (0,))
    return pl.pallas_call(
        body,
        grid=(nb // bt,),
        in_specs=[sspec,
                  pl.BlockSpec((None, M_INNER, bt), lambda j: (j, 0, 0)),
                  pl.BlockSpec((None, M_INNER, bt), lambda j: (j, 0, 0)),
                  pl.BlockSpec((bt, M_BC), lambda j: (j, 0)),
                  pl.BlockSpec((bt, M_INNER), lambda j: (j, 0)),
                  pl.BlockSpec((bt, D_MODEL), lambda j: (j, COL_Z)),
                  full(p["d_skip"]), full(p["m_norm_w"])] + x_specs,
        out_specs=[sspec, pl.BlockSpec((bt, M_INNER), lambda j: (j, 0))],
        out_shape=[jax.ShapeDtypeStruct(state.shape, F32),
                   jax.ShapeDtypeStruct((nb, M_INNER), F32)],
        scratch_shapes=[pltpu.VMEM((M_INNER, LANES), F32)],
        input_output_aliases=aliases,
        compiler_params=_cparams(("parallel",)),
        name="ssd_step",
    )(*inputs, *x_in)


def _hgrn_step_kernel(s_ref, q_ref, ef_ref, k_ref, v_ref, g_ref, nw_ref, so_ref, y_ref):
    for i in range(STEP_BT):
        for h in range(H_HEADS):
            cs = slice(h * H_KDIM, (h + 1) * H_KDIM)
            new = s_ref[i, h] * ef_ref[cs, i:i + 1] + k_ref[cs, i:i + 1] * v_ref[i:i + 1, cs]
            so_ref[i, h] = new
            y_ref[i:i + 1, cs] = jnp.sum(new * q_ref[cs, i:i + 1], axis=0, keepdims=True)
    for h in range(H_HEADS):
        cs = slice(h * H_KDIM, (h + 1) * H_KDIM)
        y_ref[:, cs] = _rms(y_ref[:, cs]) * nw_ref[:, cs] * _sigmoid(g_ref[:, cs])


def _hgrn_step(state, layer, prev, q_c, ef_c, k_c, proj_s, nw):
    nb = state.shape[1]
    bt = STEP_BT
    cspec = pl.BlockSpec((None, D_MODEL, bt), lambda j: (j, 0, 0))
    sspec = pl.BlockSpec((None, bt, H_HEADS, H_KDIM, H_VDIM), lambda j: (layer, j, 0, 0, 0))
    inputs = [state, q_c, ef_c, k_c, proj_s, proj_s, nw]
    body, x_specs, x_in, aliases = _alias_prev(_hgrn_step_kernel, len(inputs),
                                               None if prev is None else [prev], (0,))
    return pl.pallas_call(
        body,
        grid=(nb // bt,),
        in_specs=[sspec, cspec, cspec, cspec,
                  pl.BlockSpec((bt, D_MODEL), lambda j: (j, COL_HI)),
                  pl.BlockSpec((bt, D_MODEL), lambda j: (j, COL_HG)),
                  pl.BlockSpec(nw.shape, lambda j: (0, 0))] + x_specs,
        out_specs=[sspec, pl.BlockSpec((bt, D_MODEL), lambda j: (j, 0))],
        out_shape=[jax.ShapeDtypeStruct(state.shape, F32),
                   jax.ShapeDtypeStruct((nb, D_MODEL), F32)],
        input_output_aliases=aliases,
        compiler_params=_cparams(("parallel",)),
        name="hgrn_step",
    )(*inputs, *x_in)


def _ret_step_kernel(s_ref, q_ref, k_ref, v_ref, g_ref, gam_ref, so_ref, y_ref):
    gam = gam_ref[...]
    for i in range(STEP_BT):
        new = s_ref[i] * gam + k_ref[:, i:i + 1] * v_ref[i:i + 1, :]
        so_ref[i] = new
        y_ref[i:i + 1, :] = jnp.sum(new * q_ref[:, i:i + 1], axis=0, keepdims=True)
    y_ref[...] = _rms(y_ref[...]) * _silu(g_ref[...])


def _ret_step(state, layer, prev, q_c, k_c, proj_s, gam):
    nb = state.shape[1]
    bt = STEP_BT
    cspec = pl.BlockSpec((None, R_KDIM, bt), lambda j, h: (j, h, 0))
    sspec = pl.BlockSpec((None, bt, None, R_KDIM, R_VDIM), lambda j, h: (layer, j, h, 0, 0))
    inputs = [state, q_c, k_c, proj_s, proj_s, gam]
    body, x_specs, x_in, aliases = _alias_prev(_ret_step_kernel, len(inputs),
                                               None if prev is None else [prev], (0,))
    return pl.pallas_call(
        body,
        grid=(nb // bt, R_HEADS),
        in_specs=[sspec, cspec, cspec,
                  pl.BlockSpec((bt, R_VDIM), lambda j, h: (j, h)),
                  pl.BlockSpec((bt, R_VDIM), lambda j, h: (j, R_HEADS + h)),
                  pl.BlockSpec((None, 1, R_VDIM), lambda j, h: (h, 0, 0))] + x_specs,
        out_specs=[sspec, pl.BlockSpec((bt, R_VDIM), lambda j, h: (j, h))],
        out_shape=[jax.ShapeDtypeStruct(state.shape, F32),
                   jax.ShapeDtypeStruct((nb, R_HEADS * R_VDIM), F32)],
        input_output_aliases=aliases,
        compiler_params=_cparams(("parallel", "parallel")),
        name="ret_step",
    )(*inputs, *x_in)


def _rope_tables(positions):
    inv_freq = 1.0 / (ROPE_BASE ** jnp.linspace(0.0, 1.0, R_HALF, dtype=F32))
    ang = positions[:, None] * inv_freq[None, :]
    return jnp.cos(ang), jnp.sin(ang)


def _per_channel(v):
    return jnp.repeat(v.astype(F32), M_HEADDIM).reshape(1, M_INNER)


def _pad_lanes(v):
    return jnp.pad(v.astype(F32), (0, LANES - v.shape[0])).reshape(1, LANES)


def kernel(x_prompt, x_sample, state_ssm, state_conv, state_hgrn, state_ret, meta_tokens, ln_in_g, ln_in_b,
           w_in, conv_w, conv_b, dt_bias, a_log, d_skip, m_norm_w, hgrn_lb_logits, h_norm_w, w_br_m, w_br_h,
           w_br_r, w_out, ln1_g, ln1_b, w_ffn_in, w_ffn_out, ln2_g, ln2_b):
    bp, sp = x_prompt.shape[0], x_prompt.shape[1]
    nb = x_sample.shape[0]
    assert x_sample.shape[1] == 1 and nb == SMALL_ROWS - SEQ_BLOCK and nb % STEP_BT == 0
    assert sp % RET_BLOCK == 0 and meta_tokens.shape[0] == N_META

    w_main, w_dt = _wprep(w_in, 128)
    wm_b, wh_b, wr_b, wo_b = (w.astype(BF16) for w in (w_br_m, w_br_h, w_br_r, w_out))
    wfi_b, wfo_b = w_ffn_in.astype(BF16), w_ffn_out.astype(BF16)
    ln1 = (ln1_g.reshape(DEPTH, 1, D_MODEL), ln1_b.reshape(DEPTH, 1, D_MODEL))
    ln2 = (ln2_g.reshape(DEPTH, 1, D_MODEL), ln2_b.reshape(DEPTH, 1, D_MODEL))
    lb_cum = jnp.cumsum(jax.nn.softmax(hgrn_lb_logits.astype(F32), axis=0), axis=0)
    lbs = lb_cum - lb_cum[0]
    expand = (np.arange(LANES)[:, None] == (np.arange(M_INNER)[None, :] // M_HEADDIM)).astype(np.float32)
    expand = jnp.asarray(expand, BF16)
    gam = jnp.asarray(np.broadcast_to(
        np.array([1.0 - 2.0 ** (-5.0 - h) for h in range(R_HEADS)], np.float32)[:, None, None],
        (R_HEADS, 1, R_VDIM)))

    pos_real = jnp.arange(N_META, N_META + sp, dtype=F32)
    pos_meta = jnp.maximum(jnp.arange(SEQ_BLOCK, dtype=F32) - META_LEAD, 0.0)
    pos_samp = jnp.full((nb,), float(PAST_LEN), F32)
    cos_r, sin_r = _rope_tables(pos_real)
    cos_m, sin_m = _rope_tables(pos_meta)
    cos_s, sin_s = _rope_tables(pos_samp)

    x_real = _layer_norm_rows(x_prompt.reshape(bp * sp, D_MODEL), ln_in_g, ln_in_b, 512)
    small_in = jnp.concatenate([x_sample.reshape(nb, D_MODEL),
                                jnp.zeros((META_LEAD, D_MODEL), F32), meta_tokens.astype(F32)], axis=0)
    x_small = _layer_norm_rows(small_in, ln_in_g, ln_in_b, SMALL_ROWS)

    z_ssm = jnp.zeros((1, M_HEADS, M_HEADDIM, M_STATE), F32)
    z_conv = jnp.zeros((1, 8, M_CONV_DIM), F32)
    z_hgrn = jnp.zeros((1, H_HEADS, H_KDIM, H_VDIM), F32)
    z_ret = jnp.zeros((1, R_HEADS, R_KDIM, R_VDIM), F32)
    ssm_p = conv_p = hgrn_p = ret_p = ssm_s = hgrn_s = ret_s = None
    conv_s = []
    for l in range(DEPTH):
        p = dict(conv_w=conv_w[l], conv_b=conv_b[l].reshape(1, -1), dt_bias=_pad_lanes(dt_bias[l]),
                 a_neg=_pad_lanes(-jnp.exp(a_log[l].astype(F32))), d_skip=_per_channel(d_skip[l]),
                 m_norm_w=m_norm_w[l].reshape(1, -1), expand=expand, lb=lbs[l].reshape(1, -1))
        hnw = h_norm_w[l].reshape(1, -1)

        proj_r = _proj(x_real, w_main, l, 1024, 1024, "proj_real")
        dt_r = _proj(x_real, w_dt, l, 1024, LANES, "proj_dt_real")
        proj_s = _proj(x_small, w_main, l, SMALL_ROWS, 2048, "proj_small")
        dt_s = _proj(x_small, w_dt, l, SMALL_ROWS, LANES, "proj_dt_small")
        proj_r3 = proj_r.reshape(bp, sp, PROJ_COLS)
        dt_r3 = dt_r.reshape(bp, sp, LANES)
        proj_s3 = proj_s.reshape(1, SMALL_ROWS, PROJ_COLS)
        dt_s3 = dt_s.reshape(1, SMALL_ROWS, LANES)

        ym_m, ssm_m, conv_m = _ssd_seq(proj_s3, dt_s3, 1, 1, META_LEAD, p, z_ssm, z_conv, F32)
        yh_m, hgrn_m = _hgrn_seq(proj_s3, 1, 1, META_LEAD, p["lb"], hnw, z_hgrn, F32)
        yr_m, ret_m = _ret_seq(proj_s3, SEQ_BLOCK, 1, 1, META_LEAD, cos_m, sin_m, z_ret, F32)

        conv0 = jnp.pad(conv_m, ((0, 0), (8 - (M_CONV - 1), 0), (0, 0)))
        ym_r, ssm_p, conv_p = _ssd_seq(proj_r3, dt_r3, 0, sp // SEQ_BLOCK, 0, p, ssm_m, conv0, BF16,
                                       stack=(l, None if l == 0 else [ssm_p, conv_p]))
        yh_r, hgrn_p = _hgrn_seq(proj_r3, 0, sp // SEQ_BLOCK, 0, p["lb"], hnw, hgrn_m, BF16,
                                 stack=(l, None if l == 0 else [hgrn_p]))
        yr_r, ret_p = _ret_seq(proj_r3, RET_BLOCK, 0, sp // RET_BLOCK, 0, cos_r, sin_r, ret_m, BF16,
                               stack=(l, None if l == 0 else [ret_p]))

        conv_flat = state_conv[l].reshape(nb, (M_CONV - 1) * M_CONV_DIM)
        (xs_s, bc_s, xdt_s, edec_s, conv_new, hq_s, hef_s, hk_s, rq_s, rk_s) = _step_prep(
            proj_s, dt_s, conv_flat, p, cos_s, sin_s)
        ssm_s, ym_s = _ssd_step(state_ssm, l, ssm_s, _cols(xdt_s), _cols(edec_s), bc_s, xs_s, proj_s, p)
        hgrn_s, yh_s = _hgrn_step(state_hgrn, l, hgrn_s, _cols(hq_s), _cols(hef_s), _cols(hk_s), proj_s, hnw)
        ret_s, yr_s = _ret_step(state_ret, l, ret_s, _cols(rq_s), _cols(rk_s), proj_s, gam)
        conv_s.append(conv_new.reshape(nb, M_CONV - 1, M_CONV_DIM))

        ym_small = jnp.concatenate([ym_s, ym_m[0]], axis=0)
        yh_small = jnp.concatenate([yh_s, yh_m[0]], axis=0)
        yr_small = jnp.concatenate([yr_s, yr_m[0]], axis=0)
        x_real = _mix(x_real, ym_r.reshape(bp * sp, -1), yh_r.reshape(bp * sp, -1), yr_r.reshape(bp * sp, -1),
                      proj_r, l, wm_b, wh_b, wr_b, wo_b, *ln1, 256)
        x_small = _mix(x_small, ym_small, yh_small, yr_small, proj_s, l, wm_b, wh_b, wr_b, wo_b, *ln1, SMALL_ROWS)
        x_real = _ffn(x_real, l, wfi_b, wfo_b, *ln2, 512)
        x_small = _ffn(x_small, l, wfi_b, wfo_b, *ln2, SMALL_ROWS)

    return (x_real.reshape(bp, sp, D_MODEL), x_small[:nb].reshape(nb, 1, D_MODEL),
            ssm_p, conv_p, hgrn_p, ret_p, ssm_s, jnp.stack(conv_s), hgrn_s, ret_s)
```

```python
import functools
import math

import numpy as np
import jax
import jax.numpy as jnp
from jax import lax
from jax.experimental import pallas as pl
from jax.experimental.pallas import tpu as pltpu

F32 = jnp.float32
BF16 = jnp.bfloat16

D_MODEL = 1024
DEPTH = 2
N_META = 16
M_INNER = D_MODEL
M_HEADDIM = 64
M_HEADS = M_INNER // M_HEADDIM
M_GROUPS = 4
M_HPG = M_HEADS // M_GROUPS
M_STATE = 128
M_CONV = 4
M_BC = 2 * M_GROUPS * M_STATE
M_CONV_DIM = M_INNER + M_BC
M_GW = M_INNER // M_GROUPS
H_KDIM = 128
H_HEADS = D_MODEL // H_KDIM
H_VDIM = 128
R_HEADS = 4
R_KDIM = D_MODEL // R_HEADS
R_VDIM = 2 * R_KDIM
R_HALF = R_KDIM // 2
ROPE_BASE = 10000.0
D_FF = ((8 * D_MODEL // 3 + 255) // 256) * 256
FF_CHUNK = 256
DN_ALPHA = (2 * DEPTH) ** 0.25
PAST_LEN = 16384

LANES = 128
SEQ_BLOCK = 128
RET_BLOCK = 256
SMALL_ROWS = 256
META_LEAD = SEQ_BLOCK - N_META
STEP_BT = 8
VMEM_LIMIT = 56 * 1024 * 1024

COL_RV, COL_RG = 0, 1
COL_Z, COL_XS, COL_BC, COL_HQ, COL_HF, COL_HI, COL_HG, COL_RQ, COL_RK, COL_GM, COL_GH, COL_GR = range(4, 16)
PROJ_COLS = 16 * 1024

_ORIG_SPLITS = (M_INNER, M_CONV_DIM, M_HEADS, 1024, 1024, 1024, 1024, 1024, 1024, 2048, 2048, 3072)
_ORIG_OFF = np.concatenate([[0], np.cumsum(_ORIG_SPLITS)]).tolist()


def _cparams(sem):
    return pltpu.CompilerParams(dimension_semantics=sem, vmem_limit_bytes=VMEM_LIMIT)


def _sigmoid(x):
    return 1.0 / (1.0 + jnp.exp(-x))


def _silu(x):
    return x * _sigmoid(x)


def _softplus(x):
    return jnp.maximum(x, 0.0) + jnp.log1p(jnp.exp(-jnp.abs(x)))


def _layer_norm(x, g, b):
    mu = jnp.mean(x, axis=-1, keepdims=True)
    xc = x - mu
    var = jnp.mean(xc * xc, axis=-1, keepdims=True)
    return xc * lax.rsqrt(var + 1e-5) * g + b


def _rms(x):
    return x * lax.rsqrt(jnp.mean(x * x, axis=-1, keepdims=True) + 1e-6)


def _split3(x):
    hi = x.astype(BF16)
    r = x - hi.astype(F32)
    mid = r.astype(BF16)
    lo = (r - mid.astype(F32)).astype(BF16)
    return hi, mid, lo


def _dot(a, b):
    return jnp.dot(a, b, preferred_element_type=F32)


def _dot_nt(a, b):
    return lax.dot_general(a, b, (((1,), (1,)), ((), ())), preferred_element_type=F32)


def _sel_right(x, m01):
    hi, mid, lo = _split3(x)
    return _dot(hi, m01) + _dot(mid, m01) + _dot(lo, m01)


def _sel_left(m01, x):
    hi, mid, lo = _split3(x)
    return _dot(m01, hi) + _dot(m01, mid) + _dot(m01, lo)


def _iota(shape, dim):
    return lax.broadcasted_iota(jnp.int32, shape, dim)


def _ln_kernel(x_ref, g_ref, b_ref, o_ref):
    o_ref[...] = _layer_norm(x_ref[...], g_ref[...], b_ref[...])


def _layer_norm_rows(x, g, b, tm):
    m = x.shape[0]
    tm = min(tm, m)
    return pl.pallas_call(
        _ln_kernel,
        grid=(m // tm,),
        in_specs=[pl.BlockSpec((tm, D_MODEL), lambda i: (i, 0)),
                  pl.BlockSpec((1, D_MODEL), lambda i: (0, 0)),
                  pl.BlockSpec((1, D_MODEL), lambda i: (0, 0))],
        out_specs=pl.BlockSpec((tm, D_MODEL), lambda i: (i, 0)),
        out_shape=jax.ShapeDtypeStruct((m, D_MODEL), F32),
        compiler_params=_cparams(("parallel",)),
        name="ln_in",
    )(x, g.reshape(1, -1), b.reshape(1, -1))


def _proj_kernel(x_ref, w_ref, o_ref, xb_ref):
    @pl.when(pl.program_id(1) == 0)
    def _():
        xb_ref[...] = x_ref[...].astype(BF16)

    w = w_ref[0] if len(w_ref.shape) == 3 else w_ref[...]
    o_ref[...] = _dot_nt(xb_ref[...], w)


def _layer_block(a, layer):
    return pl.BlockSpec((None,) + a.shape[1:], lambda *_: (layer,) + (0,) * (a.ndim - 1))


_W_RUNS = ((0, _ORIG_OFF[9]), (4096, _ORIG_OFF[0]), (7168, _ORIG_OFF[3]), (13312, _ORIG_OFF[11]))
PROJ_TN = 1024


def _orig_col(j):
    c = j * PROJ_TN
    off = c - _W_RUNS[0][0] + _W_RUNS[0][1]
    for new0, orig0 in _W_RUNS[1:]:
        off = jnp.where(c >= new0, c - new0 + orig0, off)
    return pl.multiple_of(off, M_HEADS)


def _proj(x, wt, layer, tm, name):
    m = x.shape[0]
    tm = min(tm, m)
    return pl.pallas_call(
        _proj_kernel,
        grid=(m // tm, PROJ_COLS // PROJ_TN),
        in_specs=[pl.BlockSpec((tm, D_MODEL), lambda i, j: (i, 0)),
                  pl.BlockSpec((pl.Element(1), pl.Element(PROJ_TN), pl.Element(D_MODEL)),
                               lambda i, j: (layer, _orig_col(j), 0))],
        out_specs=pl.BlockSpec((tm, PROJ_TN), lambda i, j: (i, j)),
        out_shape=jax.ShapeDtypeStruct((m, PROJ_COLS), F32),
        scratch_shapes=[pltpu.VMEM((tm, D_MODEL), BF16)],
        compiler_params=_cparams(("parallel", "arbitrary")),
        name=name,
    )(x, wt)


def _proj_dt(x, wt, layer, tm, name):
    m = x.shape[0]
    tm = min(tm, m)
    assert _ORIG_OFF[2] % LANES == 0
    return pl.pallas_call(
        _proj_kernel,
        grid=(m // tm, 1),
        in_specs=[pl.BlockSpec((tm, D_MODEL), lambda i, j: (i, 0)),
                  pl.BlockSpec((None, LANES, D_MODEL), lambda i, j: (layer, _ORIG_OFF[2] // LANES, 0))],
        out_specs=pl.BlockSpec((tm, LANES), lambda i, j: (i, 0)),
        out_shape=jax.ShapeDtypeStruct((m, LANES), F32),
        scratch_shapes=[pltpu.VMEM((tm, D_MODEL), BF16)],
        compiler_params=_cparams(("parallel", "arbitrary")),
        name=name,
    )(x, wt)


def _mix_kernel(x_ref, ym_ref, yh_ref, yr_ref, gm_ref, gh_ref, gr_ref,
                wm_ref, wh_ref, wr_ref, wo_ref, g_ref, b_ref, o_ref):
    mixed = _sigmoid(gm_ref[...]) * _dot(ym_ref[...].astype(BF16), wm_ref[...])
    mixed += _sigmoid(gh_ref[...]) * _dot(yh_ref[...].astype(BF16), wh_ref[...])
    mixed += _sigmoid(gr_ref[...]) * _dot(yr_ref[...].astype(BF16), wr_ref[...])
    h = _dot(mixed.astype(BF16), wo_ref[...])
    o_ref[...] = _layer_norm(DN_ALPHA * x_ref[...] + h, g_ref[...], b_ref[...])


def _mix(x, ym, yh, yr, proj, layer, wm, wh, wr, wo, g2, b2, tm):
    m = x.shape[0]
    tm = min(tm, m)
    row = lambda w: pl.BlockSpec((tm, w), lambda i: (i, 0))
    col = lambda c: pl.BlockSpec((tm, D_MODEL), lambda i, c=c: (i, c))
    full = lambda a: _layer_block(a, layer)
    return pl.pallas_call(
        _mix_kernel,
        grid=(m // tm,),
        in_specs=[row(D_MODEL), row(M_INNER), row(D_MODEL), row(R_HEADS * R_VDIM),
                  col(COL_GM), col(COL_GH), col(COL_GR),
                  full(wm), full(wh), full(wr), full(wo), full(g2), full(b2)],
        out_specs=row(D_MODEL),
        out_shape=jax.ShapeDtypeStruct((m, D_MODEL), F32),
        compiler_params=_cparams(("parallel",)),
        name="mix",
    )(x, ym, yh, yr, proj, proj, proj, wm, wh, wr, wo, g2, b2)


def _ffn_kernel(x_ref, wi_ref, wo_ref, g_ref, b_ref, o_ref):
    x = x_ref[...]
    xb = x.astype(BF16)
    acc = jnp.zeros(x.shape, F32)
    for j in range(0, D_FF, FF_CHUNK):
        hg = _dot(xb, wi_ref[:, j:j + FF_CHUNK])
        hu = _dot(xb, wi_ref[:, D_FF + j:D_FF + j + FF_CHUNK])
        acc += _dot((_silu(hg) * hu).astype(BF16), wo_ref[j:j + FF_CHUNK, :])
    o_ref[...] = _layer_norm(DN_ALPHA * x + acc, g_ref[...], b_ref[...])


def _ffn(x, layer, wi, wo, g2, b2, tm):
    m = x.shape[0]
    tm = min(tm, m)
    full = lambda a: _layer_block(a, layer)
    return pl.pallas_call(
        _ffn_kernel,
        grid=(m // tm,),
        in_specs=[pl.BlockSpec((tm, D_MODEL), lambda i: (i, 0)), full(wi), full(wo), full(g2), full(b2)],
        out_specs=pl.BlockSpec((tm, D_MODEL), lambda i: (i, 0)),
        out_shape=jax.ShapeDtypeStruct((m, D_MODEL), F32),
        compiler_params=_cparams(("parallel",)),
        name="ffn",
    )(x, wi, wo, g2, b2)


def _ssd_seq_kernel(lead, z_ref, xs_ref, bc_ref, dt_ref, cw_ref, cb_ref, dtb_ref, a_ref, dsk_ref, nw_ref,
                    ex_ref, s0_ref, c0_ref, y_ref, sout_ref, cout_ref, ext_ref, st_ref):
    t = SEQ_BLOCK
    c = pl.program_id(1)

    @pl.when(c == 0)
    def _():
        ext_ref[0:8, :] = c0_ref[...]
        for g in range(M_GROUPS):
            st_ref[g] = s0_ref[M_HPG * g:M_HPG * (g + 1)].reshape(M_GW, M_STATE).T

    rows = _iota((t, 1), 0)
    xs_raw, bc_raw = xs_ref[...], bc_ref[...]
    if lead:
        xs_raw = jnp.where(rows >= lead, xs_raw, 0.0)
        bc_raw = jnp.where(rows >= lead, bc_raw, 0.0)
    ext_ref[8:8 + t, 0:M_INNER] = xs_raw
    ext_ref[8:8 + t, M_INNER:M_CONV_DIM] = bc_raw

    acc = cb_ref[...] + cw_ref[3:4, :] * ext_ref[8:8 + t, :]
    for k in range(M_CONV - 1):
        acc += cw_ref[k:k + 1, :] * ext_ref[5 + k:5 + k + t, :]
    tail = ext_ref[t + 5:t + 8, :]
    ext_ref[5:8, :] = tail

    @pl.when(c == pl.num_programs(1) - 1)
    def _():
        cout_ref[...] = tail

    xbc = _silu(acc)
    xs = xbc[:, 0:M_INNER]
    dt = _softplus(dt_ref[...] + dtb_ref[...])
    if lead:
        dt = jnp.where(rows >= lead, dt, 0.0)
    a = dt * a_ref[...]
    ti, si = _iota((t, t), 0), _iota((t, t), 1)
    tril = si <= ti
    cum = _sel_left(jnp.where(tril, 1.0, 0.0).astype(BF16), a)
    ecum = jnp.exp(cum)
    tailw = jnp.exp(cum[t - 1:t, :] - cum) * dt
    cum_t, dt_t = cum.T, dt.T
    ex = ex_ref[...]
    ecum_full = _sel_right(ecum, ex)
    xw = (xs * _sel_right(tailw, ex)).astype(BF16)
    xs_b = xs.astype(BF16)
    head_of_lane = jnp.right_shift(_iota((1, M_GW), 1), int(math.log2(M_HEADDIM)))

    for g in range(M_GROUPS):
        gs = slice(g * M_GW, (g + 1) * M_GW)
        bm_f = xbc[:, M_INNER + g * M_STATE:M_INNER + (g + 1) * M_STATE]
        bm = bm_f.astype(BF16)
        cm = xbc[:, M_INNER + M_BC // 2 + g * M_STATE:M_INNER + M_BC // 2 + (g + 1) * M_STATE].astype(BF16)
        cb = _dot_nt(cm, bm)
        st = st_ref[g]
        y_g = _dot(cm, st.astype(BF16)) * ecum_full[:, gs]
        for hh in range(M_HPG):
            h = g * M_HPG + hh
            diff = cum[:, h:h + 1] - cum_t[h:h + 1, :]
            w = cb * jnp.where(tril, jnp.exp(jnp.minimum(diff, 0.0)), 0.0) * dt_t[h:h + 1, :]
            x_h = jnp.where(head_of_lane == hh, xs_b[:, gs], jnp.zeros((), BF16))
            y_g = y_g + _dot(w.astype(BF16), x_h)
        st_ref[g] = st * ecum_full[t - 1:t, gs] + _dot(bm_f.T.astype(BF16), xw[:, gs])
        y_g = (y_g + dsk_ref[:, gs] * xs[:, gs]) * _silu(z_ref[:, gs])
        y_ref[:, gs] = (_rms(y_g) * nw_ref[:, gs]).astype(y_ref.dtype)

    @pl.when(c == pl.num_programs(1) - 1)
    def _():
        for g in range(M_GROUPS):
            sout_ref[M_HPG * g:M_HPG * (g + 1)] = st_ref[g].T.reshape(M_HPG, M_HEADDIM, M_STATE)


def _alias_prev(body, n_in, prevs, out_ids):
    if not prevs:
        return body, [], [], {}
    k = len(prevs)
    wrapped = lambda *refs: body(*refs[:n_in], *refs[n_in + k:])
    return (wrapped, [pl.BlockSpec(memory_space=pl.ANY)] * k, list(prevs),
            {n_in + i: o for i, o in enumerate(out_ids)})


def _seq_state_out(stack, bsz, dims):
    zeros = (0,) * len(dims)
    if stack is None:
        return (jax.ShapeDtypeStruct((bsz,) + dims, F32),
                pl.BlockSpec((None,) + dims, lambda b, c: (b,) + zeros))
    layer = stack[0]
    return (jax.ShapeDtypeStruct((DEPTH, bsz) + dims, F32),
            pl.BlockSpec((None, None) + dims, lambda b, c: (layer, b) + zeros))


def _ssd_seq(proj3, dt3, row_blk0, nblk, lead, p, s0, c0, out_dtype, stack=None):
    bsz = proj3.shape[0]
    t = SEQ_BLOCK
    colspec = lambda cblk: pl.BlockSpec((None, t, D_MODEL), lambda b, c, cblk=cblk: (b, row_blk0 + c, cblk))
    full = lambda a: pl.BlockSpec(a.shape, lambda b, c: (0,) * a.ndim)
    bcast = lambda a: pl.BlockSpec((None,) + a.shape[1:], lambda b, c: (0,) * a.ndim)
    params = [p["conv_w"], p["conv_b"], p["dt_bias"], p["a_neg"], p["d_skip"], p["m_norm_w"], p["expand"]]
    inputs = [proj3, proj3, proj3, dt3, *params, s0, c0]
    st_shape, st_spec = _seq_state_out(stack, bsz, (M_HEADS, M_HEADDIM, M_STATE))
    cv_shape, cv_spec = _seq_state_out(stack, bsz, (M_CONV - 1, M_CONV_DIM))
    body, x_specs, x_in, aliases = _alias_prev(functools.partial(_ssd_seq_kernel, lead), len(inputs),
                                               stack and stack[1], (1, 2))
    return pl.pallas_call(
        body,
        grid=(bsz, nblk),
        in_specs=[colspec(COL_Z), colspec(COL_XS), colspec(COL_BC),
                  pl.BlockSpec((None, t, LANES), lambda b, c: (b, row_blk0 + c, 0))]
                 + [full(a) for a in params] + [bcast(s0), bcast(c0)] + x_specs,
        out_specs=[pl.BlockSpec((None, t, M_INNER), lambda b, c: (b, c, 0)), st_spec, cv_spec],
        out_shape=[jax.ShapeDtypeStruct((bsz, nblk * t, M_INNER), out_dtype), st_shape, cv_shape],
        scratch_shapes=[pltpu.VMEM((t + 8, M_CONV_DIM), F32),
                        pltpu.VMEM((M_GROUPS, M_STATE, M_GW), F32)],
        input_output_aliases=aliases,
        compiler_params=_cparams(("parallel", "arbitrary")),
        name="ssd_seq",
    )(*inputs, *x_in)


def _hgrn_gates(fz, lb):
    e = jnp.exp(-jnp.abs(fz))
    r = 1.0 / (1.0 + e)
    pos = fz >= 0.0
    sig_pos = jnp.where(pos, r, e * r)
    sig_neg = jnp.where(pos, e * r, r)
    log_sig = jnp.minimum(fz, 0.0) - jnp.log(1.0 + e)
    logf = jnp.where(lb > 0.0, jnp.log(lb + (1.0 - lb) * sig_pos), log_sig)
    k = (1.0 - lb) * sig_neg
    return logf, k


H_LEVELS = int(math.log2(SEQ_BLOCK))


def _hgrn_tables():
    n = SEQ_BLOCK
    t = np.arange(n)[:, None]
    u = np.arange(n)[None, :]
    pair = []
    for b in range(H_LEVELS):
        bit = ((t >> b) & 1) == 1
        pair.append(((t >> (b + 1)) == (u >> (b + 1))) & bit & (((u >> b) & 1) == 0))
    m1 = ((t >> 1) | 1) << 1
    lvl1 = np.where(((t >> 1) & 1) == 1, (u >= m1) & (u <= t), (u > t) & (u < m1))
    sums = np.concatenate([u <= t, lvl1], 0).astype(np.float32)
    return np.concatenate([sums] * 3, 1), np.concatenate(pair, 0).astype(np.float32)


def _exp_neg_abs(d):
    return jnp.exp2(jnp.abs(d) * (-1.0 / math.log(2.0)))


def _hgrn_seq_kernel(lead, q_ref, f_ref, i_ref, g_ref, lb_ref, nw_ref, sums_ref, pair_ref, s0_ref,
                     y_ref, sout_ref, st_ref, ex_ref, q_s, k_s, z_ref, qd_ref, kd_ref, sc_ref):
    t = SEQ_BLOCK
    c = pl.program_id(1)

    @pl.when(c == 0)
    def _():
        for h in range(H_HEADS):
            st_ref[h] = s0_ref[h].T

    rows = _iota((t, 1), 0)
    logf, k = _hgrn_gates(f_ref[...], lb_ref[...])
    if lead:
        logf = jnp.where(rows >= lead, logf, 0.0)
        k = jnp.where(rows >= lead, k, 0.0)
    q = q_ref[...] * (H_KDIM ** -0.5)
    q_s[...] = q
    k_s[...] = k
    ex_ref[...] = _dot(sums_ref[...], jnp.concatenate(_split3(logf), axis=0))

    def side_of(b):
        return (jnp.right_shift(rows, b) & 1) == 1

    z_ref[0] = jnp.where(side_of(0), q * jnp.exp(logf), k).astype(BF16)
    z_ref[1] = (jnp.exp(ex_ref[t:2 * t, :]) * jnp.where(side_of(1), q_s[...], k_s[...])).astype(BF16)
    for b in range(2, H_LEVELS):
        half = 1 << b
        d = jnp.concatenate([ex_ref[g0:g0 + 2 * half, :] - ex_ref[g0 + half - 1:g0 + half, :]
                             for g0 in range(0, t, 2 * half)], axis=0)
        z_ref[b] = (_exp_neg_abs(d) * jnp.where(side_of(b), q_s[...], k_s[...])).astype(BF16)
    cum = ex_ref[0:t, :]
    qd_ref[...] = (q_s[...] * jnp.exp(cum)).astype(BF16)
    kd_ref[...] = (k_s[...] * jnp.exp(ex_ref[t - 1:t, :] - cum)).astype(BF16)

    for h in range(H_HEADS):
        cs = slice(h * H_KDIM, (h + 1) * H_KDIM)
        scores = None
        for b in range(H_LEVELS):
            z = z_ref[b, :, cs]
            p = _dot_nt(z, z) * pair_ref[b * t:(b + 1) * t, :]
            scores = p if scores is None else scores + p
        sc_ref[h] = scores.astype(BF16)

    for h in range(H_HEADS):
        cs = slice(h * H_KDIM, (h + 1) * H_KDIM)
        v = i_ref[:, cs]
        st = st_ref[h]
        o = (_dot(sc_ref[h], v.astype(BF16))
             + jnp.sum(q_s[:, cs] * k_s[:, cs], axis=-1, keepdims=True) * v
             + _dot_nt(qd_ref[:, cs], st.astype(BF16)))
        y = _rms(o) * nw_ref[:, cs] * _sigmoid(g_ref[:, cs])
        y_ref[:, cs] = y.astype(y_ref.dtype)
        st_ref[h] = st * jnp.exp(ex_ref[t - 1:t, cs]) + _dot(v.T.astype(BF16), kd_ref[:, cs])

    @pl.when(c == pl.num_programs(1) - 1)
    def _():
        for h in range(H_HEADS):
            sout_ref[h] = st_ref[h].T


def _hgrn_seq(proj3, row_blk0, nblk, lead, lb, nw, tables, s0, out_dtype, stack=None):
    bsz = proj3.shape[0]
    t = SEQ_BLOCK
    colspec = lambda cblk: pl.BlockSpec((None, t, D_MODEL), lambda b, c, cblk=cblk: (b, row_blk0 + c, cblk))
    full = lambda a: pl.BlockSpec(a.shape, lambda b, c: (0,) * a.ndim)
    bcast = lambda a: pl.BlockSpec((None,) + a.shape[1:], lambda b, c: (0,) * a.ndim)
    inputs = [proj3, proj3, proj3, proj3, lb, nw, *tables, s0]
    st_shape, st_spec = _seq_state_out(stack, bsz, (H_HEADS, H_KDIM, H_VDIM))
    body, x_specs, x_in, aliases = _alias_prev(functools.partial(_hgrn_seq_kernel, lead), len(inputs),
                                               stack and stack[1], (1,))
    return pl.pallas_call(
        body,
        grid=(bsz, nblk),
        in_specs=[colspec(COL_HQ), colspec(COL_HF), colspec(COL_HI), colspec(COL_HG),
                  full(lb), full(nw), full(tables[0]), full(tables[1]), bcast(s0)] + x_specs,
        out_specs=[pl.BlockSpec((None, t, D_MODEL), lambda b, c: (b, c, 0)), st_spec],
        out_shape=[jax.ShapeDtypeStruct((bsz, nblk * t, D_MODEL), out_dtype), st_shape],
        input_output_aliases=aliases,
        scratch_shapes=[pltpu.VMEM((H_HEADS, H_VDIM, H_KDIM), F32),
                        pltpu.VMEM((2 * t, D_MODEL), F32),
                        pltpu.VMEM((t, D_MODEL), F32), pltpu.VMEM((t, D_MODEL), F32),
                        pltpu.VMEM((H_LEVELS, t, D_MODEL), BF16),
                        pltpu.VMEM((t, D_MODEL), BF16), pltpu.VMEM((t, D_MODEL), BF16),
                        pltpu.VMEM((H_HEADS, t, t), BF16)],
        compiler_params=_cparams(("parallel", "arbitrary")),
        name="hgrn_seq",
    )(*inputs, *x_in)


def _log_gamma(h):
    return math.log(1.0 - 2.0 ** (-5.0 - h))


def _rotary(x, cos, sin):
    x1, x2 = x[:, :R_HALF], x[:, R_HALF:]
    return jnp.concatenate([x1 * cos - x2 * sin, x2 * cos + x1 * sin], axis=1)


def _ret_seq_kernel(lead, t, q_ref, k_ref, v_ref, g_ref, cos_ref, sin_ref, s0_ref, y_ref, sout_ref, st_ref):
    c = pl.program_id(1)

    @pl.when(c == 0)
    def _():
        st_ref[...] = s0_ref[...]

    cos, sin = cos_ref[...], sin_ref[...]
    ti, si = _iota((t, t), 0), _iota((t, t), 1)
    tril = si <= ti
    dpos = (ti - si).astype(F32)
    tcol = _iota((t, 1), 0).astype(F32)
    for h in range(R_HEADS):
        lg = _log_gamma(h)
        ks = slice(h * R_KDIM, (h + 1) * R_KDIM)
        vs = slice(h * R_VDIM, (h + 1) * R_VDIM)
        qh = _rotary(q_ref[:, ks], cos, sin)
        kh = _rotary(k_ref[:, ks], cos, sin) * (R_KDIM ** -0.5)
        if lead:
            kh = jnp.where(_iota((t, 1), 0) >= lead, kh, 0.0)
        qb, kb, vb = qh.astype(BF16), kh.astype(BF16), v_ref[:, vs].astype(BF16)
        decay = jnp.where(tril, jnp.exp(jnp.where(tril, dpos, 0.0) * lg), 0.0)
        scores = _dot_nt(qb, kb) * decay
        st = st_ref[h]
        o = _dot(scores.astype(BF16), vb) + _dot(qb, st.astype(BF16)) * jnp.exp((tcol + 1.0) * lg)
        kdec = (kh * jnp.exp((t - 1.0 - tcol) * lg)).T.astype(BF16)
        st_ref[h] = st * math.exp(t * lg) + _dot(kdec, vb)
        y_ref[:, vs] = (_rms(o) * _silu(g_ref[:, vs])).astype(y_ref.dtype)

    @pl.when(c == pl.num_programs(1) - 1)
    def _():
        sout_ref[...] = st_ref[...]


def _ret_seq(proj3, t, row_blk0, nblk, lead, cos, sin, s0, out_dtype, stack=None):
    bsz = proj3.shape[0]
    wide = R_HEADS * R_VDIM
    bcast = lambda a: pl.BlockSpec((None,) + a.shape[1:], lambda b, c: (0,) * a.ndim)
    inputs = [proj3, proj3, proj3, proj3, cos, sin, s0]
    st_shape, st_spec = _seq_state_out(stack, bsz, (R_HEADS, R_KDIM, R_VDIM))
    body, x_specs, x_in, aliases = _alias_prev(functools.partial(_ret_seq_kernel, lead, t), len(inputs),
                                               stack and stack[1], (1,))
    return pl.pallas_call(
        body,
        grid=(bsz, nblk),
        in_specs=[pl.BlockSpec((None, t, D_MODEL), lambda b, c: (b, row_blk0 + c, COL_RQ)),
                  pl.BlockSpec((None, t, D_MODEL), lambda b, c: (b, row_blk0 + c, COL_RK)),
                  pl.BlockSpec((None, t, wide), lambda b, c: (b, row_blk0 + c, COL_RV)),
                  pl.BlockSpec((None, t, wide), lambda b, c: (b, row_blk0 + c, COL_RG)),
                  pl.BlockSpec((t, R_HALF), lambda b, c: (c, 0)),
                  pl.BlockSpec((t, R_HALF), lambda b, c: (c, 0)),
                  bcast(s0)] + x_specs,
        out_specs=[pl.BlockSpec((None, t, wide), lambda b, c: (b, c, 0)), st_spec],
        out_shape=[jax.ShapeDtypeStruct((bsz, nblk * t, wide), out_dtype), st_shape],
        scratch_shapes=[pltpu.VMEM((R_HEADS, R_KDIM, R_VDIM), F32)],
        input_output_aliases=aliases,
        compiler_params=_cparams(("parallel", "arbitrary")),
        name="ret_seq",
    )(*inputs, *x_in)


def _step_prep_kernel(xs_ref, bc_ref, dt_ref, hq_ref, hf_ref, rq_ref, rk_ref, conv_ref,
                      cw_ref, cb_ref, dtb_ref, a_ref, ex_ref, lb_ref, cos_ref, sin_ref,
                      xs_o, bc_o, xdt_o, edec_o, conv_o, hq_o, hef_o, hk_o, rq_o, rk_o):
    raw = jnp.concatenate([xs_ref[...], bc_ref[...]], axis=1)
    buf = conv_ref[...]
    acc = cb_ref[...] + cw_ref[3:4, :] * raw
    for k in range(M_CONV - 1):
        acc += cw_ref[k:k + 1, :] * buf[:, k * M_CONV_DIM:(k + 1) * M_CONV_DIM]
    conv_o[:, 0:2 * M_CONV_DIM] = buf[:, M_CONV_DIM:]
    conv_o[:, 2 * M_CONV_DIM:] = raw
    xbc = _silu(acc)
    xs = xbc[:, :M_INNER]
    xs_o[...] = xs
    bc_o[...] = xbc[:, M_INNER:]
    dt = _softplus(dt_ref[...] + dtb_ref[...])
    ex = ex_ref[...]
    xdt_o[...] = xs * _sel_right(dt, ex)
    edec_o[...] = _sel_right(jnp.exp(dt * a_ref[...]), ex)
    logf, k = _hgrn_gates(hf_ref[...], lb_ref[...])
    hq_o[...] = hq_ref[...] * (H_KDIM ** -0.5)
    hef_o[...] = jnp.exp(logf)
    hk_o[...] = k
    cos, sin = cos_ref[...], sin_ref[...]
    for h in range(R_HEADS):
        ks = slice(h * R_KDIM, (h + 1) * R_KDIM)
        rq_o[:, ks] = _rotary(rq_ref[:, ks], cos, sin)
        rk_o[:, ks] = _rotary(rk_ref[:, ks], cos, sin) * (R_KDIM ** -0.5)


def _step_prep(proj_s, dt_s, conv_flat, p, cos, sin):
    nb = conv_flat.shape[0]
    col = lambda cblk: pl.BlockSpec((nb, D_MODEL), lambda i, cblk=cblk: (0, cblk))
    full = lambda a: pl.BlockSpec(a.shape, lambda i: (0,) * a.ndim)
    params = [p["conv_w"], p["conv_b"], p["dt_bias"], p["a_neg"], p["expand"], p["lb"], cos, sin]
    o = lambda w: jax.ShapeDtypeStruct((nb, w), F32)
    widths = [M_INNER, M_BC, M_INNER, M_INNER, (M_CONV - 1) * M_CONV_DIM, D_MODEL, D_MODEL, D_MODEL, D_MODEL, D_MODEL]
    return pl.pallas_call(
        _step_prep_kernel,
        grid=(1,),
        in_specs=[col(COL_XS), col(COL_BC), pl.BlockSpec((nb, LANES), lambda i: (0, 0)),
                  col(COL_HQ), col(COL_HF), col(COL_RQ), col(COL_RK), full(conv_flat)]
                 + [full(a) for a in params],
        out_specs=[pl.BlockSpec((nb, w), lambda i: (0, 0)) for w in widths],
        out_shape=[o(w) for w in widths],
        compiler_params=_cparams(("arbitrary",)),
        name="step_prep",
    )(proj_s, proj_s, dt_s, proj_s, proj_s, proj_s, proj_s, conv_flat, *params)


def _cols(x):
    nb, w = x.shape
    return jnp.transpose(x.reshape(nb // STEP_BT, STEP_BT, w), (0, 2, 1))


def _ssd_step_kernel(s_ref, xdt_ref, edec_ref, bc_ref, xs_ref, z_ref, dsk_ref, nw_ref, so_ref, y_ref, yt_ref):
    yt_ref[...] = jnp.zeros(yt_ref.shape, F32)
    for i in range(STEP_BT):
        for g in range(M_GROUPS):
            gs = slice(g * M_GW, (g + 1) * M_GW)
            hs = slice(M_HPG * g, M_HPG * (g + 1))
            st = s_ref[i, hs].reshape(M_GW, M_STATE)
            brow = bc_ref[i:i + 1, g * M_STATE:(g + 1) * M_STATE]
            crow = bc_ref[i:i + 1, M_BC // 2 + g * M_STATE:M_BC // 2 + (g + 1) * M_STATE]
            new = st * edec_ref[gs, i:i + 1] + xdt_ref[gs, i:i + 1] * brow
            so_ref[i, hs] = new.reshape(M_HPG, M_HEADDIM, M_STATE)
            yt_ref[gs, i:i + 1] = jnp.sum(new * crow, axis=-1, keepdims=True)
    y = yt_ref[...].T[0:STEP_BT, :]
    xs = xs_ref[...]
    y = (y + dsk_ref[...] * xs) * _silu(z_ref[...])
    for g in range(M_GROUPS):
        gs = slice(g * M_GW, (g + 1) * M_GW)
        y_ref[:, gs] = _rms(y[:, gs]) * nw_ref[:, gs]


def _ssd_step(state, layer, prev, xdt_c, edec_c, bc, xs, proj_s, p):
    nb = xs.shape[0]
    bt = STEP_BT
    full = lambda a: pl.BlockSpec(a.shape, lambda j: (0,) * a.ndim)
    sspec = pl.BlockSpec((None, bt, M_HEADS, M_HEADDIM, M_STATE), lambda j: (layer, j, 0, 0, 0))
    inputs = [state, xdt_c, edec_c, bc, xs, proj_s, p["d_skip"], p["m_norm_w"]]
    body, x_specs, x_in, aliases = _alias_prev(_ssd_step_kernel, len(inputs),
                                               None if prev is None else [prev], (0,))
    return pl.pallas_call(
        body,
        grid=(nb // bt,),
        in_specs=[sspec,
                  pl.BlockSpec((None, M_INNER, bt), lambda j: (j, 0, 0)),
                  pl.BlockSpec((None, M_INNER, bt), lambda j: (j, 0, 0)),
                  pl.BlockSpec((bt, M_BC), lambda j: (j, 0)),
                  pl.BlockSpec((bt, M_INNER), lambda j: (j, 0)),
                  pl.BlockSpec((bt, D_MODEL), lambda j: (j, COL_Z)),
                  full(p["d_skip"]), full(p["m_norm_w"])] + x_specs,
        out_specs=[sspec, pl.BlockSpec((bt, M_INNER), lambda j: (j, 0))],
        out_shape=[jax.ShapeDtypeStruct(state.shape, F32),
                   jax.ShapeDtypeStruct((nb, M_INNER), F32)],
        scratch_shapes=[pltpu.VMEM((M_INNER, LANES), F32)],
        input_output_aliases=aliases,
        compiler_params=_cparams(("parallel",)),
        name="ssd_step",
    )(*inputs, *x_in)


def _hgrn_step_kernel(s_ref, q_ref, ef_ref, k_ref, v_ref, g_ref, nw_ref, so_ref, y_ref):
    for i in range(STEP_BT):
        for h in range(H_HEADS):
            cs = slice(h * H_KDIM, (h + 1) * H_KDIM)
            new = s_ref[i, h] * ef_ref[cs, i:i + 1] + k_ref[cs, i:i + 1] * v_ref[i:i + 1, cs]
            so_ref[i, h] = new
            y_ref[i:i + 1, cs] = jnp.sum(new * q_ref[cs, i:i + 1], axis=0, keepdims=True)
    for h in range(H_HEADS):
        cs = slice(h * H_KDIM, (h + 1) * H_KDIM)
        y_ref[:, cs] = _rms(y_ref[:, cs]) * nw_ref[:, cs] * _sigmoid(g_ref[:, cs])


def _hgrn_step(state, layer, prev, q_c, ef_c, k_c, proj_s, nw):
    nb = state.shape[1]
    bt = STEP_BT
    cspec = pl.BlockSpec((None, D_MODEL, bt), lambda j: (j, 0, 0))
    sspec = pl.BlockSpec((None, bt, H_HEADS, H_KDIM, H_VDIM), lambda j: (layer, j, 0, 0, 0))
    inputs = [state, q_c, ef_c, k_c, proj_s, proj_s, nw]
    body, x_specs, x_in, aliases = _alias_prev(_hgrn_step_kernel, len(inputs),
                                               None if prev is None else [prev], (0,))
    return pl.pallas_call(
        body,
        grid=(nb // bt,),
        in_specs=[sspec, cspec, cspec, cspec,
                  pl.BlockSpec((bt, D_MODEL), lambda j: (j, COL_HI)),
                  pl.BlockSpec((bt, D_MODEL), lambda j: (j, COL_HG)),
                  pl.BlockSpec(nw.shape, lambda j: (0, 0))] + x_specs,
        out_specs=[sspec, pl.BlockSpec((bt, D_MODEL), lambda j: (j, 0))],
        out_shape=[jax.ShapeDtypeStruct(state.shape, F32),
                   jax.ShapeDtypeStruct((nb, D_MODEL), F32)],
        input_output_aliases=aliases,
        compiler_params=_cparams(("parallel",)),
        name="hgrn_step",
    )(*inputs, *x_in)


def _ret_step_kernel(s_ref, q_ref, k_ref, v_ref, g_ref, gam_ref, so_ref, y_ref):
    gam = gam_ref[...]
    for i in range(STEP_BT):
        new = s_ref[i] * gam + k_ref[:, i:i + 1] * v_ref[i:i + 1, :]
        so_ref[i] = new
        y_ref[i:i + 1, :] = jnp.sum(new * q_ref[:, i:i + 1], axis=0, keepdims=True)
    y_ref[...] = _rms(y_ref[...]) * _silu(g_ref[...])


def _ret_step(state, layer, prev, q_c, k_c, proj_s, gam):
    nb = state.shape[1]
    bt = STEP_BT
    cspec = pl.BlockSpec((None, R_KDIM, bt), lambda j, h: (j, h, 0))
    sspec = pl.BlockSpec((None, bt, None, R_KDIM, R_VDIM), lambda j, h: (layer, j, h, 0, 0))
    inputs = [state, q_c, k_c, proj_s, proj_s, gam]
    body, x_specs, x_in, aliases = _alias_prev(_ret_step_kernel, len(inputs),
                                               None if prev is None else [prev], (0,))
    return pl.pallas_call(
        body,
        grid=(nb // bt, R_HEADS),
        in_specs=[sspec, cspec, cspec,
                  pl.BlockSpec((bt, R_VDIM), lambda j, h: (j, h)),
                  pl.BlockSpec((bt, R_VDIM), lambda j, h: (j, R_HEADS + h)),
                  pl.BlockSpec((None, 1, R_VDIM), lambda j, h: (h, 0, 0))] + x_specs,
        out_specs=[sspec, pl.BlockSpec((bt, R_VDIM), lambda j, h: (j, h))],
        out_shape=[jax.ShapeDtypeStruct(state.shape, F32),
                   jax.ShapeDtypeStruct((nb, R_HEADS * R_VDIM), F32)],
        input_output_aliases=aliases,
        compiler_params=_cparams(("parallel", "parallel")),
        name="ret_step",
    )(*inputs, *x_in)


def _rope_tables(positions):
    inv_freq = 1.0 / (ROPE_BASE ** jnp.linspace(0.0, 1.0, R_HALF, dtype=F32))
    ang = positions[:, None] * inv_freq[None, :]
    return jnp.cos(ang), jnp.sin(ang)


def _per_channel(v):
    return jnp.repeat(v.astype(F32), M_HEADDIM).reshape(1, M_INNER)


def _pad_lanes(v):
    return jnp.pad(v.astype(F32), (0, LANES - v.shape[0])).reshape(1, LANES)


def kernel(x_prompt, x_sample, state_ssm, state_conv, state_hgrn, state_ret, meta_tokens, ln_in_g, ln_in_b,
           w_in, conv_w, conv_b, dt_bias, a_log, d_skip, m_norm_w, hgrn_lb_logits, h_norm_w, w_br_m, w_br_h,
           w_br_r, w_out, ln1_g, ln1_b, w_ffn_in, w_ffn_out, ln2_g, ln2_b):
    bp, sp = x_prompt.shape[0], x_prompt.shape[1]
    nb = x_sample.shape[0]
    assert x_sample.shape[1] == 1 and nb == SMALL_ROWS - SEQ_BLOCK and nb % STEP_BT == 0
    assert sp % RET_BLOCK == 0 and meta_tokens.shape[0] == N_META

    wt = jnp.swapaxes(w_in, 1, 2).astype(BF16)
    wm_b, wh_b, wr_b, wo_b = (w.astype(BF16) for w in (w_br_m, w_br_h, w_br_r, w_out))
    wfi_b, wfo_b = w_ffn_in.astype(BF16), w_ffn_out.astype(BF16)
    ln1 = (ln1_g.reshape(DEPTH, 1, D_MODEL), ln1_b.reshape(DEPTH, 1, D_MODEL))
    ln2 = (ln2_g.reshape(DEPTH, 1, D_MODEL), ln2_b.reshape(DEPTH, 1, D_MODEL))
    lb_cum = jnp.cumsum(jax.nn.softmax(hgrn_lb_logits.astype(F32), axis=0), axis=0)
    lbs = lb_cum - lb_cum[0]
    expand = (np.arange(LANES)[:, None] == (np.arange(M_INNER)[None, :] // M_HEADDIM)).astype(np.float32)
    expand = jnp.asarray(expand, BF16)
    h_sums, h_pair = _hgrn_tables()
    h_tables = (jnp.asarray(h_sums, BF16), jnp.asarray(h_pair, F32))
    gam = jnp.asarray(np.broadcast_to(
        np.array([1.0 - 2.0 ** (-5.0 - h) for h in range(R_HEADS)], np.float32)[:, None, None],
        (R_HEADS, 1, R_VDIM)))

    pos_real = jnp.arange(N_META, N_META + sp, dtype=F32)
    pos_meta = jnp.maximum(jnp.arange(SEQ_BLOCK, dtype=F32) - META_LEAD, 0.0)
    pos_samp = jnp.full((nb,), float(PAST_LEN), F32)
    cos_r, sin_r = _rope_tables(pos_real)
    cos_m, sin_m = _rope_tables(pos_meta)
    cos_s, sin_s = _rope_tables(pos_samp)

    x_real = _layer_norm_rows(x_prompt.reshape(bp * sp, D_MODEL), ln_in_g, ln_in_b, 512)
    small_in = jnp.concatenate([x_sample.reshape(nb, D_MODEL),
                                jnp.zeros((META_LEAD, D_MODEL), F32), meta_tokens.astype(F32)], axis=0)
    x_small = _layer_norm_rows(small_in, ln_in_g, ln_in_b, SMALL_ROWS)

    z_ssm = jnp.zeros((1, M_HEADS, M_HEADDIM, M_STATE), F32)
    z_conv = jnp.zeros((1, 8, M_CONV_DIM), F32)
    z_hgrn = jnp.zeros((1, H_HEADS, H_KDIM, H_VDIM), F32)
    z_ret = jnp.zeros((1, R_HEADS, R_KDIM, R_VDIM), F32)
    ssm_p = conv_p = hgrn_p = ret_p = ssm_s = hgrn_s = ret_s = None
    conv_s = []
    for l in range(DEPTH):
        p = dict(conv_w=conv_w[l], conv_b=conv_b[l].reshape(1, -1), dt_bias=_pad_lanes(dt_bias[l]),
                 a_neg=_pad_lanes(-jnp.exp(a_log[l].astype(F32))), d_skip=_per_channel(d_skip[l]),
                 m_norm_w=m_norm_w[l].reshape(1, -1), expand=expand, lb=lbs[l].reshape(1, -1))
        hnw = h_norm_w[l].reshape(1, -1)

        proj_r = _proj(x_real, wt, l, 1024, "proj_real")
        dt_r = _proj_dt(x_real, wt, l, 1024, "proj_dt_real")
        proj_s = _proj(x_small, wt, l, SMALL_ROWS, "proj_small")
        dt_s = _proj_dt(x_small, wt, l, SMALL_ROWS, "proj_dt_small")
        proj_r3 = proj_r.reshape(bp, sp, PROJ_COLS)
        dt_r3 = dt_r.reshape(bp, sp, LANES)
        proj_s3 = proj_s.reshape(1, SMALL_ROWS, PROJ_COLS)
        dt_s3 = dt_s.reshape(1, SMALL_ROWS, LANES)

        ym_m, ssm_m, conv_m = _ssd_seq(proj_s3, dt_s3, 1, 1, META_LEAD, p, z_ssm, z_conv, F32)
        yh_m, hgrn_m = _hgrn_seq(proj_s3, 1, 1, META_LEAD, p["lb"], hnw, h_tables, z_hgrn, F32)
        yr_m, ret_m = _ret_seq(proj_s3, SEQ_BLOCK, 1, 1, META_LEAD, cos_m, sin_m, z_ret, F32)

        conv0 = jnp.pad(conv_m, ((0, 0), (8 - (M_CONV - 1), 0), (0, 0)))
        ym_r, ssm_p, conv_p = _ssd_seq(proj_r3, dt_r3, 0, sp // SEQ_BLOCK, 0, p, ssm_m, conv0, BF16,
                                       stack=(l, None if l == 0 else [ssm_p, conv_p]))
        yh_r, hgrn_p = _hgrn_seq(proj_r3, 0, sp // SEQ_BLOCK, 0, p["lb"], hnw, h_tables, hgrn_m, BF16,
                                 stack=(l, None if l == 0 else [hgrn_p]))
        yr_r, ret_p = _ret_seq(proj_r3, RET_BLOCK, 0, sp // RET_BLOCK, 0, cos_r, sin_r, ret_m, BF16,
                               stack=(l, None if l == 0 else [ret_p]))

        conv_flat = state_conv[l].reshape(nb, (M_CONV - 1) * M_CONV_DIM)
        (xs_s, bc_s, xdt_s, edec_s, conv_new, hq_s, hef_s, hk_s, rq_s, rk_s) = _step_prep(
            proj_s, dt_s, conv_flat, p, cos_s, sin_s)
        ssm_s, ym_s = _ssd_step(state_ssm, l, ssm_s, _cols(xdt_s), _cols(edec_s), bc_s, xs_s, proj_s, p)
        hgrn_s, yh_s = _hgrn_step(state_hgrn, l, hgrn_s, _cols(hq_s), _cols(hef_s), _cols(hk_s), proj_s, hnw)
        ret_s, yr_s = _ret_step(state_ret, l, ret_s, _cols(rq_s), _cols(rk_s), proj_s, gam)
        conv_s.append(conv_new.reshape(nb, M_CONV - 1, M_CONV_DIM))

        ym_small = jnp.concatenate([ym_s, ym_m[0]], axis=0)
        yh_small = jnp.concatenate([yh_s, yh_m[0]], axis=0)
        yr_small = jnp.concatenate([yr_s, yr_m[0]], axis=0)
        x_real = _mix(x_real, ym_r.reshape(bp * sp, -1), yh_r.reshape(bp * sp, -1), yr_r.reshape(bp * sp, -1),
                      proj_r, l, wm_b, wh_b, wr_b, wo_b, *ln1, 256)
        x_small = _mix(x_small, ym_small, yh_small, yr_small, proj_s, l, wm_b, wh_b, wr_b, wo_b, *ln1, SMALL_ROWS)
        x_real = _ffn(x_real, l, wfi_b, wfo_b, *ln2, 512)
        x_small = _ffn(x_small, l, wfi_b, wfo_b, *ln2, SMALL_ROWS)

    return (x_real.reshape(bp, sp, D_MODEL), x_small[:nb].reshape(nb, 1, D_MODEL),
            ssm_p, conv_p, hgrn_p, ret_p, ssm_s, jnp.stack(conv_s), hgrn_s, ret_s)
```

```python
import functools
import math

import numpy as np
import jax
import jax.numpy as jnp
from jax import lax
from jax.experimental import pallas as pl
from jax.experimental.pallas import tpu as pltpu

F32 = jnp.float32
BF16 = jnp.bfloat16

D_MODEL = 1024
DEPTH = 2
N_META = 16
M_INNER = D_MODEL
M_HEADDIM = 64
M_HEADS = M_INNER // M_HEADDIM
M_GROUPS = 4
M_HPG = M_HEADS // M_GROUPS
M_STATE = 128
M_CONV = 4
M_BC = 2 * M_GROUPS * M_STATE
M_CONV_DIM = M_INNER + M_BC
M_GW = M_INNER // M_GROUPS
H_KDIM = 128
H_HEADS = D_MODEL // H_KDIM
H_VDIM = 128
R_HEADS = 4
R_KDIM = D_MODEL // R_HEADS
R_VDIM = 2 * R_KDIM
R_HALF = R_KDIM // 2
ROPE_BASE = 10000.0
D_FF = ((8 * D_MODEL // 3 + 255) // 256) * 256
FF_CHUNK = 256
DN_ALPHA = (2 * DEPTH) ** 0.25
PAST_LEN = 16384

LANES = 128
SEQ_BLOCK = 128
RET_BLOCK = 256
SMALL_ROWS = 256
META_LEAD = SEQ_BLOCK - N_META
STEP_BT = 8
VMEM_LIMIT = 56 * 1024 * 1024

COL_RV, COL_RG = 0, 1
COL_Z, COL_XS, COL_BC, COL_HQ, COL_HF, COL_HI, COL_HG, COL_RQ, COL_RK, COL_GM, COL_GH, COL_GR = range(4, 16)
PROJ_COLS = 16 * 1024

_ORIG_SPLITS = (M_INNER, M_CONV_DIM, M_HEADS, 1024, 1024, 1024, 1024, 1024, 1024, 2048, 2048, 3072)
_ORIG_OFF = np.concatenate([[0], np.cumsum(_ORIG_SPLITS)]).tolist()


def _cparams(sem):
    return pltpu.CompilerParams(dimension_semantics=sem, vmem_limit_bytes=VMEM_LIMIT)


def _sigmoid(x):
    return 1.0 / (1.0 + jnp.exp(-x))


def _silu(x):
    return x * _sigmoid(x)


def _softplus(x):
    return jnp.maximum(x, 0.0) + jnp.log1p(jnp.exp(-jnp.abs(x)))


def _layer_norm(x, g, b):
    mu = jnp.mean(x, axis=-1, keepdims=True)
    xc = x - mu
    var = jnp.mean(xc * xc, axis=-1, keepdims=True)
    return xc * lax.rsqrt(var + 1e-5) * g + b


def _rms(x):
    return x * lax.rsqrt(jnp.mean(x * x, axis=-1, keepdims=True) + 1e-6)


def _split3(x):
    hi = x.astype(BF16)
    r = x - hi.astype(F32)
    mid = r.astype(BF16)
    lo = (r - mid.astype(F32)).astype(BF16)
    return hi, mid, lo


def _dot(a, b):
    return jnp.dot(a, b, preferred_element_type=F32)


def _dot_nt(a, b):
    return lax.dot_general(a, b, (((1,), (1,)), ((), ())), preferred_element_type=F32)


def _sel_right(x, m01):
    hi, mid, lo = _split3(x)
    return _dot(hi, m01) + _dot(mid, m01) + _dot(lo, m01)


def _sel_left(m01, x):
    hi, mid, lo = _split3(x)
    return _dot(m01, hi) + _dot(m01, mid) + _dot(m01, lo)


def _iota(shape, dim):
    return lax.broadcasted_iota(jnp.int32, shape, dim)


def _ln_kernel(x_ref, g_ref, b_ref, o_ref):
    o_ref[...] = _layer_norm(x_ref[...], g_ref[...], b_ref[...])


def _layer_norm_rows(x, g, b, tm):
    m = x.shape[0]
    tm = min(tm, m)
    return pl.pallas_call(
        _ln_kernel,
        grid=(m // tm,),
        in_specs=[pl.BlockSpec((tm, D_MODEL), lambda i: (i, 0)),
                  pl.BlockSpec((1, D_MODEL), lambda i: (0, 0)),
                  pl.BlockSpec((1, D_MODEL), lambda i: (0, 0))],
        out_specs=pl.BlockSpec((tm, D_MODEL), lambda i: (i, 0)),
        out_shape=jax.ShapeDtypeStruct((m, D_MODEL), F32),
        compiler_params=_cparams(("parallel",)),
        name="ln_in",
    )(x, g.reshape(1, -1), b.reshape(1, -1))


def _proj_kernel(x_ref, w_ref, o_ref, xb_ref):
    @pl.when(pl.program_id(1) == 0)
    def _():
        xb_ref[...] = x_ref[...].astype(BF16)

    w = w_ref[0] if len(w_ref.shape) == 3 else w_ref[...]
    o_ref[...] = _dot_nt(xb_ref[...], w)


def _layer_block(a, layer):
    return pl.BlockSpec((None,) + a.shape[1:], lambda *_: (layer,) + (0,) * (a.ndim - 1))


_W_RUNS = ((0, _ORIG_OFF[9]), (4096, _ORIG_OFF[0]), (7168, _ORIG_OFF[3]), (13312, _ORIG_OFF[11]))
PROJ_TN = 1024


def _orig_col(j):
    c = j * PROJ_TN
    off = c - _W_RUNS[0][0] + _W_RUNS[0][1]
    for new0, orig0 in _W_RUNS[1:]:
        off = jnp.where(c >= new0, c - new0 + orig0, off)
    return pl.multiple_of(off, M_HEADS)


def _proj(x, wt, layer, tm, name):
    m = x.shape[0]
    tm = min(tm, m)
    return pl.pallas_call(
        _proj_kernel,
        grid=(m // tm, PROJ_COLS // PROJ_TN),
        in_specs=[pl.BlockSpec((tm, D_MODEL), lambda i, j: (i, 0)),
                  pl.BlockSpec((pl.Element(1), pl.Element(PROJ_TN), pl.Element(D_MODEL)),
                               lambda i, j: (layer, _orig_col(j), 0))],
        out_specs=pl.BlockSpec((tm, PROJ_TN), lambda i, j: (i, j)),
        out_shape=jax.ShapeDtypeStruct((m, PROJ_COLS), F32),
        scratch_shapes=[pltpu.VMEM((tm, D_MODEL), BF16)],
        compiler_params=_cparams(("parallel", "arbitrary")),
        name=name,
    )(x, wt)


def _proj_dt(x, wt, layer, tm, name):
    m = x.shape[0]
    tm = min(tm, m)
    assert _ORIG_OFF[2] % LANES == 0
    return pl.pallas_call(
        _proj_kernel,
        grid=(m // tm, 1),
        in_specs=[pl.BlockSpec((tm, D_MODEL), lambda i, j: (i, 0)),
                  pl.BlockSpec((None, LANES, D_MODEL), lambda i, j: (layer, _ORIG_OFF[2] // LANES, 0))],
        out_specs=pl.BlockSpec((tm, LANES), lambda i, j: (i, 0)),
        out_shape=jax.ShapeDtypeStruct((m, LANES), F32),
        scratch_shapes=[pltpu.VMEM((tm, D_MODEL), BF16)],
        compiler_params=_cparams(("parallel", "arbitrary")),
        name=name,
    )(x, wt)


def _mix_kernel(x_ref, ym_ref, yh_ref, yr_ref, gm_ref, gh_ref, gr_ref,
                wm_ref, wh_ref, wr_ref, wo_ref, g_ref, b_ref, o_ref):
    mixed = _sigmoid(gm_ref[...]) * _dot(ym_ref[...].astype(BF16), wm_ref[...])
    mixed += _sigmoid(gh_ref[...]) * _dot(yh_ref[...].astype(BF16), wh_ref[...])
    mixed += _sigmoid(gr_ref[...]) * _dot(yr_ref[...].astype(BF16), wr_ref[...])
    h = _dot(mixed.astype(BF16), wo_ref[...])
    o_ref[...] = _layer_norm(DN_ALPHA * x_ref[...] + h, g_ref[...], b_ref[...])


def _mix(x, ym, yh, yr, proj, layer, wm, wh, wr, wo, g2, b2, tm):
    m = x.shape[0]
    tm = min(tm, m)
    row = lambda w: pl.BlockSpec((tm, w), lambda i: (i, 0))
    col = lambda c: pl.BlockSpec((tm, D_MODEL), lambda i, c=c: (i, c))
    full = lambda a: _layer_block(a, layer)
    return pl.pallas_call(
        _mix_kernel,
        grid=(m // tm,),
        in_specs=[row(D_MODEL), row(M_INNER), row(D_MODEL), row(R_HEADS * R_VDIM),
                  col(COL_GM), col(COL_GH), col(COL_GR),
                  full(wm), full(wh), full(wr), full(wo), full(g2), full(b2)],
        out_specs=row(D_MODEL),
        out_shape=jax.ShapeDtypeStruct((m, D_MODEL), F32),
        compiler_params=_cparams(("parallel",)),
        name="mix",
    )(x, ym, yh, yr, proj, proj, proj, wm, wh, wr, wo, g2, b2)


def _ffn_kernel(x_ref, wi_ref, wo_ref, g_ref, b_ref, o_ref):
    x = x_ref[...]
    xb = x.astype(BF16)
    acc = jnp.zeros(x.shape, F32)
    for j in range(0, D_FF, FF_CHUNK):
        hg = _dot(xb, wi_ref[:, j:j + FF_CHUNK])
        hu = _dot(xb, wi_ref[:, D_FF + j:D_FF + j + FF_CHUNK])
        acc += _dot((_silu(hg) * hu).astype(BF16), wo_ref[j:j + FF_CHUNK, :])
    o_ref[...] = _layer_norm(DN_ALPHA * x + acc, g_ref[...], b_ref[...])


def _ffn(x, layer, wi, wo, g2, b2, tm):
    m = x.shape[0]
    tm = min(tm, m)
    full = lambda a: _layer_block(a, layer)
    return pl.pallas_call(
        _ffn_kernel,
        grid=(m // tm,),
        in_specs=[pl.BlockSpec((tm, D_MODEL), lambda i: (i, 0)), full(wi), full(wo), full(g2), full(b2)],
        out_specs=pl.BlockSpec((tm, D_MODEL), lambda i: (i, 0)),
        out_shape=jax.ShapeDtypeStruct((m, D_MODEL), F32),
        compiler_params=_cparams(("parallel",)),
        name="ffn",
    )(x, wi, wo, g2, b2)


def _ssd_seq_kernel(lead, z_ref, xs_ref, bc_ref, dt_ref, cw_ref, cb_ref, dtb_ref, a_ref, dsk_ref, nw_ref,
                    ex_ref, s0_ref, c0_ref, y_ref, sout_ref, cout_ref, ext_ref, st_ref):
    t = SEQ_BLOCK
    c = pl.program_id(1)

    @pl.when(c == 0)
    def _():
        ext_ref[0:8, :] = c0_ref[...]
        for g in range(M_GROUPS):
            st_ref[g] = s0_ref[M_HPG * g:M_HPG * (g + 1)].reshape(M_GW, M_STATE).T

    rows = _iota((t, 1), 0)
    xs_raw, bc_raw = xs_ref[...], bc_ref[...]
    if lead:
        xs_raw = jnp.where(rows >= lead, xs_raw, 0.0)
        bc_raw = jnp.where(rows >= lead, bc_raw, 0.0)
    ext_ref[8:8 + t, 0:M_INNER] = xs_raw
    ext_ref[8:8 + t, M_INNER:M_CONV_DIM] = bc_raw

    acc = cb_ref[...] + cw_ref[3:4, :] * ext_ref[8:8 + t, :]
    for k in range(M_CONV - 1):
        acc += cw_ref[k:k + 1, :] * ext_ref[5 + k:5 + k + t, :]
    tail = ext_ref[t + 5:t + 8, :]
    ext_ref[5:8, :] = tail

    @pl.when(c == pl.num_programs(1) - 1)
    def _():
        cout_ref[...] = tail

    xbc = _silu(acc)
    xs = xbc[:, 0:M_INNER]
    dt = _softplus(dt_ref[...] + dtb_ref[...])
    if lead:
        dt = jnp.where(rows >= lead, dt, 0.0)
    a = dt * a_ref[...]
    ti, si = _iota((t, t), 0), _iota((t, t), 1)
    tril = si <= ti
    cum = _sel_left(jnp.where(tril, 1.0, 0.0).astype(BF16), a)
    ecum = jnp.exp(cum)
    tailw = jnp.exp(cum[t - 1:t, :] - cum) * dt
    cum_t, dt_t = cum.T, dt.T
    ex = ex_ref[...]
    ecum_full = _sel_right(ecum, ex)
    xw = (xs * _sel_right(tailw, ex)).astype(BF16)
    xs_b = xs.astype(BF16)
    head_of_lane = jnp.right_shift(_iota((1, M_GW), 1), int(math.log2(M_HEADDIM)))

    for g in range(M_GROUPS):
        gs = slice(g * M_GW, (g + 1) * M_GW)
        bm_f = xbc[:, M_INNER + g * M_STATE:M_INNER + (g + 1) * M_STATE]
        bm = bm_f.astype(BF16)
        cm = xbc[:, M_INNER + M_BC // 2 + g * M_STATE:M_INNER + M_BC // 2 + (g + 1) * M_STATE].astype(BF16)
        cb = _dot_nt(cm, bm)
        st = st_ref[g]
        y_g = _dot(cm, st.astype(BF16)) * ecum_full[:, gs]
        for hh in range(M_HPG):
            h = g * M_HPG + hh
            diff = cum[:, h:h + 1] - cum_t[h:h + 1, :]
            w = cb * jnp.where(tril, jnp.exp(jnp.minimum(diff, 0.0)), 0.0) * dt_t[h:h + 1, :]
            x_h = jnp.where(head_of_lane == hh, xs_b[:, gs], jnp.zeros((), BF16))
            y_g = y_g + _dot(w.astype(BF16), x_h)
        st_ref[g] = st * ecum_full[t - 1:t, gs] + _dot(bm_f.T.astype(BF16), xw[:, gs])
        y_g = (y_g + dsk_ref[:, gs] * xs[:, gs]) * _silu(z_ref[:, gs])
        y_ref[:, gs] = (_rms(y_g) * nw_ref[:, gs]).astype(y_ref.dtype)

    @pl.when(c == pl.num_programs(1) - 1)
    def _():
        for g in range(M_GROUPS):
            sout_ref[M_HPG * g:M_HPG * (g + 1)] = st_ref[g].T.reshape(M_HPG, M_HEADDIM, M_STATE)


def _alias_prev(body, n_in, prevs, out_ids):
    if not prevs:
        return body, [], [], {}
    k = len(prevs)
    wrapped = lambda *refs: body(*refs[:n_in], *refs[n_in + k:])
    return (wrapped, [pl.BlockSpec(memory_space=pl.ANY)] * k, list(prevs),
            {n_in + i: o for i, o in enumerate(out_ids)})


def _seq_state_out(stack, bsz, dims):
    zeros = (0,) * len(dims)
    if stack is None:
        return (jax.ShapeDtypeStruct((bsz,) + dims, F32),
                pl.BlockSpec((None,) + dims, lambda b, c: (b,) + zeros))
    layer = stack[0]
    return (jax.ShapeDtypeStruct((DEPTH, bsz) + dims, F32),
            pl.BlockSpec((None, None) + dims, lambda b, c: (layer, b) + zeros))


def _ssd_seq(proj3, dt3, row_blk0, nblk, lead, p, s0, c0, out_dtype, stack=None):
    bsz = proj3.shape[0]
    t = SEQ_BLOCK
    colspec = lambda cblk: pl.BlockSpec((None, t, D_MODEL), lambda b, c, cblk=cblk: (b, row_blk0 + c, cblk))
    full = lambda a: pl.BlockSpec(a.shape, lambda b, c: (0,) * a.ndim)
    bcast = lambda a: pl.BlockSpec((None,) + a.shape[1:], lambda b, c: (0,) * a.ndim)
    params = [p["conv_w"], p["conv_b"], p["dt_bias"], p["a_neg"], p["d_skip"], p["m_norm_w"], p["expand"]]
    inputs = [proj3, proj3, proj3, dt3, *params, s0, c0]
    st_shape, st_spec = _seq_state_out(stack, bsz, (M_HEADS, M_HEADDIM, M_STATE))
    cv_shape, cv_spec = _seq_state_out(stack, bsz, (M_CONV - 1, M_CONV_DIM))
    body, x_specs, x_in, aliases = _alias_prev(functools.partial(_ssd_seq_kernel, lead), len(inputs),
                                               stack and stack[1], (1, 2))
    return pl.pallas_call(
        body,
        grid=(bsz, nblk),
        in_specs=[colspec(COL_Z), colspec(COL_XS), colspec(COL_BC),
                  pl.BlockSpec((None, t, LANES), lambda b, c: (b, row_blk0 + c, 0))]
                 + [full(a) for a in params] + [bcast(s0), bcast(c0)] + x_specs,
        out_specs=[pl.BlockSpec((None, t, M_INNER), lambda b, c: (b, c, 0)), st_spec, cv_spec],
        out_shape=[jax.ShapeDtypeStruct((bsz, nblk * t, M_INNER), out_dtype), st_shape, cv_shape],
        scratch_shapes=[pltpu.VMEM((t + 8, M_CONV_DIM), F32),
                        pltpu.VMEM((M_GROUPS, M_STATE, M_GW), F32)],
        input_output_aliases=aliases,
        compiler_params=_cparams(("parallel", "arbitrary")),
        name="ssd_seq",
    )(*inputs, *x_in)


def _hgrn_gates(fz, lb):
    e = jnp.exp(-jnp.abs(fz))
    r = 1.0 / (1.0 + e)
    pos = fz >= 0.0
    sig_pos = jnp.where(pos, r, e * r)
    sig_neg = jnp.where(pos, e * r, r)
    log_sig = jnp.minimum(fz, 0.0) - jnp.log(1.0 + e)
    logf = jnp.where(lb > 0.0, jnp.log(lb + (1.0 - lb) * sig_pos), log_sig)
    k = (1.0 - lb) * sig_neg
    return logf, k


H_LEVELS = int(math.log2(SEQ_BLOCK))


def _hgrn_tables():
    n = SEQ_BLOCK
    t = np.arange(n)[:, None]
    u = np.arange(n)[None, :]
    pair = []
    for b in range(H_LEVELS):
        bit = ((t >> b) & 1) == 1
        pair.append(((t >> (b + 1)) == (u >> (b + 1))) & bit & (((u >> b) & 1) == 0))
    m1 = ((t >> 1) | 1) << 1
    lvl1 = np.where(((t >> 1) & 1) == 1, (u >= m1) & (u <= t), (u > t) & (u < m1))
    sums = np.concatenate([u <= t, lvl1], 0).astype(np.float32)
    return np.concatenate([sums] * 3, 1), np.concatenate(pair, 0).astype(np.float32)


def _exp_neg_abs(d):
    return jnp.exp2(jnp.abs(d) * (-1.0 / math.log(2.0)))


def _hgrn_seq_kernel(lead, q_ref, f_ref, i_ref, g_ref, lb_ref, nw_ref, sums_ref, pair_ref, s0_ref,
                     y_ref, sout_ref, st_ref, ex_ref, q_s, k_s, z_ref, qd_ref, kd_ref, sc_ref):
    t = SEQ_BLOCK
    c = pl.program_id(1)

    @pl.when(c == 0)
    def _():
        for h in range(H_HEADS):
            st_ref[h] = s0_ref[h].T

    rows = _iota((t, 1), 0)
    logf, k = _hgrn_gates(f_ref[...], lb_ref[...])
    if lead:
        logf = jnp.where(rows >= lead, logf, 0.0)
        k = jnp.where(rows >= lead, k, 0.0)
    q = q_ref[...] * (H_KDIM ** -0.5)
    q_s[...] = q
    k_s[...] = k
    ex_ref[...] = _dot(sums_ref[...], jnp.concatenate(_split3(logf), axis=0))

    def side_of(b):
        return (jnp.right_shift(rows, b) & 1) == 1

    z_ref[0] = jnp.where(side_of(0), q * jnp.exp(logf), k).astype(BF16)
    z_ref[1] = (jnp.exp(ex_ref[t:2 * t, :]) * jnp.where(side_of(1), q_s[...], k_s[...])).astype(BF16)
    for b in range(2, H_LEVELS):
        half = 1 << b
        d = jnp.concatenate([ex_ref[g0:g0 + 2 * half, :] - ex_ref[g0 + half - 1:g0 + half, :]
                             for g0 in range(0, t, 2 * half)], axis=0)
        z_ref[b] = (_exp_neg_abs(d) * jnp.where(side_of(b), q_s[...], k_s[...])).astype(BF16)
    cum = ex_ref[0:t, :]
    qd_ref[...] = (q_s[...] * jnp.exp(cum)).astype(BF16)
    kd_ref[...] = (k_s[...] * jnp.exp(ex_ref[t - 1:t, :] - cum)).astype(BF16)

    for h in range(H_HEADS):
        cs = slice(h * H_KDIM, (h + 1) * H_KDIM)
        scores = None
        for b in range(H_LEVELS):
            z = z_ref[b, :, cs]
            p = _dot_nt(z, z) * pair_ref[b * t:(b + 1) * t, :]
            scores = p if scores is None else scores + p
        sc_ref[h] = scores.astype(BF16)

    for h in range(H_HEADS):
        cs = slice(h * H_KDIM, (h + 1) * H_KDIM)
        v = i_ref[:, cs]
        st = st_ref[h]
        o = (_dot(sc_ref[h], v.astype(BF16))
             + jnp.sum(q_s[:, cs] * k_s[:, cs], axis=-1, keepdims=True) * v
             + _dot_nt(qd_ref[:, cs], st.astype(BF16)))
        y = _rms(o) * nw_ref[:, cs] * _sigmoid(g_ref[:, cs])
        y_ref[:, cs] = y.astype(y_ref.dtype)
        st_ref[h] = st * jnp.exp(ex_ref[t - 1:t, cs]) + _dot(v.T.astype(BF16), kd_ref[:, cs])

    @pl.when(c == pl.num_programs(1) - 1)
    def _():
        for h in range(H_HEADS):
            sout_ref[h] = st_ref[h].T


def _hgrn_seq(proj3, row_blk0, nblk, lead, lb, nw, tables, s0, out_dtype, stack=None):
    bsz = proj3.shape[0]
    t = SEQ_BLOCK
    colspec = lambda cblk: pl.BlockSpec((None, t, D_MODEL), lambda b, c, cblk=cblk: (b, row_blk0 + c, cblk))
    full = lambda a: pl.BlockSpec(a.shape, lambda b, c: (0,) * a.ndim)
    bcast = lambda a: pl.BlockSpec((None,) + a.shape[1:], lambda b, c: (0,) * a.ndim)
    inputs = [proj3, proj3, proj3, proj3, lb, nw, *tables, s0]
    st_shape, st_spec = _seq_state_out(stack, bsz, (H_HEADS, H_KDIM, H_VDIM))
    body, x_specs, x_in, aliases = _alias_prev(functools.partial(_hgrn_seq_kernel, lead), len(inputs),
                                               stack and stack[1], (1,))
    return pl.pallas_call(
        body,
        grid=(bsz, nblk),
        in_specs=[colspec(COL_HQ), colspec(COL_HF), colspec(COL_HI), colspec(COL_HG),
                  full(lb), full(nw), full(tables[0]), full(tables[1]), bcast(s0)] + x_specs,
        out_specs=[pl.BlockSpec((None, t, D_MODEL), lambda b, c: (b, c, 0)), st_spec],
        out_shape=[jax.ShapeDtypeStruct((bsz, nblk * t, D_MODEL), out_dtype), st_shape],
        input_output_aliases=aliases,
        scratch_shapes=[pltpu.VMEM((H_HEADS, H_VDIM, H_KDIM), F32),
                        pltpu.VMEM((2 * t, D_MODEL), F32),
                        pltpu.VMEM((t, D_MODEL), F32), pltpu.VMEM((t, D_MODEL), F32),
                        pltpu.VMEM((H_LEVELS, t, D_MODEL), BF16),
                        pltpu.VMEM((t, D_MODEL), BF16), pltpu.VMEM((t, D_MODEL), BF16),
                        pltpu.VMEM((H_HEADS, t, t), BF16)],
        compiler_params=_cparams(("parallel", "arbitrary")),
        name="hgrn_seq",
    )(*inputs, *x_in)


def _log_gamma(h):
    return math.log(1.0 - 2.0 ** (-5.0 - h))


def _rotary(x, cos, sin):
    x1, x2 = x[:, :R_HALF], x[:, R_HALF:]
    return jnp.concatenate([x1 * cos - x2 * sin, x2 * cos + x1 * sin], axis=1)


def _ret_seq_kernel(lead, t, q_ref, k_ref, v_ref, g_ref, cos_ref, sin_ref, s0_ref, y_ref, sout_ref, st_ref):
    c = pl.program_id(1)

    @pl.when(c == 0)
    def _():
        st_ref[...] = s0_ref[...]

    cos, sin = cos_ref[...], sin_ref[...]
    ti, si = _iota((t, t), 0), _iota((t, t), 1)
    tril = si <= ti
    dpos = (ti - si).astype(F32)
    tcol = _iota((t, 1), 0).astype(F32)
    for h in range(R_HEADS):
        lg = _log_gamma(h)
        ks = slice(h * R_KDIM, (h + 1) * R_KDIM)
        vs = slice(h * R_VDIM, (h + 1) * R_VDIM)
        qh = _rotary(q_ref[:, ks], cos, sin)
        kh = _rotary(k_ref[:, ks], cos, sin) * (R_KDIM ** -0.5)
        if lead:
            kh = jnp.where(_iota((t, 1), 0) >= lead, kh, 0.0)
        qb, kb, vb = qh.astype(BF16), kh.astype(BF16), v_ref[:, vs].astype(BF16)
        decay = jnp.where(tril, jnp.exp(jnp.where(tril, dpos, 0.0) * lg), 0.0)
        scores = _dot_nt(qb, kb) * decay
        st = st_ref[h]
        o = _dot(scores.astype(BF16), vb) + _dot(qb, st.astype(BF16)) * jnp.exp((tcol + 1.0) * lg)
        kdec = (kh * jnp.exp((t - 1.0 - tcol) * lg)).T.astype(BF16)
        st_ref[h] = st * math.exp(t * lg) + _dot(kdec, vb)
        y_ref[:, vs] = (_rms(o) * _silu(g_ref[:, vs])).astype(y_ref.dtype)

    @pl.when(c == pl.num_programs(1) - 1)
    def _():
        sout_ref[...] = st_ref[...]


def _ret_seq(proj3, t, row_blk0, nblk, lead, cos, sin, s0, out_dtype, stack=None):
    bsz = proj3.shape[0]
    wide = R_HEADS * R_VDIM
    bcast = lambda a: pl.BlockSpec((None,) + a.shape[1:], lambda b, c: (0,) * a.ndim)
    inputs = [proj3, proj3, proj3, proj3, cos, sin, s0]
    st_shape, st_spec = _seq_state_out(stack, bsz, (R_HEADS, R_KDIM, R_VDIM))
    body, x_specs, x_in, aliases = _alias_prev(functools.partial(_ret_seq_kernel, lead, t), len(inputs),
                                               stack and stack[1], (1,))
    return pl.pallas_call(
        body,
        grid=(bsz, nblk),
        in_specs=[pl.BlockSpec((None, t, D_MODEL), lambda b, c: (b, row_blk0 + c, COL_RQ)),
                  pl.BlockSpec((None, t, D_MODEL), lambda b, c: (b, row_blk0 + c, COL_RK)),
                  pl.BlockSpec((None, t, wide), lambda b, c: (b, row_blk0 + c, COL_RV)),
                  pl.BlockSpec((None, t, wide), lambda b, c: (b, row_blk0 + c, COL_RG)),
                  pl.BlockSpec((t, R_HALF), lambda b, c: (c, 0)),
                  pl.BlockSpec((t, R_HALF), lambda b, c: (c, 0)),
                  bcast(s0)] + x_specs,
        out_specs=[pl.BlockSpec((None, t, wide), lambda b, c: (b, c, 0)), st_spec],
        out_shape=[jax.ShapeDtypeStruct((bsz, nblk * t, wide), out_dtype), st_shape],
        scratch_shapes=[pltpu.VMEM((R_HEADS, R_KDIM, R_VDIM), F32)],
        input_output_aliases=aliases,
        compiler_params=_cparams(("parallel", "arbitrary")),
        name="ret_seq",
    )(*inputs, *x_in)


def _col_pieces(x):
    return jnp.concatenate(_split3(x.T), axis=1)


def _pick_col(n):
    r = _iota((3 * LANES, LANES), 0) & (LANES - 1)
    return jnp.where(r == n, 1.0, 0.0).astype(BF16)


def _pick_col_pair(n):
    r = _iota((3 * LANES, 2 * LANES), 0) & (LANES - 1)
    want = n + jnp.right_shift(_iota((3 * LANES, 2 * LANES), 1), int(math.log2(LANES)))
    return jnp.where(r == want, 1.0, 0.0).astype(BF16)


def _step_prep_kernel(xs_ref, bc_ref, dt_ref, hq_ref, hf_ref, rq_ref, rk_ref, conv_ref,
                      cw_ref, cb_ref, dtb_ref, a_ref, ex_ref, lb_ref, cos_ref, sin_ref,
                      xs_o, bc_o, xdt_o, edec_o, conv_o, hq_o, hef_o, hk_o, rq_o, rk_o):
    raw = jnp.concatenate([xs_ref[...], bc_ref[...]], axis=1)
    buf = conv_ref[...]
    acc = cb_ref[...] + cw_ref[3:4, :] * raw
    for k in range(M_CONV - 1):
        acc += cw_ref[k:k + 1, :] * buf[:, k * M_CONV_DIM:(k + 1) * M_CONV_DIM]
    conv_o[:, 0:2 * M_CONV_DIM] = buf[:, M_CONV_DIM:]
    conv_o[:, 2 * M_CONV_DIM:] = raw
    xbc = _silu(acc)
    xs = xbc[:, :M_INNER]
    xs_o[...] = xs
    bc_o[...] = xbc[:, M_INNER:]
    dt = _softplus(dt_ref[...] + dtb_ref[...])
    ex = ex_ref[...]
    xdt_o[...] = _col_pieces(xs * _sel_right(dt, ex))
    edec_o[...] = _col_pieces(_sel_right(jnp.exp(dt * a_ref[...]), ex))
    logf, k = _hgrn_gates(hf_ref[...], lb_ref[...])
    hq_o[...] = hq_ref[...] * (H_KDIM ** -0.5)
    hef_o[...] = _col_pieces(jnp.exp(logf))
    hk_o[...] = _col_pieces(k)
    cos, sin = cos_ref[...], sin_ref[...]
    rq = jnp.concatenate([_rotary(rq_ref[:, h * R_KDIM:(h + 1) * R_KDIM], cos, sin) for h in range(R_HEADS)], axis=1)
    rk = jnp.concatenate([_rotary(rk_ref[:, h * R_KDIM:(h + 1) * R_KDIM], cos, sin) for h in range(R_HEADS)], axis=1)
    rq_o[...] = _col_pieces(rq)
    rk_o[...] = _col_pieces(rk * (R_KDIM ** -0.5))


def _step_prep(proj_s, dt_s, conv_flat, p, cos, sin):
    nb = conv_flat.shape[0]
    col = lambda cblk: pl.BlockSpec((nb, D_MODEL), lambda i, cblk=cblk: (0, cblk))
    full = lambda a: pl.BlockSpec(a.shape, lambda i: (0,) * a.ndim)
    params = [p["conv_w"], p["conv_b"], p["dt_bias"], p["a_neg"], p["expand"], p["lb"], cos, sin]
    assert nb == LANES
    rows = lambda w: jax.ShapeDtypeStruct((nb, w), F32)
    cols = lambda w: jax.ShapeDtypeStruct((w, 3 * nb), BF16)
    shapes = [rows(M_INNER), rows(M_BC), cols(M_INNER), cols(M_INNER), rows((M_CONV - 1) * M_CONV_DIM),
              rows(D_MODEL), cols(D_MODEL), cols(D_MODEL), cols(D_MODEL), cols(D_MODEL)]
    return pl.pallas_call(
        _step_prep_kernel,
        grid=(1,),
        in_specs=[col(COL_XS), col(COL_BC), pl.BlockSpec((nb, LANES), lambda i: (0, 0)),
                  col(COL_HQ), col(COL_HF), col(COL_RQ), col(COL_RK), full(conv_flat)]
                 + [full(a) for a in params],
        out_specs=[pl.BlockSpec(s.shape, lambda i: (0, 0)) for s in shapes],
        out_shape=shapes,
        compiler_params=_cparams(("arbitrary",)),
        name="step_prep",
    )(proj_s, proj_s, dt_s, proj_s, proj_s, proj_s, proj_s, conv_flat, *params)


def _ssd_step_kernel(s_ref, xdt_ref, edec_ref, bc_ref, xs_ref, z_ref, dsk_ref, nw_ref, so_ref, y_ref,
                     yt_ref, xdt_b, edec_b):
    yt_ref[...] = jnp.zeros(yt_ref.shape, F32)
    for i in range(STEP_BT):
        ls = slice((i % 2) * LANES, (i % 2 + 1) * LANES)
        if i % 2 == 0:
            pick = _pick_col_pair(pl.program_id(0) * STEP_BT + i)
            xdt_b[...] = _dot(xdt_ref[...], pick)
            edec_b[...] = _dot(edec_ref[...], pick)
        for g in range(M_GROUPS):
            gs = slice(g * M_GW, (g + 1) * M_GW)
            hs = slice(M_HPG * g, M_HPG * (g + 1))
            st = s_ref[i, hs].reshape(M_GW, M_STATE)
            brow = bc_ref[i:i + 1, g * M_STATE:(g + 1) * M_STATE]
            crow = bc_ref[i:i + 1, M_BC // 2 + g * M_STATE:M_BC // 2 + (g + 1) * M_STATE]
            new = st * edec_b[gs, ls] + xdt_b[gs, ls] * brow
            so_ref[i, hs] = new.reshape(M_HPG, M_HEADDIM, M_STATE)
            yt_ref[gs, i:i + 1] = jnp.sum(new * crow, axis=-1, keepdims=True)
    y = yt_ref[...].T[0:STEP_BT, :]
    xs = xs_ref[...]
    y = (y + dsk_ref[...] * xs) * _silu(z_ref[...])
    for g in range(M_GROUPS):
        gs = slice(g * M_GW, (g + 1) * M_GW)
        y_ref[:, gs] = _rms(y[:, gs]) * nw_ref[:, gs]


def _ssd_step(state, layer, prev, xdt_c, edec_c, bc, xs, proj_s, p):
    nb = xs.shape[0]
    bt = STEP_BT
    full = lambda a: pl.BlockSpec(a.shape, lambda j: (0,) * a.ndim)
    sspec = pl.BlockSpec((None, bt, M_HEADS, M_HEADDIM, M_STATE), lambda j: (layer, j, 0, 0, 0))
    inputs = [state, xdt_c, edec_c, bc, xs, proj_s, p["d_skip"], p["m_norm_w"]]
    body, x_specs, x_in, aliases = _alias_prev(_ssd_step_kernel, len(inputs),
                                               None if prev is None else [prev], (0,))
    return pl.pallas_call(
        body,
        grid=(nb // bt,),
        in_specs=[sspec, full(xdt_c), full(edec_c),
                  pl.BlockSpec((bt, M_BC), lambda j: (j, 0)),
                  pl.BlockSpec((bt, M_INNER), lambda j: (j, 0)),
                  pl.BlockSpec((bt, D_MODEL), lambda j: (j, COL_Z)),
                  full(p["d_skip"]), full(p["m_norm_w"])] + x_specs,
        out_specs=[sspec, pl.BlockSpec((bt, M_INNER), lambda j: (j, 0))],
        out_shape=[jax.ShapeDtypeStruct(state.shape, F32),
                   jax.ShapeDtypeStruct((nb, M_INNER), F32)],
        scratch_shapes=[pltpu.VMEM((M_INNER, LANES), F32)] + [pltpu.VMEM((M_INNER, 2 * LANES), F32)] * 2,
        input_output_aliases=aliases,
        compiler_params=_cparams(("parallel",)),
        name="ssd_step",
    )(*inputs, *x_in)


def _cols(x):
    nb, w = x.shape
    return jnp.transpose(x.reshape(nb // STEP_BT, STEP_BT, w), (0, 2, 1))


def _hgrn_step_kernel(s_ref, q_ref, ef_ref, k_ref, v_ref, g_ref, nw_ref, so_ref, y_ref, ef_b, k_b):
    for i in range(STEP_BT):
        ls = slice((i % 2) * LANES, (i % 2 + 1) * LANES)
        if i % 2 == 0:
            pick = _pick_col_pair(pl.program_id(0) * STEP_BT + i)
            ef_b[...] = _dot(ef_ref[...], pick)
            k_b[...] = _dot(k_ref[...], pick)
        for h in range(H_HEADS):
            cs = slice(h * H_KDIM, (h + 1) * H_KDIM)
            new = s_ref[i, h] * ef_b[cs, ls] + k_b[cs, ls] * v_ref[i:i + 1, cs]
            so_ref[i, h] = new
            y_ref[i:i + 1, cs] = jnp.sum(new * q_ref[cs, i:i + 1], axis=0, keepdims=True)
    for h in range(H_HEADS):
        cs = slice(h * H_KDIM, (h + 1) * H_KDIM)
        y_ref[:, cs] = _rms(y_ref[:, cs]) * nw_ref[:, cs] * _sigmoid(g_ref[:, cs])


def _hgrn_step(state, layer, prev, q_c, ef_c, k_c, proj_s, nw):
    nb = state.shape[1]
    bt = STEP_BT
    cspec = pl.BlockSpec(ef_c.shape, lambda j: (0, 0))
    qspec = pl.BlockSpec((None, D_MODEL, bt), lambda j: (j, 0, 0))
    sspec = pl.BlockSpec((None, bt, H_HEADS, H_KDIM, H_VDIM), lambda j: (layer, j, 0, 0, 0))
    inputs = [state, q_c, ef_c, k_c, proj_s, proj_s, nw]
    body, x_specs, x_in, aliases = _alias_prev(_hgrn_step_kernel, len(inputs),
                                               None if prev is None else [prev], (0,))
    return pl.pallas_call(
        body,
        grid=(nb // bt,),
        in_specs=[sspec, qspec, cspec, cspec,
                  pl.BlockSpec((bt, D_MODEL), lambda j: (j, COL_HI)),
                  pl.BlockSpec((bt, D_MODEL), lambda j: (j, COL_HG)),
                  pl.BlockSpec(nw.shape, lambda j: (0, 0))] + x_specs,
        out_specs=[sspec, pl.BlockSpec((bt, D_MODEL), lambda j: (j, 0))],
        out_shape=[jax.ShapeDtypeStruct(state.shape, F32),
                   jax.ShapeDtypeStruct((nb, D_MODEL), F32)],
        scratch_shapes=[pltpu.VMEM((D_MODEL, 2 * LANES), F32)] * 2,
        input_output_aliases=aliases,
        compiler_params=_cparams(("parallel",)),
        name="hgrn_step",
    )(*inputs, *x_in)


def _ret_step_kernel(s_ref, q_ref, k_ref, v_ref, g_ref, gam_ref, so_ref, y_ref):
    gam = gam_ref[...]
    head_rows = pl.ds(pl.multiple_of(pl.program_id(1) * R_KDIM, R_KDIM), R_KDIM)
    for i in range(STEP_BT):
        pick = _pick_col(pl.program_id(0) * STEP_BT + i)
        q_b = _dot(q_ref[head_rows, :], pick)
        k_b = _dot(k_ref[head_rows, :], pick)
        for c in range(0, R_VDIM, LANES):
            cs = slice(c, c + LANES)
            new = s_ref[i, :, cs] * gam[:, cs] + k_b * v_ref[i:i + 1, cs]
            so_ref[i, :, cs] = new
            y_ref[i:i + 1, cs] = jnp.sum(new * q_b, axis=0, keepdims=True)
    y_ref[...] = _rms(y_ref[...]) * _silu(g_ref[...])


def _ret_step(state, layer, prev, q_c, k_c, proj_s, gam):
    nb = state.shape[1]
    bt = STEP_BT
    cspec = pl.BlockSpec(q_c.shape, lambda j, h: (0, 0))
    sspec = pl.BlockSpec((None, bt, None, R_KDIM, R_VDIM), lambda j, h: (layer, j, h, 0, 0))
    inputs = [state, q_c, k_c, proj_s, proj_s, gam]
    body, x_specs, x_in, aliases = _alias_prev(_ret_step_kernel, len(inputs),
                                               None if prev is None else [prev], (0,))
    return pl.pallas_call(
        body,
        grid=(nb // bt, R_HEADS),
        in_specs=[sspec, cspec, cspec,
                  pl.BlockSpec((bt, R_VDIM), lambda j, h: (j, h)),
                  pl.BlockSpec((bt, R_VDIM), lambda j, h: (j, R_HEADS + h)),
                  pl.BlockSpec((None, 1, R_VDIM), lambda j, h: (h, 0, 0))] + x_specs,
        out_specs=[sspec, pl.BlockSpec((bt, R_VDIM), lambda j, h: (j, h))],
        out_shape=[jax.ShapeDtypeStruct(state.shape, F32),
                   jax.ShapeDtypeStruct((nb, R_HEADS * R_VDIM), F32)],
        input_output_aliases=aliases,
        compiler_params=_cparams(("parallel", "parallel")),
        name="ret_step",
    )(*inputs, *x_in)


def _rope_tables(positions):
    inv_freq = 1.0 / (ROPE_BASE ** jnp.linspace(0.0, 1.0, R_HALF, dtype=F32))
    ang = positions[:, None] * inv_freq[None, :]
    return jnp.cos(ang), jnp.sin(ang)


def _per_channel(v):
    return jnp.repeat(v.astype(F32), M_HEADDIM).reshape(1, M_INNER)


def _pad_lanes(v):
    return jnp.pad(v.astype(F32), (0, LANES - v.shape[0])).reshape(1, LANES)


def kernel(x_prompt, x_sample, state_ssm, state_conv, state_hgrn, state_ret, meta_tokens, ln_in_g, ln_in_b,
           w_in, conv_w, conv_b, dt_bias, a_log, d_skip, m_norm_w, hgrn_lb_logits, h_norm_w, w_br_m, w_br_h,
           w_br_r, w_out, ln1_g, ln1_b, w_ffn_in, w_ffn_out, ln2_g, ln2_b):
    bp, sp = x_prompt.shape[0], x_prompt.shape[1]
    nb = x_sample.shape[0]
    assert x_sample.shape[1] == 1 and nb == SMALL_ROWS - SEQ_BLOCK and nb % STEP_BT == 0
    assert sp % RET_BLOCK == 0 and meta_tokens.shape[0] == N_META

    wt = jnp.swapaxes(w_in, 1, 2).astype(BF16)
    wm_b, wh_b, wr_b, wo_b = (w.astype(BF16) for w in (w_br_m, w_br_h, w_br_r, w_out))
    wfi_b, wfo_b = w_ffn_in.astype(BF16), w_ffn_out.astype(BF16)
    ln1 = (ln1_g.reshape(DEPTH, 1, D_MODEL), ln1_b.reshape(DEPTH, 1, D_MODEL))
    ln2 = (ln2_g.reshape(DEPTH, 1, D_MODEL), ln2_b.reshape(DEPTH, 1, D_MODEL))
    lb_cum = jnp.cumsum(jax.nn.softmax(hgrn_lb_logits.astype(F32), axis=0), axis=0)
    lbs = lb_cum - lb_cum[0]
    expand = (np.arange(LANES)[:, None] == (np.arange(M_INNER)[None, :] // M_HEADDIM)).astype(np.float32)
    expand = jnp.asarray(expand, BF16)
    h_sums, h_pair = _hgrn_tables()
    h_tables = (jnp.asarray(h_sums, BF16), jnp.asarray(h_pair, F32))
    gam = jnp.asarray(np.broadcast_to(
        np.array([1.0 - 2.0 ** (-5.0 - h) for h in range(R_HEADS)], np.float32)[:, None, None],
        (R_HEADS, 1, R_VDIM)))

    pos_real = jnp.arange(N_META, N_META + sp, dtype=F32)
    pos_meta = jnp.maximum(jnp.arange(SEQ_BLOCK, dtype=F32) - META_LEAD, 0.0)
    pos_samp = jnp.full((nb,), float(PAST_LEN), F32)
    cos_r, sin_r = _rope_tables(pos_real)
    cos_m, sin_m = _rope_tables(pos_meta)
    cos_s, sin_s = _rope_tables(pos_samp)

    x_real = _layer_norm_rows(x_prompt.reshape(bp * sp, D_MODEL), ln_in_g, ln_in_b, 512)
    small_in = jnp.concatenate([x_sample.reshape(nb, D_MODEL),
                                jnp.zeros((META_LEAD, D_MODEL), F32), meta_tokens.astype(F32)], axis=0)
    x_small = _layer_norm_rows(small_in, ln_in_g, ln_in_b, SMALL_ROWS)

    z_ssm = jnp.zeros((1, M_HEADS, M_HEADDIM, M_STATE), F32)
    z_conv = jnp.zeros((1, 8, M_CONV_DIM), F32)
    z_hgrn = jnp.zeros((1, H_HEADS, H_KDIM, H_VDIM), F32)
    z_ret = jnp.zeros((1, R_HEADS, R_KDIM, R_VDIM), F32)
    ssm_p = conv_p = hgrn_p = ret_p = ssm_s = hgrn_s = ret_s = None
    conv_s = []
    for l in range(DEPTH):
        p = dict(conv_w=conv_w[l], conv_b=conv_b[l].reshape(1, -1), dt_bias=_pad_lanes(dt_bias[l]),
                 a_neg=_pad_lanes(-jnp.exp(a_log[l].astype(F32))), d_skip=_per_channel(d_skip[l]),
                 m_norm_w=m_norm_w[l].reshape(1, -1), expand=expand, lb=lbs[l].reshape(1, -1))
        hnw = h_norm_w[l].reshape(1, -1)

        proj_r = _proj(x_real, wt, l, 1024, "proj_real")
        dt_r = _proj_dt(x_real, wt, l, 1024, "proj_dt_real")
        proj_s = _proj(x_small, wt, l, SMALL_ROWS, "proj_small")
        dt_s = _proj_dt(x_small, wt, l, SMALL_ROWS, "proj_dt_small")
        proj_r3 = proj_r.reshape(bp, sp, PROJ_COLS)
        dt_r3 = dt_r.reshape(bp, sp, LANES)
        proj_s3 = proj_s.reshape(1, SMALL_ROWS, PROJ_COLS)
        dt_s3 = dt_s.reshape(1, SMALL_ROWS, LANES)

        ym_m, ssm_m, conv_m = _ssd_seq(proj_s3, dt_s3, 1, 1, META_LEAD, p, z_ssm, z_conv, F32)
        yh_m, hgrn_m = _hgrn_seq(proj_s3, 1, 1, META_LEAD, p["lb"], hnw, h_tables, z_hgrn, F32)
        yr_m, ret_m = _ret_seq(proj_s3, SEQ_BLOCK, 1, 1, META_LEAD, cos_m, sin_m, z_ret, F32)

        conv0 = jnp.pad(conv_m, ((0, 0), (8 - (M_CONV - 1), 0), (0, 0)))
        ym_r, ssm_p, conv_p = _ssd_seq(proj_r3, dt_r3, 0, sp // SEQ_BLOCK, 0, p, ssm_m, conv0, BF16,
                                       stack=(l, None if l == 0 else [ssm_p, conv_p]))
        yh_r, hgrn_p = _hgrn_seq(proj_r3, 0, sp // SEQ_BLOCK, 0, p["lb"], hnw, h_tables, hgrn_m, BF16,
                                 stack=(l, None if l == 0 else [hgrn_p]))
        yr_r, ret_p = _ret_seq(proj_r3, RET_BLOCK, 0, sp // RET_BLOCK, 0, cos_r, sin_r, ret_m, BF16,
                               stack=(l, None if l == 0 else [ret_p]))

        conv_flat = state_conv[l].reshape(nb, (M_CONV - 1) * M_CONV_DIM)
        (xs_s, bc_s, xdt_s, edec_s, conv_new, hq_s, hef_s, hk_s, rq_s, rk_s) = _step_prep(
            proj_s, dt_s, conv_flat, p, cos_s, sin_s)
        ssm_s, ym_s = _ssd_step(state_ssm, l, ssm_s, xdt_s, edec_s, bc_s, xs_s, proj_s, p)
        hgrn_s, yh_s = _hgrn_step(state_hgrn, l, hgrn_s, _cols(hq_s), hef_s, hk_s, proj_s, hnw)
        ret_s, yr_s = _ret_step(state_ret, l, ret_s, rq_s, rk_s, proj_s, gam)
        conv_s.append(conv_new.reshape(nb, M_CONV - 1, M_CONV_DIM))

        ym_small = jnp.concatenate([ym_s, ym_m[0]], axis=0)
        yh_small = jnp.concatenate([yh_s, yh_m[0]], axis=0)
        yr_small = jnp.concatenate([yr_s, yr_m[0]], axis=0)
        x_real = _mix(x_real, ym_r.reshape(bp * sp, -1), yh_r.reshape(bp * sp, -1), yr_r.reshape(bp * sp, -1),
                      proj_r, l, wm_b, wh_b, wr_b, wo_b, *ln1, 256)
        x_small = _mix(x_small, ym_small, yh_small, yr_small, proj_s, l, wm_b, wh_b, wr_b, wo_b, *ln1, SMALL_ROWS)
        x_real = _ffn(x_real, l, wfi_b, wfo_b, *ln2, 512)
        x_small = _ffn(x_small, l, wfi_b, wfo_b, *ln2, SMALL_ROWS)

    return (x_real.reshape(bp, sp, D_MODEL), x_small[:nb].reshape(nb, 1, D_MODEL),
            ssm_p, conv_p, hgrn_p, ret_p, ssm_s, jnp.stack(conv_s), hgrn_s, ret_s)
```

```python
import functools
import math

import numpy as np
import jax
import jax.numpy as jnp
from jax import lax
from jax.experimental import pallas as pl
from jax.experimental.pallas import tpu as pltpu

F32 = jnp.float32
BF16 = jnp.bfloat16

D_MODEL = 1024
DEPTH = 2
N_META = 16
M_INNER = D_MODEL
M_HEADDIM = 64
M_HEADS = M_INNER // M_HEADDIM
M_GROUPS = 4
M_HPG = M_HEADS // M_GROUPS
M_STATE = 128
M_CONV = 4
M_BC = 2 * M_GROUPS * M_STATE
M_CONV_DIM = M_INNER + M_BC
M_GW = M_INNER // M_GROUPS
H_KDIM = 128
H_HEADS = D_MODEL // H_KDIM
H_VDIM = 128
R_HEADS = 4
R_KDIM = D_MODEL // R_HEADS
R_VDIM = 2 * R_KDIM
R_HALF = R_KDIM // 2
ROPE_BASE = 10000.0
D_FF = ((8 * D_MODEL // 3 + 255) // 256) * 256
FF_CHUNK = 256
DN_ALPHA = (2 * DEPTH) ** 0.25
PAST_LEN = 16384

LANES = 128
SEQ_BLOCK = 128
RET_BLOCK = 256
SMALL_ROWS = 256
META_LEAD = SEQ_BLOCK - N_META
STEP_BT = 8
VMEM_LIMIT = 56 * 1024 * 1024

COL_RV, COL_RG = 0, 1
COL_Z, COL_XS, COL_BC, COL_HQ, COL_HF, COL_HI, COL_HG, COL_RQ, COL_RK, COL_GM, COL_GH, COL_GR = range(4, 16)
PROJ_COLS = 16 * 1024

_ORIG_SPLITS = (M_INNER, M_CONV_DIM, M_HEADS, 1024, 1024, 1024, 1024, 1024, 1024, 2048, 2048, 3072)
_ORIG_OFF = np.concatenate([[0], np.cumsum(_ORIG_SPLITS)]).tolist()


def _cparams(sem):
    return pltpu.CompilerParams(dimension_semantics=sem, vmem_limit_bytes=VMEM_LIMIT)


def _sigmoid(x):
    return 1.0 / (1.0 + jnp.exp(-x))


def _silu(x):
    return x * _sigmoid(x)


def _softplus(x):
    return jnp.maximum(x, 0.0) + jnp.log1p(jnp.exp(-jnp.abs(x)))


def _layer_norm(x, g, b):
    mu = jnp.mean(x, axis=-1, keepdims=True)
    xc = x - mu
    var = jnp.mean(xc * xc, axis=-1, keepdims=True)
    return xc * lax.rsqrt(var + 1e-5) * g + b


def _rms(x):
    return x * lax.rsqrt(jnp.mean(x * x, axis=-1, keepdims=True) + 1e-6)


def _split3(x):
    hi = x.astype(BF16)
    r = x - hi.astype(F32)
    mid = r.astype(BF16)
    lo = (r - mid.astype(F32)).astype(BF16)
    return hi, mid, lo


def _dot(a, b):
    return jnp.dot(a, b, preferred_element_type=F32)


def _dot_nt(a, b):
    return lax.dot_general(a, b, (((1,), (1,)), ((), ())), preferred_element_type=F32)


def _sel_right(x, m3):
    return _dot(jnp.concatenate(_split3(x), axis=1), m3)


def _sel_left(m3, x):
    return _dot(m3, jnp.concatenate(_split3(x), axis=0))


def _iota(shape, dim):
    return lax.broadcasted_iota(jnp.int32, shape, dim)


def _proj_kernel(ln, x_ref, *refs):
    if ln:
        g_ref, b_ref, w_ref, o_ref, xn_ref, xb_ref = refs
    else:
        w_ref, o_ref, xb_ref = refs

    @pl.when(pl.program_id(1) == 0)
    def _():
        x = x_ref[...]
        if ln:
            x = _layer_norm(x, g_ref[...], b_ref[...])
            xn_ref[...] = x
        xb_ref[...] = x.astype(BF16)

    o_ref[...] = _dot_nt(xb_ref[...], w_ref[0])


def _layer_block(a, layer):
    return pl.BlockSpec((None,) + a.shape[1:], lambda *_: (layer,) + (0,) * (a.ndim - 1))


_W_RUNS = ((0, _ORIG_OFF[9]), (4096, _ORIG_OFF[0]), (7168, _ORIG_OFF[3]), (13312, _ORIG_OFF[11]))
PROJ_TN = 1024
PROJ_TM = 2048
PROJ_TM_LN = 1024


def _orig_col(j):
    c = j * PROJ_TN
    off = c - _W_RUNS[0][0] + _W_RUNS[0][1]
    for new0, orig0 in _W_RUNS[1:]:
        off = jnp.where(c >= new0, c - new0 + orig0, off)
    return pl.multiple_of(off, M_HEADS)


def _proj(x, wt, layer, tm, name, ln=None):
    m = x.shape[0]
    tm = min(tm, m)
    row = pl.BlockSpec((tm, D_MODEL), lambda i, j: (i, 0))
    vec = pl.BlockSpec((1, D_MODEL), lambda i, j: (0, 0))
    out_specs = [pl.BlockSpec((tm, PROJ_TN), lambda i, j: (i, j))]
    out_shape = [jax.ShapeDtypeStruct((m, PROJ_COLS), F32)]
    if ln:
        out_specs.append(row)
        out_shape.append(jax.ShapeDtypeStruct((m, D_MODEL), F32))
    res = pl.pallas_call(
        functools.partial(_proj_kernel, bool(ln)),
        grid=(m // tm, PROJ_COLS // PROJ_TN),
        in_specs=[row] + ([vec, vec] if ln else [])
                 + [pl.BlockSpec((pl.Element(1), pl.Element(PROJ_TN), pl.Element(D_MODEL)),
                                 lambda i, j: (layer, _orig_col(j), 0))],
        out_specs=out_specs,
        out_shape=out_shape,
        scratch_shapes=[pltpu.VMEM((tm, D_MODEL), BF16)],
        compiler_params=_cparams(("parallel", "arbitrary")),
        name=name,
    )(x, *([a.reshape(1, D_MODEL) for a in ln] if ln else []), wt)
    return res if ln else res[0]


def _dt_weight_spec(layer):
    assert _ORIG_OFF[2] % LANES == 0
    return pl.BlockSpec((None, LANES, D_MODEL), lambda *_: (layer, _ORIG_OFF[2] // LANES, 0))


def _mix_kernel(x_ref, ym_ref, yh_ref, yr_ref, gm_ref, gh_ref, gr_ref,
                wm_ref, wh_ref, wr_ref, wo_ref, g_ref, b_ref, o_ref):
    mixed = _sigmoid(gm_ref[...]) * _dot(ym_ref[...].astype(BF16), wm_ref[...])
    mixed += _sigmoid(gh_ref[...]) * _dot(yh_ref[...].astype(BF16), wh_ref[...])
    mixed += _sigmoid(gr_ref[...]) * _dot(yr_ref[...].astype(BF16), wr_ref[...])
    h = _dot(mixed.astype(BF16), wo_ref[...])
    o_ref[...] = _layer_norm(DN_ALPHA * x_ref[...] + h, g_ref[...], b_ref[...])


def _mix(x, ym, yh, yr, proj, layer, wm, wh, wr, wo, g2, b2, tm):
    m = x.shape[0]
    tm = min(tm, m)
    row = lambda w: pl.BlockSpec((tm, w), lambda i: (i, 0))
    col = lambda c: pl.BlockSpec((tm, D_MODEL), lambda i, c=c: (i, c))
    full = lambda a: _layer_block(a, layer)
    return pl.pallas_call(
        _mix_kernel,
        grid=(m // tm,),
        in_specs=[row(D_MODEL), row(M_INNER), row(D_MODEL), row(R_HEADS * R_VDIM),
                  col(COL_GM), col(COL_GH), col(COL_GR),
                  full(wm), full(wh), full(wr), full(wo), full(g2), full(b2)],
        out_specs=row(D_MODEL),
        out_shape=jax.ShapeDtypeStruct((m, D_MODEL), F32),
        compiler_params=_cparams(("parallel",)),
        name="mix",
    )(x, ym, yh, yr, proj, proj, proj, wm, wh, wr, wo, g2, b2)


def _ffn_kernel(x_ref, wi_ref, wo_ref, g_ref, b_ref, o_ref):
    x = x_ref[...]
    xb = x.astype(BF16)
    acc = jnp.zeros(x.shape, F32)
    for j in range(0, D_FF, FF_CHUNK):
        hg = _dot(xb, wi_ref[:, j:j + FF_CHUNK])
        hu = _dot(xb, wi_ref[:, D_FF + j:D_FF + j + FF_CHUNK])
        acc += _dot((_silu(hg) * hu).astype(BF16), wo_ref[j:j + FF_CHUNK, :])
    o_ref[...] = _layer_norm(DN_ALPHA * x + acc, g_ref[...], b_ref[...])


def _ffn(x, layer, wi, wo, g2, b2, tm):
    m = x.shape[0]
    tm = min(tm, m)
    full = lambda a: _layer_block(a, layer)
    return pl.pallas_call(
        _ffn_kernel,
        grid=(m // tm,),
        in_specs=[pl.BlockSpec((tm, D_MODEL), lambda i: (i, 0)), full(wi), full(wo), full(g2), full(b2)],
        out_specs=pl.BlockSpec((tm, D_MODEL), lambda i: (i, 0)),
        out_shape=jax.ShapeDtypeStruct((m, D_MODEL), F32),
        compiler_params=_cparams(("parallel",)),
        name="ffn",
    )(x, wi, wo, g2, b2)


def _ssd_seq_kernel(lead, z_ref, xs_ref, bc_ref, x_ref, wdt_ref, cw_ref, cb_ref, dtb_ref, a_ref, dsk_ref, nw_ref,
                    ex_ref, s0_ref, c0_ref, y_ref, sout_ref, cout_ref, ext_ref, st_ref):
    t = SEQ_BLOCK
    c = pl.program_id(1)

    @pl.when(c == 0)
    def _():
        ext_ref[0:8, :] = c0_ref[...]
        for g in range(M_GROUPS):
            st_ref[g] = s0_ref[M_HPG * g:M_HPG * (g + 1)].reshape(M_GW, M_STATE).T

    rows = _iota((t, 1), 0)
    xs_raw, bc_raw = xs_ref[...], bc_ref[...]
    if lead:
        xs_raw = jnp.where(rows >= lead, xs_raw, 0.0)
        bc_raw = jnp.where(rows >= lead, bc_raw, 0.0)
    ext_ref[8:8 + t, 0:M_INNER] = xs_raw
    ext_ref[8:8 + t, M_INNER:M_CONV_DIM] = bc_raw

    acc = cb_ref[...] + cw_ref[3:4, :] * ext_ref[8:8 + t, :]
    for k in range(M_CONV - 1):
        acc += cw_ref[k:k + 1, :] * ext_ref[5 + k:5 + k + t, :]
    tail = ext_ref[t + 5:t + 8, :]
    ext_ref[5:8, :] = tail

    @pl.when(c == pl.num_programs(1) - 1)
    def _():
        cout_ref[...] = tail

    xbc = _silu(acc)
    xs = xbc[:, 0:M_INNER]
    dt_raw = _dot_nt(x_ref[...].astype(BF16), wdt_ref[...])
    dt = _softplus(dt_raw + dtb_ref[...])
    if lead:
        dt = jnp.where(rows >= lead, dt, 0.0)
    a = dt * a_ref[...]
    ti, si = _iota((t, t), 0), _iota((t, t), 1)
    tril = si <= ti
    tri = jnp.where(tril, 1.0, 0.0).astype(BF16)
    cum = _sel_left(jnp.concatenate([tri] * 3, axis=1), a)
    ecum = jnp.exp(cum)
    tailw = jnp.exp(cum[t - 1:t, :] - cum) * dt
    cum_t, dt_t = cum.T, dt.T
    ex = ex_ref[...]
    ecum_full = _sel_right(ecum, ex)
    xw = (xs * _sel_right(tailw, ex)).astype(BF16)
    xs_b = xs.astype(BF16)
    head_of_lane = jnp.right_shift(_iota((1, M_GW), 1), int(math.log2(M_HEADDIM)))

    for g in range(M_GROUPS):
        gs = slice(g * M_GW, (g + 1) * M_GW)
        bm_f = xbc[:, M_INNER + g * M_STATE:M_INNER + (g + 1) * M_STATE]
        bm = bm_f.astype(BF16)
        cm = xbc[:, M_INNER + M_BC // 2 + g * M_STATE:M_INNER + M_BC // 2 + (g + 1) * M_STATE].astype(BF16)
        cb = _dot_nt(cm, bm)
        st = st_ref[g]
        y_g = _dot(cm, st.astype(BF16)) * ecum_full[:, gs]
        for hh in range(M_HPG):
            h = g * M_HPG + hh
            diff = cum[:, h:h + 1] - cum_t[h:h + 1, :]
            w = cb * jnp.exp(jnp.where(tril, diff, -1e30)) * dt_t[h:h + 1, :]
            x_h = jnp.where(head_of_lane == hh, xs_b[:, gs], jnp.zeros((), BF16))
            y_g = y_g + _dot(w.astype(BF16), x_h)
        st_ref[g] = st * ecum_full[t - 1:t, gs] + _dot(bm_f.T.astype(BF16), xw[:, gs])
        y_g = (y_g + dsk_ref[:, gs] * xs[:, gs]) * _silu(z_ref[:, gs])
        y_ref[:, gs] = (_rms(y_g) * nw_ref[:, gs]).astype(y_ref.dtype)

    @pl.when(c == pl.num_programs(1) - 1)
    def _():
        for g in range(M_GROUPS):
            sout_ref[M_HPG * g:M_HPG * (g + 1)] = st_ref[g].T.reshape(M_HPG, M_HEADDIM, M_STATE)


def _alias_prev(body, n_in, prevs, out_ids):
    if not prevs:
        return body, [], [], {}
    k = len(prevs)
    wrapped = lambda *refs: body(*refs[:n_in], *refs[n_in + k:])
    return (wrapped, [pl.BlockSpec(memory_space=pl.ANY)] * k, list(prevs),
            {n_in + i: o for i, o in enumerate(out_ids)})


def _seq_state_out(stack, bsz, dims):
    zeros = (0,) * len(dims)
    if stack is None:
        return (jax.ShapeDtypeStruct((bsz,) + dims, F32),
                pl.BlockSpec((None,) + dims, lambda b, c: (b,) + zeros))
    layer = stack[0]
    return (jax.ShapeDtypeStruct((DEPTH, bsz) + dims, F32),
            pl.BlockSpec((None, None) + dims, lambda b, c: (layer, b) + zeros))


def _ssd_seq(proj3, x3, wt, layer, row_blk0, nblk, lead, p, s0, c0, out_dtype, stack=None):
    bsz = proj3.shape[0]
    t = SEQ_BLOCK
    colspec = lambda cblk: pl.BlockSpec((None, t, D_MODEL), lambda b, c, cblk=cblk: (b, row_blk0 + c, cblk))
    full = lambda a: pl.BlockSpec(a.shape, lambda b, c: (0,) * a.ndim)
    bcast = lambda a: pl.BlockSpec((None,) + a.shape[1:], lambda b, c: (0,) * a.ndim)
    params = [p["conv_w"], p["conv_b"], p["dt_bias"], p["a_neg"], p["d_skip"], p["m_norm_w"], p["expand"]]
    inputs = [proj3, proj3, proj3, x3, wt, *params, s0, c0]
    st_shape, st_spec = _seq_state_out(stack, bsz, (M_HEADS, M_HEADDIM, M_STATE))
    cv_shape, cv_spec = _seq_state_out(stack, bsz, (M_CONV - 1, M_CONV_DIM))
    body, x_specs, x_in, aliases = _alias_prev(functools.partial(_ssd_seq_kernel, lead), len(inputs),
                                               stack and stack[1], (1, 2))
    return pl.pallas_call(
        body,
        grid=(bsz, nblk),
        in_specs=[colspec(COL_Z), colspec(COL_XS), colspec(COL_BC),
                  pl.BlockSpec((None, t, D_MODEL), lambda b, c: (b, row_blk0 + c, 0)), _dt_weight_spec(layer)]
                 + [full(a) for a in params] + [bcast(s0), bcast(c0)] + x_specs,
        out_specs=[pl.BlockSpec((None, t, M_INNER), lambda b, c: (b, c, 0)), st_spec, cv_spec],
        out_shape=[jax.ShapeDtypeStruct((bsz, nblk * t, M_INNER), out_dtype), st_shape, cv_shape],
        scratch_shapes=[pltpu.VMEM((t + 8, M_CONV_DIM), F32),
                        pltpu.VMEM((M_GROUPS, M_STATE, M_GW), F32)],
        input_output_aliases=aliases,
        compiler_params=_cparams(("parallel", "arbitrary")),
        name="ssd_seq",
    )(*inputs, *x_in)


def _hgrn_gates(fz, lb, lb_is_zero):
    e = jnp.exp(-jnp.abs(fz))
    r = 1.0 / (1.0 + e)
    pos = fz >= 0.0
    sig_neg = jnp.where(pos, e * r, r)
    log_sig = jnp.minimum(fz, 0.0) - jnp.log(1.0 + e)
    if lb_is_zero:
        return log_sig, sig_neg
    sig_pos = jnp.where(pos, r, e * r)
    logf = jnp.where(lb > 0.0, jnp.log(lb + (1.0 - lb) * sig_pos), log_sig)
    return logf, (1.0 - lb) * sig_neg


H_LEVELS = int(math.log2(SEQ_BLOCK))


def _hgrn_tables():
    n = SEQ_BLOCK
    t = np.arange(n)[:, None]
    u = np.arange(n)[None, :]
    pair = []
    for b in range(H_LEVELS):
        bit = ((t >> b) & 1) == 1
        pair.append(((t >> (b + 1)) == (u >> (b + 1))) & bit & (((u >> b) & 1) == 0))
    m1 = ((t >> 1) | 1) << 1
    lvl1 = np.where(((t >> 1) & 1) == 1, (u >= m1) & (u <= t), (u > t) & (u < m1))
    sums = np.concatenate([u <= t, lvl1], 0).astype(np.float32)
    return np.concatenate([sums] * 3, 1), np.concatenate(pair, 0).astype(np.float32)


def _exp_neg_abs(d):
    return jnp.exp2(jnp.abs(d) * (-1.0 / math.log(2.0)))


def _hgrn_seq_kernel(lead, lb_is_zero, q_ref, f_ref, i_ref, g_ref, lb_ref, nw_ref, sums_ref, pair_ref, s0_ref,
                     y_ref, sout_ref, st_ref, ex_ref, q_s, k_s, z_ref, qd_ref, kd_ref, sc_ref):
    t = SEQ_BLOCK
    c = pl.program_id(1)

    @pl.when(c == 0)
    def _():
        for h in range(H_HEADS):
            st_ref[h] = s0_ref[h].T

    rows = _iota((t, 1), 0)
    logf, k = _hgrn_gates(f_ref[...], lb_ref[...], lb_is_zero)
    if lead:
        logf = jnp.where(rows >= lead, logf, 0.0)
        k = jnp.where(rows >= lead, k, 0.0)
    q = q_ref[...] * (H_KDIM ** -0.5)
    q_s[...] = q
    k_s[...] = k
    ex_ref[...] = _dot(sums_ref[...], jnp.concatenate(_split3(logf), axis=0))

    def side_of(b):
        return (jnp.right_shift(rows, b) & 1) == 1

    z_ref[0] = jnp.where(side_of(0), q * jnp.exp(logf), k).astype(BF16)
    z_ref[1] = (jnp.exp(ex_ref[t:2 * t, :]) * jnp.where(side_of(1), q_s[...], k_s[...])).astype(BF16)
    for b in range(2, H_LEVELS):
        half = 1 << b
        groups = range(0, t, 2 * half)
        d = jnp.concatenate([ex_ref[g0:g0 + 2 * half, :] - ex_ref[g0 + half - 1:g0 + half, :]
                             for g0 in groups], axis=0)
        if half >= 8:
            qk = jnp.concatenate([ref[g0 + o:g0 + o + half, :] for g0 in groups
                                  for ref, o in ((k_s, 0), (q_s, half))], axis=0)
        else:
            qk = jnp.where(side_of(b), q_s[...], k_s[...])
        z_ref[b] = (_exp_neg_abs(d) * qk).astype(BF16)
    cum = ex_ref[0:t, :]
    qd_ref[...] = (q_s[...] * jnp.exp(cum)).astype(BF16)
    kd_ref[...] = (k_s[...] * jnp.exp(ex_ref[t - 1:t, :] - cum)).astype(BF16)

    for h in range(H_HEADS):
        cs = slice(h * H_KDIM, (h + 1) * H_KDIM)
        scores = None
        for b in range(H_LEVELS):
            z = z_ref[b, :, cs]
            p = _dot_nt(z, z) * pair_ref[b * t:(b + 1) * t, :]
            scores = p if scores is None else scores + p
        sc_ref[h] = scores.astype(BF16)

    for h in range(H_HEADS):
        cs = slice(h * H_KDIM, (h + 1) * H_KDIM)
        v = i_ref[:, cs]
        st = st_ref[h]
        o = (_dot(sc_ref[h], v.astype(BF16))
             + jnp.sum(q_s[:, cs] * k_s[:, cs], axis=-1, keepdims=True) * v
             + _dot_nt(qd_ref[:, cs], st.astype(BF16)))
        y = _rms(o) * nw_ref[:, cs] * _sigmoid(g_ref[:, cs])
        y_ref[:, cs] = y.astype(y_ref.dtype)
        st_ref[h] = st * jnp.exp(ex_ref[t - 1:t, cs]) + _dot(v.T.astype(BF16), kd_ref[:, cs])

    @pl.when(c == pl.num_programs(1) - 1)
    def _():
        for h in range(H_HEADS):
            sout_ref[h] = st_ref[h].T


def _hgrn_seq(proj3, row_blk0, nblk, lead, lb, lb_is_zero, nw, tables, s0, out_dtype, stack=None):
    bsz = proj3.shape[0]
    t = SEQ_BLOCK
    colspec = lambda cblk: pl.BlockSpec((None, t, D_MODEL), lambda b, c, cblk=cblk: (b, row_blk0 + c, cblk))
    full = lambda a: pl.BlockSpec(a.shape, lambda b, c: (0,) * a.ndim)
    bcast = lambda a: pl.BlockSpec((None,) + a.shape[1:], lambda b, c: (0,) * a.ndim)
    inputs = [proj3, proj3, proj3, proj3, lb, nw, *tables, s0]
    st_shape, st_spec = _seq_state_out(stack, bsz, (H_HEADS, H_KDIM, H_VDIM))
    body, x_specs, x_in, aliases = _alias_prev(functools.partial(_hgrn_seq_kernel, lead, lb_is_zero), len(inputs),
                                               stack and stack[1], (1,))
    return pl.pallas_call(
        body,
        grid=(bsz, nblk),
        in_specs=[colspec(COL_HQ), colspec(COL_HF), colspec(COL_HI), colspec(COL_HG),
                  full(lb), full(nw), full(tables[0]), full(tables[1]), bcast(s0)] + x_specs,
        out_specs=[pl.BlockSpec((None, t, D_MODEL), lambda b, c: (b, c, 0)), st_spec],
        out_shape=[jax.ShapeDtypeStruct((bsz, nblk * t, D_MODEL), out_dtype), st_shape],
        input_output_aliases=aliases,
        scratch_shapes=[pltpu.VMEM((H_HEADS, H_VDIM, H_KDIM), F32),
                        pltpu.VMEM((2 * t, D_MODEL), F32),
                        pltpu.VMEM((t, D_MODEL), F32), pltpu.VMEM((t, D_MODEL), F32),
                        pltpu.VMEM((H_LEVELS, t, D_MODEL), BF16),
                        pltpu.VMEM((t, D_MODEL), BF16), pltpu.VMEM((t, D_MODEL), BF16),
                        pltpu.VMEM((H_HEADS, t, t), BF16)],
        compiler_params=_cparams(("parallel", "arbitrary")),
        name="hgrn_seq",
    )(*inputs, *x_in)


def _log_gamma(h):
    return math.log(1.0 - 2.0 ** (-5.0 - h))


def _rotary(x, cos, sin):
    x1, x2 = x[:, :R_HALF], x[:, R_HALF:]
    return jnp.concatenate([x1 * cos - x2 * sin, x2 * cos + x1 * sin], axis=1)


def _ret_decay(t):
    d = np.arange(t)[:, None] - np.arange(t)[None, :]
    return np.stack([np.where(d >= 0, np.exp(np.maximum(d, 0) * _log_gamma(h)), 0.0)
                     for h in range(R_HEADS)]).astype(np.float32)


def _ret_seq_kernel(lead, t, q_ref, k_ref, v_ref, g_ref, cos_ref, sin_ref, dec_ref, s0_ref,
                    y_ref, sout_ref, st_ref):
    c = pl.program_id(1)

    @pl.when(c == 0)
    def _():
        st_ref[...] = s0_ref[...]

    cos, sin = cos_ref[...], sin_ref[...]
    tcol = _iota((t, 1), 0).astype(F32)
    for h in range(R_HEADS):
        lg = _log_gamma(h)
        ks = slice(h * R_KDIM, (h + 1) * R_KDIM)
        vs = slice(h * R_VDIM, (h + 1) * R_VDIM)
        qh = _rotary(q_ref[:, ks], cos, sin)
        kh = _rotary(k_ref[:, ks], cos, sin) * (R_KDIM ** -0.5)
        if lead:
            kh = jnp.where(_iota((t, 1), 0) >= lead, kh, 0.0)
        qb, kb, vb = qh.astype(BF16), kh.astype(BF16), v_ref[:, vs].astype(BF16)
        scores = _dot_nt(qb, kb) * dec_ref[h]
        st = st_ref[h]
        o = _dot(scores.astype(BF16), vb) + _dot(qb, st.astype(BF16)) * jnp.exp((tcol + 1.0) * lg)
        kdec = (kh * jnp.exp((t - 1.0 - tcol) * lg)).T.astype(BF16)
        st_ref[h] = st * math.exp(t * lg) + _dot(kdec, vb)
        y_ref[:, vs] = (_rms(o) * _silu(g_ref[:, vs])).astype(y_ref.dtype)

    @pl.when(c == pl.num_programs(1) - 1)
    def _():
        sout_ref[...] = st_ref[...]


def _ret_seq(proj3, t, row_blk0, nblk, lead, cos, sin, s0, out_dtype, stack=None):
    bsz = proj3.shape[0]
    wide = R_HEADS * R_VDIM
    bcast = lambda a: pl.BlockSpec((None,) + a.shape[1:], lambda b, c: (0,) * a.ndim)
    decay = jnp.asarray(_ret_decay(t))
    inputs = [proj3, proj3, proj3, proj3, cos, sin, decay, s0]
    st_shape, st_spec = _seq_state_out(stack, bsz, (R_HEADS, R_KDIM, R_VDIM))
    body, x_specs, x_in, aliases = _alias_prev(functools.partial(_ret_seq_kernel, lead, t), len(inputs),
                                               stack and stack[1], (1,))
    return pl.pallas_call(
        body,
        grid=(bsz, nblk),
        in_specs=[pl.BlockSpec((None, t, D_MODEL), lambda b, c: (b, row_blk0 + c, COL_RQ)),
                  pl.BlockSpec((None, t, D_MODEL), lambda b, c: (b, row_blk0 + c, COL_RK)),
                  pl.BlockSpec((None, t, wide), lambda b, c: (b, row_blk0 + c, COL_RV)),
                  pl.BlockSpec((None, t, wide), lambda b, c: (b, row_blk0 + c, COL_RG)),
                  pl.BlockSpec((t, R_HALF), lambda b, c: (c, 0)),
                  pl.BlockSpec((t, R_HALF), lambda b, c: (c, 0)),
                  pl.BlockSpec(decay.shape, lambda b, c: (0, 0, 0)),
                  bcast(s0)] + x_specs,
        out_specs=[pl.BlockSpec((None, t, wide), lambda b, c: (b, c, 0)), st_spec],
        out_shape=[jax.ShapeDtypeStruct((bsz, nblk * t, wide), out_dtype), st_shape],
        scratch_shapes=[pltpu.VMEM((R_HEADS, R_KDIM, R_VDIM), F32)],
        input_output_aliases=aliases,
        compiler_params=_cparams(("parallel", "arbitrary")),
        name="ret_seq",
    )(*inputs, *x_in)


def _col_pieces(x):
    return jnp.concatenate(_split3(x.T), axis=1)


def _pick_col(n):
    r = _iota((3 * LANES, LANES), 0) & (LANES - 1)
    return jnp.where(r == n, 1.0, 0.0).astype(BF16)


def _pick_col_pair(n):
    r = _iota((3 * LANES, 2 * LANES), 0) & (LANES - 1)
    want = n + jnp.right_shift(_iota((3 * LANES, 2 * LANES), 1), int(math.log2(LANES)))
    return jnp.where(r == want, 1.0, 0.0).astype(BF16)


def _step_prep_kernel(lb_is_zero, xs_ref, bc_ref, x_ref, wdt_ref, hq_ref, hf_ref, rq_ref, rk_ref, conv_ref,
                      cw_ref, cb_ref, dtb_ref, a_ref, ex_ref, lb_ref, cos_ref, sin_ref,
                      xs_o, bc_o, xdt_o, edec_o, conv_o, hq_o, hef_o, hk_o, rq_o, rk_o):
    raw = jnp.concatenate([xs_ref[...], bc_ref[...]], axis=1)
    buf = conv_ref[...]
    acc = cb_ref[...] + cw_ref[3:4, :] * raw
    for k in range(M_CONV - 1):
        acc += cw_ref[k:k + 1, :] * buf[:, k * M_CONV_DIM:(k + 1) * M_CONV_DIM]
    conv_o[:, 0:2 * M_CONV_DIM] = buf[:, M_CONV_DIM:]
    conv_o[:, 2 * M_CONV_DIM:] = raw
    xbc = _silu(acc)
    xs = xbc[:, :M_INNER]
    xs_o[...] = xs
    bc_o[...] = xbc[:, M_INNER:]
    dt = _softplus(_dot_nt(x_ref[...].astype(BF16), wdt_ref[...]) + dtb_ref[...])
    ex = ex_ref[...]
    xdt_o[...] = _col_pieces(xs * _sel_right(dt, ex))
    edec_o[...] = _col_pieces(_sel_right(jnp.exp(dt * a_ref[...]), ex))
    logf, k = _hgrn_gates(hf_ref[...], lb_ref[...], lb_is_zero)
    hq_o[...] = hq_ref[...] * (H_KDIM ** -0.5)
    hef_o[...] = _col_pieces(jnp.exp(logf))
    hk_o[...] = _col_pieces(k)
    cos, sin = cos_ref[...], sin_ref[...]
    rq = jnp.concatenate([_rotary(rq_ref[:, h * R_KDIM:(h + 1) * R_KDIM], cos, sin) for h in range(R_HEADS)], axis=1)
    rk = jnp.concatenate([_rotary(rk_ref[:, h * R_KDIM:(h + 1) * R_KDIM], cos, sin) for h in range(R_HEADS)], axis=1)
    rq_o[...] = _col_pieces(rq)
    rk_o[...] = _col_pieces(rk * (R_KDIM ** -0.5))


def _step_prep(proj_s, x_s, wt, layer, conv_flat, p, cos, sin):
    nb = conv_flat.shape[0]
    col = lambda cblk: pl.BlockSpec((nb, D_MODEL), lambda i, cblk=cblk: (0, cblk))
    full = lambda a: pl.BlockSpec(a.shape, lambda i: (0,) * a.ndim)
    params = [p["conv_w"], p["conv_b"], p["dt_bias"], p["a_neg"], p["expand"], p["lb"], cos, sin]
    assert nb == LANES
    rows = lambda w: jax.ShapeDtypeStruct((nb, w), F32)
    cols = lambda w: jax.ShapeDtypeStruct((w, 3 * nb), BF16)
    shapes = [rows(M_INNER), rows(M_BC), cols(M_INNER), cols(M_INNER), rows((M_CONV - 1) * M_CONV_DIM),
              rows(D_MODEL), cols(D_MODEL), cols(D_MODEL), cols(D_MODEL), cols(D_MODEL)]
    return pl.pallas_call(
        functools.partial(_step_prep_kernel, layer == 0),
        grid=(1,),
        in_specs=[col(COL_XS), col(COL_BC), pl.BlockSpec((nb, D_MODEL), lambda i: (0, 0)), _dt_weight_spec(layer),
                  col(COL_HQ), col(COL_HF), col(COL_RQ), col(COL_RK), full(conv_flat)]
                 + [full(a) for a in params],
        out_specs=[pl.BlockSpec(s.shape, lambda i: (0, 0)) for s in shapes],
        out_shape=shapes,
        compiler_params=_cparams(("arbitrary",)),
        name="step_prep",
    )(proj_s, proj_s, x_s, wt, proj_s, proj_s, proj_s, proj_s, conv_flat, *params)


def _ssd_step_kernel(s_ref, xdt_ref, edec_ref, bc_ref, xs_ref, z_ref, dsk_ref, nw_ref, so_ref, y_ref,
                     yt_ref, xdt_b, edec_b):
    yt_ref[...] = jnp.zeros(yt_ref.shape, F32)
    for i in range(STEP_BT):
        ls = slice((i % 2) * LANES, (i % 2 + 1) * LANES)
        if i % 2 == 0:
            pick = _pick_col_pair(pl.program_id(0) * STEP_BT + i)
            xdt_b[...] = _dot(xdt_ref[...], pick)
            edec_b[...] = _dot(edec_ref[...], pick)
        for g in range(M_GROUPS):
            gs = slice(g * M_GW, (g + 1) * M_GW)
            hs = slice(M_HPG * g, M_HPG * (g + 1))
            st = s_ref[i, hs].reshape(M_GW, M_STATE)
            brow = bc_ref[i:i + 1, g * M_STATE:(g + 1) * M_STATE]
            crow = bc_ref[i:i + 1, M_BC // 2 + g * M_STATE:M_BC // 2 + (g + 1) * M_STATE]
            new = st * edec_b[gs, ls] + xdt_b[gs, ls] * brow
            so_ref[i, hs] = new.reshape(M_HPG, M_HEADDIM, M_STATE)
            yt_ref[gs, i:i + 1] = jnp.sum(new * crow, axis=-1, keepdims=True)
    y = yt_ref[...].T[0:STEP_BT, :]
    xs = xs_ref[...]
    y = (y + dsk_ref[...] * xs) * _silu(z_ref[...])
    for g in range(M_GROUPS):
        gs = slice(g * M_GW, (g + 1) * M_GW)
        y_ref[:, gs] = _rms(y[:, gs]) * nw_ref[:, gs]


def _ssd_step(state, layer, prev, xdt_c, edec_c, bc, xs, proj_s, p):
    nb = xs.shape[0]
    bt = STEP_BT
    full = lambda a: pl.BlockSpec(a.shape, lambda j: (0,) * a.ndim)
    sspec = pl.BlockSpec((None, bt, M_HEADS, M_HEADDIM, M_STATE), lambda j: (layer, j, 0, 0, 0))
    inputs = [state, xdt_c, edec_c, bc, xs, proj_s, p["d_skip"], p["m_norm_w"]]
    body, x_specs, x_in, aliases = _alias_prev(_ssd_step_kernel, len(inputs),
                                               None if prev is None else [prev], (0,))
    return pl.pallas_call(
        body,
        grid=(nb // bt,),
        in_specs=[sspec, full(xdt_c), full(edec_c),
                  pl.BlockSpec((bt, M_BC), lambda j: (j, 0)),
                  pl.BlockSpec((bt, M_INNER), lambda j: (j, 0)),
                  pl.BlockSpec((bt, D_MODEL), lambda j: (j, COL_Z)),
                  full(p["d_skip"]), full(p["m_norm_w"])] + x_specs,
        out_specs=[sspec, pl.BlockSpec((bt, M_INNER), lambda j: (j, 0))],
        out_shape=[jax.ShapeDtypeStruct(state.shape, F32),
                   jax.ShapeDtypeStruct((nb, M_INNER), F32)],
        scratch_shapes=[pltpu.VMEM((M_INNER, LANES), F32)] + [pltpu.VMEM((M_INNER, 2 * LANES), F32)] * 2,
        input_output_aliases=aliases,
        compiler_params=_cparams(("parallel",)),
        name="ssd_step",
    )(*inputs, *x_in)


def _cols(x):
    nb, w = x.shape
    return jnp.transpose(x.reshape(nb // STEP_BT, STEP_BT, w), (0, 2, 1))


def _hgrn_step_kernel(s_ref, q_ref, ef_ref, k_ref, v_ref, g_ref, nw_ref, so_ref, y_ref, ef_b, k_b):
    for i in range(STEP_BT):
        ls = slice((i % 2) * LANES, (i % 2 + 1) * LANES)
        if i % 2 == 0:
            pick = _pick_col_pair(pl.program_id(0) * STEP_BT + i)
            ef_b[...] = _dot(ef_ref[...], pick)
            k_b[...] = _dot(k_ref[...], pick)
        for h in range(H_HEADS):
            cs = slice(h * H_KDIM, (h + 1) * H_KDIM)
            new = s_ref[i, h] * ef_b[cs, ls] + k_b[cs, ls] * v_ref[i:i + 1, cs]
            so_ref[i, h] = new
            y_ref[i:i + 1, cs] = jnp.sum(new * q_ref[cs, i:i + 1], axis=0, keepdims=True)
    for h in range(H_HEADS):
        cs = slice(h * H_KDIM, (h + 1) * H_KDIM)
        y_ref[:, cs] = _rms(y_ref[:, cs]) * nw_ref[:, cs] * _sigmoid(g_ref[:, cs])


def _hgrn_step(state, layer, prev, q_c, ef_c, k_c, proj_s, nw):
    nb = state.shape[1]
    bt = STEP_BT
    cspec = pl.BlockSpec(ef_c.shape, lambda j: (0, 0))
    qspec = pl.BlockSpec((None, D_MODEL, bt), lambda j: (j, 0, 0))
    sspec = pl.BlockSpec((None, bt, H_HEADS, H_KDIM, H_VDIM), lambda j: (layer, j, 0, 0, 0))
    inputs = [state, q_c, ef_c, k_c, proj_s, proj_s, nw]
    body, x_specs, x_in, aliases = _alias_prev(_hgrn_step_kernel, len(inputs),
                                               None if prev is None else [prev], (0,))
    return pl.pallas_call(
        body,
        grid=(nb // bt,),
        in_specs=[sspec, qspec, cspec, cspec,
                  pl.BlockSpec((bt, D_MODEL), lambda j: (j, COL_HI)),
                  pl.BlockSpec((bt, D_MODEL), lambda j: (j, COL_HG)),
                  pl.BlockSpec(nw.shape, lambda j: (0, 0))] + x_specs,
        out_specs=[sspec, pl.BlockSpec((bt, D_MODEL), lambda j: (j, 0))],
        out_shape=[jax.ShapeDtypeStruct(state.shape, F32),
                   jax.ShapeDtypeStruct((nb, D_MODEL), F32)],
        scratch_shapes=[pltpu.VMEM((D_MODEL, 2 * LANES), F32)] * 2,
        input_output_aliases=aliases,
        compiler_params=_cparams(("parallel",)),
        name="hgrn_step",
    )(*inputs, *x_in)


def _ret_step_kernel(s_ref, q_ref, k_ref, v_ref, g_ref, gam_ref, so_ref, y_ref):
    gam = gam_ref[...]
    head_rows = pl.ds(pl.multiple_of(pl.program_id(1) * R_KDIM, R_KDIM), R_KDIM)
    for i in range(STEP_BT):
        pick = _pick_col(pl.program_id(0) * STEP_BT + i)
        q_b = _dot(q_ref[head_rows, :], pick)
        k_b = _dot(k_ref[head_rows, :], pick)
        for c in range(0, R_VDIM, LANES):
            cs = slice(c, c + LANES)
            new = s_ref[i, :, cs] * gam[:, cs] + k_b * v_ref[i:i + 1, cs]
            so_ref[i, :, cs] = new
            y_ref[i:i + 1, cs] = jnp.sum(new * q_b, axis=0, keepdims=True)
    y_ref[...] = _rms(y_ref[...]) * _silu(g_ref[...])


def _ret_step(state, layer, prev, q_c, k_c, proj_s, gam):
    nb = state.shape[1]
    bt = STEP_BT
    cspec = pl.BlockSpec(q_c.shape, lambda j, h: (0, 0))
    sspec = pl.BlockSpec((None, bt, None, R_KDIM, R_VDIM), lambda j, h: (layer, j, h, 0, 0))
    inputs = [state, q_c, k_c, proj_s, proj_s, gam]
    body, x_specs, x_in, aliases = _alias_prev(_ret_step_kernel, len(inputs),
                                               None if prev is None else [prev], (0,))
    return pl.pallas_call(
        body,
        grid=(nb // bt, R_HEADS),
        in_specs=[sspec, cspec, cspec,
                  pl.BlockSpec((bt, R_VDIM), lambda j, h: (j, h)),
                  pl.BlockSpec((bt, R_VDIM), lambda j, h: (j, R_HEADS + h)),
                  pl.BlockSpec((None, 1, R_VDIM), lambda j, h: (h, 0, 0))] + x_specs,
        out_specs=[sspec, pl.BlockSpec((bt, R_VDIM), lambda j, h: (j, h))],
        out_shape=[jax.ShapeDtypeStruct(state.shape, F32),
                   jax.ShapeDtypeStruct((nb, R_HEADS * R_VDIM), F32)],
        input_output_aliases=aliases,
        compiler_params=_cparams(("parallel", "parallel")),
        name="ret_step",
    )(*inputs, *x_in)


def _rope_tables(positions):
    inv_freq = 1.0 / (ROPE_BASE ** jnp.linspace(0.0, 1.0, R_HALF, dtype=F32))
    ang = positions[:, None] * inv_freq[None, :]
    return jnp.cos(ang), jnp.sin(ang)


def _per_channel(v):
    return jnp.repeat(v.astype(F32), M_HEADDIM).reshape(1, M_INNER)


def _pad_lanes(v):
    return jnp.pad(v.astype(F32), (0, LANES - v.shape[0])).reshape(1, LANES)


def kernel(x_prompt, x_sample, state_ssm, state_conv, state_hgrn, state_ret, meta_tokens, ln_in_g, ln_in_b,
           w_in, conv_w, conv_b, dt_bias, a_log, d_skip, m_norm_w, hgrn_lb_logits, h_norm_w, w_br_m, w_br_h,
           w_br_r, w_out, ln1_g, ln1_b, w_ffn_in, w_ffn_out, ln2_g, ln2_b):
    bp, sp = x_prompt.shape[0], x_prompt.shape[1]
    nb = x_sample.shape[0]
    assert x_sample.shape[1] == 1 and nb == SMALL_ROWS - SEQ_BLOCK and nb % STEP_BT == 0
    assert sp % RET_BLOCK == 0 and meta_tokens.shape[0] == N_META

    wt = jnp.swapaxes(w_in, 1, 2).astype(BF16)
    wm_b, wh_b, wr_b, wo_b = (w.astype(BF16) for w in (w_br_m, w_br_h, w_br_r, w_out))
    wfi_b, wfo_b = w_ffn_in.astype(BF16), w_ffn_out.astype(BF16)
    ln1 = (ln1_g.reshape(DEPTH, 1, D_MODEL), ln1_b.reshape(DEPTH, 1, D_MODEL))
    ln2 = (ln2_g.reshape(DEPTH, 1, D_MODEL), ln2_b.reshape(DEPTH, 1, D_MODEL))
    lb_cum = jnp.cumsum(jax.nn.softmax(hgrn_lb_logits.astype(F32), axis=0), axis=0)
    lbs = lb_cum - lb_cum[0]
    expand = (np.arange(LANES)[:, None] == (np.arange(M_INNER)[None, :] // M_HEADDIM)).astype(np.float32)
    expand = jnp.asarray(np.concatenate([expand] * 3, 0), BF16)
    h_sums, h_pair = _hgrn_tables()
    h_tables = (jnp.asarray(h_sums, BF16), jnp.asarray(h_pair, F32))
    gam = jnp.asarray(np.broadcast_to(
        np.array([1.0 - 2.0 ** (-5.0 - h) for h in range(R_HEADS)], np.float32)[:, None, None],
        (R_HEADS, 1, R_VDIM)))

    pos_real = jnp.arange(N_META, N_META + sp, dtype=F32)
    pos_meta = jnp.maximum(jnp.arange(SEQ_BLOCK, dtype=F32) - META_LEAD, 0.0)
    pos_samp = jnp.full((nb,), float(PAST_LEN), F32)
    cos_r, sin_r = _rope_tables(pos_real)
    cos_m, sin_m = _rope_tables(pos_meta)
    cos_s, sin_s = _rope_tables(pos_samp)

    x_real = x_prompt.reshape(bp * sp, D_MODEL)
    x_small = jnp.concatenate([x_sample.reshape(nb, D_MODEL),
                               jnp.zeros((META_LEAD, D_MODEL), F32), meta_tokens.astype(F32)], axis=0)

    z_ssm = jnp.zeros((1, M_HEADS, M_HEADDIM, M_STATE), F32)
    z_conv = jnp.zeros((1, 8, M_CONV_DIM), F32)
    z_hgrn = jnp.zeros((1, H_HEADS, H_KDIM, H_VDIM), F32)
    z_ret = jnp.zeros((1, R_HEADS, R_KDIM, R_VDIM), F32)
    ssm_p = conv_p = hgrn_p = ret_p = ssm_s = hgrn_s = ret_s = None
    conv_s = []
    for l in range(DEPTH):
        p = dict(conv_w=conv_w[l], conv_b=conv_b[l].reshape(1, -1), dt_bias=_pad_lanes(dt_bias[l]),
                 a_neg=_pad_lanes(-jnp.exp(a_log[l].astype(F32))), d_skip=_per_channel(d_skip[l]),
                 m_norm_w=m_norm_w[l].reshape(1, -1), expand=expand, lb=lbs[l].reshape(1, -1))
        hnw = h_norm_w[l].reshape(1, -1)

        if l == 0:
            ln_in = (ln_in_g, ln_in_b)
            proj_r, x_real = _proj(x_real, wt, l, PROJ_TM_LN, "proj_real", ln=ln_in)
            proj_s, x_small = _proj(x_small, wt, l, SMALL_ROWS, "proj_small", ln=ln_in)
        else:
            proj_r = _proj(x_real, wt, l, PROJ_TM, "proj_real")
            proj_s = _proj(x_small, wt, l, SMALL_ROWS, "proj_small")
        proj_r3, x_real3 = proj_r.reshape(bp, sp, PROJ_COLS), x_real.reshape(bp, sp, D_MODEL)
        proj_s3, x_small3 = proj_s.reshape(1, SMALL_ROWS, PROJ_COLS), x_small.reshape(1, SMALL_ROWS, D_MODEL)

        ym_m, ssm_m, conv_m = _ssd_seq(proj_s3, x_small3, wt, l, 1, 1, META_LEAD, p, z_ssm, z_conv, F32)
        yh_m, hgrn_m = _hgrn_seq(proj_s3, 1, 1, META_LEAD, p["lb"], l == 0, hnw, h_tables, z_hgrn, F32)
        yr_m, ret_m = _ret_seq(proj_s3, SEQ_BLOCK, 1, 1, META_LEAD, cos_m, sin_m, z_ret, F32)

        conv0 = jnp.pad(conv_m, ((0, 0), (8 - (M_CONV - 1), 0), (0, 0)))
        ym_r, ssm_p, conv_p = _ssd_seq(proj_r3, x_real3, wt, l, 0, sp // SEQ_BLOCK, 0, p, ssm_m, conv0, BF16,
                                       stack=(l, None if l == 0 else [ssm_p, conv_p]))
        yh_r, hgrn_p = _hgrn_seq(proj_r3, 0, sp // SEQ_BLOCK, 0, p["lb"], l == 0, hnw, h_tables, hgrn_m, BF16,
                                 stack=(l, None if l == 0 else [hgrn_p]))
        yr_r, ret_p = _ret_seq(proj_r3, RET_BLOCK, 0, sp // RET_BLOCK, 0, cos_r, sin_r, ret_m, BF16,
                               stack=(l, None if l == 0 else [ret_p]))

        conv_flat = state_conv[l].reshape(nb, (M_CONV - 1) * M_CONV_DIM)
        (xs_s, bc_s, xdt_s, edec_s, conv_new, hq_s, hef_s, hk_s, rq_s, rk_s) = _step_prep(
            proj_s, x_small, wt, l, conv_flat, p, cos_s, sin_s)
        ssm_s, ym_s = _ssd_step(state_ssm, l, ssm_s, xdt_s, edec_s, bc_s, xs_s, proj_s, p)
        hgrn_s, yh_s = _hgrn_step(state_hgrn, l, hgrn_s, _cols(hq_s), hef_s, hk_s, proj_s, hnw)
        ret_s, yr_s = _ret_step(state_ret, l, ret_s, rq_s, rk_s, proj_s, gam)
        conv_s.append(conv_new.reshape(nb, M_CONV - 1, M_CONV_DIM))

        ym_small = jnp.concatenate([ym_s, ym_m[0]], axis=0)
        yh_small = jnp.concatenate([yh_s, yh_m[0]], axis=0)
        yr_small = jnp.concatenate([yr_s, yr_m[0]], axis=0)
        x_real = _mix(x_real, ym_r.reshape(bp * sp, -1), yh_r.reshape(bp * sp, -1), yr_r.reshape(bp * sp, -1),
                      proj_r, l, wm_b, wh_b, wr_b, wo_b, *ln1, 256)
        x_small = _mix(x_small, ym_small, yh_small, yr_small, proj_s, l, wm_b, wh_b, wr_b, wo_b, *ln1, SMALL_ROWS)
        x_real = _ffn(x_real, l, wfi_b, wfo_b, *ln2, 512)
        x_small = _ffn(x_small, l, wfi_b, wfo_b, *ln2, SMALL_ROWS)

    return (x_real.reshape(bp, sp, D_MODEL), x_small[:nb].reshape(nb, 1, D_MODEL),
            ssm_p, conv_p, hgrn_p, ret_p, ssm_s, jnp.stack(conv_s), hgrn_s, ret_s)
```

```python
import functools
import math

import numpy as np
import jax
import jax.numpy as jnp
from jax import lax
from jax.experimental import pallas as pl
from jax.experimental.pallas import tpu as pltpu

F32 = jnp.float32
BF16 = jnp.bfloat16

D_MODEL = 1024
DEPTH = 2
N_META = 16
M_INNER = D_MODEL
M_HEADDIM = 64
M_HEADS = M_INNER // M_HEADDIM
M_GROUPS = 4
M_HPG = M_HEADS // M_GROUPS
M_STATE = 128
M_CONV = 4
M_BC = 2 * M_GROUPS * M_STATE
M_CONV_DIM = M_INNER + M_BC
M_GW = M_INNER // M_GROUPS
H_KDIM = 128
H_HEADS = D_MODEL // H_KDIM
H_VDIM = 128
R_HEADS = 4
R_KDIM = D_MODEL // R_HEADS
R_VDIM = 2 * R_KDIM
R_HALF = R_KDIM // 2
ROPE_BASE = 10000.0
D_FF = ((8 * D_MODEL // 3 + 255) // 256) * 256
FF_CHUNK = 256
DN_ALPHA = (2 * DEPTH) ** 0.25
PAST_LEN = 16384

LANES = 128
SEQ_BLOCK = 128
RET_BLOCK = 256
SMALL_ROWS = 256
META_LEAD = SEQ_BLOCK - N_META
STEP_BT = 8
VMEM_LIMIT = 56 * 1024 * 1024

COL_RV, COL_RG = 0, 1
COL_Z, COL_XS, COL_BC, COL_HQ, COL_HF, COL_HI, COL_HG, COL_RQ, COL_RK, COL_GM, COL_GH, COL_GR = range(4, 16)
PROJ_COLS = 16 * 1024

_ORIG_SPLITS = (M_INNER, M_CONV_DIM, M_HEADS, 1024, 1024, 1024, 1024, 1024, 1024, 2048, 2048, 3072)
_ORIG_OFF = np.concatenate([[0], np.cumsum(_ORIG_SPLITS)]).tolist()


def _cparams(sem):
    return pltpu.CompilerParams(dimension_semantics=sem, vmem_limit_bytes=VMEM_LIMIT)


def _sigmoid(x):
    return 1.0 / (1.0 + jnp.exp(-x))


def _silu(x):
    return x * _sigmoid(x)


def _softplus(x):
    return jnp.maximum(x, 0.0) + jnp.log1p(jnp.exp(-jnp.abs(x)))


def _layer_norm(x, g, b):
    mu = jnp.mean(x, axis=-1, keepdims=True)
    xc = x - mu
    var = jnp.mean(xc * xc, axis=-1, keepdims=True)
    return xc * lax.rsqrt(var + 1e-5) * g + b


def _rms(x):
    return x * lax.rsqrt(jnp.mean(x * x, axis=-1, keepdims=True) + 1e-6)


def _split3(x):
    hi = x.astype(BF16)
    r = x - hi.astype(F32)
    mid = r.astype(BF16)
    lo = (r - mid.astype(F32)).astype(BF16)
    return hi, mid, lo


def _dot(a, b):
    return jnp.dot(a, b, preferred_element_type=F32)


def _dot_nt(a, b):
    return lax.dot_general(a, b, (((1,), (1,)), ((), ())), preferred_element_type=F32)


def _sel_right(x, m3):
    return _dot(jnp.concatenate(_split3(x), axis=1), m3)


def _sel_left(m3, x):
    return _dot(m3, jnp.concatenate(_split3(x), axis=0))


def _iota(shape, dim):
    return lax.broadcasted_iota(jnp.int32, shape, dim)


def _ln_kernel(x_ref, g_ref, b_ref, o_ref):
    o_ref[...] = _layer_norm(x_ref[...], g_ref[...], b_ref[...])


def _layer_norm_rows(x, g, b, tm):
    m = x.shape[0]
    tm = min(tm, m)
    return pl.pallas_call(
        _ln_kernel,
        grid=(m // tm,),
        in_specs=[pl.BlockSpec((tm, D_MODEL), lambda i: (i, 0)),
                  pl.BlockSpec((1, D_MODEL), lambda i: (0, 0)),
                  pl.BlockSpec((1, D_MODEL), lambda i: (0, 0))],
        out_specs=pl.BlockSpec((tm, D_MODEL), lambda i: (i, 0)),
        out_shape=jax.ShapeDtypeStruct((m, D_MODEL), F32),
        compiler_params=_cparams(("parallel",)),
        name="ln_in",
    )(x, g.reshape(1, -1), b.reshape(1, -1))


def _proj_kernel(x_ref, w_ref, o_ref, xb_ref):
    @pl.when(pl.program_id(1) == 0)
    def _():
        xb_ref[...] = x_ref[...].astype(BF16)

    o_ref[...] = _dot_nt(xb_ref[...], w_ref[0])


def _layer_block(a, layer):
    return pl.BlockSpec((None,) + a.shape[1:], lambda *_: (layer,) + (0,) * (a.ndim - 1))


_W_RUNS = ((0, _ORIG_OFF[9]), (4096, _ORIG_OFF[0]), (7168, _ORIG_OFF[3]), (13312, _ORIG_OFF[11]))
PROJ_TN = 1024
PROJ_TM = 2048


def _orig_col(j):
    c = j * PROJ_TN
    off = c - _W_RUNS[0][0] + _W_RUNS[0][1]
    for new0, orig0 in _W_RUNS[1:]:
        off = jnp.where(c >= new0, c - new0 + orig0, off)
    return pl.multiple_of(off, M_HEADS)


def _proj(x, wt, layer, tm, name):
    m = x.shape[0]
    tm = min(tm, m)
    return pl.pallas_call(
        _proj_kernel,
        grid=(m // tm, PROJ_COLS // PROJ_TN),
        in_specs=[pl.BlockSpec((tm, D_MODEL), lambda i, j: (i, 0)),
                  pl.BlockSpec((pl.Element(1), pl.Element(PROJ_TN), pl.Element(D_MODEL)),
                               lambda i, j: (layer, _orig_col(j), 0))],
        out_specs=pl.BlockSpec((tm, PROJ_TN), lambda i, j: (i, j)),
        out_shape=jax.ShapeDtypeStruct((m, PROJ_COLS), F32),
        scratch_shapes=[pltpu.VMEM((tm, D_MODEL), BF16)],
        compiler_params=_cparams(("parallel", "arbitrary")),
        name=name,
    )(x, wt)


def _dt_weight_spec(layer):
    assert _ORIG_OFF[2] % LANES == 0
    return pl.BlockSpec((None, LANES, D_MODEL), lambda *_: (layer, _ORIG_OFF[2] // LANES, 0))


def _mix_kernel(x_ref, ym_ref, yh_ref, yr_ref, gm_ref, gh_ref, gr_ref,
                wm_ref, wh_ref, wr_ref, wo_ref, g_ref, b_ref, o_ref):
    mixed = _sigmoid(gm_ref[...]) * _dot(ym_ref[...].astype(BF16), wm_ref[...])
    mixed += _sigmoid(gh_ref[...]) * _dot(yh_ref[...].astype(BF16), wh_ref[...])
    mixed += _sigmoid(gr_ref[...]) * _dot(yr_ref[...].astype(BF16), wr_ref[...])
    h = _dot(mixed.astype(BF16), wo_ref[...])
    o_ref[...] = _layer_norm(DN_ALPHA * x_ref[...] + h, g_ref[...], b_ref[...])


def _mix(x, ym, yh, yr, proj, layer, wm, wh, wr, wo, g2, b2, tm):
    m = x.shape[0]
    tm = min(tm, m)
    row = lambda w: pl.BlockSpec((tm, w), lambda i: (i, 0))
    col = lambda c: pl.BlockSpec((tm, D_MODEL), lambda i, c=c: (i, c))
    full = lambda a: _layer_block(a, layer)
    return pl.pallas_call(
        _mix_kernel,
        grid=(m // tm,),
        in_specs=[row(D_MODEL), row(M_INNER), row(D_MODEL), row(R_HEADS * R_VDIM),
                  col(COL_GM), col(COL_GH), col(COL_GR),
                  full(wm), full(wh), full(wr), full(wo), full(g2), full(b2)],
        out_specs=row(D_MODEL),
        out_shape=jax.ShapeDtypeStruct((m, D_MODEL), F32),
        compiler_params=_cparams(("parallel",)),
        name="mix",
    )(x, ym, yh, yr, proj, proj, proj, wm, wh, wr, wo, g2, b2)


def _ffn_kernel(x_ref, wi_ref, wo_ref, g_ref, b_ref, o_ref):
    x = x_ref[...]
    xb = x.astype(BF16)
    acc = jnp.zeros(x.shape, F32)
    for j in range(0, D_FF, FF_CHUNK):
        hg = _dot(xb, wi_ref[:, j:j + FF_CHUNK])
        hu = _dot(xb, wi_ref[:, D_FF + j:D_FF + j + FF_CHUNK])
        acc += _dot((_silu(hg) * hu).astype(BF16), wo_ref[j:j + FF_CHUNK, :])
    o_ref[...] = _layer_norm(DN_ALPHA * x + acc, g_ref[...], b_ref[...])


def _ffn(x, layer, wi, wo, g2, b2, tm):
    m = x.shape[0]
    tm = min(tm, m)
    full = lambda a: _layer_block(a, layer)
    return pl.pallas_call(
        _ffn_kernel,
        grid=(m // tm,),
        in_specs=[pl.BlockSpec((tm, D_MODEL), lambda i: (i, 0)), full(wi), full(wo), full(g2), full(b2)],
        out_specs=pl.BlockSpec((tm, D_MODEL), lambda i: (i, 0)),
        out_shape=jax.ShapeDtypeStruct((m, D_MODEL), F32),
        compiler_params=_cparams(("parallel",)),
        name="ffn",
    )(x, wi, wo, g2, b2)


def _ssd_seq_kernel(lead, z_ref, xs_ref, bc_ref, x_ref, wdt_ref, cw_ref, cb_ref, dtb_ref, a_ref, dsk_ref, nw_ref,
                    ex_ref, s0_ref, c0_ref, y_ref, sout_ref, cout_ref, ext_ref, st_ref, xbc_ref):
    t = SEQ_BLOCK
    c = pl.program_id(1)

    @pl.when(c == 0)
    def _():
        ext_ref[0:8, :] = c0_ref[...]
        for g in range(M_GROUPS):
            st_ref[g] = s0_ref[M_HPG * g:M_HPG * (g + 1)].reshape(M_GW, M_STATE).T

    rows = _iota((t, 1), 0)

    def conv_chunk(q):
        w = M_CONV_DIM // 4
        cs = slice(q * w, (q + 1) * w)
        src = xs_ref if q < 2 else bc_ref
        raw = src[:, (q % 2) * w:(q % 2 + 1) * w]
        if lead:
            raw = jnp.where(rows >= lead, raw, 0.0)
        ext_ref[8:8 + t, cs] = raw
        acc = cb_ref[:, cs] + cw_ref[3:4, cs] * raw
        for k in range(M_CONV - 1):
            acc += cw_ref[k:k + 1, cs] * ext_ref[5 + k:5 + k + t, cs]
        ext_ref[5:8, cs] = ext_ref[t + 5:t + 8, cs]
        xbc_ref[:, cs] = _silu(acc)

    dt_raw = _dot_nt(x_ref[...].astype(BF16), wdt_ref[...])
    conv_chunk(0)
    dt = _softplus(dt_raw + dtb_ref[...])
    if lead:
        dt = jnp.where(rows >= lead, dt, 0.0)
    a = dt * a_ref[...]
    ti, si = _iota((t, t), 0), _iota((t, t), 1)
    tril = si <= ti
    tri = jnp.where(tril, 1.0, 0.0).astype(BF16)
    cum = _sel_left(jnp.concatenate([tri] * 3, axis=1), a)
    conv_chunk(1)
    ecum = jnp.exp(cum)
    tailw = jnp.exp(cum[t - 1:t, :] - cum) * dt
    cum_t, dt_t = cum.T, dt.T
    ex = ex_ref[...]
    ecum_full = _sel_right(ecum, ex)
    conv_chunk(2)
    tailw_full = _sel_right(tailw, ex)
    conv_chunk(3)

    xs = xbc_ref[:, 0:M_INNER]
    xw = (xs * tailw_full).astype(BF16)
    xs_b = xs.astype(BF16)
    head_of_lane = jnp.right_shift(_iota((1, M_GW), 1), int(math.log2(M_HEADDIM)))

    for g in range(M_GROUPS):
        gs = slice(g * M_GW, (g + 1) * M_GW)
        bm_f = xbc_ref[:, M_INNER + g * M_STATE:M_INNER + (g + 1) * M_STATE]
        bm = bm_f.astype(BF16)
        cm = xbc_ref[:, M_INNER + M_BC // 2 + g * M_STATE:M_INNER + M_BC // 2 + (g + 1) * M_STATE].astype(BF16)
        cb = _dot_nt(cm, bm)
        st = st_ref[g]
        y_g = _dot(cm, st.astype(BF16)) * ecum_full[:, gs]
        for hh in range(M_HPG):
            h = g * M_HPG + hh
            diff = cum[:, h:h + 1] - cum_t[h:h + 1, :]
            w = cb * jnp.exp(jnp.where(tril, diff, -1e30)) * dt_t[h:h + 1, :]
            x_h = jnp.where(head_of_lane == hh, xs_b[:, gs], jnp.zeros((), BF16))
            y_g = y_g + _dot(w.astype(BF16), x_h)
        st_ref[g] = st * ecum_full[t - 1:t, gs] + _dot(bm_f.T.astype(BF16), xw[:, gs])
        y_g = (y_g + dsk_ref[:, gs] * xs[:, gs]) * _silu(z_ref[:, gs])
        y_ref[:, gs] = (_rms(y_g) * nw_ref[:, gs]).astype(y_ref.dtype)

    @pl.when(c == pl.num_programs(1) - 1)
    def _():
        cout_ref[...] = ext_ref[5:8, :]
        for g in range(M_GROUPS):
            sout_ref[M_HPG * g:M_HPG * (g + 1)] = st_ref[g].T.reshape(M_HPG, M_HEADDIM, M_STATE)


def _alias_prev(body, n_in, prevs, out_ids):
    if not prevs:
        return body, [], [], {}
    k = len(prevs)
    wrapped = lambda *refs: body(*refs[:n_in], *refs[n_in + k:])
    return (wrapped, [pl.BlockSpec(memory_space=pl.ANY)] * k, list(prevs),
            {n_in + i: o for i, o in enumerate(out_ids)})


def _seq_state_out(stack, bsz, dims):
    zeros = (0,) * len(dims)
    if stack is None:
        return (jax.ShapeDtypeStruct((bsz,) + dims, F32),
                pl.BlockSpec((None,) + dims, lambda b, c: (b,) + zeros))
    layer = stack[0]
    return (jax.ShapeDtypeStruct((DEPTH, bsz) + dims, F32),
            pl.BlockSpec((None, None) + dims, lambda b, c: (layer, b) + zeros))


def _ssd_seq(proj3, x3, wt, layer, row_blk0, nblk, lead, p, s0, c0, out_dtype, stack=None):
    bsz = proj3.shape[0]
    t = SEQ_BLOCK
    colspec = lambda cblk: pl.BlockSpec((None, t, D_MODEL), lambda b, c, cblk=cblk: (b, row_blk0 + c, cblk))
    full = lambda a: pl.BlockSpec(a.shape, lambda b, c: (0,) * a.ndim)
    bcast = lambda a: pl.BlockSpec((None,) + a.shape[1:], lambda b, c: (0,) * a.ndim)
    params = [p["conv_w"], p["conv_b"], p["dt_bias"], p["a_neg"], p["d_skip"], p["m_norm_w"], p["expand"]]
    inputs = [proj3, proj3, proj3, x3, wt, *params, s0, c0]
    st_shape, st_spec = _seq_state_out(stack, bsz, (M_HEADS, M_HEADDIM, M_STATE))
    cv_shape, cv_spec = _seq_state_out(stack, bsz, (M_CONV - 1, M_CONV_DIM))
    body, x_specs, x_in, aliases = _alias_prev(functools.partial(_ssd_seq_kernel, lead), len(inputs),
                                               stack and stack[1], (1, 2))
    return pl.pallas_call(
        body,
        grid=(bsz, nblk),
        in_specs=[colspec(COL_Z), colspec(COL_XS), colspec(COL_BC),
                  pl.BlockSpec((None, t, D_MODEL), lambda b, c: (b, row_blk0 + c, 0)), _dt_weight_spec(layer)]
                 + [full(a) for a in params] + [bcast(s0), bcast(c0)] + x_specs,
        out_specs=[pl.BlockSpec((None, t, M_INNER), lambda b, c: (b, c, 0)), st_spec, cv_spec],
        out_shape=[jax.ShapeDtypeStruct((bsz, nblk * t, M_INNER), out_dtype), st_shape, cv_shape],
        scratch_shapes=[pltpu.VMEM((t + 8, M_CONV_DIM), F32),
                        pltpu.VMEM((M_GROUPS, M_STATE, M_GW), F32),
                        pltpu.VMEM((t, M_CONV_DIM), F32)],
        input_output_aliases=aliases,
        compiler_params=_cparams(("parallel", "arbitrary")),
        name="ssd_seq",
    )(*inputs, *x_in)


def _hgrn_gates(fz, lb, lb_is_zero):
    e = jnp.exp(-jnp.abs(fz))
    r = 1.0 / (1.0 + e)
    pos = fz >= 0.0
    sig_neg = jnp.where(pos, e * r, r)
    log_sig = jnp.minimum(fz, 0.0) - jnp.log(1.0 + e)
    if lb_is_zero:
        return log_sig, sig_neg
    sig_pos = jnp.where(pos, r, e * r)
    logf = jnp.where(lb > 0.0, jnp.log(lb + (1.0 - lb) * sig_pos), log_sig)
    return logf, (1.0 - lb) * sig_neg


H_LEVELS = int(math.log2(SEQ_BLOCK))


def _hgrn_tables():
    n = SEQ_BLOCK
    t = np.arange(n)[:, None]
    u = np.arange(n)[None, :]
    pair = []
    for b in range(H_LEVELS):
        bit = ((t >> b) & 1) == 1
        pair.append(((t >> (b + 1)) == (u >> (b + 1))) & bit & (((u >> b) & 1) == 0))
    m1 = ((t >> 1) | 1) << 1
    lvl1 = np.where(((t >> 1) & 1) == 1, (u >= m1) & (u <= t), (u > t) & (u < m1))
    sums = np.concatenate([u <= t, lvl1], 0).astype(np.float32)
    return np.concatenate([sums] * 3, 1), np.concatenate(pair, 0).astype(np.float32)


def _exp_neg_abs(d):
    return jnp.exp2(jnp.abs(d) * (-1.0 / math.log(2.0)))


def _hgrn_seq_kernel(lead, lb_is_zero, q_ref, f_ref, i_ref, g_ref, lb_ref, nw_ref, sums_ref, pair_ref, s0_ref,
                     y_ref, sout_ref, st_ref, ex_ref, q_s, k_s, z_ref, qd_ref, kd_ref, sc_ref):
    t = SEQ_BLOCK
    c = pl.program_id(1)

    @pl.when(c == 0)
    def _():
        for h in range(H_HEADS):
            st_ref[h] = s0_ref[h].T

    rows = _iota((t, 1), 0)
    logf, k = _hgrn_gates(f_ref[...], lb_ref[...], lb_is_zero)
    if lead:
        logf = jnp.where(rows >= lead, logf, 0.0)
        k = jnp.where(rows >= lead, k, 0.0)
    q = q_ref[...] * (H_KDIM ** -0.5)
    q_s[...] = q
    k_s[...] = k
    ex_ref[...] = _dot(sums_ref[...], jnp.concatenate(_split3(logf), axis=0))

    def side_of(b):
        return (jnp.right_shift(rows, b) & 1) == 1

    z_ref[0] = jnp.where(side_of(0), q * jnp.exp(logf), k).astype(BF16)
    z_ref[1] = (jnp.exp(ex_ref[t:2 * t, :]) * jnp.where(side_of(1), q_s[...], k_s[...])).astype(BF16)
    for b in range(2, H_LEVELS):
        half = 1 << b
        groups = range(0, t, 2 * half)
        d = jnp.concatenate([ex_ref[g0:g0 + 2 * half, :] - ex_ref[g0 + half - 1:g0 + half, :]
                             for g0 in groups], axis=0)
        if half >= 8:
            qk = jnp.concatenate([ref[g0 + o:g0 + o + half, :] for g0 in groups
                                  for ref, o in ((k_s, 0), (q_s, half))], axis=0)
        else:
            qk = jnp.where(side_of(b), q_s[...], k_s[...])
        z_ref[b] = (_exp_neg_abs(d) * qk).astype(BF16)
    cum = ex_ref[0:t, :]
    qd_ref[...] = (q_s[...] * jnp.exp(cum)).astype(BF16)
    kd_ref[...] = (k_s[...] * jnp.exp(ex_ref[t - 1:t, :] - cum)).astype(BF16)

    for h in range(H_HEADS):
        cs = slice(h * H_KDIM, (h + 1) * H_KDIM)
        scores = None
        for b in range(H_LEVELS):
            z = z_ref[b, :, cs]
            p = _dot_nt(z, z) * pair_ref[b * t:(b + 1) * t, :]
            scores = p if scores is None else scores + p
        sc_ref[h] = scores.astype(BF16)

    for h in range(H_HEADS):
        cs = slice(h * H_KDIM, (h + 1) * H_KDIM)
        v = i_ref[:, cs]
        st = st_ref[h]
        o = (_dot(sc_ref[h], v.astype(BF16))
             + jnp.sum(q_s[:, cs] * k_s[:, cs], axis=-1, keepdims=True) * v
             + _dot_nt(qd_ref[:, cs], st.astype(BF16)))
        y = _rms(o) * nw_ref[:, cs] * _sigmoid(g_ref[:, cs])
        y_ref[:, cs] = y.astype(y_ref.dtype)
        st_ref[h] = st * jnp.exp(ex_ref[t - 1:t, cs]) + _dot(v.T.astype(BF16), kd_ref[:, cs])

    @pl.when(c == pl.num_programs(1) - 1)
    def _():
        for h in range(H_HEADS):
            sout_ref[h] = st_ref[h].T


def _hgrn_seq(proj3, row_blk0, nblk, lead, lb, lb_is_zero, nw, tables, s0, out_dtype, stack=None):
    bsz = proj3.shape[0]
    t = SEQ_BLOCK
    colspec = lambda cblk: pl.BlockSpec((None, t, D_MODEL), lambda b, c, cblk=cblk: (b, row_blk0 + c, cblk))
    full = lambda a: pl.BlockSpec(a.shape, lambda b, c: (0,) * a.ndim)
    bcast = lambda a: pl.BlockSpec((None,) + a.shape[1:], lambda b, c: (0,) * a.ndim)
    inputs = [proj3, proj3, proj3, proj3, lb, nw, *tables, s0]
    st_shape, st_spec = _seq_state_out(stack, bsz, (H_HEADS, H_KDIM, H_VDIM))
    body, x_specs, x_in, aliases = _alias_prev(functools.partial(_hgrn_seq_kernel, lead, lb_is_zero), len(inputs),
                                               stack and stack[1], (1,))
    return pl.pallas_call(
        body,
        grid=(bsz, nblk),
        in_specs=[colspec(COL_HQ), colspec(COL_HF), colspec(COL_HI), colspec(COL_HG),
                  full(lb), full(nw), full(tables[0]), full(tables[1]), bcast(s0)] + x_specs,
        out_specs=[pl.BlockSpec((None, t, D_MODEL), lambda b, c: (b, c, 0)), st_spec],
        out_shape=[jax.ShapeDtypeStruct((bsz, nblk * t, D_MODEL), out_dtype), st_shape],
        input_output_aliases=aliases,
        scratch_shapes=[pltpu.VMEM((H_HEADS, H_VDIM, H_KDIM), F32),
                        pltpu.VMEM((2 * t, D_MODEL), F32),
                        pltpu.VMEM((t, D_MODEL), F32), pltpu.VMEM((t, D_MODEL), F32),
                        pltpu.VMEM((H_LEVELS, t, D_MODEL), BF16),
                        pltpu.VMEM((t, D_MODEL), BF16), pltpu.VMEM((t, D_MODEL), BF16),
                        pltpu.VMEM((H_HEADS, t, t), BF16)],
        compiler_params=_cparams(("parallel", "arbitrary")),
        name="hgrn_seq",
    )(*inputs, *x_in)


def _log_gamma(h):
    return math.log(1.0 - 2.0 ** (-5.0 - h))


def _rotary(x, cos, sin):
    x1, x2 = x[:, :R_HALF], x[:, R_HALF:]
    return jnp.concatenate([x1 * cos - x2 * sin, x2 * cos + x1 * sin], axis=1)


def _ret_decay(t):
    d = np.arange(t)[:, None] - np.arange(t)[None, :]
    return np.stack([np.where(d >= 0, np.exp(np.maximum(d, 0) * _log_gamma(h)), 0.0)
                     for h in range(R_HEADS)]).astype(np.float32)


def _ret_seq_kernel(lead, t, q_ref, k_ref, v_ref, g_ref, cos_ref, sin_ref, dec_ref, s0_ref,
                    y_ref, sout_ref, st_ref):
    c = pl.program_id(1)

    @pl.when(c == 0)
    def _():
        st_ref[...] = s0_ref[...]

    cos, sin = cos_ref[...], sin_ref[...]
    tcol = _iota((t, 1), 0).astype(F32)
    for h in range(R_HEADS):
        lg = _log_gamma(h)
        ks = slice(h * R_KDIM, (h + 1) * R_KDIM)
        vs = slice(h * R_VDIM, (h + 1) * R_VDIM)
        qh = _rotary(q_ref[:, ks], cos, sin)
        kh = _rotary(k_ref[:, ks], cos, sin) * (R_KDIM ** -0.5)
        if lead:
            kh = jnp.where(_iota((t, 1), 0) >= lead, kh, 0.0)
        qb, kb, vb = qh.astype(BF16), kh.astype(BF16), v_ref[:, vs].astype(BF16)
        scores = _dot_nt(qb, kb) * dec_ref[h]
        st = st_ref[h]
        o = _dot(scores.astype(BF16), vb) + _dot(qb, st.astype(BF16)) * jnp.exp((tcol + 1.0) * lg)
        kdec = (kh * jnp.exp((t - 1.0 - tcol) * lg)).T.astype(BF16)
        st_ref[h] = st * math.exp(t * lg) + _dot(kdec, vb)
        y_ref[:, vs] = (_rms(o) * _silu(g_ref[:, vs])).astype(y_ref.dtype)

    @pl.when(c == pl.num_programs(1) - 1)
    def _():
        sout_ref[...] = st_ref[...]


def _ret_seq(proj3, t, row_blk0, nblk, lead, cos, sin, s0, out_dtype, stack=None):
    bsz = proj3.shape[0]
    wide = R_HEADS * R_VDIM
    bcast = lambda a: pl.BlockSpec((None,) + a.shape[1:], lambda b, c: (0,) * a.ndim)
    decay = jnp.asarray(_ret_decay(t))
    inputs = [proj3, proj3, proj3, proj3, cos, sin, decay, s0]
    st_shape, st_spec = _seq_state_out(stack, bsz, (R_HEADS, R_KDIM, R_VDIM))
    body, x_specs, x_in, aliases = _alias_prev(functools.partial(_ret_seq_kernel, lead, t), len(inputs),
                                               stack and stack[1], (1,))
    return pl.pallas_call(
        body,
        grid=(bsz, nblk),
        in_specs=[pl.BlockSpec((None, t, D_MODEL), lambda b, c: (b, row_blk0 + c, COL_RQ)),
                  pl.BlockSpec((None, t, D_MODEL), lambda b, c: (b, row_blk0 + c, COL_RK)),
                  pl.BlockSpec((None, t, wide), lambda b, c: (b, row_blk0 + c, COL_RV)),
                  pl.BlockSpec((None, t, wide), lambda b, c: (b, row_blk0 + c, COL_RG)),
                  pl.BlockSpec((t, R_HALF), lambda b, c: (c, 0)),
                  pl.BlockSpec((t, R_HALF), lambda b, c: (c, 0)),
                  pl.BlockSpec(decay.shape, lambda b, c: (0, 0, 0)),
                  bcast(s0)] + x_specs,
        out_specs=[pl.BlockSpec((None, t, wide), lambda b, c: (b, c, 0)), st_spec],
        out_shape=[jax.ShapeDtypeStruct((bsz, nblk * t, wide), out_dtype), st_shape],
        scratch_shapes=[pltpu.VMEM((R_HEADS, R_KDIM, R_VDIM), F32)],
        input_output_aliases=aliases,
        compiler_params=_cparams(("parallel", "arbitrary")),
        name="ret_seq",
    )(*inputs, *x_in)


def _col_pieces(x):
    return jnp.concatenate(_split3(x.T), axis=1)


def _pick_col(n):
    r = _iota((3 * LANES, LANES), 0) & (LANES - 1)
    return jnp.where(r == n, 1.0, 0.0).astype(BF16)


def _pick_col_pair(n):
    r = _iota((3 * LANES, 2 * LANES), 0) & (LANES - 1)
    want = n + jnp.right_shift(_iota((3 * LANES, 2 * LANES), 1), int(math.log2(LANES)))
    return jnp.where(r == want, 1.0, 0.0).astype(BF16)


def _step_prep_kernel(lb_is_zero, xs_ref, bc_ref, x_ref, wdt_ref, hq_ref, hf_ref, rq_ref, rk_ref, conv_ref,
                      cw_ref, cb_ref, dtb_ref, a_ref, ex_ref, lb_ref, cos_ref, sin_ref,
                      xs_o, bc_o, xdt_o, edec_o, conv_o, hq_o, hef_o, hk_o, rq_o, rk_o):
    raw = jnp.concatenate([xs_ref[...], bc_ref[...]], axis=1)
    buf = conv_ref[...]
    acc = cb_ref[...] + cw_ref[3:4, :] * raw
    for k in range(M_CONV - 1):
        acc += cw_ref[k:k + 1, :] * buf[:, k * M_CONV_DIM:(k + 1) * M_CONV_DIM]
    conv_o[:, 0:2 * M_CONV_DIM] = buf[:, M_CONV_DIM:]
    conv_o[:, 2 * M_CONV_DIM:] = raw
    xbc = _silu(acc)
    xs = xbc[:, :M_INNER]
    xs_o[...] = xs
    bc_o[...] = xbc[:, M_INNER:]
    dt = _softplus(_dot_nt(x_ref[...].astype(BF16), wdt_ref[...]) + dtb_ref[...])
    ex = ex_ref[...]
    xdt_o[...] = _col_pieces(xs * _sel_right(dt, ex))
    edec_o[...] = _col_pieces(_sel_right(jnp.exp(dt * a_ref[...]), ex))
    logf, k = _hgrn_gates(hf_ref[...], lb_ref[...], lb_is_zero)
    hq_o[...] = hq_ref[...] * (H_KDIM ** -0.5)
    hef_o[...] = _col_pieces(jnp.exp(logf))
    hk_o[...] = _col_pieces(k)
    cos, sin = cos_ref[...], sin_ref[...]
    rq = jnp.concatenate([_rotary(rq_ref[:, h * R_KDIM:(h + 1) * R_KDIM], cos, sin) for h in range(R_HEADS)], axis=1)
    rk = jnp.concatenate([_rotary(rk_ref[:, h * R_KDIM:(h + 1) * R_KDIM], cos, sin) for h in range(R_HEADS)], axis=1)
    rq_o[...] = _col_pieces(rq)
    rk_o[...] = _col_pieces(rk * (R_KDIM ** -0.5))


def _step_prep(proj_s, x_s, wt, layer, conv_flat, p, cos, sin):
    nb = conv_flat.shape[0]
    col = lambda cblk: pl.BlockSpec((nb, D_MODEL), lambda i, cblk=cblk: (0, cblk))
    full = lambda a: pl.BlockSpec(a.shape, lambda i: (0,) * a.ndim)
    params = [p["conv_w"], p["conv_b"], p["dt_bias"], p["a_neg"], p["expand"], p["lb"], cos, sin]
    assert nb == LANES
    rows = lambda w: jax.ShapeDtypeStruct((nb, w), F32)
    cols = lambda w: jax.ShapeDtypeStruct((w, 3 * nb), BF16)
    shapes = [rows(M_INNER), rows(M_BC), cols(M_INNER), cols(M_INNER), rows((M_CONV - 1) * M_CONV_DIM),
              rows(D_MODEL), cols(D_MODEL), cols(D_MODEL), cols(D_MODEL), cols(D_MODEL)]
    return pl.pallas_call(
        functools.partial(_step_prep_kernel, layer == 0),
        grid=(1,),
        in_specs=[col(COL_XS), col(COL_BC), pl.BlockSpec((nb, D_MODEL), lambda i: (0, 0)), _dt_weight_spec(layer),
                  col(COL_HQ), col(COL_HF), col(COL_RQ), col(COL_RK), full(conv_flat)]
                 + [full(a) for a in params],
        out_specs=[pl.BlockSpec(s.shape, lambda i: (0, 0)) for s in shapes],
        out_shape=shapes,
        compiler_params=_cparams(("arbitrary",)),
        name="step_prep",
    )(proj_s, proj_s, x_s, wt, proj_s, proj_s, proj_s, proj_s, conv_flat, *params)


def _ssd_step_kernel(s_ref, xdt_ref, edec_ref, bc_ref, xs_ref, z_ref, dsk_ref, nw_ref, so_ref, y_ref,
                     yt_ref, xdt_b, edec_b):
    yt_ref[...] = jnp.zeros(yt_ref.shape, F32)
    for i in range(STEP_BT):
        ls = slice((i % 2) * LANES, (i % 2 + 1) * LANES)
        if i % 2 == 0:
            pick = _pick_col_pair(pl.program_id(0) * STEP_BT + i)
            xdt_b[...] = _dot(xdt_ref[...], pick)
            edec_b[...] = _dot(edec_ref[...], pick)
        for g in range(M_GROUPS):
            gs = slice(g * M_GW, (g + 1) * M_GW)
            hs = slice(M_HPG * g, M_HPG * (g + 1))
            st = s_ref[i, hs].reshape(M_GW, M_STATE)
            brow = bc_ref[i:i + 1, g * M_STATE:(g + 1) * M_STATE]
            crow = bc_ref[i:i + 1, M_BC // 2 + g * M_STATE:M_BC // 2 + (g + 1) * M_STATE]
            new = st * edec_b[gs, ls] + xdt_b[gs, ls] * brow
            so_ref[i, hs] = new.reshape(M_HPG, M_HEADDIM, M_STATE)
            yt_ref[gs, i:i + 1] = jnp.sum(new * crow, axis=-1, keepdims=True)
    y = yt_ref[...].T[0:STEP_BT, :]
    xs = xs_ref[...]
    y = (y + dsk_ref[...] * xs) * _silu(z_ref[...])
    for g in range(M_GROUPS):
        gs = slice(g * M_GW, (g + 1) * M_GW)
        y_ref[:, gs] = _rms(y[:, gs]) * nw_ref[:, gs]


def _ssd_step(state, layer, prev, xdt_c, edec_c, bc, xs, proj_s, p):
    nb = xs.shape[0]
    bt = STEP_BT
    full = lambda a: pl.BlockSpec(a.shape, lambda j: (0,) * a.ndim)
    sspec = pl.BlockSpec((None, bt, M_HEADS, M_HEADDIM, M_STATE), lambda j: (layer, j, 0, 0, 0))
    inputs = [state, xdt_c, edec_c, bc, xs, proj_s, p["d_skip"], p["m_norm_w"]]
    body, x_specs, x_in, aliases = _alias_prev(_ssd_step_kernel, len(inputs),
                                               None if prev is None else [prev], (0,))
    return pl.pallas_call(
        body,
        grid=(nb // bt,),
        in_specs=[sspec, full(xdt_c), full(edec_c),
                  pl.BlockSpec((bt, M_BC), lambda j: (j, 0)),
                  pl.BlockSpec((bt, M_INNER), lambda j: (j, 0)),
                  pl.BlockSpec((bt, D_MODEL), lambda j: (j, COL_Z)),
                  full(p["d_skip"]), full(p["m_norm_w"])] + x_specs,
        out_specs=[sspec, pl.BlockSpec((bt, M_INNER), lambda j: (j, 0))],
        out_shape=[jax.ShapeDtypeStruct(state.shape, F32),
                   jax.ShapeDtypeStruct((nb, M_INNER), F32)],
        scratch_shapes=[pltpu.VMEM((M_INNER, LANES), F32)] + [pltpu.VMEM((M_INNER, 2 * LANES), F32)] * 2,
        input_output_aliases=aliases,
        compiler_params=_cparams(("parallel",)),
        name="ssd_step",
    )(*inputs, *x_in)


def _cols(x):
    nb, w = x.shape
    return jnp.transpose(x.reshape(nb // STEP_BT, STEP_BT, w), (0, 2, 1))


def _hgrn_step_kernel(s_ref, q_ref, ef_ref, k_ref, v_ref, g_ref, nw_ref, so_ref, y_ref, ef_b, k_b):
    for i in range(STEP_BT):
        ls = slice((i % 2) * LANES, (i % 2 + 1) * LANES)
        if i % 2 == 0:
            pick = _pick_col_pair(pl.program_id(0) * STEP_BT + i)
            ef_b[...] = _dot(ef_ref[...], pick)
            k_b[...] = _dot(k_ref[...], pick)
        for h in range(H_HEADS):
            cs = slice(h * H_KDIM, (h + 1) * H_KDIM)
            new = s_ref[i, h] * ef_b[cs, ls] + k_b[cs, ls] * v_ref[i:i + 1, cs]
            so_ref[i, h] = new
            y_ref[i:i + 1, cs] = jnp.sum(new * q_ref[cs, i:i + 1], axis=0, keepdims=True)
    for h in range(H_HEADS):
        cs = slice(h * H_KDIM, (h + 1) * H_KDIM)
        y_ref[:, cs] = _rms(y_ref[:, cs]) * nw_ref[:, cs] * _sigmoid(g_ref[:, cs])


def _hgrn_step(state, layer, prev, q_c, ef_c, k_c, proj_s, nw):
    nb = state.shape[1]
    bt = STEP_BT
    cspec = pl.BlockSpec(ef_c.shape, lambda j: (0, 0))
    qspec = pl.BlockSpec((None, D_MODEL, bt), lambda j: (j, 0, 0))
    sspec = pl.BlockSpec((None, bt, H_HEADS, H_KDIM, H_VDIM), lambda j: (layer, j, 0, 0, 0))
    inputs = [state, q_c, ef_c, k_c, proj_s, proj_s, nw]
    body, x_specs, x_in, aliases = _alias_prev(_hgrn_step_kernel, len(inputs),
                                               None if prev is None else [prev], (0,))
    return pl.pallas_call(
        body,
        grid=(nb // bt,),
        in_specs=[sspec, qspec, cspec, cspec,
                  pl.BlockSpec((bt, D_MODEL), lambda j: (j, COL_HI)),
                  pl.BlockSpec((bt, D_MODEL), lambda j: (j, COL_HG)),
                  pl.BlockSpec(nw.shape, lambda j: (0, 0))] + x_specs,
        out_specs=[sspec, pl.BlockSpec((bt, D_MODEL), lambda j: (j, 0))],
        out_shape=[jax.ShapeDtypeStruct(state.shape, F32),
                   jax.ShapeDtypeStruct((nb, D_MODEL), F32)],
        scratch_shapes=[pltpu.VMEM((D_MODEL, 2 * LANES), F32)] * 2,
        input_output_aliases=aliases,
        compiler_params=_cparams(("parallel",)),
        name="hgrn_step",
    )(*inputs, *x_in)


def _ret_step_kernel(s_ref, q_ref, k_ref, v_ref, g_ref, gam_ref, so_ref, y_ref):
    gam = gam_ref[...]
    head_rows = pl.ds(pl.multiple_of(pl.program_id(1) * R_KDIM, R_KDIM), R_KDIM)
    for i in range(STEP_BT):
        pick = _pick_col(pl.program_id(0) * STEP_BT + i)
        q_b = _dot(q_ref[head_rows, :], pick)
        k_b = _dot(k_ref[head_rows, :], pick)
        for c in range(0, R_VDIM, LANES):
            cs = slice(c, c + LANES)
            new = s_ref[i, :, cs] * gam[:, cs] + k_b * v_ref[i:i + 1, cs]
            so_ref[i, :, cs] = new
            y_ref[i:i + 1, cs] = jnp.sum(new * q_b, axis=0, keepdims=True)
    y_ref[...] = _rms(y_ref[...]) * _silu(g_ref[...])


def _ret_step(state, layer, prev, q_c, k_c, proj_s, gam):
    nb = state.shape[1]
    bt = STEP_BT
    cspec = pl.BlockSpec(q_c.shape, lambda j, h: (0, 0))
    sspec = pl.BlockSpec((None, bt, None, R_KDIM, R_VDIM), lambda j, h: (layer, j, h, 0, 0))
    inputs = [state, q_c, k_c, proj_s, proj_s, gam]
    body, x_specs, x_in, aliases = _alias_prev(_ret_step_kernel, len(inputs),
                                               None if prev is None else [prev], (0,))
    return pl.pallas_call(
        body,
        grid=(nb // bt, R_HEADS),
        in_specs=[sspec, cspec, cspec,
                  pl.BlockSpec((bt, R_VDIM), lambda j, h: (j, h)),
                  pl.BlockSpec((bt, R_VDIM), lambda j, h: (j, R_HEADS + h)),
                  pl.BlockSpec((None, 1, R_VDIM), lambda j, h: (h, 0, 0))] + x_specs,
        out_specs=[sspec, pl.BlockSpec((bt, R_VDIM), lambda j, h: (j, h))],
        out_shape=[jax.ShapeDtypeStruct(state.shape, F32),
                   jax.ShapeDtypeStruct((nb, R_HEADS * R_VDIM), F32)],
        input_output_aliases=aliases,
        compiler_params=_cparams(("parallel", "parallel")),
        name="ret_step",
    )(*inputs, *x_in)


def _rope_tables(positions):
    inv_freq = 1.0 / (ROPE_BASE ** jnp.linspace(0.0, 1.0, R_HALF, dtype=F32))
    ang = positions[:, None] * inv_freq[None, :]
    return jnp.cos(ang), jnp.sin(ang)


def _per_channel(v):
    return jnp.repeat(v.astype(F32), M_HEADDIM).reshape(1, M_INNER)


def _pad_lanes(v):
    return jnp.pad(v.astype(F32), (0, LANES - v.shape[0])).reshape(1, LANES)


def kernel(x_prompt, x_sample, state_ssm, state_conv, state_hgrn, state_ret, meta_tokens, ln_in_g, ln_in_b,
           w_in, conv_w, conv_b, dt_bias, a_log, d_skip, m_norm_w, hgrn_lb_logits, h_norm_w, w_br_m, w_br_h,
           w_br_r, w_out, ln1_g, ln1_b, w_ffn_in, w_ffn_out, ln2_g, ln2_b):
    bp, sp = x_prompt.shape[0], x_prompt.shape[1]
    nb = x_sample.shape[0]
    assert x_sample.shape[1] == 1 and nb == SMALL_ROWS - SEQ_BLOCK and nb % STEP_BT == 0
    assert sp % RET_BLOCK == 0 and meta_tokens.shape[0] == N_META

    wt = jnp.swapaxes(w_in, 1, 2).astype(BF16)
    wm_b, wh_b, wr_b, wo_b = (w.astype(BF16) for w in (w_br_m, w_br_h, w_br_r, w_out))
    wfi_b, wfo_b = w_ffn_in.astype(BF16), w_ffn_out.astype(BF16)
    ln1 = (ln1_g.reshape(DEPTH, 1, D_MODEL), ln1_b.reshape(DEPTH, 1, D_MODEL))
    ln2 = (ln2_g.reshape(DEPTH, 1, D_MODEL), ln2_b.reshape(DEPTH, 1, D_MODEL))
    lb_cum = jnp.cumsum(jax.nn.softmax(hgrn_lb_logits.astype(F32), axis=0), axis=0)
    lbs = lb_cum - lb_cum[0]
    expand = (np.arange(LANES)[:, None] == (np.arange(M_INNER)[None, :] // M_HEADDIM)).astype(np.float32)
    expand = jnp.asarray(np.concatenate([expand] * 3, 0), BF16)
    h_sums, h_pair = _hgrn_tables()
    h_tables = (jnp.asarray(h_sums, BF16), jnp.asarray(h_pair, F32))
    gam = jnp.asarray(np.broadcast_to(
        np.array([1.0 - 2.0 ** (-5.0 - h) for h in range(R_HEADS)], np.float32)[:, None, None],
        (R_HEADS, 1, R_VDIM)))

    pos_real = jnp.arange(N_META, N_META + sp, dtype=F32)
    pos_meta = jnp.maximum(jnp.arange(SEQ_BLOCK, dtype=F32) - META_LEAD, 0.0)
    pos_samp = jnp.full((nb,), float(PAST_LEN), F32)
    cos_r, sin_r = _rope_tables(pos_real)
    cos_m, sin_m = _rope_tables(pos_meta)
    cos_s, sin_s = _rope_tables(pos_samp)

    x_real = _layer_norm_rows(x_prompt.reshape(bp * sp, D_MODEL), ln_in_g, ln_in_b, 512)
    small_in = jnp.concatenate([x_sample.reshape(nb, D_MODEL),
                                jnp.zeros((META_LEAD, D_MODEL), F32), meta_tokens.astype(F32)], axis=0)
    x_small = _layer_norm_rows(small_in, ln_in_g, ln_in_b, SMALL_ROWS)

    z_ssm = jnp.zeros((1, M_HEADS, M_HEADDIM, M_STATE), F32)
    z_conv = jnp.zeros((1, 8, M_CONV_DIM), F32)
    z_hgrn = jnp.zeros((1, H_HEADS, H_KDIM, H_VDIM), F32)
    z_ret = jnp.zeros((1, R_HEADS, R_KDIM, R_VDIM), F32)
    ssm_p = conv_p = hgrn_p = ret_p = ssm_s = hgrn_s = ret_s = None
    conv_s = []
    for l in range(DEPTH):
        p = dict(conv_w=conv_w[l], conv_b=conv_b[l].reshape(1, -1), dt_bias=_pad_lanes(dt_bias[l]),
                 a_neg=_pad_lanes(-jnp.exp(a_log[l].astype(F32))), d_skip=_per_channel(d_skip[l]),
                 m_norm_w=m_norm_w[l].reshape(1, -1), expand=expand, lb=lbs[l].reshape(1, -1))
        hnw = h_norm_w[l].reshape(1, -1)

        proj_r = _proj(x_real, wt, l, PROJ_TM, "proj_real")
        proj_s = _proj(x_small, wt, l, SMALL_ROWS, "proj_small")
        proj_r3, x_real3 = proj_r.reshape(bp, sp, PROJ_COLS), x_real.reshape(bp, sp, D_MODEL)
        proj_s3, x_small3 = proj_s.reshape(1, SMALL_ROWS, PROJ_COLS), x_small.reshape(1, SMALL_ROWS, D_MODEL)

        ym_m, ssm_m, conv_m = _ssd_seq(proj_s3, x_small3, wt, l, 1, 1, META_LEAD, p, z_ssm, z_conv, F32)
        yh_m, hgrn_m = _hgrn_seq(proj_s3, 1, 1, META_LEAD, p["lb"], l == 0, hnw, h_tables, z_hgrn, F32)
        yr_m, ret_m = _ret_seq(proj_s3, SEQ_BLOCK, 1, 1, META_LEAD, cos_m, sin_m, z_ret, F32)

        conv0 = jnp.pad(conv_m, ((0, 0), (8 - (M_CONV - 1), 0), (0, 0)))
        ym_r, ssm_p, conv_p = _ssd_seq(proj_r3, x_real3, wt, l, 0, sp // SEQ_BLOCK, 0, p, ssm_m, conv0, BF16,
                                       stack=(l, None if l == 0 else [ssm_p, conv_p]))
        yh_r, hgrn_p = _hgrn_seq(proj_r3, 0, sp // SEQ_BLOCK, 0, p["lb"], l == 0, hnw, h_tables, hgrn_m, BF16,
                                 stack=(l, None if l == 0 else [hgrn_p]))
        yr_r, ret_p = _ret_seq(proj_r3, RET_BLOCK, 0, sp // RET_BLOCK, 0, cos_r, sin_r, ret_m, BF16,
                               stack=(l, None if l == 0 else [ret_p]))

        conv_flat = state_conv[l].reshape(nb, (M_CONV - 1) * M_CONV_DIM)
        (xs_s, bc_s, xdt_s, edec_s, conv_new, hq_s, hef_s, hk_s, rq_s, rk_s) = _step_prep(
            proj_s, x_small, wt, l, conv_flat, p, cos_s, sin_s)
        ssm_s, ym_s = _ssd_step(state_ssm, l, ssm_s, xdt_s, edec_s, bc_s, xs_s, proj_s, p)
        hgrn_s, yh_s = _hgrn_step(state_hgrn, l, hgrn_s, _cols(hq_s), hef_s, hk_s, proj_s, hnw)
        ret_s, yr_s = _ret_step(state_ret, l, ret_s, rq_s, rk_s, proj_s, gam)
        conv_s.append(conv_new.reshape(nb, M_CONV - 1, M_CONV_DIM))

        ym_small = jnp.concatenate([ym_s, ym_m[0]], axis=0)
        yh_small = jnp.concatenate([yh_s, yh_m[0]], axis=0)
        yr_small = jnp.concatenate([yr_s, yr_m[0]], axis=0)
        x_real = _mix(x_real, ym_r.reshape(bp * sp, -1), yh_r.reshape(bp * sp, -1), yr_r.reshape(bp * sp, -1),
                      proj_r, l, wm_b, wh_b, wr_b, wo_b, *ln1, 256)
        x_small = _mix(x_small, ym_small, yh_small, yr_small, proj_s, l, wm_b, wh_b, wr_b, wo_b, *ln1, SMALL_ROWS)
        x_real = _ffn(x_real, l, wfi_b, wfo_b, *ln2, 512)
        x_small = _ffn(x_small, l, wfi_b, wfo_b, *ln2, SMALL_ROWS)

    return (x_real.reshape(bp, sp, D_MODEL), x_small[:nb].reshape(nb, 1, D_MODEL),
            ssm_p, conv_p, hgrn_p, ret_p, ssm_s, jnp.stack(conv_s), hgrn_s, ret_s)
```

```python
import functools
import math

import numpy as np
import jax
import jax.numpy as jnp
from jax import lax
from jax.experimental import pallas as pl
from jax.experimental.pallas import tpu as pltpu

F32 = jnp.float32
BF16 = jnp.bfloat16

D_MODEL = 1024
DEPTH = 2
N_META = 16
M_INNER = D_MODEL
M_HEADDIM = 64
M_HEADS = M_INNER // M_HEADDIM
M_GROUPS = 4
M_HPG = M_HEADS // M_GROUPS
M_STATE = 128
M_CONV = 4
M_BC = 2 * M_GROUPS * M_STATE
M_CONV_DIM = M_INNER + M_BC
M_GW = M_INNER // M_GROUPS
H_KDIM = 128
H_HEADS = D_MODEL // H_KDIM
H_VDIM = 128
R_HEADS = 4
R_KDIM = D_MODEL // R_HEADS
R_VDIM = 2 * R_KDIM
R_HALF = R_KDIM // 2
ROPE_BASE = 10000.0
D_FF = ((8 * D_MODEL // 3 + 255) // 256) * 256
FF_CHUNK = 256
DN_ALPHA = (2 * DEPTH) ** 0.25
PAST_LEN = 16384

LANES = 128
SEQ_BLOCK = 128
RET_BLOCK = 256
SMALL_ROWS = 256
META_LEAD = SEQ_BLOCK - N_META
STEP_BT = 8
VMEM_LIMIT = 56 * 1024 * 1024

COL_RV, COL_RG = 0, 1
COL_Z, COL_XS, COL_BC, COL_HQ, COL_HF, COL_HI, COL_HG, COL_RQ, COL_RK, COL_GM, COL_GH, COL_GR = range(4, 16)
PROJ_COLS = 16 * 1024

_ORIG_SPLITS = (M_INNER, M_CONV_DIM, M_HEADS, 1024, 1024, 1024, 1024, 1024, 1024, 2048, 2048, 3072)
_ORIG_OFF = np.concatenate([[0], np.cumsum(_ORIG_SPLITS)]).tolist()


def _cparams(sem):
    return pltpu.CompilerParams(dimension_semantics=sem, vmem_limit_bytes=VMEM_LIMIT)


def _sigmoid(x):
    return 1.0 / (1.0 + jnp.exp(-x))


def _silu(x):
    return x * _sigmoid(x)


def _softplus(x):
    return jnp.maximum(x, 0.0) + jnp.log1p(jnp.exp(-jnp.abs(x)))


def _layer_norm(x, g, b):
    mu = jnp.mean(x, axis=-1, keepdims=True)
    xc = x - mu
    var = jnp.mean(xc * xc, axis=-1, keepdims=True)
    return xc * lax.rsqrt(var + 1e-5) * g + b


def _rms(x):
    return x * lax.rsqrt(jnp.mean(x * x, axis=-1, keepdims=True) + 1e-6)


def _split3(x):
    hi = x.astype(BF16)
    r = x - hi.astype(F32)
    mid = r.astype(BF16)
    lo = (r - mid.astype(F32)).astype(BF16)
    return hi, mid, lo


def _dot(a, b):
    return jnp.dot(a, b, preferred_element_type=F32)


def _dot_nt(a, b):
    return lax.dot_general(a, b, (((1,), (1,)), ((), ())), preferred_element_type=F32)


def _sel_right(x, m3):
    return _dot(jnp.concatenate(_split3(x), axis=1), m3)


def _sel_left(m3, x):
    return _dot(m3, jnp.concatenate(_split3(x), axis=0))


def _iota(shape, dim):
    return lax.broadcasted_iota(jnp.int32, shape, dim)


def _ln_kernel(x_ref, g_ref, b_ref, o_ref):
    o_ref[...] = _layer_norm(x_ref[...], g_ref[...], b_ref[...])


def _layer_norm_rows(x, g, b, tm):
    m = x.shape[0]
    tm = min(tm, m)
    return pl.pallas_call(
        _ln_kernel,
        grid=(m // tm,),
        in_specs=[pl.BlockSpec((tm, D_MODEL), lambda i: (i, 0)),
                  pl.BlockSpec((1, D_MODEL), lambda i: (0, 0)),
                  pl.BlockSpec((1, D_MODEL), lambda i: (0, 0))],
        out_specs=pl.BlockSpec((tm, D_MODEL), lambda i: (i, 0)),
        out_shape=jax.ShapeDtypeStruct((m, D_MODEL), F32),
        compiler_params=_cparams(("parallel",)),
        name="ln_in",
    )(x, g.reshape(1, -1), b.reshape(1, -1))


def _proj_kernel(x_ref, w_ref, o_ref, xb_ref):
    @pl.when(pl.program_id(1) == 0)
    def _():
        xb_ref[...] = x_ref[...].astype(BF16)

    o_ref[...] = _dot_nt(xb_ref[...], w_ref[0])


def _layer_block(a, layer):
    return pl.BlockSpec((None,) + a.shape[1:], lambda *_: (layer,) + (0,) * (a.ndim - 1))


_W_RUNS = ((0, _ORIG_OFF[9]), (4096, _ORIG_OFF[0]), (7168, _ORIG_OFF[3]), (13312, _ORIG_OFF[11]))
PROJ_TN = 1024
PROJ_TM = 2048


def _orig_col(j):
    c = j * PROJ_TN
    off = c - _W_RUNS[0][0] + _W_RUNS[0][1]
    for new0, orig0 in _W_RUNS[1:]:
        off = jnp.where(c >= new0, c - new0 + orig0, off)
    return pl.multiple_of(off, M_HEADS)


def _proj(x, wt, layer, tm, name):
    m = x.shape[0]
    tm = min(tm, m)
    return pl.pallas_call(
        _proj_kernel,
        grid=(m // tm, PROJ_COLS // PROJ_TN),
        in_specs=[pl.BlockSpec((tm, D_MODEL), lambda i, j: (i, 0)),
                  pl.BlockSpec((pl.Element(1), pl.Element(PROJ_TN), pl.Element(D_MODEL)),
                               lambda i, j: (layer, _orig_col(j), 0))],
        out_specs=pl.BlockSpec((tm, PROJ_TN), lambda i, j: (i, j)),
        out_shape=jax.ShapeDtypeStruct((m, PROJ_COLS), F32),
        scratch_shapes=[pltpu.VMEM((tm, D_MODEL), BF16)],
        compiler_params=_cparams(("parallel", "arbitrary")),
        name=name,
    )(x, wt)


def _dt_weight_spec(layer):
    assert _ORIG_OFF[2] % LANES == 0
    return pl.BlockSpec((None, LANES, D_MODEL), lambda *_: (layer, _ORIG_OFF[2] // LANES, 0))


def _mix_kernel(x_ref, ym_ref, yh_ref, yr_ref, gm_ref, gh_ref, gr_ref,
                wm_ref, wh_ref, wr_ref, wo_ref, g_ref, b_ref, o_ref):
    mixed = _sigmoid(gm_ref[...]) * _dot(ym_ref[...].astype(BF16), wm_ref[...])
    mixed += _sigmoid(gh_ref[...]) * _dot(yh_ref[...].astype(BF16), wh_ref[...])
    mixed += _sigmoid(gr_ref[...]) * _dot(yr_ref[...].astype(BF16), wr_ref[...])
    h = _dot(mixed.astype(BF16), wo_ref[...])
    o_ref[...] = _layer_norm(DN_ALPHA * x_ref[...] + h, g_ref[...], b_ref[...])


def _mix(x, ym, yh, yr, proj, layer, wm, wh, wr, wo, g2, b2, tm):
    m = x.shape[0]
    tm = min(tm, m)
    row = lambda w: pl.BlockSpec((tm, w), lambda i: (i, 0))
    col = lambda c: pl.BlockSpec((tm, D_MODEL), lambda i, c=c: (i, c))
    full = lambda a: _layer_block(a, layer)
    return pl.pallas_call(
        _mix_kernel,
        grid=(m // tm,),
        in_specs=[row(D_MODEL), row(M_INNER), row(D_MODEL), row(R_HEADS * R_VDIM),
                  col(COL_GM), col(COL_GH), col(COL_GR),
                  full(wm), full(wh), full(wr), full(wo), full(g2), full(b2)],
        out_specs=row(D_MODEL),
        out_shape=jax.ShapeDtypeStruct((m, D_MODEL), F32),
        compiler_params=_cparams(("parallel",)),
        name="mix",
    )(x, ym, yh, yr, proj, proj, proj, wm, wh, wr, wo, g2, b2)


def _ffn_kernel(x_ref, wi_ref, wo_ref, g_ref, b_ref, o_ref):
    x = x_ref[...]
    xb = x.astype(BF16)
    acc = jnp.zeros(x.shape, F32)
    for j in range(0, D_FF, FF_CHUNK):
        hg = _dot(xb, wi_ref[:, j:j + FF_CHUNK])
        hu = _dot(xb, wi_ref[:, D_FF + j:D_FF + j + FF_CHUNK])
        acc += _dot((_silu(hg) * hu).astype(BF16), wo_ref[j:j + FF_CHUNK, :])
    o_ref[...] = _layer_norm(DN_ALPHA * x + acc, g_ref[...], b_ref[...])


def _ffn(x, layer, wi, wo, g2, b2, tm):
    m = x.shape[0]
    tm = min(tm, m)
    full = lambda a: _layer_block(a, layer)
    return pl.pallas_call(
        _ffn_kernel,
        grid=(m // tm,),
        in_specs=[pl.BlockSpec((tm, D_MODEL), lambda i: (i, 0)), full(wi), full(wo), full(g2), full(b2)],
        out_specs=pl.BlockSpec((tm, D_MODEL), lambda i: (i, 0)),
        out_shape=jax.ShapeDtypeStruct((m, D_MODEL), F32),
        compiler_params=_cparams(("parallel",)),
        name="ffn",
    )(x, wi, wo, g2, b2)


def _ssd_seq_kernel(lead, z_ref, xs_ref, bc_ref, x_ref, wdt_ref, cw_ref, cb_ref, dtb_ref, a_ref, dsk_ref, nw_ref,
                    ex_ref, s0_ref, c0_ref, y_ref, sout_ref, cout_ref, ext_ref, st_ref, xbc_ref):
    t = SEQ_BLOCK
    c = pl.program_id(1)

    @pl.when(c == 0)
    def _():
        ext_ref[0:8, :] = c0_ref[...]
        for g in range(M_GROUPS):
            st_ref[g] = s0_ref[M_HPG * g:M_HPG * (g + 1)].reshape(M_GW, M_STATE).T

    rows = _iota((t, 1), 0)

    def conv_chunk(q):
        w = M_CONV_DIM // 4
        cs = slice(q * w, (q + 1) * w)
        src = xs_ref if q < 2 else bc_ref
        raw = src[:, (q % 2) * w:(q % 2 + 1) * w]
        if lead:
            raw = jnp.where(rows >= lead, raw, 0.0)
        ext_ref[8:8 + t, cs] = raw
        acc = cb_ref[:, cs] + cw_ref[3:4, cs] * raw
        for k in range(M_CONV - 1):
            acc += cw_ref[k:k + 1, cs] * ext_ref[5 + k:5 + k + t, cs]
        ext_ref[5:8, cs] = ext_ref[t + 5:t + 8, cs]
        xbc_ref[:, cs] = _silu(acc)

    dt_raw = _dot_nt(x_ref[...].astype(BF16), wdt_ref[...])
    conv_chunk(0)
    dt = _softplus(dt_raw + dtb_ref[...])
    if lead:
        dt = jnp.where(rows >= lead, dt, 0.0)
    a = dt * a_ref[...]
    ti, si = _iota((t, t), 0), _iota((t, t), 1)
    tril = si <= ti
    tri = jnp.where(tril, 1.0, 0.0).astype(BF16)
    cum = _sel_left(jnp.concatenate([tri] * 3, axis=1), a)
    conv_chunk(1)
    ecum = jnp.exp(cum)
    tailw = jnp.exp(cum[t - 1:t, :] - cum) * dt
    cum_t, dt_t = cum.T, dt.T
    ex = ex_ref[...]
    ecum_full = _sel_right(ecum, ex)
    conv_chunk(2)
    tailw_full = _sel_right(tailw, ex)
    conv_chunk(3)

    xs = xbc_ref[:, 0:M_INNER]
    xw = (xs * tailw_full).astype(BF16)
    xs_b = xs.astype(BF16)
    head_of_lane = jnp.right_shift(_iota((1, M_GW), 1), int(math.log2(M_HEADDIM)))

    for g in range(M_GROUPS):
        gs = slice(g * M_GW, (g + 1) * M_GW)
        bm_f = xbc_ref[:, M_INNER + g * M_STATE:M_INNER + (g + 1) * M_STATE]
        bm = bm_f.astype(BF16)
        cm = xbc_ref[:, M_INNER + M_BC // 2 + g * M_STATE:M_INNER + M_BC // 2 + (g + 1) * M_STATE].astype(BF16)
        cb = _dot_nt(cm, bm)
        st = st_ref[g]
        y_g = _dot(cm, st.astype(BF16)) * ecum_full[:, gs]
        for hh in range(M_HPG):
            h = g * M_HPG + hh
            diff = cum[:, h:h + 1] - cum_t[h:h + 1, :]
            w = cb * jnp.exp(jnp.where(tril, diff, -1e30)) * dt_t[h:h + 1, :]
            x_h = jnp.where(head_of_lane == hh, xs_b[:, gs], jnp.zeros((), BF16))
            y_g = y_g + _dot(w.astype(BF16), x_h)
        st_ref[g] = st * ecum_full[t - 1:t, gs] + _dot(bm_f.T.astype(BF16), xw[:, gs])
        y_g = (y_g + dsk_ref[:, gs] * xs[:, gs]) * _silu(z_ref[:, gs])
        y_ref[:, gs] = (_rms(y_g) * nw_ref[:, gs]).astype(y_ref.dtype)

    @pl.when(c == pl.num_programs(1) - 1)
    def _():
        cout_ref[...] = ext_ref[5:8, :]
        for g in range(M_GROUPS):
            sout_ref[M_HPG * g:M_HPG * (g + 1)] = st_ref[g].T.reshape(M_HPG, M_HEADDIM, M_STATE)


def _alias_prev(body, n_in, prevs, out_ids):
    if not prevs:
        return body, [], [], {}
    k = len(prevs)
    wrapped = lambda *refs: body(*refs[:n_in], *refs[n_in + k:])
    return (wrapped, [pl.BlockSpec(memory_space=pl.ANY)] * k, list(prevs),
            {n_in + i: o for i, o in enumerate(out_ids)})


def _seq_state_out(stack, bsz, dims):
    zeros = (0,) * len(dims)
    if stack is None:
        return (jax.ShapeDtypeStruct((bsz,) + dims, F32),
                pl.BlockSpec((None,) + dims, lambda b, c: (b,) + zeros))
    layer = stack[0]
    return (jax.ShapeDtypeStruct((DEPTH, bsz) + dims, F32),
            pl.BlockSpec((None, None) + dims, lambda b, c: (layer, b) + zeros))


def _ssd_seq(proj3, x3, wt, layer, row_blk0, nblk, lead, p, s0, c0, out_dtype, stack=None):
    bsz = proj3.shape[0]
    t = SEQ_BLOCK
    colspec = lambda cblk: pl.BlockSpec((None, t, D_MODEL), lambda b, c, cblk=cblk: (b, row_blk0 + c, cblk))
    full = lambda a: pl.BlockSpec(a.shape, lambda b, c: (0,) * a.ndim)
    bcast = lambda a: pl.BlockSpec((None,) + a.shape[1:], lambda b, c: (0,) * a.ndim)
    params = [p["conv_w"], p["conv_b"], p["dt_bias"], p["a_neg"], p["d_skip"], p["m_norm_w"], p["expand"]]
    inputs = [proj3, proj3, proj3, x3, wt, *params, s0, c0]
    st_shape, st_spec = _seq_state_out(stack, bsz, (M_HEADS, M_HEADDIM, M_STATE))
    cv_shape, cv_spec = _seq_state_out(stack, bsz, (M_CONV - 1, M_CONV_DIM))
    body, x_specs, x_in, aliases = _alias_prev(functools.partial(_ssd_seq_kernel, lead), len(inputs),
                                               stack and stack[1], (1, 2))
    return pl.pallas_call(
        body,
        grid=(bsz, nblk),
        in_specs=[colspec(COL_Z), colspec(COL_XS), colspec(COL_BC),
                  pl.BlockSpec((None, t, D_MODEL), lambda b, c: (b, row_blk0 + c, 0)), _dt_weight_spec(layer)]
                 + [full(a) for a in params] + [bcast(s0), bcast(c0)] + x_specs,
        out_specs=[pl.BlockSpec((None, t, M_INNER), lambda b, c: (b, c, 0)), st_spec, cv_spec],
        out_shape=[jax.ShapeDtypeStruct((bsz, nblk * t, M_INNER), out_dtype), st_shape, cv_shape],
        scratch_shapes=[pltpu.VMEM((t + 8, M_CONV_DIM), F32),
                        pltpu.VMEM((M_GROUPS, M_STATE, M_GW), F32),
                        pltpu.VMEM((t, M_CONV_DIM), F32)],
        input_output_aliases=aliases,
        compiler_params=_cparams(("parallel", "arbitrary")),
        name="ssd_seq",
    )(*inputs, *x_in)


def _hgrn_gates(fz, lb, lb_is_zero):
    e = jnp.exp(-jnp.abs(fz))
    r = 1.0 / (1.0 + e)
    pos = fz >= 0.0
    sig_neg = jnp.where(pos, e * r, r)
    log_sig = jnp.minimum(fz, 0.0) - jnp.log(1.0 + e)
    if lb_is_zero:
        return log_sig, sig_neg
    sig_pos = jnp.where(pos, r, e * r)
    logf = jnp.where(lb > 0.0, jnp.log(lb + (1.0 - lb) * sig_pos), log_sig)
    return logf, (1.0 - lb) * sig_neg


H_LEVELS = int(math.log2(SEQ_BLOCK))


def _hgrn_tables():
    n = SEQ_BLOCK
    t = np.arange(n)[:, None]
    u = np.arange(n)[None, :]
    pair = []
    for b in range(H_LEVELS):
        bit = ((t >> b) & 1) == 1
        pair.append(((t >> (b + 1)) == (u >> (b + 1))) & bit & (((u >> b) & 1) == 0))
    m1 = ((t >> 1) | 1) << 1
    lvl1 = np.where(((t >> 1) & 1) == 1, (u >= m1) & (u <= t), (u > t) & (u < m1))
    sums = np.concatenate([u <= t, lvl1], 0).astype(np.float32)
    return np.concatenate([sums] * 3, 1), np.concatenate(pair, 0).astype(np.float32)


def _exp_neg_abs(d):
    return jnp.exp2(jnp.abs(d) * (-1.0 / math.log(2.0)))


def _ret_token_update(n, s_ref, q_ref, k_ref, v_ref, g_ref, gam_ref, so_ref, y_ref):
    pick = _pick_col(n)
    for h in range(R_HEADS):
        rows = slice(h * R_KDIM, (h + 1) * R_KDIM)
        q_b = _dot(q_ref[rows, :], pick)
        k_b = _dot(k_ref[rows, :], pick)
        outs = []
        for c0 in range(0, R_VDIM, LANES):
            cs = slice(c0, c0 + LANES)
            new = s_ref[h, :, cs] * gam_ref[h, :, cs] + k_b * v_ref[:, h * R_VDIM + c0:h * R_VDIM + c0 + LANES]
            so_ref[h, :, cs] = new
            outs.append(jnp.sum(new * q_b, axis=0, keepdims=True))
        vs = slice(h * R_VDIM, (h + 1) * R_VDIM)
        y_ref[:, vs] = _rms(jnp.concatenate(outs, axis=1)) * _silu(g_ref[:, vs])


def _hgrn_seq_kernel(lead, lb_is_zero, ride, *refs):
    (q_ref, f_ref, i_ref, g_ref, lb_ref, nw_ref, sums_ref, pair_ref, s0_ref), refs = refs[:9], refs[9:]
    if ride:
        ride_in, refs = refs[:6], refs[6:]
    (y_ref, sout_ref), refs = refs[:2], refs[2:]
    if ride:
        ride_out, refs = refs[:2], refs[2:]
    st_ref, ex_ref, q_s, k_s, z_ref, qd_ref, kd_ref, sc_ref = refs
    t = SEQ_BLOCK
    c = pl.program_id(1)
    if ride:
        _ret_token_update(pl.program_id(0) * pl.num_programs(1) + c, *ride_in, *ride_out)

    @pl.when(c == 0)
    def _():
        for h in range(H_HEADS):
            st_ref[h] = s0_ref[h].T

    rows = _iota((t, 1), 0)
    logf, k = _hgrn_gates(f_ref[...], lb_ref[...], lb_is_zero)
    if lead:
        logf = jnp.where(rows >= lead, logf, 0.0)
        k = jnp.where(rows >= lead, k, 0.0)
    q = q_ref[...] * (H_KDIM ** -0.5)
    q_s[...] = q
    k_s[...] = k
    ex_ref[...] = _dot(sums_ref[...], jnp.concatenate(_split3(logf), axis=0))

    def side_of(b):
        return (jnp.right_shift(rows, b) & 1) == 1

    z_ref[0] = jnp.where(side_of(0), q * jnp.exp(logf), k).astype(BF16)
    z_ref[1] = (jnp.exp(ex_ref[t:2 * t, :]) * jnp.where(side_of(1), q_s[...], k_s[...])).astype(BF16)
    for b in range(2, H_LEVELS):
        half = 1 << b
        groups = range(0, t, 2 * half)
        d = jnp.concatenate([ex_ref[g0:g0 + 2 * half, :] - ex_ref[g0 + half - 1:g0 + half, :]
                             for g0 in groups], axis=0)
        if half >= 8:
            qk = jnp.concatenate([ref[g0 + o:g0 + o + half, :] for g0 in groups
                                  for ref, o in ((k_s, 0), (q_s, half))], axis=0)
        else:
            qk = jnp.where(side_of(b), q_s[...], k_s[...])
        z_ref[b] = (_exp_neg_abs(d) * qk).astype(BF16)
    cum = ex_ref[0:t, :]
    qd_ref[...] = (q_s[...] * jnp.exp(cum)).astype(BF16)
    kd_ref[...] = (k_s[...] * jnp.exp(ex_ref[t - 1:t, :] - cum)).astype(BF16)

    for h in range(H_HEADS):
        cs = slice(h * H_KDIM, (h + 1) * H_KDIM)
        scores = None
        for b in range(H_LEVELS):
            z = z_ref[b, :, cs]
            p = _dot_nt(z, z) * pair_ref[b * t:(b + 1) * t, :]
            scores = p if scores is None else scores + p
        sc_ref[h] = scores.astype(BF16)

    for h in range(H_HEADS):
        cs = slice(h * H_KDIM, (h + 1) * H_KDIM)
        v = i_ref[:, cs]
        st = st_ref[h]
        o = (_dot(sc_ref[h], v.astype(BF16))
             + jnp.sum(q_s[:, cs] * k_s[:, cs], axis=-1, keepdims=True) * v
             + _dot_nt(qd_ref[:, cs], st.astype(BF16)))
        y = _rms(o) * nw_ref[:, cs] * _sigmoid(g_ref[:, cs])
        y_ref[:, cs] = y.astype(y_ref.dtype)
        st_ref[h] = st * jnp.exp(ex_ref[t - 1:t, cs]) + _dot(v.T.astype(BF16), kd_ref[:, cs])

    @pl.when(c == pl.num_programs(1) - 1)
    def _():
        for h in range(H_HEADS):
            sout_ref[h] = st_ref[h].T


def _hgrn_seq(proj3, row_blk0, nblk, lead, lb, lb_is_zero, nw, tables, s0, out_dtype, stack=None, ride=None):
    bsz = proj3.shape[0]
    t = SEQ_BLOCK
    colspec = lambda cblk: pl.BlockSpec((None, t, D_MODEL), lambda b, c, cblk=cblk: (b, row_blk0 + c, cblk))
    full = lambda a: pl.BlockSpec(a.shape, lambda b, c: (0,) * a.ndim)
    bcast = lambda a: pl.BlockSpec((None,) + a.shape[1:], lambda b, c: (0,) * a.ndim)
    inputs = [proj3, proj3, proj3, proj3, lb, nw, *tables, s0]
    in_specs = [colspec(COL_HQ), colspec(COL_HF), colspec(COL_HI), colspec(COL_HG),
                full(lb), full(nw), full(tables[0]), full(tables[1]), bcast(s0)]
    st_shape, st_spec = _seq_state_out(stack, bsz, (H_HEADS, H_KDIM, H_VDIM))
    out_specs = [pl.BlockSpec((None, t, D_MODEL), lambda b, c: (b, c, 0)), st_spec]
    out_shape = [jax.ShapeDtypeStruct((bsz, nblk * t, D_MODEL), out_dtype), st_shape]
    alias_outs = (1,)
    if ride:
        state, q_c, k_c, v3, g3, gam = ride
        layer, nb = stack[0], state.shape[1]
        assert bsz * nblk == nb
        seq_row = lambda a: pl.BlockSpec((None,) + a.shape[1:], lambda b, c: (b * nblk + c, 0, 0))
        sspec = pl.BlockSpec((None, None, R_HEADS, R_KDIM, R_VDIM), lambda b, c: (layer, b * nblk + c, 0, 0, 0))
        inputs += [state, q_c, k_c, v3, g3, gam]
        in_specs += [sspec, full(q_c), full(k_c), seq_row(v3), seq_row(g3), full(gam)]
        out_specs += [sspec, seq_row(v3)]
        out_shape += [jax.ShapeDtypeStruct(state.shape, F32), jax.ShapeDtypeStruct(v3.shape, F32)]
        alias_outs = (1, 2)
    body, x_specs, x_in, aliases = _alias_prev(
        functools.partial(_hgrn_seq_kernel, lead, lb_is_zero, bool(ride)), len(inputs), stack and stack[1], alias_outs)
    return pl.pallas_call(
        body,
        grid=(bsz, nblk),
        in_specs=in_specs + x_specs,
        out_specs=out_specs,
        out_shape=out_shape,
        input_output_aliases=aliases,
        scratch_shapes=[pltpu.VMEM((H_HEADS, H_VDIM, H_KDIM), F32),
                        pltpu.VMEM((2 * t, D_MODEL), F32),
                        pltpu.VMEM((t, D_MODEL), F32), pltpu.VMEM((t, D_MODEL), F32),
                        pltpu.VMEM((H_LEVELS, t, D_MODEL), BF16),
                        pltpu.VMEM((t, D_MODEL), BF16), pltpu.VMEM((t, D_MODEL), BF16),
                        pltpu.VMEM((H_HEADS, t, t), BF16)],
        compiler_params=_cparams(("parallel", "arbitrary")),
        name="hgrn_seq",
    )(*inputs, *x_in)


def _log_gamma(h):
    return math.log(1.0 - 2.0 ** (-5.0 - h))


def _rotary(x, cos, sin):
    x1, x2 = x[:, :R_HALF], x[:, R_HALF:]
    return jnp.concatenate([x1 * cos - x2 * sin, x2 * cos + x1 * sin], axis=1)


def _ret_decay(t):
    d = np.arange(t)[:, None] - np.arange(t)[None, :]
    return np.stack([np.where(d >= 0, np.exp(np.maximum(d, 0) * _log_gamma(h)), 0.0)
                     for h in range(R_HEADS)]).astype(np.float32)


def _ret_seq_kernel(lead, t, q_ref, k_ref, v_ref, g_ref, cos_ref, sin_ref, dec_ref, s0_ref,
                    y_ref, sout_ref, st_ref):
    c = pl.program_id(1)

    @pl.when(c == 0)
    def _():
        st_ref[...] = s0_ref[...]

    cos, sin = cos_ref[...], sin_ref[...]
    tcol = _iota((t, 1), 0).astype(F32)
    for h in range(R_HEADS):
        lg = _log_gamma(h)
        ks = slice(h * R_KDIM, (h + 1) * R_KDIM)
        vs = slice(h * R_VDIM, (h + 1) * R_VDIM)
        qh = _rotary(q_ref[:, ks], cos, sin)
        kh = _rotary(k_ref[:, ks], cos, sin) * (R_KDIM ** -0.5)
        if lead:
            kh = jnp.where(_iota((t, 1), 0) >= lead, kh, 0.0)
        qb, kb, vb = qh.astype(BF16), kh.astype(BF16), v_ref[:, vs].astype(BF16)
        scores = _dot_nt(qb, kb) * dec_ref[h]
        st = st_ref[h]
        o = _dot(scores.astype(BF16), vb) + _dot(qb, st.astype(BF16)) * jnp.exp((tcol + 1.0) * lg)
        kdec = (kh * jnp.exp((t - 1.0 - tcol) * lg)).T.astype(BF16)
        st_ref[h] = st * math.exp(t * lg) + _dot(kdec, vb)
        y_ref[:, vs] = (_rms(o) * _silu(g_ref[:, vs])).astype(y_ref.dtype)

    @pl.when(c == pl.num_programs(1) - 1)
    def _():
        sout_ref[...] = st_ref[...]


def _ret_seq(proj3, t, row_blk0, nblk, lead, cos, sin, s0, out_dtype, stack=None):
    bsz = proj3.shape[0]
    wide = R_HEADS * R_VDIM
    bcast = lambda a: pl.BlockSpec((None,) + a.shape[1:], lambda b, c: (0,) * a.ndim)
    decay = jnp.asarray(_ret_decay(t))
    inputs = [proj3, proj3, proj3, proj3, cos, sin, decay, s0]
    st_shape, st_spec = _seq_state_out(stack, bsz, (R_HEADS, R_KDIM, R_VDIM))
    body, x_specs, x_in, aliases = _alias_prev(functools.partial(_ret_seq_kernel, lead, t), len(inputs),
                                               stack and stack[1], (1,))
    return pl.pallas_call(
        body,
        grid=(bsz, nblk),
        in_specs=[pl.BlockSpec((None, t, D_MODEL), lambda b, c: (b, row_blk0 + c, COL_RQ)),
                  pl.BlockSpec((None, t, D_MODEL), lambda b, c: (b, row_blk0 + c, COL_RK)),
                  pl.BlockSpec((None, t, wide), lambda b, c: (b, row_blk0 + c, COL_RV)),
                  pl.BlockSpec((None, t, wide), lambda b, c: (b, row_blk0 + c, COL_RG)),
                  pl.BlockSpec((t, R_HALF), lambda b, c: (c, 0)),
                  pl.BlockSpec((t, R_HALF), lambda b, c: (c, 0)),
                  pl.BlockSpec(decay.shape, lambda b, c: (0, 0, 0)),
                  bcast(s0)] + x_specs,
        out_specs=[pl.BlockSpec((None, t, wide), lambda b, c: (b, c, 0)), st_spec],
        out_shape=[jax.ShapeDtypeStruct((bsz, nblk * t, wide), out_dtype), st_shape],
        scratch_shapes=[pltpu.VMEM((R_HEADS, R_KDIM, R_VDIM), F32)],
        input_output_aliases=aliases,
        compiler_params=_cparams(("parallel", "arbitrary")),
        name="ret_seq",
    )(*inputs, *x_in)


def _col_pieces(x):
    return jnp.concatenate(_split3(x.T), axis=1)


def _pick_col(n):
    r = _iota((3 * LANES, LANES), 0) & (LANES - 1)
    return jnp.where(r == n, 1.0, 0.0).astype(BF16)


def _pick_col_pair(n):
    r = _iota((3 * LANES, 2 * LANES), 0) & (LANES - 1)
    want = n + jnp.right_shift(_iota((3 * LANES, 2 * LANES), 1), int(math.log2(LANES)))
    return jnp.where(r == want, 1.0, 0.0).astype(BF16)


def _step_prep_kernel(lb_is_zero, xs_ref, bc_ref, x_ref, wdt_ref, hq_ref, hf_ref, rq_ref, rk_ref, conv_ref,
                      cw_ref, cb_ref, dtb_ref, a_ref, ex_ref, lb_ref, cos_ref, sin_ref,
                      xs_o, bc_o, xdt_o, edec_o, conv_o, hq_o, hef_o, hk_o, rq_o, rk_o):
    raw = jnp.concatenate([xs_ref[...], bc_ref[...]], axis=1)
    buf = conv_ref[...]
    acc = cb_ref[...] + cw_ref[3:4, :] * raw
    for k in range(M_CONV - 1):
        acc += cw_ref[k:k + 1, :] * buf[:, k * M_CONV_DIM:(k + 1) * M_CONV_DIM]
    conv_o[:, 0:2 * M_CONV_DIM] = buf[:, M_CONV_DIM:]
    conv_o[:, 2 * M_CONV_DIM:] = raw
    xbc = _silu(acc)
    xs = xbc[:, :M_INNER]
    xs_o[...] = xs
    bc_o[...] = xbc[:, M_INNER:]
    dt = _softplus(_dot_nt(x_ref[...].astype(BF16), wdt_ref[...]) + dtb_ref[...])
    ex = ex_ref[...]
    xdt_o[...] = _col_pieces(xs * _sel_right(dt, ex))
    edec_o[...] = _col_pieces(_sel_right(jnp.exp(dt * a_ref[...]), ex))
    logf, k = _hgrn_gates(hf_ref[...], lb_ref[...], lb_is_zero)
    hq_o[...] = hq_ref[...] * (H_KDIM ** -0.5)
    hef_o[...] = _col_pieces(jnp.exp(logf))
    hk_o[...] = _col_pieces(k)
    cos, sin = cos_ref[...], sin_ref[...]
    rq = jnp.concatenate([_rotary(rq_ref[:, h * R_KDIM:(h + 1) * R_KDIM], cos, sin) for h in range(R_HEADS)], axis=1)
    rk = jnp.concatenate([_rotary(rk_ref[:, h * R_KDIM:(h + 1) * R_KDIM], cos, sin) for h in range(R_HEADS)], axis=1)
    rq_o[...] = _col_pieces(rq)
    rk_o[...] = _col_pieces(rk * (R_KDIM ** -0.5))


def _step_prep(proj_s, x_s, wt, layer, conv_flat, p, cos, sin):
    nb = conv_flat.shape[0]
    col = lambda cblk: pl.BlockSpec((nb, D_MODEL), lambda i, cblk=cblk: (0, cblk))
    full = lambda a: pl.BlockSpec(a.shape, lambda i: (0,) * a.ndim)
    params = [p["conv_w"], p["conv_b"], p["dt_bias"], p["a_neg"], p["expand"], p["lb"], cos, sin]
    assert nb == LANES
    rows = lambda w: jax.ShapeDtypeStruct((nb, w), F32)
    cols = lambda w: jax.ShapeDtypeStruct((w, 3 * nb), BF16)
    shapes = [rows(M_INNER), rows(M_BC), cols(M_INNER), cols(M_INNER), rows((M_CONV - 1) * M_CONV_DIM),
              rows(D_MODEL), cols(D_MODEL), cols(D_MODEL), cols(D_MODEL), cols(D_MODEL)]
    return pl.pallas_call(
        functools.partial(_step_prep_kernel, layer == 0),
        grid=(1,),
        in_specs=[col(COL_XS), col(COL_BC), pl.BlockSpec((nb, D_MODEL), lambda i: (0, 0)), _dt_weight_spec(layer),
                  col(COL_HQ), col(COL_HF), col(COL_RQ), col(COL_RK), full(conv_flat)]
                 + [full(a) for a in params],
        out_specs=[pl.BlockSpec(s.shape, lambda i: (0, 0)) for s in shapes],
        out_shape=shapes,
        compiler_params=_cparams(("arbitrary",)),
        name="step_prep",
    )(proj_s, proj_s, x_s, wt, proj_s, proj_s, proj_s, proj_s, conv_flat, *params)


def _ssd_step_kernel(s_ref, xdt_ref, edec_ref, bc_ref, xs_ref, z_ref, dsk_ref, nw_ref, so_ref, y_ref,
                     yt_ref, xdt_b, edec_b):
    yt_ref[...] = jnp.zeros(yt_ref.shape, F32)
    for i in range(STEP_BT):
        ls = slice((i % 2) * LANES, (i % 2 + 1) * LANES)
        if i % 2 == 0:
            pick = _pick_col_pair(pl.program_id(0) * STEP_BT + i)
            xdt_b[...] = _dot(xdt_ref[...], pick)
            edec_b[...] = _dot(edec_ref[...], pick)
        for g in range(M_GROUPS):
            gs = slice(g * M_GW, (g + 1) * M_GW)
            hs = slice(M_HPG * g, M_HPG * (g + 1))
            st = s_ref[i, hs].reshape(M_GW, M_STATE)
            brow = bc_ref[i:i + 1, g * M_STATE:(g + 1) * M_STATE]
            crow = bc_ref[i:i + 1, M_BC // 2 + g * M_STATE:M_BC // 2 + (g + 1) * M_STATE]
            new = st * edec_b[gs, ls] + xdt_b[gs, ls] * brow
            so_ref[i, hs] = new.reshape(M_HPG, M_HEADDIM, M_STATE)
            yt_ref[gs, i:i + 1] = jnp.sum(new * crow, axis=-1, keepdims=True)
    y = yt_ref[...].T[0:STEP_BT, :]
    xs = xs_ref[...]
    y = (y + dsk_ref[...] * xs) * _silu(z_ref[...])
    for g in range(M_GROUPS):
        gs = slice(g * M_GW, (g + 1) * M_GW)
        y_ref[:, gs] = _rms(y[:, gs]) * nw_ref[:, gs]


def _ssd_step(state, layer, prev, xdt_c, edec_c, bc, xs, proj_s, p):
    nb = xs.shape[0]
    bt = STEP_BT
    full = lambda a: pl.BlockSpec(a.shape, lambda j: (0,) * a.ndim)
    sspec = pl.BlockSpec((None, bt, M_HEADS, M_HEADDIM, M_STATE), lambda j: (layer, j, 0, 0, 0))
    inputs = [state, xdt_c, edec_c, bc, xs, proj_s, p["d_skip"], p["m_norm_w"]]
    body, x_specs, x_in, aliases = _alias_prev(_ssd_step_kernel, len(inputs),
                                               None if prev is None else [prev], (0,))
    return pl.pallas_call(
        body,
        grid=(nb // bt,),
        in_specs=[sspec, full(xdt_c), full(edec_c),
                  pl.BlockSpec((bt, M_BC), lambda j: (j, 0)),
                  pl.BlockSpec((bt, M_INNER), lambda j: (j, 0)),
                  pl.BlockSpec((bt, D_MODEL), lambda j: (j, COL_Z)),
                  full(p["d_skip"]), full(p["m_norm_w"])] + x_specs,
        out_specs=[sspec, pl.BlockSpec((bt, M_INNER), lambda j: (j, 0))],
        out_shape=[jax.ShapeDtypeStruct(state.shape, F32),
                   jax.ShapeDtypeStruct((nb, M_INNER), F32)],
        scratch_shapes=[pltpu.VMEM((M_INNER, LANES), F32)] + [pltpu.VMEM((M_INNER, 2 * LANES), F32)] * 2,
        input_output_aliases=aliases,
        compiler_params=_cparams(("parallel",)),
        name="ssd_step",
    )(*inputs, *x_in)


def _cols(x):
    nb, w = x.shape
    return jnp.transpose(x.reshape(nb // STEP_BT, STEP_BT, w), (0, 2, 1))


def _hgrn_step_kernel(s_ref, q_ref, ef_ref, k_ref, v_ref, g_ref, nw_ref, so_ref, y_ref, ef_b, k_b):
    for i in range(STEP_BT):
        ls = slice((i % 2) * LANES, (i % 2 + 1) * LANES)
        if i % 2 == 0:
            pick = _pick_col_pair(pl.program_id(0) * STEP_BT + i)
            ef_b[...] = _dot(ef_ref[...], pick)
            k_b[...] = _dot(k_ref[...], pick)
        for h in range(H_HEADS):
            cs = slice(h * H_KDIM, (h + 1) * H_KDIM)
            new = s_ref[i, h] * ef_b[cs, ls] + k_b[cs, ls] * v_ref[i:i + 1, cs]
            so_ref[i, h] = new
            y_ref[i:i + 1, cs] = jnp.sum(new * q_ref[cs, i:i + 1], axis=0, keepdims=True)
    for h in range(H_HEADS):
        cs = slice(h * H_KDIM, (h + 1) * H_KDIM)
        y_ref[:, cs] = _rms(y_ref[:, cs]) * nw_ref[:, cs] * _sigmoid(g_ref[:, cs])


def _hgrn_step(state, layer, prev, q_c, ef_c, k_c, proj_s, nw):
    nb = state.shape[1]
    bt = STEP_BT
    cspec = pl.BlockSpec(ef_c.shape, lambda j: (0, 0))
    qspec = pl.BlockSpec((None, D_MODEL, bt), lambda j: (j, 0, 0))
    sspec = pl.BlockSpec((None, bt, H_HEADS, H_KDIM, H_VDIM), lambda j: (layer, j, 0, 0, 0))
    inputs = [state, q_c, ef_c, k_c, proj_s, proj_s, nw]
    body, x_specs, x_in, aliases = _alias_prev(_hgrn_step_kernel, len(inputs),
                                               None if prev is None else [prev], (0,))
    return pl.pallas_call(
        body,
        grid=(nb // bt,),
        in_specs=[sspec, qspec, cspec, cspec,
                  pl.BlockSpec((bt, D_MODEL), lambda j: (j, COL_HI)),
                  pl.BlockSpec((bt, D_MODEL), lambda j: (j, COL_HG)),
                  pl.BlockSpec(nw.shape, lambda j: (0, 0))] + x_specs,
        out_specs=[sspec, pl.BlockSpec((bt, D_MODEL), lambda j: (j, 0))],
        out_shape=[jax.ShapeDtypeStruct(state.shape, F32),
                   jax.ShapeDtypeStruct((nb, D_MODEL), F32)],
        scratch_shapes=[pltpu.VMEM((D_MODEL, 2 * LANES), F32)] * 2,
        input_output_aliases=aliases,
        compiler_params=_cparams(("parallel",)),
        name="hgrn_step",
    )(*inputs, *x_in)


def _ret_step_kernel(*refs):
    _ret_token_update(pl.program_id(0), *refs)


def _ret_step(state, layer, prev, q_c, k_c, v3, g3, gam):
    nb = state.shape[1]
    full = lambda a: pl.BlockSpec(a.shape, lambda n: (0,) * a.ndim)
    seq_row = lambda a: pl.BlockSpec((None,) + a.shape[1:], lambda n: (n, 0, 0))
    sspec = pl.BlockSpec((None, None, R_HEADS, R_KDIM, R_VDIM), lambda n: (layer, n, 0, 0, 0))
    inputs = [state, q_c, k_c, v3, g3, gam]
    body, x_specs, x_in, aliases = _alias_prev(_ret_step_kernel, len(inputs),
                                               None if prev is None else [prev], (0,))
    return pl.pallas_call(
        body,
        grid=(nb,),
        in_specs=[sspec, full(q_c), full(k_c), seq_row(v3), seq_row(g3), full(gam)] + x_specs,
        out_specs=[sspec, seq_row(v3)],
        out_shape=[jax.ShapeDtypeStruct(state.shape, F32), jax.ShapeDtypeStruct(v3.shape, F32)],
        input_output_aliases=aliases,
        compiler_params=_cparams(("parallel",)),
        name="ret_step",
    )(*inputs, *x_in)


def _rope_tables(positions):
    inv_freq = 1.0 / (ROPE_BASE ** jnp.linspace(0.0, 1.0, R_HALF, dtype=F32))
    ang = positions[:, None] * inv_freq[None, :]
    return jnp.cos(ang), jnp.sin(ang)


def _per_channel(v):
    return jnp.repeat(v.astype(F32), M_HEADDIM).reshape(1, M_INNER)


def _pad_lanes(v):
    return jnp.pad(v.astype(F32), (0, LANES - v.shape[0])).reshape(1, LANES)


def kernel(x_prompt, x_sample, state_ssm, state_conv, state_hgrn, state_ret, meta_tokens, ln_in_g, ln_in_b,
           w_in, conv_w, conv_b, dt_bias, a_log, d_skip, m_norm_w, hgrn_lb_logits, h_norm_w, w_br_m, w_br_h,
           w_br_r, w_out, ln1_g, ln1_b, w_ffn_in, w_ffn_out, ln2_g, ln2_b):
    bp, sp = x_prompt.shape[0], x_prompt.shape[1]
    nb = x_sample.shape[0]
    assert x_sample.shape[1] == 1 and nb == SMALL_ROWS - SEQ_BLOCK and nb % STEP_BT == 0
    assert sp % RET_BLOCK == 0 and meta_tokens.shape[0] == N_META

    wt = jnp.swapaxes(w_in, 1, 2).astype(BF16)
    wm_b, wh_b, wr_b, wo_b = (w.astype(BF16) for w in (w_br_m, w_br_h, w_br_r, w_out))
    wfi_b, wfo_b = w_ffn_in.astype(BF16), w_ffn_out.astype(BF16)
    ln1 = (ln1_g.reshape(DEPTH, 1, D_MODEL), ln1_b.reshape(DEPTH, 1, D_MODEL))
    ln2 = (ln2_g.reshape(DEPTH, 1, D_MODEL), ln2_b.reshape(DEPTH, 1, D_MODEL))
    lb_cum = jnp.cumsum(jax.nn.softmax(hgrn_lb_logits.astype(F32), axis=0), axis=0)
    lbs = lb_cum - lb_cum[0]
    expand = (np.arange(LANES)[:, None] == (np.arange(M_INNER)[None, :] // M_HEADDIM)).astype(np.float32)
    expand = jnp.asarray(np.concatenate([expand] * 3, 0), BF16)
    h_sums, h_pair = _hgrn_tables()
    h_tables = (jnp.asarray(h_sums, BF16), jnp.asarray(h_pair, F32))
    gam = jnp.asarray(np.broadcast_to(
        np.array([1.0 - 2.0 ** (-5.0 - h) for h in range(R_HEADS)], np.float32)[:, None, None],
        (R_HEADS, 1, R_VDIM)))

    pos_real = jnp.arange(N_META, N_META + sp, dtype=F32)
    pos_meta = jnp.maximum(jnp.arange(SEQ_BLOCK, dtype=F32) - META_LEAD, 0.0)
    pos_samp = jnp.full((nb,), float(PAST_LEN), F32)
    cos_r, sin_r = _rope_tables(pos_real)
    cos_m, sin_m = _rope_tables(pos_meta)
    cos_s, sin_s = _rope_tables(pos_samp)

    x_real = _layer_norm_rows(x_prompt.reshape(bp * sp, D_MODEL), ln_in_g, ln_in_b, 512)
    small_in = jnp.concatenate([x_sample.reshape(nb, D_MODEL),
                                jnp.zeros((META_LEAD, D_MODEL), F32), meta_tokens.astype(F32)], axis=0)
    x_small = _layer_norm_rows(small_in, ln_in_g, ln_in_b, SMALL_ROWS)

    z_ssm = jnp.zeros((1, M_HEADS, M_HEADDIM, M_STATE), F32)
    z_conv = jnp.zeros((1, 8, M_CONV_DIM), F32)
    z_hgrn = jnp.zeros((1, H_HEADS, H_KDIM, H_VDIM), F32)
    z_ret = jnp.zeros((1, R_HEADS, R_KDIM, R_VDIM), F32)
    ssm_p = conv_p = hgrn_p = ret_p = ssm_s = hgrn_s = ret_s = None
    conv_s = []
    for l in range(DEPTH):
        p = dict(conv_w=conv_w[l], conv_b=conv_b[l].reshape(1, -1), dt_bias=_pad_lanes(dt_bias[l]),
                 a_neg=_pad_lanes(-jnp.exp(a_log[l].astype(F32))), d_skip=_per_channel(d_skip[l]),
                 m_norm_w=m_norm_w[l].reshape(1, -1), expand=expand, lb=lbs[l].reshape(1, -1))
        hnw = h_norm_w[l].reshape(1, -1)

        proj_r = _proj(x_real, wt, l, PROJ_TM, "proj_real")
        proj_s = _proj(x_small, wt, l, SMALL_ROWS, "proj_small")
        proj_r3, x_real3 = proj_r.reshape(bp, sp, PROJ_COLS), x_real.reshape(bp, sp, D_MODEL)
        proj_s3, x_small3 = proj_s.reshape(1, SMALL_ROWS, PROJ_COLS), x_small.reshape(1, SMALL_ROWS, D_MODEL)

        ym_m, ssm_m, conv_m = _ssd_seq(proj_s3, x_small3, wt, l, 1, 1, META_LEAD, p, z_ssm, z_conv, F32)
        yh_m, hgrn_m = _hgrn_seq(proj_s3, 1, 1, META_LEAD, p["lb"], l == 0, hnw, h_tables, z_hgrn, F32)
        yr_m, ret_m = _ret_seq(proj_s3, SEQ_BLOCK, 1, 1, META_LEAD, cos_m, sin_m, z_ret, F32)

        conv_flat = state_conv[l].reshape(nb, (M_CONV - 1) * M_CONV_DIM)
        (xs_s, bc_s, xdt_s, edec_s, conv_new, hq_s, hef_s, hk_s, rq_s, rk_s) = _step_prep(
            proj_s, x_small, wt, l, conv_flat, p, cos_s, sin_s)
        ssm_s, ym_s = _ssd_step(state_ssm, l, ssm_s, xdt_s, edec_s, bc_s, xs_s, proj_s, p)
        hgrn_s, yh_s = _hgrn_step(state_hgrn, l, hgrn_s, _cols(hq_s), hef_s, hk_s, proj_s, hnw)
        conv_s.append(conv_new.reshape(nb, M_CONV - 1, M_CONV_DIM))
        wide = R_HEADS * R_VDIM
        rv3 = proj_s[:nb, 0:wide].reshape(nb, 1, wide)
        rg3 = proj_s[:nb, wide:2 * wide].reshape(nb, 1, wide)
        ret_ride = (state_ret, rq_s, rk_s, rv3, rg3, gam)

        conv0 = jnp.pad(conv_m, ((0, 0), (8 - (M_CONV - 1), 0), (0, 0)))
        ym_r, ssm_p, conv_p = _ssd_seq(proj_r3, x_real3, wt, l, 0, sp // SEQ_BLOCK, 0, p, ssm_m, conv0, BF16,
                                       stack=(l, None if l == 0 else [ssm_p, conv_p]))
        if bp * (sp // SEQ_BLOCK) == nb:
            yh_r, hgrn_p, ret_s, yr_s = _hgrn_seq(
                proj_r3, 0, sp // SEQ_BLOCK, 0, p["lb"], l == 0, hnw, h_tables, hgrn_m, BF16,
                stack=(l, None if l == 0 else [hgrn_p, ret_s]), ride=ret_ride)
        else:
            yh_r, hgrn_p = _hgrn_seq(proj_r3, 0, sp // SEQ_BLOCK, 0, p["lb"], l == 0, hnw, h_tables, hgrn_m, BF16,
                                     stack=(l, None if l == 0 else [hgrn_p]))
            ret_s, yr_s = _ret_step(state_ret, l, ret_s, *ret_ride[1:])
        yr_s = yr_s.reshape(nb, wide)
        yr_r, ret_p = _ret_seq(proj_r3, RET_BLOCK, 0, sp // RET_BLOCK, 0, cos_r, sin_r, ret_m, BF16,
                               stack=(l, None if l == 0 else [ret_p]))

        ym_small = jnp.concatenate([ym_s, ym_m[0]], axis=0)
        yh_small = jnp.concatenate([yh_s, yh_m[0]], axis=0)
        yr_small = jnp.concatenate([yr_s, yr_m[0]], axis=0)
        x_real = _mix(x_real, ym_r.reshape(bp * sp, -1), yh_r.reshape(bp * sp, -1), yr_r.reshape(bp * sp, -1),
                      proj_r, l, wm_b, wh_b, wr_b, wo_b, *ln1, 256)
        x_small = _mix(x_small, ym_small, yh_small, yr_small, proj_s, l, wm_b, wh_b, wr_b, wo_b, *ln1, SMALL_ROWS)
        x_real = _ffn(x_real, l, wfi_b, wfo_b, *ln2, 512)
        x_small = _ffn(x_small, l, wfi_b, wfo_b, *ln2, SMALL_ROWS)

    return (x_real.reshape(bp, sp, D_MODEL), x_small[:nb].reshape(nb, 1, D_MODEL),
            ssm_p, conv_p, hgrn_p, ret_p, ssm_s, jnp.stack(conv_s), hgrn_s, ret_s)
```

```python
import functools
import math

import numpy as np
import jax
import jax.numpy as jnp
from jax import lax
from jax.experimental import pallas as pl
from jax.experimental.pallas import tpu as pltpu

F32 = jnp.float32
BF16 = jnp.bfloat16

D_MODEL = 1024
DEPTH = 2
N_META = 16
M_INNER = D_MODEL
M_HEADDIM = 64
M_HEADS = M_INNER // M_HEADDIM
M_GROUPS = 4
M_HPG = M_HEADS // M_GROUPS
M_STATE = 128
M_CONV = 4
M_BC = 2 * M_GROUPS * M_STATE
M_CONV_DIM = M_INNER + M_BC
M_GW = M_INNER // M_GROUPS
H_KDIM = 128
H_HEADS = D_MODEL // H_KDIM
H_VDIM = 128
R_HEADS = 4
R_KDIM = D_MODEL // R_HEADS
R_VDIM = 2 * R_KDIM
R_HALF = R_KDIM // 2
ROPE_BASE = 10000.0
D_FF = ((8 * D_MODEL // 3 + 255) // 256) * 256
FF_CHUNK = 256
DN_ALPHA = (2 * DEPTH) ** 0.25
PAST_LEN = 16384

LANES = 128
SEQ_BLOCK = 128
RET_BLOCK = 256
SMALL_ROWS = 256
META_LEAD = SEQ_BLOCK - N_META
STEP_BT = 8
VMEM_LIMIT = 56 * 1024 * 1024

COL_RV, COL_RG = 0, 1
COL_Z, COL_XS, COL_BC, COL_HQ, COL_HF, COL_HI, COL_HG, COL_RQ, COL_RK, COL_GM, COL_GH, COL_GR = range(4, 16)
PROJ_COLS = 16 * 1024

_ORIG_SPLITS = (M_INNER, M_CONV_DIM, M_HEADS, 1024, 1024, 1024, 1024, 1024, 1024, 2048, 2048, 3072)
_ORIG_OFF = np.concatenate([[0], np.cumsum(_ORIG_SPLITS)]).tolist()


def _cparams(sem):
    return pltpu.CompilerParams(dimension_semantics=sem, vmem_limit_bytes=VMEM_LIMIT)


def _sigmoid(x):
    return 1.0 / (1.0 + jnp.exp(-x))


def _silu(x):
    return x * _sigmoid(x)


def _softplus(x):
    return jnp.maximum(x, 0.0) + jnp.log1p(jnp.exp(-jnp.abs(x)))


def _layer_norm(x, g, b):
    mu = jnp.mean(x, axis=-1, keepdims=True)
    xc = x - mu
    var = jnp.mean(xc * xc, axis=-1, keepdims=True)
    return xc * lax.rsqrt(var + 1e-5) * g + b


def _rms(x):
    return x * lax.rsqrt(jnp.mean(x * x, axis=-1, keepdims=True) + 1e-6)


def _split3(x):
    hi = x.astype(BF16)
    r = x - hi.astype(F32)
    mid = r.astype(BF16)
    lo = (r - mid.astype(F32)).astype(BF16)
    return hi, mid, lo


def _dot(a, b):
    return jnp.dot(a, b, preferred_element_type=F32)


def _dot_nt(a, b):
    return lax.dot_general(a, b, (((1,), (1,)), ((), ())), preferred_element_type=F32)


def _sel_right(x, m3):
    return _dot(jnp.concatenate(_split3(x), axis=1), m3)


def _sel_left(m3, x):
    return _dot(m3, jnp.concatenate(_split3(x), axis=0))


def _iota(shape, dim):
    return lax.broadcasted_iota(jnp.int32, shape, dim)


def _ln_kernel(x_ref, g_ref, b_ref, o_ref):
    o_ref[...] = _layer_norm(x_ref[...], g_ref[...], b_ref[...])


def _layer_norm_rows(x, g, b, tm):
    m = x.shape[0]
    tm = min(tm, m)
    return pl.pallas_call(
        _ln_kernel,
        grid=(m // tm,),
        in_specs=[pl.BlockSpec((tm, D_MODEL), lambda i: (i, 0)),
                  pl.BlockSpec((1, D_MODEL), lambda i: (0, 0)),
                  pl.BlockSpec((1, D_MODEL), lambda i: (0, 0))],
        out_specs=pl.BlockSpec((tm, D_MODEL), lambda i: (i, 0)),
        out_shape=jax.ShapeDtypeStruct((m, D_MODEL), F32),
        compiler_params=_cparams(("parallel",)),
        name="ln_in",
    )(x, g.reshape(1, -1), b.reshape(1, -1))


def _proj_kernel(ride, x_ref, w_ref, *refs):
    if ride:
        ride_in, (o_ref, so_ref, ry_ref, xb_ref) = refs[:6], refs[6:]
    else:
        o_ref, xb_ref = refs

    @pl.when(pl.program_id(1) == 0)
    def _():
        xb_ref[...] = x_ref[...].astype(BF16)

    o_ref[...] = _dot_nt(xb_ref[...], w_ref[0])
    if ride:
        _ret_token_update(pl.program_id(0) * pl.num_programs(1) + pl.program_id(1), *ride_in, so_ref, ry_ref)


def _layer_block(a, layer):
    return pl.BlockSpec((None,) + a.shape[1:], lambda *_: (layer,) + (0,) * (a.ndim - 1))


_W_RUNS = ((0, _ORIG_OFF[9]), (4096, _ORIG_OFF[0]), (7168, _ORIG_OFF[3]), (13312, _ORIG_OFF[11]))
PROJ_TN = 1024
PROJ_TM = 2048


def _orig_col(j):
    c = j * PROJ_TN
    off = c - _W_RUNS[0][0] + _W_RUNS[0][1]
    for new0, orig0 in _W_RUNS[1:]:
        off = jnp.where(c >= new0, c - new0 + orig0, off)
    return pl.multiple_of(off, M_HEADS)


def _proj_steps(m, tm):
    return (m // min(tm, m)) * (PROJ_COLS // PROJ_TN)


def _proj(x, wt, layer, tm, name, ride=None):
    m = x.shape[0]
    tm = min(tm, m)
    nj = PROJ_COLS // PROJ_TN
    inputs = [x, wt]
    in_specs = [pl.BlockSpec((tm, D_MODEL), lambda i, j: (i, 0)),
                pl.BlockSpec((pl.Element(1), pl.Element(PROJ_TN), pl.Element(D_MODEL)),
                             lambda i, j: (layer, _orig_col(j), 0))]
    out_specs = [pl.BlockSpec((tm, PROJ_TN), lambda i, j: (i, j))]
    out_shape = [jax.ShapeDtypeStruct((m, PROJ_COLS), F32)]
    prev = None
    if ride:
        state, prev, q_c, k_c, v3, g3, gam = ride
        assert _proj_steps(m, tm) == state.shape[1]
        full = lambda a: pl.BlockSpec(a.shape, lambda i, j: (0,) * a.ndim)
        seq_row = lambda a: pl.BlockSpec((None,) + a.shape[1:], lambda i, j: (i * nj + j, 0, 0))
        sspec = pl.BlockSpec((None, None, R_HEADS, R_KDIM, R_VDIM), lambda i, j: (layer, i * nj + j, 0, 0, 0))
        inputs += [state, q_c, k_c, v3, g3, gam]
        in_specs[0] = pl.BlockSpec((tm, D_MODEL), lambda i, j: (i, 0), pipeline_mode=pl.Buffered(1))
        in_specs += [sspec, full(q_c), full(k_c), seq_row(v3), seq_row(g3), full(gam)]
        out_specs += [sspec, seq_row(v3)]
        out_shape += [jax.ShapeDtypeStruct(state.shape, F32), jax.ShapeDtypeStruct(v3.shape, F32)]
    body, x_specs, x_in, aliases = _alias_prev(functools.partial(_proj_kernel, bool(ride)), len(inputs),
                                               None if prev is None else [prev], (1,))
    res = pl.pallas_call(
        body,
        grid=(m // tm, nj),
        in_specs=in_specs + x_specs,
        out_specs=out_specs,
        out_shape=out_shape,
        scratch_shapes=[pltpu.VMEM((tm, D_MODEL), BF16)],
        input_output_aliases=aliases,
        compiler_params=_cparams(("parallel", "arbitrary")),
        name=name,
    )(*inputs, *x_in)
    return res if ride else res[0]


def _dt_weight_spec(layer):
    assert _ORIG_OFF[2] % LANES == 0
    return pl.BlockSpec((None, LANES, D_MODEL), lambda *_: (layer, _ORIG_OFF[2] // LANES, 0))


def _mix_kernel(x_ref, ym_ref, yh_ref, yr_ref, gm_ref, gh_ref, gr_ref,
                wm_ref, wh_ref, wr_ref, wo_ref, g_ref, b_ref, o_ref):
    mixed = _sigmoid(gm_ref[...]) * _dot(ym_ref[...].astype(BF16), wm_ref[...])
    mixed += _sigmoid(gh_ref[...]) * _dot(yh_ref[...].astype(BF16), wh_ref[...])
    mixed += _sigmoid(gr_ref[...]) * _dot(yr_ref[...].astype(BF16), wr_ref[...])
    h = _dot(mixed.astype(BF16), wo_ref[...])
    o_ref[...] = _layer_norm(DN_ALPHA * x_ref[...] + h, g_ref[...], b_ref[...])


def _mix(x, ym, yh, yr, proj, layer, wm, wh, wr, wo, g2, b2, tm):
    m = x.shape[0]
    tm = min(tm, m)
    row = lambda w: pl.BlockSpec((tm, w), lambda i: (i, 0))
    col = lambda c: pl.BlockSpec((tm, D_MODEL), lambda i, c=c: (i, c))
    full = lambda a: _layer_block(a, layer)
    return pl.pallas_call(
        _mix_kernel,
        grid=(m // tm,),
        in_specs=[row(D_MODEL), row(M_INNER), row(D_MODEL), row(R_HEADS * R_VDIM),
                  col(COL_GM), col(COL_GH), col(COL_GR),
                  full(wm), full(wh), full(wr), full(wo), full(g2), full(b2)],
        out_specs=row(D_MODEL),
        out_shape=jax.ShapeDtypeStruct((m, D_MODEL), F32),
        compiler_params=_cparams(("parallel",)),
        name="mix",
    )(x, ym, yh, yr, proj, proj, proj, wm, wh, wr, wo, g2, b2)


def _ffn_kernel(x_ref, wi_ref, wo_ref, g_ref, b_ref, o_ref):
    x = x_ref[...]
    xb = x.astype(BF16)
    acc = jnp.zeros(x.shape, F32)
    for j in range(0, D_FF, FF_CHUNK):
        hg = _dot(xb, wi_ref[:, j:j + FF_CHUNK])
        hu = _dot(xb, wi_ref[:, D_FF + j:D_FF + j + FF_CHUNK])
        acc += _dot((_silu(hg) * hu).astype(BF16), wo_ref[j:j + FF_CHUNK, :])
    o_ref[...] = _layer_norm(DN_ALPHA * x + acc, g_ref[...], b_ref[...])


def _ffn(x, layer, wi, wo, g2, b2, tm):
    m = x.shape[0]
    tm = min(tm, m)
    full = lambda a: _layer_block(a, layer)
    return pl.pallas_call(
        _ffn_kernel,
        grid=(m // tm,),
        in_specs=[pl.BlockSpec((tm, D_MODEL), lambda i: (i, 0)), full(wi), full(wo), full(g2), full(b2)],
        out_specs=pl.BlockSpec((tm, D_MODEL), lambda i: (i, 0)),
        out_shape=jax.ShapeDtypeStruct((m, D_MODEL), F32),
        compiler_params=_cparams(("parallel",)),
        name="ffn",
    )(x, wi, wo, g2, b2)


def _ssd_seq_kernel(lead, z_ref, xs_ref, bc_ref, x_ref, wdt_ref, cw_ref, cb_ref, dtb_ref, a_ref, dsk_ref, nw_ref,
                    ex_ref, s0_ref, c0_ref, y_ref, sout_ref, cout_ref, ext_ref, st_ref, xbc_ref):
    t = SEQ_BLOCK
    c = pl.program_id(1)

    @pl.when(c == 0)
    def _():
        ext_ref[0:8, :] = c0_ref[...]
        for g in range(M_GROUPS):
            st_ref[g] = s0_ref[M_HPG * g:M_HPG * (g + 1)].reshape(M_GW, M_STATE).T

    rows = _iota((t, 1), 0)

    def conv_chunk(q):
        w = M_CONV_DIM // 4
        cs = slice(q * w, (q + 1) * w)
        src = xs_ref if q < 2 else bc_ref
        raw = src[:, (q % 2) * w:(q % 2 + 1) * w]
        if lead:
            raw = jnp.where(rows >= lead, raw, 0.0)
        ext_ref[8:8 + t, cs] = raw
        acc = cb_ref[:, cs] + cw_ref[3:4, cs] * raw
        for k in range(M_CONV - 1):
            acc += cw_ref[k:k + 1, cs] * ext_ref[5 + k:5 + k + t, cs]
        ext_ref[5:8, cs] = ext_ref[t + 5:t + 8, cs]
        xbc_ref[:, cs] = _silu(acc)

    dt_raw = _dot_nt(x_ref[...].astype(BF16), wdt_ref[...])
    conv_chunk(0)
    dt = _softplus(dt_raw + dtb_ref[...])
    if lead:
        dt = jnp.where(rows >= lead, dt, 0.0)
    a = dt * a_ref[...]
    ti, si = _iota((t, t), 0), _iota((t, t), 1)
    tril = si <= ti
    tri = jnp.where(tril, 1.0, 0.0).astype(BF16)
    cum = _sel_left(jnp.concatenate([tri] * 3, axis=1), a)
    conv_chunk(1)
    ecum = jnp.exp(cum)
    tailw = jnp.exp(cum[t - 1:t, :] - cum) * dt
    cum_t, dt_t = cum.T, dt.T
    ex = ex_ref[...]
    ecum_full = _sel_right(ecum, ex)
    conv_chunk(2)
    tailw_full = _sel_right(tailw, ex)
    conv_chunk(3)

    xs = xbc_ref[:, 0:M_INNER]
    xw = (xs * tailw_full).astype(BF16)
    xs_b = xs.astype(BF16)
    head_of_lane = jnp.right_shift(_iota((1, M_GW), 1), int(math.log2(M_HEADDIM)))

    for g in range(M_GROUPS):
        gs = slice(g * M_GW, (g + 1) * M_GW)
        bm_f = xbc_ref[:, M_INNER + g * M_STATE:M_INNER + (g + 1) * M_STATE]
        bm = bm_f.astype(BF16)
        cm = xbc_ref[:, M_INNER + M_BC // 2 + g * M_STATE:M_INNER + M_BC // 2 + (g + 1) * M_STATE].astype(BF16)
        cb = _dot_nt(cm, bm)
        st = st_ref[g]
        y_g = _dot(cm, st.astype(BF16)) * ecum_full[:, gs]
        for hh in range(M_HPG):
            h = g * M_HPG + hh
            diff = cum[:, h:h + 1] - cum_t[h:h + 1, :]
            w = cb * jnp.exp(jnp.where(tril, diff, -1e30)) * dt_t[h:h + 1, :]
            x_h = jnp.where(head_of_lane == hh, xs_b[:, gs], jnp.zeros((), BF16))
            y_g = y_g + _dot(w.astype(BF16), x_h)
        st_ref[g] = st * ecum_full[t - 1:t, gs] + _dot(bm_f.T.astype(BF16), xw[:, gs])
        y_g = (y_g + dsk_ref[:, gs] * xs[:, gs]) * _silu(z_ref[:, gs])
        y_ref[:, gs] = (_rms(y_g) * nw_ref[:, gs]).astype(y_ref.dtype)

    @pl.when(c == pl.num_programs(1) - 1)
    def _():
        cout_ref[...] = ext_ref[5:8, :]
        for g in range(M_GROUPS):
            sout_ref[M_HPG * g:M_HPG * (g + 1)] = st_ref[g].T.reshape(M_HPG, M_HEADDIM, M_STATE)


def _alias_prev(body, n_in, prevs, out_ids):
    if not prevs:
        return body, [], [], {}
    k = len(prevs)
    wrapped = lambda *refs: body(*refs[:n_in], *refs[n_in + k:])
    return (wrapped, [pl.BlockSpec(memory_space=pl.ANY)] * k, list(prevs),
            {n_in + i: o for i, o in enumerate(out_ids)})


def _seq_state_out(stack, bsz, dims):
    zeros = (0,) * len(dims)
    if stack is None:
        return (jax.ShapeDtypeStruct((bsz,) + dims, F32),
                pl.BlockSpec((None,) + dims, lambda b, c: (b,) + zeros))
    layer = stack[0]
    return (jax.ShapeDtypeStruct((DEPTH, bsz) + dims, F32),
            pl.BlockSpec((None, None) + dims, lambda b, c: (layer, b) + zeros))


def _ssd_seq(proj3, x3, wt, layer, row_blk0, nblk, lead, p, s0, c0, out_dtype, stack=None):
    bsz = proj3.shape[0]
    t = SEQ_BLOCK
    colspec = lambda cblk: pl.BlockSpec((None, t, D_MODEL), lambda b, c, cblk=cblk: (b, row_blk0 + c, cblk))
    full = lambda a: pl.BlockSpec(a.shape, lambda b, c: (0,) * a.ndim)
    bcast = lambda a: pl.BlockSpec((None,) + a.shape[1:], lambda b, c: (0,) * a.ndim)
    params = [p["conv_w"], p["conv_b"], p["dt_bias"], p["a_neg"], p["d_skip"], p["m_norm_w"], p["expand"]]
    inputs = [proj3, proj3, proj3, x3, wt, *params, s0, c0]
    st_shape, st_spec = _seq_state_out(stack, bsz, (M_HEADS, M_HEADDIM, M_STATE))
    cv_shape, cv_spec = _seq_state_out(stack, bsz, (M_CONV - 1, M_CONV_DIM))
    body, x_specs, x_in, aliases = _alias_prev(functools.partial(_ssd_seq_kernel, lead), len(inputs),
                                               stack and stack[1], (1, 2))
    return pl.pallas_call(
        body,
        grid=(bsz, nblk),
        in_specs=[colspec(COL_Z), colspec(COL_XS), colspec(COL_BC),
                  pl.BlockSpec((None, t, D_MODEL), lambda b, c: (b, row_blk0 + c, 0)), _dt_weight_spec(layer)]
                 + [full(a) for a in params] + [bcast(s0), bcast(c0)] + x_specs,
        out_specs=[pl.BlockSpec((None, t, M_INNER), lambda b, c: (b, c, 0)), st_spec, cv_spec],
        out_shape=[jax.ShapeDtypeStruct((bsz, nblk * t, M_INNER), out_dtype), st_shape, cv_shape],
        scratch_shapes=[pltpu.VMEM((t + 8, M_CONV_DIM), F32),
                        pltpu.VMEM((M_GROUPS, M_STATE, M_GW), F32),
                        pltpu.VMEM((t, M_CONV_DIM), F32)],
        input_output_aliases=aliases,
        compiler_params=_cparams(("parallel", "arbitrary")),
        name="ssd_seq",
    )(*inputs, *x_in)


def _hgrn_gates(fz, lb, lb_is_zero):
    e = jnp.exp(-jnp.abs(fz))
    r = 1.0 / (1.0 + e)
    pos = fz >= 0.0
    sig_neg = jnp.where(pos, e * r, r)
    log_sig = jnp.minimum(fz, 0.0) - jnp.log(1.0 + e)
    if lb_is_zero:
        return log_sig, sig_neg
    sig_pos = jnp.where(pos, r, e * r)
    logf = jnp.where(lb > 0.0, jnp.log(lb + (1.0 - lb) * sig_pos), log_sig)
    return logf, (1.0 - lb) * sig_neg


H_LEVELS = int(math.log2(SEQ_BLOCK))


def _hgrn_tables():
    n = SEQ_BLOCK
    t = np.arange(n)[:, None]
    u = np.arange(n)[None, :]
    pair = []
    for b in range(H_LEVELS):
        bit = ((t >> b) & 1) == 1
        pair.append(((t >> (b + 1)) == (u >> (b + 1))) & bit & (((u >> b) & 1) == 0))
    m1 = ((t >> 1) | 1) << 1
    lvl1 = np.where(((t >> 1) & 1) == 1, (u >= m1) & (u <= t), (u > t) & (u < m1))
    sums = np.concatenate([u <= t, lvl1], 0).astype(np.float32)
    return np.concatenate([sums] * 3, 1), np.concatenate(pair, 0).astype(np.float32)


def _exp_neg_abs(d):
    return jnp.exp2(jnp.abs(d) * (-1.0 / math.log(2.0)))


def _ret_token_update(n, s_ref, q_ref, k_ref, v_ref, g_ref, gam_ref, so_ref, y_ref):
    pick = _pick_col(n)
    for h in range(R_HEADS):
        rows = slice(h * R_KDIM, (h + 1) * R_KDIM)
        q_b = _dot(q_ref[rows, :], pick)
        k_b = _dot(k_ref[rows, :], pick)
        outs = []
        for c0 in range(0, R_VDIM, LANES):
            cs = slice(c0, c0 + LANES)
            new = s_ref[h, :, cs] * gam_ref[h, :, cs] + k_b * v_ref[:, h * R_VDIM + c0:h * R_VDIM + c0 + LANES]
            so_ref[h, :, cs] = new
            outs.append(jnp.sum(new * q_b, axis=0, keepdims=True))
        vs = slice(h * R_VDIM, (h + 1) * R_VDIM)
        y_ref[:, vs] = _rms(jnp.concatenate(outs, axis=1)) * _silu(g_ref[:, vs])


def _hgrn_seq_kernel(lead, lb_is_zero, q_ref, f_ref, i_ref, g_ref, lb_ref, nw_ref, sums_ref, pair_ref, s0_ref,
                     y_ref, sout_ref, st_ref, ex_ref, q_s, k_s, z_ref, qd_ref, kd_ref, sc_ref):
    t = SEQ_BLOCK
    c = pl.program_id(1)

    @pl.when(c == 0)
    def _():
        for h in range(H_HEADS):
            st_ref[h] = s0_ref[h].T

    rows = _iota((t, 1), 0)
    logf, k = _hgrn_gates(f_ref[...], lb_ref[...], lb_is_zero)
    if lead:
        logf = jnp.where(rows >= lead, logf, 0.0)
        k = jnp.where(rows >= lead, k, 0.0)
    q = q_ref[...] * (H_KDIM ** -0.5)
    q_s[...] = q
    k_s[...] = k
    ex_ref[...] = _dot(sums_ref[...], jnp.concatenate(_split3(logf), axis=0))

    def side_of(b):
        return (jnp.right_shift(rows, b) & 1) == 1

    z_ref[0] = jnp.where(side_of(0), q * jnp.exp(logf), k).astype(BF16)
    z_ref[1] = (jnp.exp(ex_ref[t:2 * t, :]) * jnp.where(side_of(1), q_s[...], k_s[...])).astype(BF16)
    for b in range(2, H_LEVELS):
        half = 1 << b
        groups = range(0, t, 2 * half)
        d = jnp.concatenate([ex_ref[g0:g0 + 2 * half, :] - ex_ref[g0 + half - 1:g0 + half, :]
                             for g0 in groups], axis=0)
        if half >= 8:
            qk = jnp.concatenate([ref[g0 + o:g0 + o + half, :] for g0 in groups
                                  for ref, o in ((k_s, 0), (q_s, half))], axis=0)
        else:
            qk = jnp.where(side_of(b), q_s[...], k_s[...])
        z_ref[b] = (_exp_neg_abs(d) * qk).astype(BF16)
    cum = ex_ref[0:t, :]
    qd_ref[...] = (q_s[...] * jnp.exp(cum)).astype(BF16)
    kd_ref[...] = (k_s[...] * jnp.exp(ex_ref[t - 1:t, :] - cum)).astype(BF16)

    for h in range(H_HEADS):
        cs = slice(h * H_KDIM, (h + 1) * H_KDIM)
        scores = None
        for b in range(H_LEVELS):
            z = z_ref[b, :, cs]
            p = _dot_nt(z, z) * pair_ref[b * t:(b + 1) * t, :]
            scores = p if scores is None else scores + p
        sc_ref[h] = scores.astype(BF16)

    for h in range(H_HEADS):
        cs = slice(h * H_KDIM, (h + 1) * H_KDIM)
        v = i_ref[:, cs]
        st = st_ref[h]
        o = (_dot(sc_ref[h], v.astype(BF16))
             + jnp.sum(q_s[:, cs] * k_s[:, cs], axis=-1, keepdims=True) * v
             + _dot_nt(qd_ref[:, cs], st.astype(BF16)))
        y = _rms(o) * nw_ref[:, cs] * _sigmoid(g_ref[:, cs])
        y_ref[:, cs] = y.astype(y_ref.dtype)
        st_ref[h] = st * jnp.exp(ex_ref[t - 1:t, cs]) + _dot(v.T.astype(BF16), kd_ref[:, cs])

    @pl.when(c == pl.num_programs(1) - 1)
    def _():
        for h in range(H_HEADS):
            sout_ref[h] = st_ref[h].T


def _hgrn_seq(proj3, row_blk0, nblk, lead, lb, lb_is_zero, nw, tables, s0, out_dtype, stack=None):
    bsz = proj3.shape[0]
    t = SEQ_BLOCK
    colspec = lambda cblk: pl.BlockSpec((None, t, D_MODEL), lambda b, c, cblk=cblk: (b, row_blk0 + c, cblk))
    full = lambda a: pl.BlockSpec(a.shape, lambda b, c: (0,) * a.ndim)
    bcast = lambda a: pl.BlockSpec((None,) + a.shape[1:], lambda b, c: (0,) * a.ndim)
    inputs = [proj3, proj3, proj3, proj3, lb, nw, *tables, s0]
    st_shape, st_spec = _seq_state_out(stack, bsz, (H_HEADS, H_KDIM, H_VDIM))
    body, x_specs, x_in, aliases = _alias_prev(functools.partial(_hgrn_seq_kernel, lead, lb_is_zero), len(inputs),
                                               stack and stack[1], (1,))
    return pl.pallas_call(
        body,
        grid=(bsz, nblk),
        in_specs=[colspec(COL_HQ), colspec(COL_HF), colspec(COL_HI), colspec(COL_HG),
                  full(lb), full(nw), full(tables[0]), full(tables[1]), bcast(s0)] + x_specs,
        out_specs=[pl.BlockSpec((None, t, D_MODEL), lambda b, c: (b, c, 0)), st_spec],
        out_shape=[jax.ShapeDtypeStruct((bsz, nblk * t, D_MODEL), out_dtype), st_shape],
        input_output_aliases=aliases,
        scratch_shapes=[pltpu.VMEM((H_HEADS, H_VDIM, H_KDIM), F32),
                        pltpu.VMEM((2 * t, D_MODEL), F32),
                        pltpu.VMEM((t, D_MODEL), F32), pltpu.VMEM((t, D_MODEL), F32),
                        pltpu.VMEM((H_LEVELS, t, D_MODEL), BF16),
                        pltpu.VMEM((t, D_MODEL), BF16), pltpu.VMEM((t, D_MODEL), BF16),
                        pltpu.VMEM((H_HEADS, t, t), BF16)],
        compiler_params=_cparams(("parallel", "arbitrary")),
        name="hgrn_seq",
    )(*inputs, *x_in)


def _log_gamma(h):
    return math.log(1.0 - 2.0 ** (-5.0 - h))


def _rotary(x, cos, sin):
    x1, x2 = x[:, :R_HALF], x[:, R_HALF:]
    return jnp.concatenate([x1 * cos - x2 * sin, x2 * cos + x1 * sin], axis=1)


def _ret_decay(t):
    d = np.arange(t)[:, None] - np.arange(t)[None, :]
    return np.stack([np.where(d >= 0, np.exp(np.maximum(d, 0) * _log_gamma(h)), 0.0)
                     for h in range(R_HEADS)]).astype(np.float32)


def _ret_seq_kernel(lead, t, q_ref, k_ref, v_ref, g_ref, cos_ref, sin_ref, dec_ref, s0_ref,
                    y_ref, sout_ref, st_ref):
    c = pl.program_id(1)

    @pl.when(c == 0)
    def _():
        st_ref[...] = s0_ref[...]

    cos, sin = cos_ref[...], sin_ref[...]
    tcol = _iota((t, 1), 0).astype(F32)
    for h in range(R_HEADS):
        lg = _log_gamma(h)
        ks = slice(h * R_KDIM, (h + 1) * R_KDIM)
        vs = slice(h * R_VDIM, (h + 1) * R_VDIM)
        qh = _rotary(q_ref[:, ks], cos, sin)
        kh = _rotary(k_ref[:, ks], cos, sin) * (R_KDIM ** -0.5)
        if lead:
            kh = jnp.where(_iota((t, 1), 0) >= lead, kh, 0.0)
        qb, kb, vb = qh.astype(BF16), kh.astype(BF16), v_ref[:, vs].astype(BF16)
        scores = _dot_nt(qb, kb) * dec_ref[h]
        st = st_ref[h]
        o = _dot(scores.astype(BF16), vb) + _dot(qb, st.astype(BF16)) * jnp.exp((tcol + 1.0) * lg)
        kdec = (kh * jnp.exp((t - 1.0 - tcol) * lg)).T.astype(BF16)
        st_ref[h] = st * math.exp(t * lg) + _dot(kdec, vb)
        y_ref[:, vs] = (_rms(o) * _silu(g_ref[:, vs])).astype(y_ref.dtype)

    @pl.when(c == pl.num_programs(1) - 1)
    def _():
        sout_ref[...] = st_ref[...]


def _ret_seq(proj3, t, row_blk0, nblk, lead, cos, sin, s0, out_dtype, stack=None):
    bsz = proj3.shape[0]
    wide = R_HEADS * R_VDIM
    bcast = lambda a: pl.BlockSpec((None,) + a.shape[1:], lambda b, c: (0,) * a.ndim)
    decay = jnp.asarray(_ret_decay(t))
    inputs = [proj3, proj3, proj3, proj3, cos, sin, decay, s0]
    st_shape, st_spec = _seq_state_out(stack, bsz, (R_HEADS, R_KDIM, R_VDIM))
    body, x_specs, x_in, aliases = _alias_prev(functools.partial(_ret_seq_kernel, lead, t), len(inputs),
                                               stack and stack[1], (1,))
    return pl.pallas_call(
        body,
        grid=(bsz, nblk),
        in_specs=[pl.BlockSpec((None, t, D_MODEL), lambda b, c: (b, row_blk0 + c, COL_RQ)),
                  pl.BlockSpec((None, t, D_MODEL), lambda b, c: (b, row_blk0 + c, COL_RK)),
                  pl.BlockSpec((None, t, wide), lambda b, c: (b, row_blk0 + c, COL_RV)),
                  pl.BlockSpec((None, t, wide), lambda b, c: (b, row_blk0 + c, COL_RG)),
                  pl.BlockSpec((t, R_HALF), lambda b, c: (c, 0)),
                  pl.BlockSpec((t, R_HALF), lambda b, c: (c, 0)),
                  pl.BlockSpec(decay.shape, lambda b, c: (0, 0, 0)),
                  bcast(s0)] + x_specs,
        out_specs=[pl.BlockSpec((None, t, wide), lambda b, c: (b, c, 0)), st_spec],
        out_shape=[jax.ShapeDtypeStruct((bsz, nblk * t, wide), out_dtype), st_shape],
        scratch_shapes=[pltpu.VMEM((R_HEADS, R_KDIM, R_VDIM), F32)],
        input_output_aliases=aliases,
        compiler_params=_cparams(("parallel", "arbitrary")),
        name="ret_seq",
    )(*inputs, *x_in)


def _col_pieces(x):
    return jnp.concatenate(_split3(x.T), axis=1)


def _pick_col(n):
    r = _iota((3 * LANES, LANES), 0) & (LANES - 1)
    return jnp.where(r == n, 1.0, 0.0).astype(BF16)


def _pick_col_pair(n):
    r = _iota((3 * LANES, 2 * LANES), 0) & (LANES - 1)
    want = n + jnp.right_shift(_iota((3 * LANES, 2 * LANES), 1), int(math.log2(LANES)))
    return jnp.where(r == want, 1.0, 0.0).astype(BF16)


def _step_prep_kernel(lb_is_zero, xs_ref, bc_ref, x_ref, wdt_ref, hq_ref, hf_ref, rq_ref, rk_ref, conv_ref,
                      cw_ref, cb_ref, dtb_ref, a_ref, ex_ref, lb_ref, cos_ref, sin_ref,
                      xs_o, bc_o, xdt_o, edec_o, conv_o, hq_o, hef_o, hk_o, rq_o, rk_o):
    raw = jnp.concatenate([xs_ref[...], bc_ref[...]], axis=1)
    buf = conv_ref[...]
    acc = cb_ref[...] + cw_ref[3:4, :] * raw
    for k in range(M_CONV - 1):
        acc += cw_ref[k:k + 1, :] * buf[:, k * M_CONV_DIM:(k + 1) * M_CONV_DIM]
    conv_o[:, 0:2 * M_CONV_DIM] = buf[:, M_CONV_DIM:]
    conv_o[:, 2 * M_CONV_DIM:] = raw
    xbc = _silu(acc)
    xs = xbc[:, :M_INNER]
    xs_o[...] = xs
    bc_o[...] = xbc[:, M_INNER:]
    dt = _softplus(_dot_nt(x_ref[...].astype(BF16), wdt_ref[...]) + dtb_ref[...])
    ex = ex_ref[...]
    xdt_o[...] = _col_pieces(xs * _sel_right(dt, ex))
    edec_o[...] = _col_pieces(_sel_right(jnp.exp(dt * a_ref[...]), ex))
    logf, k = _hgrn_gates(hf_ref[...], lb_ref[...], lb_is_zero)
    hq_o[...] = hq_ref[...] * (H_KDIM ** -0.5)
    hef_o[...] = _col_pieces(jnp.exp(logf))
    hk_o[...] = _col_pieces(k)
    cos, sin = cos_ref[...], sin_ref[...]
    rq = jnp.concatenate([_rotary(rq_ref[:, h * R_KDIM:(h + 1) * R_KDIM], cos, sin) for h in range(R_HEADS)], axis=1)
    rk = jnp.concatenate([_rotary(rk_ref[:, h * R_KDIM:(h + 1) * R_KDIM], cos, sin) for h in range(R_HEADS)], axis=1)
    rq_o[...] = _col_pieces(rq)
    rk_o[...] = _col_pieces(rk * (R_KDIM ** -0.5))


def _step_prep(proj_s, x_s, wt, layer, conv_flat, p, cos, sin):
    nb = conv_flat.shape[0]
    col = lambda cblk: pl.BlockSpec((nb, D_MODEL), lambda i, cblk=cblk: (0, cblk))
    full = lambda a: pl.BlockSpec(a.shape, lambda i: (0,) * a.ndim)
    params = [p["conv_w"], p["conv_b"], p["dt_bias"], p["a_neg"], p["expand"], p["lb"], cos, sin]
    assert nb == LANES
    rows = lambda w: jax.ShapeDtypeStruct((nb, w), F32)
    cols = lambda w: jax.ShapeDtypeStruct((w, 3 * nb), BF16)
    shapes = [rows(M_INNER), rows(M_BC), cols(M_INNER), cols(M_INNER), rows((M_CONV - 1) * M_CONV_DIM),
              rows(D_MODEL), cols(D_MODEL), cols(D_MODEL), cols(D_MODEL), cols(D_MODEL)]
    return pl.pallas_call(
        functools.partial(_step_prep_kernel, layer == 0),
        grid=(1,),
        in_specs=[col(COL_XS), col(COL_BC), pl.BlockSpec((nb, D_MODEL), lambda i: (0, 0)), _dt_weight_spec(layer),
                  col(COL_HQ), col(COL_HF), col(COL_RQ), col(COL_RK), full(conv_flat)]
                 + [full(a) for a in params],
        out_specs=[pl.BlockSpec(s.shape, lambda i: (0, 0)) for s in shapes],
        out_shape=shapes,
        compiler_params=_cparams(("arbitrary",)),
        name="step_prep",
    )(proj_s, proj_s, x_s, wt, proj_s, proj_s, proj_s, proj_s, conv_flat, *params)


def _ssd_step_kernel(s_ref, xdt_ref, edec_ref, bc_ref, xs_ref, z_ref, dsk_ref, nw_ref, so_ref, y_ref,
                     yt_ref, xdt_b, edec_b):
    yt_ref[...] = jnp.zeros(yt_ref.shape, F32)
    for i in range(STEP_BT):
        ls = slice((i % 2) * LANES, (i % 2 + 1) * LANES)
        if i % 2 == 0:
            pick = _pick_col_pair(pl.program_id(0) * STEP_BT + i)
            xdt_b[...] = _dot(xdt_ref[...], pick)
            edec_b[...] = _dot(edec_ref[...], pick)
        for g in range(M_GROUPS):
            gs = slice(g * M_GW, (g + 1) * M_GW)
            hs = slice(M_HPG * g, M_HPG * (g + 1))
            st = s_ref[i, hs].reshape(M_GW, M_STATE)
            brow = bc_ref[i:i + 1, g * M_STATE:(g + 1) * M_STATE]
            crow = bc_ref[i:i + 1, M_BC // 2 + g * M_STATE:M_BC // 2 + (g + 1) * M_STATE]
            new = st * edec_b[gs, ls] + xdt_b[gs, ls] * brow
            so_ref[i, hs] = new.reshape(M_HPG, M_HEADDIM, M_STATE)
            yt_ref[gs, i:i + 1] = jnp.sum(new * crow, axis=-1, keepdims=True)
    y = yt_ref[...].T[0:STEP_BT, :]
    xs = xs_ref[...]
    y = (y + dsk_ref[...] * xs) * _silu(z_ref[...])
    for g in range(M_GROUPS):
        gs = slice(g * M_GW, (g + 1) * M_GW)
        y_ref[:, gs] = _rms(y[:, gs]) * nw_ref[:, gs]


def _ssd_step(state, layer, prev, xdt_c, edec_c, bc, xs, proj_s, p):
    nb = xs.shape[0]
    bt = STEP_BT
    full = lambda a: pl.BlockSpec(a.shape, lambda j: (0,) * a.ndim)
    sspec = pl.BlockSpec((None, bt, M_HEADS, M_HEADDIM, M_STATE), lambda j: (layer, j, 0, 0, 0))
    inputs = [state, xdt_c, edec_c, bc, xs, proj_s, p["d_skip"], p["m_norm_w"]]
    body, x_specs, x_in, aliases = _alias_prev(_ssd_step_kernel, len(inputs),
                                               None if prev is None else [prev], (0,))
    return pl.pallas_call(
        body,
        grid=(nb // bt,),
        in_specs=[sspec, full(xdt_c), full(edec_c),
                  pl.BlockSpec((bt, M_BC), lambda j: (j, 0)),
                  pl.BlockSpec((bt, M_INNER), lambda j: (j, 0)),
                  pl.BlockSpec((bt, D_MODEL), lambda j: (j, COL_Z)),
                  full(p["d_skip"]), full(p["m_norm_w"])] + x_specs,
        out_specs=[sspec, pl.BlockSpec((bt, M_INNER), lambda j: (j, 0))],
        out_shape=[jax.ShapeDtypeStruct(state.shape, F32),
                   jax.ShapeDtypeStruct((nb, M_INNER), F32)],
        scratch_shapes=[pltpu.VMEM((M_INNER, LANES), F32)] + [pltpu.VMEM((M_INNER, 2 * LANES), F32)] * 2,
        input_output_aliases=aliases,
        compiler_params=_cparams(("parallel",)),
        name="ssd_step",
    )(*inputs, *x_in)


def _cols(x):
    nb, w = x.shape
    return jnp.transpose(x.reshape(nb // STEP_BT, STEP_BT, w), (0, 2, 1))


def _hgrn_step_kernel(s_ref, q_ref, ef_ref, k_ref, v_ref, g_ref, nw_ref, so_ref, y_ref, ef_b, k_b):
    for i in range(STEP_BT):
        ls = slice((i % 2) * LANES, (i % 2 + 1) * LANES)
        if i % 2 == 0:
            pick = _pick_col_pair(pl.program_id(0) * STEP_BT + i)
            ef_b[...] = _dot(ef_ref[...], pick)
            k_b[...] = _dot(k_ref[...], pick)
        for h in range(H_HEADS):
            cs = slice(h * H_KDIM, (h + 1) * H_KDIM)
            new = s_ref[i, h] * ef_b[cs, ls] + k_b[cs, ls] * v_ref[i:i + 1, cs]
            so_ref[i, h] = new
            y_ref[i:i + 1, cs] = jnp.sum(new * q_ref[cs, i:i + 1], axis=0, keepdims=True)
    for h in range(H_HEADS):
        cs = slice(h * H_KDIM, (h + 1) * H_KDIM)
        y_ref[:, cs] = _rms(y_ref[:, cs]) * nw_ref[:, cs] * _sigmoid(g_ref[:, cs])


def _hgrn_step(state, layer, prev, q_c, ef_c, k_c, proj_s, nw):
    nb = state.shape[1]
    bt = STEP_BT
    cspec = pl.BlockSpec(ef_c.shape, lambda j: (0, 0))
    qspec = pl.BlockSpec((None, D_MODEL, bt), lambda j: (j, 0, 0))
    sspec = pl.BlockSpec((None, bt, H_HEADS, H_KDIM, H_VDIM), lambda j: (layer, j, 0, 0, 0))
    inputs = [state, q_c, ef_c, k_c, proj_s, proj_s, nw]
    body, x_specs, x_in, aliases = _alias_prev(_hgrn_step_kernel, len(inputs),
                                               None if prev is None else [prev], (0,))
    return pl.pallas_call(
        body,
        grid=(nb // bt,),
        in_specs=[sspec, qspec, cspec, cspec,
                  pl.BlockSpec((bt, D_MODEL), lambda j: (j, COL_HI)),
                  pl.BlockSpec((bt, D_MODEL), lambda j: (j, COL_HG)),
                  pl.BlockSpec(nw.shape, lambda j: (0, 0))] + x_specs,
        out_specs=[sspec, pl.BlockSpec((bt, D_MODEL), lambda j: (j, 0))],
        out_shape=[jax.ShapeDtypeStruct(state.shape, F32),
                   jax.ShapeDtypeStruct((nb, D_MODEL), F32)],
        scratch_shapes=[pltpu.VMEM((D_MODEL, 2 * LANES), F32)] * 2,
        input_output_aliases=aliases,
        compiler_params=_cparams(("parallel",)),
        name="hgrn_step",
    )(*inputs, *x_in)


def _ret_step_kernel(*refs):
    _ret_token_update(pl.program_id(0), *refs)


def _ret_step(state, layer, prev, q_c, k_c, v3, g3, gam):
    nb = state.shape[1]
    full = lambda a: pl.BlockSpec(a.shape, lambda n: (0,) * a.ndim)
    seq_row = lambda a: pl.BlockSpec((None,) + a.shape[1:], lambda n: (n, 0, 0))
    sspec = pl.BlockSpec((None, None, R_HEADS, R_KDIM, R_VDIM), lambda n: (layer, n, 0, 0, 0))
    inputs = [state, q_c, k_c, v3, g3, gam]
    body, x_specs, x_in, aliases = _alias_prev(_ret_step_kernel, len(inputs),
                                               None if prev is None else [prev], (0,))
    return pl.pallas_call(
        body,
        grid=(nb,),
        in_specs=[sspec, full(q_c), full(k_c), seq_row(v3), seq_row(g3), full(gam)] + x_specs,
        out_specs=[sspec, seq_row(v3)],
        out_shape=[jax.ShapeDtypeStruct(state.shape, F32), jax.ShapeDtypeStruct(v3.shape, F32)],
        input_output_aliases=aliases,
        compiler_params=_cparams(("parallel",)),
        name="ret_step",
    )(*inputs, *x_in)


def _rope_tables(positions):
    inv_freq = 1.0 / (ROPE_BASE ** jnp.linspace(0.0, 1.0, R_HALF, dtype=F32))
    ang = positions[:, None] * inv_freq[None, :]
    return jnp.cos(ang), jnp.sin(ang)


def _per_channel(v):
    return jnp.repeat(v.astype(F32), M_HEADDIM).reshape(1, M_INNER)


def _pad_lanes(v):
    return jnp.pad(v.astype(F32), (0, LANES - v.shape[0])).reshape(1, LANES)


def kernel(x_prompt, x_sample, state_ssm, state_conv, state_hgrn, state_ret, meta_tokens, ln_in_g, ln_in_b,
           w_in, conv_w, conv_b, dt_bias, a_log, d_skip, m_norm_w, hgrn_lb_logits, h_norm_w, w_br_m, w_br_h,
           w_br_r, w_out, ln1_g, ln1_b, w_ffn_in, w_ffn_out, ln2_g, ln2_b):
    bp, sp = x_prompt.shape[0], x_prompt.shape[1]
    nb = x_sample.shape[0]
    assert x_sample.shape[1] == 1 and nb == SMALL_ROWS - SEQ_BLOCK and nb % STEP_BT == 0
    assert sp % RET_BLOCK == 0 and meta_tokens.shape[0] == N_META

    wt = jnp.swapaxes(w_in, 1, 2).astype(BF16)
    wm_b, wh_b, wr_b, wo_b = (w.astype(BF16) for w in (w_br_m, w_br_h, w_br_r, w_out))
    wfi_b, wfo_b = w_ffn_in.astype(BF16), w_ffn_out.astype(BF16)
    ln1 = (ln1_g.reshape(DEPTH, 1, D_MODEL), ln1_b.reshape(DEPTH, 1, D_MODEL))
    ln2 = (ln2_g.reshape(DEPTH, 1, D_MODEL), ln2_b.reshape(DEPTH, 1, D_MODEL))
    lb_cum = jnp.cumsum(jax.nn.softmax(hgrn_lb_logits.astype(F32), axis=0), axis=0)
    lbs = lb_cum - lb_cum[0]
    expand = (np.arange(LANES)[:, None] == (np.arange(M_INNER)[None, :] // M_HEADDIM)).astype(np.float32)
    expand = jnp.asarray(np.concatenate([expand] * 3, 0), BF16)
    h_sums, h_pair = _hgrn_tables()
    h_tables = (jnp.asarray(h_sums, BF16), jnp.asarray(h_pair, F32))
    gam = jnp.asarray(np.broadcast_to(
        np.array([1.0 - 2.0 ** (-5.0 - h) for h in range(R_HEADS)], np.float32)[:, None, None],
        (R_HEADS, 1, R_VDIM)))

    pos_real = jnp.arange(N_META, N_META + sp, dtype=F32)
    pos_meta = jnp.maximum(jnp.arange(SEQ_BLOCK, dtype=F32) - META_LEAD, 0.0)
    pos_samp = jnp.full((nb,), float(PAST_LEN), F32)
    cos_r, sin_r = _rope_tables(pos_real)
    cos_m, sin_m = _rope_tables(pos_meta)
    cos_s, sin_s = _rope_tables(pos_samp)

    x_real = _layer_norm_rows(x_prompt.reshape(bp * sp, D_MODEL), ln_in_g, ln_in_b, 512)
    small_in = jnp.concatenate([x_sample.reshape(nb, D_MODEL),
                                jnp.zeros((META_LEAD, D_MODEL), F32), meta_tokens.astype(F32)], axis=0)
    x_small = _layer_norm_rows(small_in, ln_in_g, ln_in_b, SMALL_ROWS)

    z_ssm = jnp.zeros((1, M_HEADS, M_HEADDIM, M_STATE), F32)
    z_conv = jnp.zeros((1, 8, M_CONV_DIM), F32)
    z_hgrn = jnp.zeros((1, H_HEADS, H_KDIM, H_VDIM), F32)
    z_ret = jnp.zeros((1, R_HEADS, R_KDIM, R_VDIM), F32)
    ssm_p = conv_p = hgrn_p = ret_p = ssm_s = hgrn_s = ret_s = None
    conv_s = []
    for l in range(DEPTH):
        p = dict(conv_w=conv_w[l], conv_b=conv_b[l].reshape(1, -1), dt_bias=_pad_lanes(dt_bias[l]),
                 a_neg=_pad_lanes(-jnp.exp(a_log[l].astype(F32))), d_skip=_per_channel(d_skip[l]),
                 m_norm_w=m_norm_w[l].reshape(1, -1), expand=expand, lb=lbs[l].reshape(1, -1))
        hnw = h_norm_w[l].reshape(1, -1)

        proj_s = _proj(x_small, wt, l, SMALL_ROWS, "proj_small")
        proj_s3, x_small3 = proj_s.reshape(1, SMALL_ROWS, PROJ_COLS), x_small.reshape(1, SMALL_ROWS, D_MODEL)
        conv_flat = state_conv[l].reshape(nb, (M_CONV - 1) * M_CONV_DIM)
        (xs_s, bc_s, xdt_s, edec_s, conv_new, hq_s, hef_s, hk_s, rq_s, rk_s) = _step_prep(
            proj_s, x_small, wt, l, conv_flat, p, cos_s, sin_s)
        conv_s.append(conv_new.reshape(nb, M_CONV - 1, M_CONV_DIM))
        wide = R_HEADS * R_VDIM
        rv3 = proj_s[:nb, 0:wide].reshape(nb, 1, wide)
        rg3 = proj_s[:nb, wide:2 * wide].reshape(nb, 1, wide)

        if _proj_steps(bp * sp, PROJ_TM) == nb:
            proj_r, ret_s, yr_s = _proj(x_real, wt, l, PROJ_TM, "proj_real",
                                        ride=(state_ret, ret_s, rq_s, rk_s, rv3, rg3, gam))
        else:
            proj_r = _proj(x_real, wt, l, PROJ_TM, "proj_real")
            ret_s, yr_s = _ret_step(state_ret, l, ret_s, rq_s, rk_s, rv3, rg3, gam)
        yr_s = yr_s.reshape(nb, wide)
        proj_r3, x_real3 = proj_r.reshape(bp, sp, PROJ_COLS), x_real.reshape(bp, sp, D_MODEL)

        ssm_s, ym_s = _ssd_step(state_ssm, l, ssm_s, xdt_s, edec_s, bc_s, xs_s, proj_s, p)
        hgrn_s, yh_s = _hgrn_step(state_hgrn, l, hgrn_s, _cols(hq_s), hef_s, hk_s, proj_s, hnw)

        ym_m, ssm_m, conv_m = _ssd_seq(proj_s3, x_small3, wt, l, 1, 1, META_LEAD, p, z_ssm, z_conv, F32)
        yh_m, hgrn_m = _hgrn_seq(proj_s3, 1, 1, META_LEAD, p["lb"], l == 0, hnw, h_tables, z_hgrn, F32)
        yr_m, ret_m = _ret_seq(proj_s3, SEQ_BLOCK, 1, 1, META_LEAD, cos_m, sin_m, z_ret, F32)

        conv0 = jnp.pad(conv_m, ((0, 0), (8 - (M_CONV - 1), 0), (0, 0)))
        ym_r, ssm_p, conv_p = _ssd_seq(proj_r3, x_real3, wt, l, 0, sp // SEQ_BLOCK, 0, p, ssm_m, conv0, BF16,
                                       stack=(l, None if l == 0 else [ssm_p, conv_p]))
        yh_r, hgrn_p = _hgrn_seq(proj_r3, 0, sp // SEQ_BLOCK, 0, p["lb"], l == 0, hnw, h_tables, hgrn_m, BF16,
                                 stack=(l, None if l == 0 else [hgrn_p]))
        yr_r, ret_p = _ret_seq(proj_r3, RET_BLOCK, 0, sp // RET_BLOCK, 0, cos_r, sin_r, ret_m, BF16,
                               stack=(l, None if l == 0 else [ret_p]))

        ym_small = jnp.concatenate([ym_s, ym_m[0]], axis=0)
        yh_small = jnp.concatenate([yh_s, yh_m[0]], axis=0)
        yr_small = jnp.concatenate([yr_s, yr_m[0]], axis=0)
        x_real = _mix(x_real, ym_r.reshape(bp * sp, -1), yh_r.reshape(bp * sp, -1), yr_r.reshape(bp * sp, -1),
                      proj_r, l, wm_b, wh_b, wr_b, wo_b, *ln1, 256)
        x_small = _mix(x_small, ym_small, yh_small, yr_small, proj_s, l, wm_b, wh_b, wr_b, wo_b, *ln1, SMALL_ROWS)
        x_real = _ffn(x_real, l, wfi_b, wfo_b, *ln2, 512)
        x_small = _ffn(x_small, l, wfi_b, wfo_b, *ln2, SMALL_ROWS)

    return (x_real.reshape(bp, sp, D_MODEL), x_small[:nb].reshape(nb, 1, D_MODEL),
            ssm_p, conv_p, hgrn_p, ret_p, ssm_s, jnp.stack(conv_s), hgrn_s, ret_s)
```

```python
import functools
import math

import numpy as np
import jax
import jax.numpy as jnp
from jax import lax
from jax.experimental import pallas as pl
from jax.experimental.pallas import tpu as pltpu

F32 = jnp.float32
BF16 = jnp.bfloat16

D_MODEL = 1024
DEPTH = 2
N_META = 16
M_INNER = D_MODEL
M_HEADDIM = 64
M_HEADS = M_INNER // M_HEADDIM
M_GROUPS = 4
M_HPG = M_HEADS // M_GROUPS
M_STATE = 128
M_CONV = 4
M_BC = 2 * M_GROUPS * M_STATE
M_CONV_DIM = M_INNER + M_BC
M_GW = M_INNER // M_GROUPS
H_KDIM = 128
H_HEADS = D_MODEL // H_KDIM
H_VDIM = 128
R_HEADS = 4
R_KDIM = D_MODEL // R_HEADS
R_VDIM = 2 * R_KDIM
R_HALF = R_KDIM // 2
ROPE_BASE = 10000.0
D_FF = ((8 * D_MODEL // 3 + 255) // 256) * 256
FF_CHUNK = 256
DN_ALPHA = (2 * DEPTH) ** 0.25
PAST_LEN = 16384

LANES = 128
SEQ_BLOCK = 128
RET_BLOCK = 256
SMALL_ROWS = 256
META_LEAD = SEQ_BLOCK - N_META
STEP_BT = 8
VMEM_BYTES = 64 * 1024 * 1024
VMEM_LIMIT = VMEM_BYTES - 4 * 1024 * 1024

COL_RV, COL_RG = 0, 1
COL_Z, COL_XS, COL_BC, COL_HQ, COL_HF, COL_HI, COL_HG, COL_RQ, COL_RK, COL_GM, COL_GH, COL_GR = range(4, 16)
PROJ_COLS = 16 * 1024

_ORIG_SPLITS = (M_INNER, M_CONV_DIM, M_HEADS, 1024, 1024, 1024, 1024, 1024, 1024, 2048, 2048, 3072)
_ORIG_OFF = np.concatenate([[0], np.cumsum(_ORIG_SPLITS)]).tolist()


def _cparams(sem):
    return pltpu.CompilerParams(dimension_semantics=sem, vmem_limit_bytes=VMEM_LIMIT)


def _sigmoid(x):
    return 1.0 / (1.0 + jnp.exp(-x))


def _silu(x):
    return x * _sigmoid(x)


def _softplus(x):
    return jnp.maximum(x, 0.0) + jnp.log1p(jnp.exp(-jnp.abs(x)))


def _layer_norm(x, g, b):
    mu = jnp.mean(x, axis=-1, keepdims=True)
    xc = x - mu
    var = jnp.mean(xc * xc, axis=-1, keepdims=True)
    return xc * lax.rsqrt(var + 1e-5) * g + b


def _rms(x):
    return x * lax.rsqrt(jnp.mean(x * x, axis=-1, keepdims=True) + 1e-6)


def _split3(x):
    hi = x.astype(BF16)
    r = x - hi.astype(F32)
    mid = r.astype(BF16)
    lo = (r - mid.astype(F32)).astype(BF16)
    return hi, mid, lo


def _dot(a, b):
    return jnp.dot(a, b, preferred_element_type=F32)


def _dot_nt(a, b):
    return lax.dot_general(a, b, (((1,), (1,)), ((), ())), preferred_element_type=F32)


def _sel_right(x, m3):
    return _dot(jnp.concatenate(_split3(x), axis=1), m3)


def _sel_left(m3, x):
    return _dot(m3, jnp.concatenate(_split3(x), axis=0))


def _iota(shape, dim):
    return lax.broadcasted_iota(jnp.int32, shape, dim)


def _ln_kernel(x_ref, g_ref, b_ref, o_ref):
    o_ref[...] = _layer_norm(x_ref[...], g_ref[...], b_ref[...])


def _layer_norm_rows(x, g, b, tm):
    m = x.shape[0]
    tm = min(tm, m)
    return pl.pallas_call(
        _ln_kernel,
        grid=(m // tm,),
        in_specs=[pl.BlockSpec((tm, D_MODEL), lambda i: (i, 0)),
                  pl.BlockSpec((1, D_MODEL), lambda i: (0, 0)),
                  pl.BlockSpec((1, D_MODEL), lambda i: (0, 0))],
        out_specs=pl.BlockSpec((tm, D_MODEL), lambda i: (i, 0)),
        out_shape=jax.ShapeDtypeStruct((m, D_MODEL), F32),
        compiler_params=_cparams(("parallel",)),
        name="ln_in",
    )(x, g.reshape(1, -1), b.reshape(1, -1))


def _proj_kernel(ride, x_ref, w_ref, *refs):
    j = pl.program_id(1)
    if ride:
        ride_in, (o_ref, ov_ref, so_ref, ry_ref, xb_ref) = refs[:6], refs[6:]
    else:
        o_ref, xb_ref = refs

    @pl.when(j == 0)
    def _():
        xb_ref[...] = x_ref[...].astype(BF16)

    w = w_ref[0]
    if ride:
        @pl.when(j < RV_BLOCKS)
        def _():
            step = 512
            for r in range(0, xb_ref.shape[0], step):
                ov_ref[r:r + step, :] = _dot_nt(xb_ref[r:r + step, :], w).astype(BF16)

        @pl.when(j >= RV_BLOCKS)
        def _():
            o_ref[...] = _dot_nt(xb_ref[...], w)
    else:
        o_ref[...] = _dot_nt(xb_ref[...], w)
    if ride:
        _ret_token_update(pl.program_id(0) * pl.num_programs(1) + pl.program_id(1), *ride_in, so_ref, ry_ref)


def _layer_block(a, layer):
    return pl.BlockSpec((None,) + a.shape[1:], lambda *_: (layer,) + (0,) * (a.ndim - 1),
                        pipeline_mode=pl.Buffered(1))


_W_RUNS = ((0, _ORIG_OFF[9]), (4096, _ORIG_OFF[0]), (7168, _ORIG_OFF[3]), (13312, _ORIG_OFF[11]))
PROJ_TN = 1024
PROJ_TM = 2048
RV_BLOCKS = R_HEADS * R_VDIM // PROJ_TN


def _orig_col(j):
    c = j * PROJ_TN
    off = c - _W_RUNS[0][0] + _W_RUNS[0][1]
    for new0, orig0 in _W_RUNS[1:]:
        off = jnp.where(c >= new0, c - new0 + orig0, off)
    return pl.multiple_of(off, M_HEADS)


def _proj_steps(m, tm):
    return (m // min(tm, m)) * (PROJ_COLS // PROJ_TN)


def _proj(x, wt, layer, tm, name, ride=None):
    m = x.shape[0]
    tm = min(tm, m)
    nj = PROJ_COLS // PROJ_TN
    inputs = [x, wt]
    in_specs = [pl.BlockSpec((tm, D_MODEL), lambda i, j: (i, 0)),
                pl.BlockSpec((pl.Element(1), pl.Element(PROJ_TN), pl.Element(D_MODEL)),
                             lambda i, j: (layer, _orig_col(j), 0))]
    out_specs = [pl.BlockSpec((tm, PROJ_TN), lambda i, j: (i, j))]
    out_shape = [jax.ShapeDtypeStruct((m, PROJ_COLS), F32)]
    prev = None
    if ride:
        state, prev, q_c, k_c, v3, g3, gam = ride
        assert _proj_steps(m, tm) == state.shape[1]
        full = lambda a: pl.BlockSpec(a.shape, lambda i, j: (0,) * a.ndim)
        seq_row = lambda a: pl.BlockSpec((None,) + a.shape[1:], lambda i, j: (i * nj + j, 0, 0))
        sspec = pl.BlockSpec((None, None, R_HEADS, R_KDIM, R_VDIM), lambda i, j: (layer, i * nj + j, 0, 0, 0))
        inputs += [state, q_c, k_c, v3, g3, gam]
        in_specs[0] = pl.BlockSpec((tm, D_MODEL), lambda i, j: (i, 0), pipeline_mode=pl.Buffered(1))
        in_specs += [sspec, full(q_c), full(k_c), seq_row(v3), seq_row(g3), full(gam)]
        out_specs = [pl.BlockSpec((tm, PROJ_TN), lambda i, j: (i, jnp.maximum(j, RV_BLOCKS))),
                     pl.BlockSpec((tm, PROJ_TN), lambda i, j: (i, jnp.minimum(j, RV_BLOCKS - 1))),
                     sspec, seq_row(v3)]
        out_shape += [jax.ShapeDtypeStruct((m, RV_BLOCKS * PROJ_TN), BF16),
                      jax.ShapeDtypeStruct(state.shape, F32), jax.ShapeDtypeStruct(v3.shape, F32)]
    body, x_specs, x_in, aliases = _alias_prev(functools.partial(_proj_kernel, bool(ride)), len(inputs),
                                               None if prev is None else [prev], (2,))
    res = pl.pallas_call(
        body,
        grid=(m // tm, nj),
        in_specs=in_specs + x_specs,
        out_specs=out_specs,
        out_shape=out_shape,
        scratch_shapes=[pltpu.VMEM((tm, D_MODEL), BF16)],
        input_output_aliases=aliases,
        compiler_params=_cparams(("parallel", "arbitrary")),
        name=name,
    )(*inputs, *x_in)
    return res if ride else res[0]


def _dt_weight_spec(layer):
    assert _ORIG_OFF[2] % LANES == 0
    return pl.BlockSpec((None, LANES, D_MODEL), lambda *_: (layer, _ORIG_OFF[2] // LANES, 0))


def _mix_kernel(x_ref, ym_ref, yh_ref, yr_ref, gm_ref, gh_ref, gr_ref,
                wm_ref, wh_ref, wr_ref, wo_ref, g_ref, b_ref, o_ref):
    mixed = _sigmoid(gm_ref[...]) * _dot(ym_ref[...].astype(BF16), wm_ref[...])
    mixed += _sigmoid(gh_ref[...]) * _dot(yh_ref[...].astype(BF16), wh_ref[...])
    mixed += _sigmoid(gr_ref[...]) * _dot(yr_ref[...].astype(BF16), wr_ref[...])
    h = _dot(mixed.astype(BF16), wo_ref[...])
    o_ref[...] = _layer_norm(DN_ALPHA * x_ref[...] + h, g_ref[...], b_ref[...])


def _mix(x, ym, yh, yr, proj, layer, wm, wh, wr, wo, g2, b2, tm):
    m = x.shape[0]
    tm = min(tm, m)
    row = lambda w: pl.BlockSpec((tm, w), lambda i: (i, 0))
    col = lambda c: pl.BlockSpec((tm, D_MODEL), lambda i, c=c: (i, c))
    full = lambda a: _layer_block(a, layer)
    return pl.pallas_call(
        _mix_kernel,
        grid=(m // tm,),
        in_specs=[row(D_MODEL), row(M_INNER), row(D_MODEL), row(R_HEADS * R_VDIM),
                  col(COL_GM), col(COL_GH), col(COL_GR),
                  full(wm), full(wh), full(wr), full(wo), full(g2), full(b2)],
        out_specs=row(D_MODEL),
        out_shape=jax.ShapeDtypeStruct((m, D_MODEL), F32),
        compiler_params=_cparams(("parallel",)),
        name="mix",
    )(x, ym, yh, yr, proj, proj, proj, wm, wh, wr, wo, g2, b2)


def _ffn_kernel(x_ref, wi_ref, wo_ref, g_ref, b_ref, o_ref):
    x = x_ref[...]
    xb = x.astype(BF16)
    acc = jnp.zeros(x.shape, F32)
    for j in range(0, D_FF, FF_CHUNK):
        hg = _dot(xb, wi_ref[:, j:j + FF_CHUNK])
        hu = _dot(xb, wi_ref[:, D_FF + j:D_FF + j + FF_CHUNK])
        acc += _dot((_silu(hg) * hu).astype(BF16), wo_ref[j:j + FF_CHUNK, :])
    o_ref[...] = _layer_norm(DN_ALPHA * x + acc, g_ref[...], b_ref[...])


def _ffn(x, layer, wi, wo, g2, b2, tm):
    m = x.shape[0]
    tm = min(tm, m)
    full = lambda a: _layer_block(a, layer)
    return pl.pallas_call(
        _ffn_kernel,
        grid=(m // tm,),
        in_specs=[pl.BlockSpec((tm, D_MODEL), lambda i: (i, 0)), full(wi), full(wo), full(g2), full(b2)],
        out_specs=pl.BlockSpec((tm, D_MODEL), lambda i: (i, 0)),
        out_shape=jax.ShapeDtypeStruct((m, D_MODEL), F32),
        compiler_params=_cparams(("parallel",)),
        name="ffn",
    )(x, wi, wo, g2, b2)


def _ssd_seq_kernel(lead, z_ref, xs_ref, bc_ref, x_ref, wdt_ref, cw_ref, cb_ref, dtb_ref, a_ref, dsk_ref, nw_ref,
                    ex_ref, s0_ref, c0_ref, y_ref, sout_ref, cout_ref, ext_ref, st_ref, xbc_ref):
    t = SEQ_BLOCK
    c = pl.program_id(1)

    @pl.when(c == 0)
    def _():
        ext_ref[0:8, :] = c0_ref[...]
        for g in range(M_GROUPS):
            st_ref[g] = s0_ref[M_HPG * g:M_HPG * (g + 1)].reshape(M_GW, M_STATE).T

    rows = _iota((t, 1), 0)

    def conv_chunk(q):
        w = M_CONV_DIM // 4
        cs = slice(q * w, (q + 1) * w)
        src = xs_ref if q < 2 else bc_ref
        raw = src[:, (q % 2) * w:(q % 2 + 1) * w]
        if lead:
            raw = jnp.where(rows >= lead, raw, 0.0)
        ext_ref[8:8 + t, cs] = raw
        acc = cb_ref[:, cs] + cw_ref[3:4, cs] * raw
        for k in range(M_CONV - 1):
            acc += cw_ref[k:k + 1, cs] * ext_ref[5 + k:5 + k + t, cs]
        ext_ref[5:8, cs] = ext_ref[t + 5:t + 8, cs]
        xbc_ref[:, cs] = _silu(acc)

    dt_raw = _dot_nt(x_ref[...].astype(BF16), wdt_ref[...])
    conv_chunk(0)
    dt = _softplus(dt_raw + dtb_ref[...])
    if lead:
        dt = jnp.where(rows >= lead, dt, 0.0)
    a = dt * a_ref[...]
    ti, si = _iota((t, t), 0), _iota((t, t), 1)
    tril = si <= ti
    tri = jnp.where(tril, 1.0, 0.0).astype(BF16)
    cum = _sel_left(jnp.concatenate([tri] * 3, axis=1), a)
    conv_chunk(1)
    ecum = jnp.exp(cum)
    tailw = jnp.exp(cum[t - 1:t, :] - cum) * dt
    cum_t, dt_t = cum.T, dt.T
    ex = ex_ref[...]
    ecum_full = _sel_right(ecum, ex)
    conv_chunk(2)
    tailw_full = _sel_right(tailw, ex)
    conv_chunk(3)

    xs = xbc_ref[:, 0:M_INNER]
    xw = (xs * tailw_full).astype(BF16)
    xs_b = xs.astype(BF16)
    head_of_lane = jnp.right_shift(_iota((1, M_GW), 1), int(math.log2(M_HEADDIM)))

    for g in range(M_GROUPS):
        gs = slice(g * M_GW, (g + 1) * M_GW)
        bm_f = xbc_ref[:, M_INNER + g * M_STATE:M_INNER + (g + 1) * M_STATE]
        bm = bm_f.astype(BF16)
        cm = xbc_ref[:, M_INNER + M_BC // 2 + g * M_STATE:M_INNER + M_BC // 2 + (g + 1) * M_STATE].astype(BF16)
        cb = _dot_nt(cm, bm)
        st = st_ref[g]
        y_g = _dot(cm, st.astype(BF16)) * ecum_full[:, gs]
        for hh in range(M_HPG):
            h = g * M_HPG + hh
            diff = cum[:, h:h + 1] - cum_t[h:h + 1, :]
            w = cb * jnp.exp(jnp.where(tril, diff, -1e30)) * dt_t[h:h + 1, :]
            x_h = jnp.where(head_of_lane == hh, xs_b[:, gs], jnp.zeros((), BF16))
            y_g = y_g + _dot(w.astype(BF16), x_h)
        st_ref[g] = st * ecum_full[t - 1:t, gs] + _dot(bm_f.T.astype(BF16), xw[:, gs])
        y_g = (y_g + dsk_ref[:, gs] * xs[:, gs]) * _silu(z_ref[:, gs])
        y_ref[:, gs] = (_rms(y_g) * nw_ref[:, gs]).astype(y_ref.dtype)

    @pl.when(c == pl.num_programs(1) - 1)
    def _():
        cout_ref[...] = ext_ref[5:8, :]
        for g in range(M_GROUPS):
            sout_ref[M_HPG * g:M_HPG * (g + 1)] = st_ref[g].T.reshape(M_HPG, M_HEADDIM, M_STATE)


def _alias_prev(body, n_in, prevs, out_ids):
    if not prevs:
        return body, [], [], {}
    k = len(prevs)
    wrapped = lambda *refs: body(*refs[:n_in], *refs[n_in + k:])
    return (wrapped, [pl.BlockSpec(memory_space=pl.ANY)] * k, list(prevs),
            {n_in + i: o for i, o in enumerate(out_ids)})


def _seq_state_out(stack, bsz, dims):
    zeros = (0,) * len(dims)
    if stack is None:
        return (jax.ShapeDtypeStruct((bsz,) + dims, F32),
                pl.BlockSpec((None,) + dims, lambda b, c: (b,) + zeros))
    layer = stack[0]
    return (jax.ShapeDtypeStruct((DEPTH, bsz) + dims, F32),
            pl.BlockSpec((None, None) + dims, lambda b, c: (layer, b) + zeros))


def _ssd_seq(proj3, x3, wt, layer, row_blk0, nblk, lead, p, s0, c0, out_dtype, stack=None):
    bsz = proj3.shape[0]
    t = SEQ_BLOCK
    colspec = lambda cblk: pl.BlockSpec((None, t, D_MODEL), lambda b, c, cblk=cblk: (b, row_blk0 + c, cblk))
    full = lambda a: pl.BlockSpec(a.shape, lambda b, c: (0,) * a.ndim)
    bcast = lambda a: pl.BlockSpec((None,) + a.shape[1:], lambda b, c: (0,) * a.ndim)
    params = [p["conv_w"], p["conv_b"], p["dt_bias"], p["a_neg"], p["d_skip"], p["m_norm_w"], p["expand"]]
    inputs = [proj3, proj3, proj3, x3, wt, *params, s0, c0]
    st_shape, st_spec = _seq_state_out(stack, bsz, (M_HEADS, M_HEADDIM, M_STATE))
    cv_shape, cv_spec = _seq_state_out(stack, bsz, (M_CONV - 1, M_CONV_DIM))
    body, x_specs, x_in, aliases = _alias_prev(functools.partial(_ssd_seq_kernel, lead), len(inputs),
                                               stack and stack[1], (1, 2))
    return pl.pallas_call(
        body,
        grid=(bsz, nblk),
        in_specs=[colspec(COL_Z), colspec(COL_XS), colspec(COL_BC),
                  pl.BlockSpec((None, t, D_MODEL), lambda b, c: (b, row_blk0 + c, 0)), _dt_weight_spec(layer)]
                 + [full(a) for a in params] + [bcast(s0), bcast(c0)] + x_specs,
        out_specs=[pl.BlockSpec((None, t, M_INNER), lambda b, c: (b, c, 0)), st_spec, cv_spec],
        out_shape=[jax.ShapeDtypeStruct((bsz, nblk * t, M_INNER), out_dtype), st_shape, cv_shape],
        scratch_shapes=[pltpu.VMEM((t + 8, M_CONV_DIM), F32),
                        pltpu.VMEM((M_GROUPS, M_STATE, M_GW), F32),
                        pltpu.VMEM((t, M_CONV_DIM), F32)],
        input_output_aliases=aliases,
        compiler_params=_cparams(("parallel", "arbitrary")),
        name="ssd_seq",
    )(*inputs, *x_in)


def _hgrn_gates(fz, lb, lb_is_zero):
    e = jnp.exp(-jnp.abs(fz))
    r = 1.0 / (1.0 + e)
    pos = fz >= 0.0
    sig_neg = jnp.where(pos, e * r, r)
    log_sig = jnp.minimum(fz, 0.0) - jnp.log(1.0 + e)
    if lb_is_zero:
        return log_sig, sig_neg
    sig_pos = jnp.where(pos, r, e * r)
    logf = jnp.where(lb > 0.0, jnp.log(lb + (1.0 - lb) * sig_pos), log_sig)
    return logf, (1.0 - lb) * sig_neg


H_LEVELS = int(math.log2(SEQ_BLOCK))


def _hgrn_tables():
    n = SEQ_BLOCK
    t = np.arange(n)[:, None]
    u = np.arange(n)[None, :]
    pair = []
    for b in range(H_LEVELS):
        bit = ((t >> b) & 1) == 1
        pair.append(((t >> (b + 1)) == (u >> (b + 1))) & bit & (((u >> b) & 1) == 0))
    m1 = ((t >> 1) | 1) << 1
    lvl1 = np.where(((t >> 1) & 1) == 1, (u >= m1) & (u <= t), (u > t) & (u < m1))
    sums = np.concatenate([u <= t, lvl1], 0).astype(np.float32)
    return np.concatenate([sums] * 3, 1), np.concatenate(pair, 0).astype(np.float32)


def _exp_neg_abs(d):
    return jnp.exp2(jnp.abs(d) * (-1.0 / math.log(2.0)))


def _ret_token_update(n, s_ref, q_ref, k_ref, v_ref, g_ref, gam_ref, so_ref, y_ref):
    pick = _pick_col(n)
    for h in range(R_HEADS):
        rows = slice(h * R_KDIM, (h + 1) * R_KDIM)
        q_b = _dot(q_ref[rows, :], pick)
        k_b = _dot(k_ref[rows, :], pick)
        outs = []
        for c0 in range(0, R_VDIM, LANES):
            cs = slice(c0, c0 + LANES)
            new = s_ref[h, :, cs] * gam_ref[h, :, cs] + k_b * v_ref[:, h * R_VDIM + c0:h * R_VDIM + c0 + LANES]
            so_ref[h, :, cs] = new
            outs.append(jnp.sum(new * q_b, axis=0, keepdims=True))
        vs = slice(h * R_VDIM, (h + 1) * R_VDIM)
        y_ref[:, vs] = _rms(jnp.concatenate(outs, axis=1)) * _silu(g_ref[:, vs])


def _hgrn_seq_kernel(lead, lb_is_zero, q_ref, f_ref, i_ref, g_ref, lb_ref, nw_ref, sums_ref, pair_ref, s0_ref,
                     y_ref, sout_ref, st_ref, ex_ref, q_s, k_s, z_ref, qd_ref, kd_ref, sc_ref):
    t = SEQ_BLOCK
    c = pl.program_id(1)

    @pl.when(c == 0)
    def _():
        for h in range(H_HEADS):
            st_ref[h] = s0_ref[h].T

    rows = _iota((t, 1), 0)
    logf, k = _hgrn_gates(f_ref[...], lb_ref[...], lb_is_zero)
    if lead:
        logf = jnp.where(rows >= lead, logf, 0.0)
        k = jnp.where(rows >= lead, k, 0.0)
    q = q_ref[...] * (H_KDIM ** -0.5)
    q_s[...] = q
    k_s[...] = k
    ex_ref[...] = _dot(sums_ref[...], jnp.concatenate(_split3(logf), axis=0))

    def side_of(b):
        return (jnp.right_shift(rows, b) & 1) == 1

    z_ref[0] = jnp.where(side_of(0), q * jnp.exp(logf), k).astype(BF16)
    z_ref[1] = (jnp.exp(ex_ref[t:2 * t, :]) * jnp.where(side_of(1), q_s[...], k_s[...])).astype(BF16)
    for b in range(2, H_LEVELS):
        half = 1 << b
        groups = range(0, t, 2 * half)
        d = jnp.concatenate([ex_ref[g0:g0 + 2 * half, :] - ex_ref[g0 + half - 1:g0 + half, :]
                             for g0 in groups], axis=0)
        if half >= 8:
            qk = jnp.concatenate([ref[g0 + o:g0 + o + half, :] for g0 in groups
                                  for ref, o in ((k_s, 0), (q_s, half))], axis=0)
        else:
            qk = jnp.where(side_of(b), q_s[...], k_s[...])
        z_ref[b] = (_exp_neg_abs(d) * qk).astype(BF16)
    cum = ex_ref[0:t, :]
    qd_ref[...] = (q_s[...] * jnp.exp(cum)).astype(BF16)
    kd_ref[...] = (k_s[...] * jnp.exp(ex_ref[t - 1:t, :] - cum)).astype(BF16)

    for h in range(H_HEADS):
        cs = slice(h * H_KDIM, (h + 1) * H_KDIM)
        scores = None
        for b in range(H_LEVELS):
            z = z_ref[b, :, cs]
            p = _dot_nt(z, z) * pair_ref[b * t:(b + 1) * t, :]
            scores = p if scores is None else scores + p
        sc_ref[h] = scores.astype(BF16)

    for h in range(H_HEADS):
        cs = slice(h * H_KDIM, (h + 1) * H_KDIM)
        v = i_ref[:, cs]
        st = st_ref[h]
        o = (_dot(sc_ref[h], v.astype(BF16))
             + jnp.sum(q_s[:, cs] * k_s[:, cs], axis=-1, keepdims=True) * v
             + _dot_nt(qd_ref[:, cs], st.astype(BF16)))
        y = _rms(o) * nw_ref[:, cs] * _sigmoid(g_ref[:, cs])
        y_ref[:, cs] = y.astype(y_ref.dtype)
        st_ref[h] = st * jnp.exp(ex_ref[t - 1:t, cs]) + _dot(v.T.astype(BF16), kd_ref[:, cs])

    @pl.when(c == pl.num_programs(1) - 1)
    def _():
        for h in range(H_HEADS):
            sout_ref[h] = st_ref[h].T


def _hgrn_seq(proj3, row_blk0, nblk, lead, lb, lb_is_zero, nw, tables, s0, out_dtype, stack=None):
    bsz = proj3.shape[0]
    t = SEQ_BLOCK
    colspec = lambda cblk: pl.BlockSpec((None, t, D_MODEL), lambda b, c, cblk=cblk: (b, row_blk0 + c, cblk))
    full = lambda a: pl.BlockSpec(a.shape, lambda b, c: (0,) * a.ndim)
    bcast = lambda a: pl.BlockSpec((None,) + a.shape[1:], lambda b, c: (0,) * a.ndim)
    inputs = [proj3, proj3, proj3, proj3, lb, nw, *tables, s0]
    st_shape, st_spec = _seq_state_out(stack, bsz, (H_HEADS, H_KDIM, H_VDIM))
    body, x_specs, x_in, aliases = _alias_prev(functools.partial(_hgrn_seq_kernel, lead, lb_is_zero), len(inputs),
                                               stack and stack[1], (1,))
    return pl.pallas_call(
        body,
        grid=(bsz, nblk),
        in_specs=[colspec(COL_HQ), colspec(COL_HF), colspec(COL_HI), colspec(COL_HG),
                  full(lb), full(nw), full(tables[0]), full(tables[1]), bcast(s0)] + x_specs,
        out_specs=[pl.BlockSpec((None, t, D_MODEL), lambda b, c: (b, c, 0)), st_spec],
        out_shape=[jax.ShapeDtypeStruct((bsz, nblk * t, D_MODEL), out_dtype), st_shape],
        input_output_aliases=aliases,
        scratch_shapes=[pltpu.VMEM((H_HEADS, H_VDIM, H_KDIM), F32),
                        pltpu.VMEM((2 * t, D_MODEL), F32),
                        pltpu.VMEM((t, D_MODEL), F32), pltpu.VMEM((t, D_MODEL), F32),
                        pltpu.VMEM((H_LEVELS, t, D_MODEL), BF16),
                        pltpu.VMEM((t, D_MODEL), BF16), pltpu.VMEM((t, D_MODEL), BF16),
                        pltpu.VMEM((H_HEADS, t, t), BF16)],
        compiler_params=_cparams(("parallel", "arbitrary")),
        name="hgrn_seq",
    )(*inputs, *x_in)


def _log_gamma(h):
    return math.log(1.0 - 2.0 ** (-5.0 - h))


def _rotary(x, cos, sin):
    x1, x2 = x[:, :R_HALF], x[:, R_HALF:]
    return jnp.concatenate([x1 * cos - x2 * sin, x2 * cos + x1 * sin], axis=1)


def _ret_decay(t):
    d = np.arange(t)[:, None] - np.arange(t)[None, :]
    return np.stack([np.where(d >= 0, np.exp(np.maximum(d, 0) * _log_gamma(h)), 0.0)
                     for h in range(R_HEADS)]).astype(np.float32)


def _ret_seq_kernel(lead, t, q_ref, k_ref, v_ref, g_ref, cos_ref, sin_ref, dec_ref, s0_ref,
                    y_ref, sout_ref, st_ref):
    c = pl.program_id(1)

    @pl.when(c == 0)
    def _():
        st_ref[...] = s0_ref[...]

    cos, sin = cos_ref[...], sin_ref[...]
    tcol = _iota((t, 1), 0).astype(F32)
    for h in range(R_HEADS):
        lg = _log_gamma(h)
        ks = slice(h * R_KDIM, (h + 1) * R_KDIM)
        vs = slice(h * R_VDIM, (h + 1) * R_VDIM)
        qh = _rotary(q_ref[:, ks], cos, sin)
        kh = _rotary(k_ref[:, ks], cos, sin) * (R_KDIM ** -0.5)
        if lead:
            kh = jnp.where(_iota((t, 1), 0) >= lead, kh, 0.0)
        qb, kb, vb = qh.astype(BF16), kh.astype(BF16), v_ref[:, vs].astype(BF16)
        scores = _dot_nt(qb, kb) * dec_ref[h]
        st = st_ref[h]
        o = _dot(scores.astype(BF16), vb) + _dot(qb, st.astype(BF16)) * jnp.exp((tcol + 1.0) * lg)
        kdec = (kh * jnp.exp((t - 1.0 - tcol) * lg)).T.astype(BF16)
        st_ref[h] = st * math.exp(t * lg) + _dot(kdec, vb)
        y_ref[:, vs] = (_rms(o) * _silu(g_ref[:, vs])).astype(y_ref.dtype)

    @pl.when(c == pl.num_programs(1) - 1)
    def _():
        sout_ref[...] = st_ref[...]


def _ret_seq(proj3, t, row_blk0, nblk, lead, cos, sin, s0, out_dtype, stack=None, v3=None):
    bsz = proj3.shape[0]
    wide = R_HEADS * R_VDIM
    bcast = lambda a: pl.BlockSpec((None,) + a.shape[1:], lambda b, c: (0,) * a.ndim)
    decay = jnp.asarray(_ret_decay(t))
    v_src, v_col = (proj3, COL_RV) if v3 is None else (v3, 0)
    inputs = [proj3, proj3, v_src, proj3, cos, sin, decay, s0]
    st_shape, st_spec = _seq_state_out(stack, bsz, (R_HEADS, R_KDIM, R_VDIM))
    body, x_specs, x_in, aliases = _alias_prev(functools.partial(_ret_seq_kernel, lead, t), len(inputs),
                                               stack and stack[1], (1,))
    return pl.pallas_call(
        body,
        grid=(bsz, nblk),
        in_specs=[pl.BlockSpec((None, t, D_MODEL), lambda b, c: (b, row_blk0 + c, COL_RQ)),
                  pl.BlockSpec((None, t, D_MODEL), lambda b, c: (b, row_blk0 + c, COL_RK)),
                  pl.BlockSpec((None, t, wide), lambda b, c: (b, row_blk0 + c, v_col)),
                  pl.BlockSpec((None, t, wide), lambda b, c: (b, row_blk0 + c, COL_RG)),
                  pl.BlockSpec((t, R_HALF), lambda b, c: (c, 0)),
                  pl.BlockSpec((t, R_HALF), lambda b, c: (c, 0)),
                  pl.BlockSpec(decay.shape, lambda b, c: (0, 0, 0)),
                  bcast(s0)] + x_specs,
        out_specs=[pl.BlockSpec((None, t, wide), lambda b, c: (b, c, 0)), st_spec],
        out_shape=[jax.ShapeDtypeStruct((bsz, nblk * t, wide), out_dtype), st_shape],
        scratch_shapes=[pltpu.VMEM((R_HEADS, R_KDIM, R_VDIM), F32)],
        input_output_aliases=aliases,
        compiler_params=_cparams(("parallel", "arbitrary")),
        name="ret_seq",
    )(*inputs, *x_in)


def _col_pieces(x):
    return jnp.concatenate(_split3(x.T), axis=1)


def _pick_col(n):
    r = _iota((3 * LANES, LANES), 0) & (LANES - 1)
    return jnp.where(r == n, 1.0, 0.0).astype(BF16)


def _pick_col_pair(n):
    r = _iota((3 * LANES, 2 * LANES), 0) & (LANES - 1)
    want = n + jnp.right_shift(_iota((3 * LANES, 2 * LANES), 1), int(math.log2(LANES)))
    return jnp.where(r == want, 1.0, 0.0).astype(BF16)


def _step_prep_kernel(lb_is_zero, xs_ref, bc_ref, x_ref, wdt_ref, hq_ref, hf_ref, rq_ref, rk_ref, conv_ref,
                      cw_ref, cb_ref, dtb_ref, a_ref, ex_ref, lb_ref, cos_ref, sin_ref,
                      xs_o, bc_o, xdt_o, edec_o, conv_o, hq_o, hef_o, hk_o, rq_o, rk_o):
    raw = jnp.concatenate([xs_ref[...], bc_ref[...]], axis=1)
    buf = conv_ref[...]
    acc = cb_ref[...] + cw_ref[3:4, :] * raw
    for k in range(M_CONV - 1):
        acc += cw_ref[k:k + 1, :] * buf[:, k * M_CONV_DIM:(k + 1) * M_CONV_DIM]
    conv_o[:, 0:2 * M_CONV_DIM] = buf[:, M_CONV_DIM:]
    conv_o[:, 2 * M_CONV_DIM:] = raw
    xbc = _silu(acc)
    xs = xbc[:, :M_INNER]
    xs_o[...] = xs
    bc_o[...] = xbc[:, M_INNER:]
    dt = _softplus(_dot_nt(x_ref[...].astype(BF16), wdt_ref[...]) + dtb_ref[...])
    ex = ex_ref[...]
    xdt_o[...] = _col_pieces(xs * _sel_right(dt, ex))
    edec_o[...] = _col_pieces(_sel_right(jnp.exp(dt * a_ref[...]), ex))
    logf, k = _hgrn_gates(hf_ref[...], lb_ref[...], lb_is_zero)
    hq_o[...] = hq_ref[...] * (H_KDIM ** -0.5)
    hef_o[...] = _col_pieces(jnp.exp(logf))
    hk_o[...] = _col_pieces(k)
    cos, sin = cos_ref[...], sin_ref[...]
    rq = jnp.concatenate([_rotary(rq_ref[:, h * R_KDIM:(h + 1) * R_KDIM], cos, sin) for h in range(R_HEADS)], axis=1)
    rk = jnp.concatenate([_rotary(rk_ref[:, h * R_KDIM:(h + 1) * R_KDIM], cos, sin) for h in range(R_HEADS)], axis=1)
    rq_o[...] = _col_pieces(rq)
    rk_o[...] = _col_pieces(rk * (R_KDIM ** -0.5))


def _step_prep(proj_s, x_s, wt, layer, conv_flat, p, cos, sin):
    nb = conv_flat.shape[0]
    col = lambda cblk: pl.BlockSpec((nb, D_MODEL), lambda i, cblk=cblk: (0, cblk))
    full = lambda a: pl.BlockSpec(a.shape, lambda i: (0,) * a.ndim)
    params = [p["conv_w"], p["conv_b"], p["dt_bias"], p["a_neg"], p["expand"], p["lb"], cos, sin]
    assert nb == LANES
    rows = lambda w: jax.ShapeDtypeStruct((nb, w), F32)
    cols = lambda w: jax.ShapeDtypeStruct((w, 3 * nb), BF16)
    shapes = [rows(M_INNER), rows(M_BC), cols(M_INNER), cols(M_INNER), rows((M_CONV - 1) * M_CONV_DIM),
              rows(D_MODEL), cols(D_MODEL), cols(D_MODEL), cols(D_MODEL), cols(D_MODEL)]
    return pl.pallas_call(
        functools.partial(_step_prep_kernel, layer == 0),
        grid=(1,),
        in_specs=[col(COL_XS), col(COL_BC), pl.BlockSpec((nb, D_MODEL), lambda i: (0, 0)), _dt_weight_spec(layer),
                  col(COL_HQ), col(COL_HF), col(COL_RQ), col(COL_RK), full(conv_flat)]
                 + [full(a) for a in params],
        out_specs=[pl.BlockSpec(s.shape, lambda i: (0, 0)) for s in shapes],
        out_shape=shapes,
        compiler_params=_cparams(("arbitrary",)),
        name="step_prep",
    )(proj_s, proj_s, x_s, wt, proj_s, proj_s, proj_s, proj_s, conv_flat, *params)


def _ssd_step_kernel(s_ref, xdt_ref, edec_ref, bc_ref, xs_ref, z_ref, dsk_ref, nw_ref, so_ref, y_ref,
                     yt_ref, xdt_b, edec_b):
    yt_ref[...] = jnp.zeros(yt_ref.shape, F32)
    for i in range(STEP_BT):
        ls = slice((i % 2) * LANES, (i % 2 + 1) * LANES)
        if i % 2 == 0:
            pick = _pick_col_pair(pl.program_id(0) * STEP_BT + i)
            xdt_b[...] = _dot(xdt_ref[...], pick)
            edec_b[...] = _dot(edec_ref[...], pick)
        for g in range(M_GROUPS):
            gs = slice(g * M_GW, (g + 1) * M_GW)
            hs = slice(M_HPG * g, M_HPG * (g + 1))
            st = s_ref[i, hs].reshape(M_GW, M_STATE)
            brow = bc_ref[i:i + 1, g * M_STATE:(g + 1) * M_STATE]
            crow = bc_ref[i:i + 1, M_BC // 2 + g * M_STATE:M_BC // 2 + (g + 1) * M_STATE]
            new = st * edec_b[gs, ls] + xdt_b[gs, ls] * brow
            so_ref[i, hs] = new.reshape(M_HPG, M_HEADDIM, M_STATE)
            yt_ref[gs, i:i + 1] = jnp.sum(new * crow, axis=-1, keepdims=True)
    y = yt_ref[...].T[0:STEP_BT, :]
    xs = xs_ref[...]
    y = (y + dsk_ref[...] * xs) * _silu(z_ref[...])
    for g in range(M_GROUPS):
        gs = slice(g * M_GW, (g + 1) * M_GW)
        y_ref[:, gs] = _rms(y[:, gs]) * nw_ref[:, gs]


def _ssd_step(state, layer, prev, xdt_c, edec_c, bc, xs, proj_s, p):
    nb = xs.shape[0]
    bt = STEP_BT
    full = lambda a: pl.BlockSpec(a.shape, lambda j: (0,) * a.ndim)
    sspec = pl.BlockSpec((None, bt, M_HEADS, M_HEADDIM, M_STATE), lambda j: (layer, j, 0, 0, 0))
    inputs = [state, xdt_c, edec_c, bc, xs, proj_s, p["d_skip"], p["m_norm_w"]]
    body, x_specs, x_in, aliases = _alias_prev(_ssd_step_kernel, len(inputs),
                                               None if prev is None else [prev], (0,))
    return pl.pallas_call(
        body,
        grid=(nb // bt,),
        in_specs=[sspec, full(xdt_c), full(edec_c),
                  pl.BlockSpec((bt, M_BC), lambda j: (j, 0)),
                  pl.BlockSpec((bt, M_INNER), lambda j: (j, 0)),
                  pl.BlockSpec((bt, D_MODEL), lambda j: (j, COL_Z)),
                  full(p["d_skip"]), full(p["m_norm_w"])] + x_specs,
        out_specs=[sspec, pl.BlockSpec((bt, M_INNER), lambda j: (j, 0))],
        out_shape=[jax.ShapeDtypeStruct(state.shape, F32),
                   jax.ShapeDtypeStruct((nb, M_INNER), F32)],
        scratch_shapes=[pltpu.VMEM((M_INNER, LANES), F32)] + [pltpu.VMEM((M_INNER, 2 * LANES), F32)] * 2,
        input_output_aliases=aliases,
        compiler_params=_cparams(("parallel",)),
        name="ssd_step",
    )(*inputs, *x_in)


def _cols(x):
    nb, w = x.shape
    return jnp.transpose(x.reshape(nb // STEP_BT, STEP_BT, w), (0, 2, 1))


def _hgrn_step_kernel(s_ref, q_ref, ef_ref, k_ref, v_ref, g_ref, nw_ref, so_ref, y_ref, ef_b, k_b):
    for i in range(STEP_BT):
        ls = slice((i % 2) * LANES, (i % 2 + 1) * LANES)
        if i % 2 == 0:
            pick = _pick_col_pair(pl.program_id(0) * STEP_BT + i)
            ef_b[...] = _dot(ef_ref[...], pick)
            k_b[...] = _dot(k_ref[...], pick)
        for h in range(H_HEADS):
            cs = slice(h * H_KDIM, (h + 1) * H_KDIM)
            new = s_ref[i, h] * ef_b[cs, ls] + k_b[cs, ls] * v_ref[i:i + 1, cs]
            so_ref[i, h] = new
            y_ref[i:i + 1, cs] = jnp.sum(new * q_ref[cs, i:i + 1], axis=0, keepdims=True)
    for h in range(H_HEADS):
        cs = slice(h * H_KDIM, (h + 1) * H_KDIM)
        y_ref[:, cs] = _rms(y_ref[:, cs]) * nw_ref[:, cs] * _sigmoid(g_ref[:, cs])


def _hgrn_step(state, layer, prev, q_c, ef_c, k_c, proj_s, nw):
    nb = state.shape[1]
    bt = STEP_BT
    cspec = pl.BlockSpec(ef_c.shape, lambda j: (0, 0))
    qspec = pl.BlockSpec((None, D_MODEL, bt), lambda j: (j, 0, 0))
    sspec = pl.BlockSpec((None, bt, H_HEADS, H_KDIM, H_VDIM), lambda j: (layer, j, 0, 0, 0))
    inputs = [state, q_c, ef_c, k_c, proj_s, proj_s, nw]
    body, x_specs, x_in, aliases = _alias_prev(_hgrn_step_kernel, len(inputs),
                                               None if prev is None else [prev], (0,))
    return pl.pallas_call(
        body,
        grid=(nb // bt,),
        in_specs=[sspec, qspec, cspec, cspec,
                  pl.BlockSpec((bt, D_MODEL), lambda j: (j, COL_HI)),
                  pl.BlockSpec((bt, D_MODEL), lambda j: (j, COL_HG)),
                  pl.BlockSpec(nw.shape, lambda j: (0, 0))] + x_specs,
        out_specs=[sspec, pl.BlockSpec((bt, D_MODEL), lambda j: (j, 0))],
        out_shape=[jax.ShapeDtypeStruct(state.shape, F32),
                   jax.ShapeDtypeStruct((nb, D_MODEL), F32)],
        scratch_shapes=[pltpu.VMEM((D_MODEL, 2 * LANES), F32)] * 2,
        input_output_aliases=aliases,
        compiler_params=_cparams(("parallel",)),
        name="hgrn_step",
    )(*inputs, *x_in)


def _ret_step_kernel(*refs):
    _ret_token_update(pl.program_id(0), *refs)


def _ret_step(state, layer, prev, q_c, k_c, v3, g3, gam):
    nb = state.shape[1]
    full = lambda a: pl.BlockSpec(a.shape, lambda n: (0,) * a.ndim)
    seq_row = lambda a: pl.BlockSpec((None,) + a.shape[1:], lambda n: (n, 0, 0))
    sspec = pl.BlockSpec((None, None, R_HEADS, R_KDIM, R_VDIM), lambda n: (layer, n, 0, 0, 0))
    inputs = [state, q_c, k_c, v3, g3, gam]
    body, x_specs, x_in, aliases = _alias_prev(_ret_step_kernel, len(inputs),
                                               None if prev is None else [prev], (0,))
    return pl.pallas_call(
        body,
        grid=(nb,),
        in_specs=[sspec, full(q_c), full(k_c), seq_row(v3), seq_row(g3), full(gam)] + x_specs,
        out_specs=[sspec, seq_row(v3)],
        out_shape=[jax.ShapeDtypeStruct(state.shape, F32), jax.ShapeDtypeStruct(v3.shape, F32)],
        input_output_aliases=aliases,
        compiler_params=_cparams(("parallel",)),
        name="ret_step",
    )(*inputs, *x_in)


def _rope_tables(positions):
    inv_freq = 1.0 / (ROPE_BASE ** jnp.linspace(0.0, 1.0, R_HALF, dtype=F32))
    ang = positions[:, None] * inv_freq[None, :]
    return jnp.cos(ang), jnp.sin(ang)


def _per_channel(v):
    return jnp.repeat(v.astype(F32), M_HEADDIM).reshape(1, M_INNER)


def _pad_lanes(v):
    return jnp.pad(v.astype(F32), (0, LANES - v.shape[0])).reshape(1, LANES)


def kernel(x_prompt, x_sample, state_ssm, state_conv, state_hgrn, state_ret, meta_tokens, ln_in_g, ln_in_b,
           w_in, conv_w, conv_b, dt_bias, a_log, d_skip, m_norm_w, hgrn_lb_logits, h_norm_w, w_br_m, w_br_h,
           w_br_r, w_out, ln1_g, ln1_b, w_ffn_in, w_ffn_out, ln2_g, ln2_b):
    bp, sp = x_prompt.shape[0], x_prompt.shape[1]
    nb = x_sample.shape[0]
    assert x_sample.shape[1] == 1 and nb == SMALL_ROWS - SEQ_BLOCK and nb % STEP_BT == 0
    assert sp % RET_BLOCK == 0 and meta_tokens.shape[0] == N_META

    wt = jnp.swapaxes(w_in, 1, 2).astype(BF16)
    wm_b, wh_b, wr_b, wo_b = (w.astype(BF16) for w in (w_br_m, w_br_h, w_br_r, w_out))
    wfi_b, wfo_b = w_ffn_in.astype(BF16), w_ffn_out.astype(BF16)
    ln1 = (ln1_g.reshape(DEPTH, 1, D_MODEL), ln1_b.reshape(DEPTH, 1, D_MODEL))
    ln2 = (ln2_g.reshape(DEPTH, 1, D_MODEL), ln2_b.reshape(DEPTH, 1, D_MODEL))
    lb_cum = jnp.cumsum(jax.nn.softmax(hgrn_lb_logits.astype(F32), axis=0), axis=0)
    lbs = lb_cum - lb_cum[0]
    expand = (np.arange(LANES)[:, None] == (np.arange(M_INNER)[None, :] // M_HEADDIM)).astype(np.float32)
    expand = jnp.asarray(np.concatenate([expand] * 3, 0), BF16)
    h_sums, h_pair = _hgrn_tables()
    h_tables = (jnp.asarray(h_sums, BF16), jnp.asarray(h_pair, F32))
    gam = jnp.asarray(np.broadcast_to(
        np.array([1.0 - 2.0 ** (-5.0 - h) for h in range(R_HEADS)], np.float32)[:, None, None],
        (R_HEADS, 1, R_VDIM)))

    pos_real = jnp.arange(N_META, N_META + sp, dtype=F32)
    pos_meta = jnp.maximum(jnp.arange(SEQ_BLOCK, dtype=F32) - META_LEAD, 0.0)
    pos_samp = jnp.full((nb,), float(PAST_LEN), F32)
    cos_r, sin_r = _rope_tables(pos_real)
    cos_m, sin_m = _rope_tables(pos_meta)
    cos_s, sin_s = _rope_tables(pos_samp)

    x_real = _layer_norm_rows(x_prompt.reshape(bp * sp, D_MODEL), ln_in_g, ln_in_b, 512)
    small_in = jnp.concatenate([x_sample.reshape(nb, D_MODEL),
                                jnp.zeros((META_LEAD, D_MODEL), F32), meta_tokens.astype(F32)], axis=0)
    x_small = _layer_norm_rows(small_in, ln_in_g, ln_in_b, SMALL_ROWS)

    z_ssm = jnp.zeros((1, M_HEADS, M_HEADDIM, M_STATE), F32)
    z_conv = jnp.zeros((1, 8, M_CONV_DIM), F32)
    z_hgrn = jnp.zeros((1, H_HEADS, H_KDIM, H_VDIM), F32)
    z_ret = jnp.zeros((1, R_HEADS, R_KDIM, R_VDIM), F32)
    ssm_p = conv_p = hgrn_p = ret_p = ssm_s = hgrn_s = ret_s = None
    conv_s = []
    for l in range(DEPTH):
        p = dict(conv_w=conv_w[l], conv_b=conv_b[l].reshape(1, -1), dt_bias=_pad_lanes(dt_bias[l]),
                 a_neg=_pad_lanes(-jnp.exp(a_log[l].astype(F32))), d_skip=_per_channel(d_skip[l]),
                 m_norm_w=m_norm_w[l].reshape(1, -1), expand=expand, lb=lbs[l].reshape(1, -1))
        hnw = h_norm_w[l].reshape(1, -1)

        proj_s = _proj(x_small, wt, l, SMALL_ROWS, "proj_small")
        proj_s3, x_small3 = proj_s.reshape(1, SMALL_ROWS, PROJ_COLS), x_small.reshape(1, SMALL_ROWS, D_MODEL)
        conv_flat = state_conv[l].reshape(nb, (M_CONV - 1) * M_CONV_DIM)
        (xs_s, bc_s, xdt_s, edec_s, conv_new, hq_s, hef_s, hk_s, rq_s, rk_s) = _step_prep(
            proj_s, x_small, wt, l, conv_flat, p, cos_s, sin_s)
        conv_s.append(conv_new.reshape(nb, M_CONV - 1, M_CONV_DIM))
        wide = R_HEADS * R_VDIM
        rv3 = proj_s[:nb, 0:wide].reshape(nb, 1, wide)
        rg3 = proj_s[:nb, wide:2 * wide].reshape(nb, 1, wide)

        if _proj_steps(bp * sp, PROJ_TM) == nb:
            proj_r, rv_r, ret_s, yr_s = _proj(x_real, wt, l, PROJ_TM, "proj_real",
                                              ride=(state_ret, ret_s, rq_s, rk_s, rv3, rg3, gam))
            rv_r3 = rv_r.reshape(bp, sp, wide)
        else:
            proj_r, rv_r3 = _proj(x_real, wt, l, PROJ_TM, "proj_real"), None
            ret_s, yr_s = _ret_step(state_ret, l, ret_s, rq_s, rk_s, rv3, rg3, gam)
        yr_s = yr_s.reshape(nb, wide)
        proj_r3, x_real3 = proj_r.reshape(bp, sp, PROJ_COLS), x_real.reshape(bp, sp, D_MODEL)

        ssm_s, ym_s = _ssd_step(state_ssm, l, ssm_s, xdt_s, edec_s, bc_s, xs_s, proj_s, p)
        hgrn_s, yh_s = _hgrn_step(state_hgrn, l, hgrn_s, _cols(hq_s), hef_s, hk_s, proj_s, hnw)

        ym_m, ssm_m, conv_m = _ssd_seq(proj_s3, x_small3, wt, l, 1, 1, META_LEAD, p, z_ssm, z_conv, F32)
        yh_m, hgrn_m = _hgrn_seq(proj_s3, 1, 1, META_LEAD, p["lb"], l == 0, hnw, h_tables, z_hgrn, F32)
        yr_m, ret_m = _ret_seq(proj_s3, SEQ_BLOCK, 1, 1, META_LEAD, cos_m, sin_m, z_ret, F32)

        conv0 = jnp.pad(conv_m, ((0, 0), (8 - (M_CONV - 1), 0), (0, 0)))
        ym_r, ssm_p, conv_p = _ssd_seq(proj_r3, x_real3, wt, l, 0, sp // SEQ_BLOCK, 0, p, ssm_m, conv0, BF16,
                                       stack=(l, None if l == 0 else [ssm_p, conv_p]))
        yh_r, hgrn_p = _hgrn_seq(proj_r3, 0, sp // SEQ_BLOCK, 0, p["lb"], l == 0, hnw, h_tables, hgrn_m, BF16,
                                 stack=(l, None if l == 0 else [hgrn_p]))
        yr_r, ret_p = _ret_seq(proj_r3, RET_BLOCK, 0, sp // RET_BLOCK, 0, cos_r, sin_r, ret_m, BF16,
                               stack=(l, None if l == 0 else [ret_p]), v3=rv_r3)

        ym_small = jnp.concatenate([ym_s, ym_m[0]], axis=0)
        yh_small = jnp.concatenate([yh_s, yh_m[0]], axis=0)
        yr_small = jnp.concatenate([yr_s, yr_m[0]], axis=0)
        x_real = _mix(x_real, ym_r.reshape(bp * sp, -1), yh_r.reshape(bp * sp, -1), yr_r.reshape(bp * sp, -1),
                      proj_r, l, wm_b, wh_b, wr_b, wo_b, *ln1, 512)
        x_small = _mix(x_small, ym_small, yh_small, yr_small, proj_s, l, wm_b, wh_b, wr_b, wo_b, *ln1, SMALL_ROWS)
        x_real = _ffn(x_real, l, wfi_b, wfo_b, *ln2, 512)
        x_small = _ffn(x_small, l, wfi_b, wfo_b, *ln2, SMALL_ROWS)

    return (x_real.reshape(bp, sp, D_MODEL), x_small[:nb].reshape(nb, 1, D_MODEL),
            ssm_p, conv_p, hgrn_p, ret_p, ssm_s, jnp.stack(conv_s), hgrn_s, ret_s)
```

```python
import functools
import math

import numpy as np
import jax
import jax.numpy as jnp
from jax import lax
from jax.experimental import pallas as pl
from jax.experimental.pallas import tpu as pltpu

F32 = jnp.float32
BF16 = jnp.bfloat16

D_MODEL = 1024
DEPTH = 2
N_META = 16
M_INNER = D_MODEL
M_HEADDIM = 64
M_HEADS = M_INNER // M_HEADDIM
M_GROUPS = 4
M_HPG = M_HEADS // M_GROUPS
M_STATE = 128
M_CONV = 4
M_BC = 2 * M_GROUPS * M_STATE
M_CONV_DIM = M_INNER + M_BC
M_GW = M_INNER // M_GROUPS
H_KDIM = 128
H_HEADS = D_MODEL // H_KDIM
H_VDIM = 128
R_HEADS = 4
R_KDIM = D_MODEL // R_HEADS
R_VDIM = 2 * R_KDIM
R_HALF = R_KDIM // 2
ROPE_BASE = 10000.0
D_FF = ((8 * D_MODEL // 3 + 255) // 256) * 256
FF_CHUNK = 256
DN_ALPHA = (2 * DEPTH) ** 0.25
PAST_LEN = 16384

LANES = 128
SEQ_BLOCK = 128
RET_BLOCK = 256
SMALL_ROWS = 256
META_LEAD = SEQ_BLOCK - N_META
STEP_BT = 8
VMEM_BYTES = 64 * 1024 * 1024
VMEM_LIMIT = VMEM_BYTES - 4 * 1024 * 1024

COL_RV, COL_RG = 0, 1
COL_Z, COL_XS, COL_BC, COL_HQ, COL_HF, COL_HI, COL_HG, COL_RQ, COL_RK, COL_GM, COL_GH, COL_GR = range(4, 16)
PROJ_COLS = 16 * 1024

_ORIG_SPLITS = (M_INNER, M_CONV_DIM, M_HEADS, 1024, 1024, 1024, 1024, 1024, 1024, 2048, 2048, 3072)
_ORIG_OFF = np.concatenate([[0], np.cumsum(_ORIG_SPLITS)]).tolist()


def _cparams(sem):
    return pltpu.CompilerParams(dimension_semantics=sem, vmem_limit_bytes=VMEM_LIMIT)


def _sigmoid(x):
    return 1.0 / (1.0 + jnp.exp(-x))


def _silu(x):
    return x * _sigmoid(x)


def _softplus(x):
    return jnp.maximum(x, 0.0) + jnp.log1p(jnp.exp(-jnp.abs(x)))


def _layer_norm(x, g, b):
    mu = jnp.mean(x, axis=-1, keepdims=True)
    xc = x - mu
    var = jnp.mean(xc * xc, axis=-1, keepdims=True)
    return xc * lax.rsqrt(var + 1e-5) * g + b


def _rms(x):
    return x * lax.rsqrt(jnp.mean(x * x, axis=-1, keepdims=True) + 1e-6)


def _split3(x):
    hi = x.astype(BF16)
    r = x - hi.astype(F32)
    mid = r.astype(BF16)
    lo = (r - mid.astype(F32)).astype(BF16)
    return hi, mid, lo


def _dot(a, b):
    return jnp.dot(a, b, preferred_element_type=F32)


def _dot_nt(a, b):
    return lax.dot_general(a, b, (((1,), (1,)), ((), ())), preferred_element_type=F32)


def _sel_right(x, m3):
    return _dot(jnp.concatenate(_split3(x), axis=1), m3)


def _sel_left(m3, x):
    return _dot(m3, jnp.concatenate(_split3(x), axis=0))


def _iota(shape, dim):
    return lax.broadcasted_iota(jnp.int32, shape, dim)


def _ln_kernel(x_ref, g_ref, b_ref, o_ref):
    o_ref[...] = _layer_norm(x_ref[...], g_ref[...], b_ref[...])


def _layer_norm_rows(x, g, b, tm):
    m = x.shape[0]
    tm = min(tm, m)
    return pl.pallas_call(
        _ln_kernel,
        grid=(m // tm,),
        in_specs=[pl.BlockSpec((tm, D_MODEL), lambda i: (i, 0)),
                  pl.BlockSpec((1, D_MODEL), lambda i: (0, 0)),
                  pl.BlockSpec((1, D_MODEL), lambda i: (0, 0))],
        out_specs=pl.BlockSpec((tm, D_MODEL), lambda i: (i, 0)),
        out_shape=jax.ShapeDtypeStruct((m, D_MODEL), F32),
        compiler_params=_cparams(("parallel",)),
        name="ln_in",
    )(x, g.reshape(1, -1), b.reshape(1, -1))


def _proj_kernel(ride, x_ref, w_ref, *refs):
    if ride:
        ride_in, (o_ref, so_ref, ry_ref, xb_ref) = refs[:6], refs[6:]
    else:
        o_ref, xb_ref = refs

    @pl.when(pl.program_id(1) == 0)
    def _():
        xb_ref[...] = x_ref[...].astype(BF16)

    o_ref[...] = _dot_nt(xb_ref[...], w_ref[0])
    if ride:
        _ret_token_update(pl.program_id(0) * pl.num_programs(1) + pl.program_id(1), *ride_in, so_ref, ry_ref)


def _layer_block(a, layer):
    return pl.BlockSpec((None,) + a.shape[1:], lambda *_: (layer,) + (0,) * (a.ndim - 1),
                        pipeline_mode=pl.Buffered(1))


_W_RUNS = ((0, _ORIG_OFF[9]), (4096, _ORIG_OFF[0]), (7168, _ORIG_OFF[3]), (13312, _ORIG_OFF[11]))
PROJ_TN = 1024
PROJ_TM = 2048


def _orig_col(j):
    c = j * PROJ_TN
    off = c - _W_RUNS[0][0] + _W_RUNS[0][1]
    for new0, orig0 in _W_RUNS[1:]:
        off = jnp.where(c >= new0, c - new0 + orig0, off)
    return pl.multiple_of(off, M_HEADS)


def _proj_steps(m, tm):
    return (m // min(tm, m)) * (PROJ_COLS // PROJ_TN)


def _proj(x, wt, layer, tm, name, ride=None):
    m = x.shape[0]
    tm = min(tm, m)
    nj = PROJ_COLS // PROJ_TN
    inputs = [x, wt]
    in_specs = [pl.BlockSpec((tm, D_MODEL), lambda i, j: (i, 0)),
                pl.BlockSpec((pl.Element(1), pl.Element(PROJ_TN), pl.Element(D_MODEL)),
                             lambda i, j: (layer, _orig_col(j), 0))]
    out_specs = [pl.BlockSpec((tm, PROJ_TN), lambda i, j: (i, j))]
    out_shape = [jax.ShapeDtypeStruct((m, PROJ_COLS), F32)]
    prev = None
    if ride:
        state, prev, q_c, k_c, v3, g3, gam = ride
        assert _proj_steps(m, tm) == state.shape[1]
        full = lambda a: pl.BlockSpec(a.shape, lambda i, j: (0,) * a.ndim)
        seq_row = lambda a: pl.BlockSpec((None,) + a.shape[1:], lambda i, j: (i * nj + j, 0, 0))
        sspec = pl.BlockSpec((None, None, R_HEADS, R_KDIM, R_VDIM), lambda i, j: (layer, i * nj + j, 0, 0, 0))
        inputs += [state, q_c, k_c, v3, g3, gam]
        in_specs[0] = pl.BlockSpec((tm, D_MODEL), lambda i, j: (i, 0), pipeline_mode=pl.Buffered(1))
        in_specs += [sspec, full(q_c), full(k_c), seq_row(v3), seq_row(g3), full(gam)]
        out_specs += [sspec, seq_row(v3)]
        out_shape += [jax.ShapeDtypeStruct(state.shape, F32), jax.ShapeDtypeStruct(v3.shape, F32)]
    body, x_specs, x_in, aliases = _alias_prev(functools.partial(_proj_kernel, bool(ride)), len(inputs),
                                               None if prev is None else [prev], (1,))
    res = pl.pallas_call(
        body,
        grid=(m // tm, nj),
        in_specs=in_specs + x_specs,
        out_specs=out_specs,
        out_shape=out_shape,
        scratch_shapes=[pltpu.VMEM((tm, D_MODEL), BF16)],
        input_output_aliases=aliases,
        compiler_params=_cparams(("parallel", "arbitrary")),
        name=name,
    )(*inputs, *x_in)
    return res if ride else res[0]


def _dt_weight_spec(layer):
    assert _ORIG_OFF[2] % LANES == 0
    return pl.BlockSpec((None, LANES, D_MODEL), lambda *_: (layer, _ORIG_OFF[2] // LANES, 0))


def _mix_kernel(x_ref, ym_ref, yh_ref, yr_ref, gm_ref, gh_ref, gr_ref,
                wm_ref, wh_ref, wr_ref, wo_ref, g_ref, b_ref, o_ref):
    mixed = _sigmoid(gm_ref[...]) * _dot(ym_ref[...].astype(BF16), wm_ref[...])
    mixed += _sigmoid(gh_ref[...]) * _dot(yh_ref[...].astype(BF16), wh_ref[...])
    mixed += _sigmoid(gr_ref[...]) * _dot(yr_ref[...].astype(BF16), wr_ref[...])
    h = _dot(mixed.astype(BF16), wo_ref[...])
    o_ref[...] = _layer_norm(DN_ALPHA * x_ref[...] + h, g_ref[...], b_ref[...])


def _mix(x, ym, yh, yr, proj, layer, wm, wh, wr, wo, g2, b2, tm):
    m = x.shape[0]
    tm = min(tm, m)
    row = lambda w: pl.BlockSpec((tm, w), lambda i: (i, 0))
    col = lambda c: pl.BlockSpec((tm, D_MODEL), lambda i, c=c: (i, c))
    full = lambda a: _layer_block(a, layer)
    return pl.pallas_call(
        _mix_kernel,
        grid=(m // tm,),
        in_specs=[row(D_MODEL), row(M_INNER), row(D_MODEL), row(R_HEADS * R_VDIM),
                  col(COL_GM), col(COL_GH), col(COL_GR),
                  full(wm), full(wh), full(wr), full(wo), full(g2), full(b2)],
        out_specs=row(D_MODEL),
        out_shape=jax.ShapeDtypeStruct((m, D_MODEL), F32),
        compiler_params=_cparams(("parallel",)),
        name="mix",
    )(x, ym, yh, yr, proj, proj, proj, wm, wh, wr, wo, g2, b2)


def _ffn_kernel(x_ref, wi_ref, wo_ref, g_ref, b_ref, o_ref):
    x = x_ref[...]
    xb = x.astype(BF16)
    acc = jnp.zeros(x.shape, F32)
    for j in range(0, D_FF, FF_CHUNK):
        hg = _dot(xb, wi_ref[:, j:j + FF_CHUNK])
        hu = _dot(xb, wi_ref[:, D_FF + j:D_FF + j + FF_CHUNK])
        acc += _dot((_silu(hg) * hu).astype(BF16), wo_ref[j:j + FF_CHUNK, :])
    o_ref[...] = _layer_norm(DN_ALPHA * x + acc, g_ref[...], b_ref[...])


def _ffn(x, layer, wi, wo, g2, b2, tm):
    m = x.shape[0]
    tm = min(tm, m)
    full = lambda a: _layer_block(a, layer)
    return pl.pallas_call(
        _ffn_kernel,
        grid=(m // tm,),
        in_specs=[pl.BlockSpec((tm, D_MODEL), lambda i: (i, 0)), full(wi), full(wo), full(g2), full(b2)],
        out_specs=pl.BlockSpec((tm, D_MODEL), lambda i: (i, 0)),
        out_shape=jax.ShapeDtypeStruct((m, D_MODEL), F32),
        compiler_params=_cparams(("parallel",)),
        name="ffn",
    )(x, wi, wo, g2, b2)


def _ssd_seq_kernel(lead, z_ref, xs_ref, bc_ref, x_ref, wdt_ref, cw_ref, cb_ref, dtb_ref, a_ref, dsk_ref, nw_ref,
                    ex_ref, s0_ref, c0_ref, y_ref, sout_ref, cout_ref, ext_ref, st_ref, xbc_ref):
    t = SEQ_BLOCK
    c = pl.program_id(1)

    @pl.when(c == 0)
    def _():
        ext_ref[0:8, :] = c0_ref[...]
        for g in range(M_GROUPS):
            st_ref[g] = s0_ref[M_HPG * g:M_HPG * (g + 1)].reshape(M_GW, M_STATE).T

    rows = _iota((t, 1), 0)

    def conv_chunk(q):
        w = M_CONV_DIM // 4
        cs = slice(q * w, (q + 1) * w)
        src = xs_ref if q < 2 else bc_ref
        raw = src[:, (q % 2) * w:(q % 2 + 1) * w]
        if lead:
            raw = jnp.where(rows >= lead, raw, 0.0)
        ext_ref[8:8 + t, cs] = raw
        acc = cb_ref[:, cs] + cw_ref[3:4, cs] * raw
        for k in range(M_CONV - 1):
            acc += cw_ref[k:k + 1, cs] * ext_ref[5 + k:5 + k + t, cs]
        ext_ref[5:8, cs] = ext_ref[t + 5:t + 8, cs]
        xbc_ref[:, cs] = _silu(acc)

    dt_raw = _dot_nt(x_ref[...].astype(BF16), wdt_ref[...])
    conv_chunk(0)
    dt = _softplus(dt_raw + dtb_ref[...])
    if lead:
        dt = jnp.where(rows >= lead, dt, 0.0)
    a = dt * a_ref[...]
    ti, si = _iota((t, t), 0), _iota((t, t), 1)
    tril = si <= ti
    tri = jnp.where(tril, 1.0, 0.0).astype(BF16)
    cum = _sel_left(jnp.concatenate([tri] * 3, axis=1), a)
    conv_chunk(1)
    ecum = jnp.exp(cum)
    tailw = jnp.exp(cum[t - 1:t, :] - cum) * dt
    cum_t, dt_t = cum.T, dt.T
    ex = ex_ref[...]
    ecum_full = _sel_right(ecum, ex)
    conv_chunk(2)
    tailw_full = _sel_right(tailw, ex)
    conv_chunk(3)

    xs = xbc_ref[:, 0:M_INNER]
    xw = (xs * tailw_full).astype(BF16)
    xs_b = xs.astype(BF16)
    head_of_lane = jnp.right_shift(_iota((1, M_GW), 1), int(math.log2(M_HEADDIM)))

    for g in range(M_GROUPS):
        gs = slice(g * M_GW, (g + 1) * M_GW)
        bm_f = xbc_ref[:, M_INNER + g * M_STATE:M_INNER + (g + 1) * M_STATE]
        bm = bm_f.astype(BF16)
        cm = xbc_ref[:, M_INNER + M_BC // 2 + g * M_STATE:M_INNER + M_BC // 2 + (g + 1) * M_STATE].astype(BF16)
        cb = _dot_nt(cm, bm)
        st = st_ref[g]
        y_g = _dot(cm, st.astype(BF16)) * ecum_full[:, gs]
        for hh in range(M_HPG):
            h = g * M_HPG + hh
            diff = cum[:, h:h + 1] - cum_t[h:h + 1, :]
            w = cb * jnp.exp(jnp.where(tril, diff, -1e30)) * dt_t[h:h + 1, :]
            x_h = jnp.where(head_of_lane == hh, xs_b[:, gs], jnp.zeros((), BF16))
            y_g = y_g + _dot(w.astype(BF16), x_h)
        st_ref[g] = st * ecum_full[t - 1:t, gs] + _dot(bm_f.T.astype(BF16), xw[:, gs])
        y_g = (y_g + dsk_ref[:, gs] * xs[:, gs]) * _silu(z_ref[:, gs])
        y_ref[:, gs] = (_rms(y_g) * nw_ref[:, gs]).astype(y_ref.dtype)

    @pl.when(c == pl.num_programs(1) - 1)
    def _():
        cout_ref[...] = ext_ref[5:8, :]
        for g in range(M_GROUPS):
            sout_ref[M_HPG * g:M_HPG * (g + 1)] = st_ref[g].T.reshape(M_HPG, M_HEADDIM, M_STATE)


def _alias_prev(body, n_in, prevs, out_ids):
    if not prevs:
        return body, [], [], {}
    k = len(prevs)
    wrapped = lambda *refs: body(*refs[:n_in], *refs[n_in + k:])
    return (wrapped, [pl.BlockSpec(memory_space=pl.ANY)] * k, list(prevs),
            {n_in + i: o for i, o in enumerate(out_ids)})


def _seq_state_out(stack, bsz, dims):
    zeros = (0,) * len(dims)
    if stack is None:
        return (jax.ShapeDtypeStruct((bsz,) + dims, F32),
                pl.BlockSpec((None,) + dims, lambda b, c: (b,) + zeros))
    layer = stack[0]
    return (jax.ShapeDtypeStruct((DEPTH, bsz) + dims, F32),
            pl.BlockSpec((None, None) + dims, lambda b, c: (layer, b) + zeros))


def _ssd_seq(proj3, x3, wt, layer, row_blk0, nblk, lead, p, s0, c0, out_dtype, stack=None):
    bsz = proj3.shape[0]
    t = SEQ_BLOCK
    colspec = lambda cblk: pl.BlockSpec((None, t, D_MODEL), lambda b, c, cblk=cblk: (b, row_blk0 + c, cblk))
    full = lambda a: pl.BlockSpec(a.shape, lambda b, c: (0,) * a.ndim)
    bcast = lambda a: pl.BlockSpec((None,) + a.shape[1:], lambda b, c: (0,) * a.ndim)
    params = [p["conv_w"], p["conv_b"], p["dt_bias"], p["a_neg"], p["d_skip"], p["m_norm_w"], p["expand"]]
    inputs = [proj3, proj3, proj3, x3, wt, *params, s0, c0]
    st_shape, st_spec = _seq_state_out(stack, bsz, (M_HEADS, M_HEADDIM, M_STATE))
    cv_shape, cv_spec = _seq_state_out(stack, bsz, (M_CONV - 1, M_CONV_DIM))
    body, x_specs, x_in, aliases = _alias_prev(functools.partial(_ssd_seq_kernel, lead), len(inputs),
                                               stack and stack[1], (1, 2))
    return pl.pallas_call(
        body,
        grid=(bsz, nblk),
        in_specs=[colspec(COL_Z), colspec(COL_XS), colspec(COL_BC),
                  pl.BlockSpec((None, t, D_MODEL), lambda b, c: (b, row_blk0 + c, 0)), _dt_weight_spec(layer)]
                 + [full(a) for a in params] + [bcast(s0), bcast(c0)] + x_specs,
        out_specs=[pl.BlockSpec((None, t, M_INNER), lambda b, c: (b, c, 0)), st_spec, cv_spec],
        out_shape=[jax.ShapeDtypeStruct((bsz, nblk * t, M_INNER), out_dtype), st_shape, cv_shape],
        scratch_shapes=[pltpu.VMEM((t + 8, M_CONV_DIM), F32),
                        pltpu.VMEM((M_GROUPS, M_STATE, M_GW), F32),
                        pltpu.VMEM((t, M_CONV_DIM), F32)],
        input_output_aliases=aliases,
        compiler_params=_cparams(("parallel", "arbitrary")),
        name="ssd_seq",
    )(*inputs, *x_in)


def _hgrn_gates(fz, lb, lb_is_zero):
    e = jnp.exp(-jnp.abs(fz))
    r = 1.0 / (1.0 + e)
    pos = fz >= 0.0
    sig_neg = jnp.where(pos, e * r, r)
    log_sig = jnp.minimum(fz, 0.0) - jnp.log(1.0 + e)
    if lb_is_zero:
        return log_sig, sig_neg
    sig_pos = jnp.where(pos, r, e * r)
    logf = jnp.where(lb > 0.0, jnp.log(lb + (1.0 - lb) * sig_pos), log_sig)
    return logf, (1.0 - lb) * sig_neg


H_LEVELS = int(math.log2(SEQ_BLOCK))


def _hgrn_tables():
    n = SEQ_BLOCK
    t = np.arange(n)[:, None]
    u = np.arange(n)[None, :]
    pair = []
    for b in range(H_LEVELS):
        bit = ((t >> b) & 1) == 1
        pair.append(((t >> (b + 1)) == (u >> (b + 1))) & bit & (((u >> b) & 1) == 0))
    m1 = ((t >> 1) | 1) << 1
    lvl1 = np.where(((t >> 1) & 1) == 1, (u >= m1) & (u <= t), (u > t) & (u < m1))
    sums = np.concatenate([u <= t, lvl1], 0).astype(np.float32)
    return np.concatenate([sums] * 3, 1), np.concatenate(pair, 0).astype(np.float32)


def _exp_neg_abs(d):
    return jnp.exp2(jnp.abs(d) * (-1.0 / math.log(2.0)))


def _ret_token_update(n, s_ref, q_ref, k_ref, v_ref, g_ref, gam_ref, so_ref, y_ref):
    pick = _pick_col(n)
    for h in range(R_HEADS):
        rows = slice(h * R_KDIM, (h + 1) * R_KDIM)
        q_b = _dot(q_ref[rows, :], pick)
        k_b = _dot(k_ref[rows, :], pick)
        outs = []
        for c0 in range(0, R_VDIM, LANES):
            cs = slice(c0, c0 + LANES)
            new = s_ref[h, :, cs] * gam_ref[h, :, cs] + k_b * v_ref[:, h * R_VDIM + c0:h * R_VDIM + c0 + LANES]
            so_ref[h, :, cs] = new
            outs.append(jnp.sum(new * q_b, axis=0, keepdims=True))
        vs = slice(h * R_VDIM, (h + 1) * R_VDIM)
        y_ref[:, vs] = _rms(jnp.concatenate(outs, axis=1)) * _silu(g_ref[:, vs])


def _hgrn_seq_kernel(lead, lb_is_zero, q_ref, f_ref, i_ref, g_ref, lb_ref, nw_ref, sums_ref, pair_ref, s0_ref,
                     y_ref, sout_ref, st_ref, ex_ref, q_s, k_s, z_ref, qd_ref, kd_ref, sc_ref):
    t = SEQ_BLOCK
    c = pl.program_id(1)

    @pl.when(c == 0)
    def _():
        for h in range(H_HEADS):
            st_ref[h] = s0_ref[h].T

    rows = _iota((t, 1), 0)
    logf, k = _hgrn_gates(f_ref[...], lb_ref[...], lb_is_zero)
    if lead:
        logf = jnp.where(rows >= lead, logf, 0.0)
        k = jnp.where(rows >= lead, k, 0.0)
    q = q_ref[...] * (H_KDIM ** -0.5)
    q_s[...] = q
    k_s[...] = k
    ex_ref[...] = _dot(sums_ref[...], jnp.concatenate(_split3(logf), axis=0))

    def side_of(b):
        return (jnp.right_shift(rows, b) & 1) == 1

    z_ref[0] = jnp.where(side_of(0), q * jnp.exp(logf), k).astype(BF16)
    z_ref[1] = (jnp.exp(ex_ref[t:2 * t, :]) * jnp.where(side_of(1), q_s[...], k_s[...])).astype(BF16)
    for b in range(2, H_LEVELS):
        half = 1 << b
        groups = range(0, t, 2 * half)
        d = jnp.concatenate([ex_ref[g0:g0 + 2 * half, :] - ex_ref[g0 + half - 1:g0 + half, :]
                             for g0 in groups], axis=0)
        if half >= 8:
            qk = jnp.concatenate([ref[g0 + o:g0 + o + half, :] for g0 in groups
                                  for ref, o in ((k_s, 0), (q_s, half))], axis=0)
        else:
            qk = jnp.where(side_of(b), q_s[...], k_s[...])
        z_ref[b] = (_exp_neg_abs(d) * qk).astype(BF16)
    cum = ex_ref[0:t, :]
    qd_ref[...] = (q_s[...] * jnp.exp(cum)).astype(BF16)
    kd_ref[...] = (k_s[...] * jnp.exp(ex_ref[t - 1:t, :] - cum)).astype(BF16)

    for h in range(H_HEADS):
        cs = slice(h * H_KDIM, (h + 1) * H_KDIM)
        scores = None
        for b in range(H_LEVELS):
            z = z_ref[b, :, cs]
            p = _dot_nt(z, z) * pair_ref[b * t:(b + 1) * t, :]
            scores = p if scores is None else scores + p
        sc_ref[h] = scores.astype(BF16)

    for h in range(H_HEADS):
        cs = slice(h * H_KDIM, (h + 1) * H_KDIM)
        v = i_ref[:, cs]
        st = st_ref[h]
        o = (_dot(sc_ref[h], v.astype(BF16))
             + jnp.sum(q_s[:, cs] * k_s[:, cs], axis=-1, keepdims=True) * v
             + _dot_nt(qd_ref[:, cs], st.astype(BF16)))
        y = _rms(o) * nw_ref[:, cs] * _sigmoid(g_ref[:, cs])
        y_ref[:, cs] = y.astype(y_ref.dtype)
        st_ref[h] = st * jnp.exp(ex_ref[t - 1:t, cs]) + _dot(v.T.astype(BF16), kd_ref[:, cs])

    @pl.when(c == pl.num_programs(1) - 1)
    def _():
        for h in range(H_HEADS):
            sout_ref[h] = st_ref[h].T


def _hgrn_seq(proj3, row_blk0, nblk, lead, lb, lb_is_zero, nw, tables, s0, out_dtype, stack=None):
    bsz = proj3.shape[0]
    t = SEQ_BLOCK
    colspec = lambda cblk: pl.BlockSpec((None, t, D_MODEL), lambda b, c, cblk=cblk: (b, row_blk0 + c, cblk))
    full = lambda a: pl.BlockSpec(a.shape, lambda b, c: (0,) * a.ndim)
    bcast = lambda a: pl.BlockSpec((None,) + a.shape[1:], lambda b, c: (0,) * a.ndim)
    inputs = [proj3, proj3, proj3, proj3, lb, nw, *tables, s0]
    st_shape, st_spec = _seq_state_out(stack, bsz, (H_HEADS, H_KDIM, H_VDIM))
    body, x_specs, x_in, aliases = _alias_prev(functools.partial(_hgrn_seq_kernel, lead, lb_is_zero), len(inputs),
                                               stack and stack[1], (1,))
    return pl.pallas_call(
        body,
        grid=(bsz, nblk),
        in_specs=[colspec(COL_HQ), colspec(COL_HF), colspec(COL_HI), colspec(COL_HG),
                  full(lb), full(nw), full(tables[0]), full(tables[1]), bcast(s0)] + x_specs,
        out_specs=[pl.BlockSpec((None, t, D_MODEL), lambda b, c: (b, c, 0)), st_spec],
        out_shape=[jax.ShapeDtypeStruct((bsz, nblk * t, D_MODEL), out_dtype), st_shape],
        input_output_aliases=aliases,
        scratch_shapes=[pltpu.VMEM((H_HEADS, H_VDIM, H_KDIM), F32),
                        pltpu.VMEM((2 * t, D_MODEL), F32),
                        pltpu.VMEM((t, D_MODEL), F32), pltpu.VMEM((t, D_MODEL), F32),
                        pltpu.VMEM((H_LEVELS, t, D_MODEL), BF16),
                        pltpu.VMEM((t, D_MODEL), BF16), pltpu.VMEM((t, D_MODEL), BF16),
                        pltpu.VMEM((H_HEADS, t, t), BF16)],
        compiler_params=_cparams(("parallel", "arbitrary")),
        name="hgrn_seq",
    )(*inputs, *x_in)


def _log_gamma(h):
    return math.log(1.0 - 2.0 ** (-5.0 - h))


def _rotary(x, cos, sin):
    x1, x2 = x[:, :R_HALF], x[:, R_HALF:]
    return jnp.concatenate([x1 * cos - x2 * sin, x2 * cos + x1 * sin], axis=1)


def _ret_decay(t):
    d = np.arange(t)[:, None] - np.arange(t)[None, :]
    return np.stack([np.where(d >= 0, np.exp(np.maximum(d, 0) * _log_gamma(h)), 0.0)
                     for h in range(R_HEADS)]).astype(np.float32)


def _ret_seq_kernel(lead, t, q_ref, k_ref, v_ref, g_ref, cos_ref, sin_ref, dec_ref, s0_ref,
                    y_ref, sout_ref, st_ref):
    c = pl.program_id(1)

    @pl.when(c == 0)
    def _():
        st_ref[...] = s0_ref[...]

    cos, sin = cos_ref[...], sin_ref[...]
    tcol = _iota((t, 1), 0).astype(F32)
    for h in range(R_HEADS):
        lg = _log_gamma(h)
        ks = slice(h * R_KDIM, (h + 1) * R_KDIM)
        vs = slice(h * R_VDIM, (h + 1) * R_VDIM)
        qh = _rotary(q_ref[:, ks], cos, sin)
        kh = _rotary(k_ref[:, ks], cos, sin) * (R_KDIM ** -0.5)
        if lead:
            kh = jnp.where(_iota((t, 1), 0) >= lead, kh, 0.0)
        qb, kb, vb = qh.astype(BF16), kh.astype(BF16), v_ref[:, vs].astype(BF16)
        scores = _dot_nt(qb, kb) * dec_ref[h]
        st = st_ref[h]
        o = _dot(scores.astype(BF16), vb) + _dot(qb, st.astype(BF16)) * jnp.exp((tcol + 1.0) * lg)
        kdec = (kh * jnp.exp((t - 1.0 - tcol) * lg)).T.astype(BF16)
        st_ref[h] = st * math.exp(t * lg) + _dot(kdec, vb)
        y_ref[:, vs] = (_rms(o) * _silu(g_ref[:, vs])).astype(y_ref.dtype)

    @pl.when(c == pl.num_programs(1) - 1)
    def _():
        sout_ref[...] = st_ref[...]


def _ret_seq(proj3, t, row_blk0, nblk, lead, cos, sin, s0, out_dtype, stack=None):
    bsz = proj3.shape[0]
    wide = R_HEADS * R_VDIM
    bcast = lambda a: pl.BlockSpec((None,) + a.shape[1:], lambda b, c: (0,) * a.ndim)
    decay = jnp.asarray(_ret_decay(t))
    inputs = [proj3, proj3, proj3, proj3, cos, sin, decay, s0]
    st_shape, st_spec = _seq_state_out(stack, bsz, (R_HEADS, R_KDIM, R_VDIM))
    body, x_specs, x_in, aliases = _alias_prev(functools.partial(_ret_seq_kernel, lead, t), len(inputs),
                                               stack and stack[1], (1,))
    return pl.pallas_call(
        body,
        grid=(bsz, nblk),
        in_specs=[pl.BlockSpec((None, t, D_MODEL), lambda b, c: (b, row_blk0 + c, COL_RQ)),
                  pl.BlockSpec((None, t, D_MODEL), lambda b, c: (b, row_blk0 + c, COL_RK)),
                  pl.BlockSpec((None, t, wide), lambda b, c: (b, row_blk0 + c, COL_RV)),
                  pl.BlockSpec((None, t, wide), lambda b, c: (b, row_blk0 + c, COL_RG)),
                  pl.BlockSpec((t, R_HALF), lambda b, c: (c, 0)),
                  pl.BlockSpec((t, R_HALF), lambda b, c: (c, 0)),
                  pl.BlockSpec(decay.shape, lambda b, c: (0, 0, 0)),
                  bcast(s0)] + x_specs,
        out_specs=[pl.BlockSpec((None, t, wide), lambda b, c: (b, c, 0)), st_spec],
        out_shape=[jax.ShapeDtypeStruct((bsz, nblk * t, wide), out_dtype), st_shape],
        scratch_shapes=[pltpu.VMEM((R_HEADS, R_KDIM, R_VDIM), F32)],
        input_output_aliases=aliases,
        compiler_params=_cparams(("parallel", "arbitrary")),
        name="ret_seq",
    )(*inputs, *x_in)


def _col_pieces(x):
    return jnp.concatenate(_split3(x.T), axis=1)


def _pick_col(n):
    r = _iota((3 * LANES, LANES), 0) & (LANES - 1)
    return jnp.where(r == n, 1.0, 0.0).astype(BF16)


def _pick_col_pair(n):
    r = _iota((3 * LANES, 2 * LANES), 0) & (LANES - 1)
    want = n + jnp.right_shift(_iota((3 * LANES, 2 * LANES), 1), int(math.log2(LANES)))
    return jnp.where(r == want, 1.0, 0.0).astype(BF16)


def _step_prep_kernel(lb_is_zero, xs_ref, bc_ref, x_ref, wdt_ref, hq_ref, hf_ref, rq_ref, rk_ref, conv_ref,
                      cw_ref, cb_ref, dtb_ref, a_ref, ex_ref, lb_ref, cos_ref, sin_ref,
                      xs_o, bc_o, xdt_o, edec_o, conv_o, hq_o, hef_o, hk_o, rq_o, rk_o):
    raw = jnp.concatenate([xs_ref[...], bc_ref[...]], axis=1)
    buf = conv_ref[...]
    acc = cb_ref[...] + cw_ref[3:4, :] * raw
    for k in range(M_CONV - 1):
        acc += cw_ref[k:k + 1, :] * buf[:, k * M_CONV_DIM:(k + 1) * M_CONV_DIM]
    conv_o[:, 0:2 * M_CONV_DIM] = buf[:, M_CONV_DIM:]
    conv_o[:, 2 * M_CONV_DIM:] = raw
    xbc = _silu(acc)
    xs = xbc[:, :M_INNER]
    xs_o[...] = xs
    bc_o[...] = xbc[:, M_INNER:]
    dt = _softplus(_dot_nt(x_ref[...].astype(BF16), wdt_ref[...]) + dtb_ref[...])
    ex = ex_ref[...]
    xdt_o[...] = _col_pieces(xs * _sel_right(dt, ex))
    edec_o[...] = _col_pieces(_sel_right(jnp.exp(dt * a_ref[...]), ex))
    logf, k = _hgrn_gates(hf_ref[...], lb_ref[...], lb_is_zero)
    hq_o[...] = hq_ref[...] * (H_KDIM ** -0.5)
    hef_o[...] = _col_pieces(jnp.exp(logf))
    hk_o[...] = _col_pieces(k)
    cos, sin = cos_ref[...], sin_ref[...]
    rq = jnp.concatenate([_rotary(rq_ref[:, h * R_KDIM:(h + 1) * R_KDIM], cos, sin) for h in range(R_HEADS)], axis=1)
    rk = jnp.concatenate([_rotary(rk_ref[:, h * R_KDIM:(h + 1) * R_KDIM], cos, sin) for h in range(R_HEADS)], axis=1)
    rq_o[...] = _col_pieces(rq)
    rk_o[...] = _col_pieces(rk * (R_KDIM ** -0.5))


def _step_prep(proj_s, x_s, wt, layer, conv_flat, p, cos, sin):
    nb = conv_flat.shape[0]
    col = lambda cblk: pl.BlockSpec((nb, D_MODEL), lambda i, cblk=cblk: (0, cblk))
    full = lambda a: pl.BlockSpec(a.shape, lambda i: (0,) * a.ndim)
    params = [p["conv_w"], p["conv_b"], p["dt_bias"], p["a_neg"], p["expand"], p["lb"], cos, sin]
    assert nb == LANES
    rows = lambda w: jax.ShapeDtypeStruct((nb, w), F32)
    cols = lambda w: jax.ShapeDtypeStruct((w, 3 * nb), BF16)
    shapes = [rows(M_INNER), rows(M_BC), cols(M_INNER), cols(M_INNER), rows((M_CONV - 1) * M_CONV_DIM),
              rows(D_MODEL), cols(D_MODEL), cols(D_MODEL), cols(D_MODEL), cols(D_MODEL)]
    return pl.pallas_call(
        functools.partial(_step_prep_kernel, layer == 0),
        grid=(1,),
        in_specs=[col(COL_XS), col(COL_BC), pl.BlockSpec((nb, D_MODEL), lambda i: (0, 0)), _dt_weight_spec(layer),
                  col(COL_HQ), col(COL_HF), col(COL_RQ), col(COL_RK), full(conv_flat)]
                 + [full(a) for a in params],
        out_specs=[pl.BlockSpec(s.shape, lambda i: (0, 0)) for s in shapes],
        out_shape=shapes,
        compiler_params=_cparams(("arbitrary",)),
        name="step_prep",
    )(proj_s, proj_s, x_s, wt, proj_s, proj_s, proj_s, proj_s, conv_flat, *params)


def _ssd_step_kernel(s_ref, xdt_ref, edec_ref, bc_ref, xs_ref, z_ref, dsk_ref, nw_ref, so_ref, y_ref,
                     yt_ref, xdt_b, edec_b):
    yt_ref[...] = jnp.zeros(yt_ref.shape, F32)
    for i in range(STEP_BT):
        ls = slice((i % 2) * LANES, (i % 2 + 1) * LANES)
        if i % 2 == 0:
            pick = _pick_col_pair(pl.program_id(0) * STEP_BT + i)
            xdt_b[...] = _dot(xdt_ref[...], pick)
            edec_b[...] = _dot(edec_ref[...], pick)
        for g in range(M_GROUPS):
            gs = slice(g * M_GW, (g + 1) * M_GW)
            hs = slice(M_HPG * g, M_HPG * (g + 1))
            st = s_ref[i, hs].reshape(M_GW, M_STATE)
            brow = bc_ref[i:i + 1, g * M_STATE:(g + 1) * M_STATE]
            crow = bc_ref[i:i + 1, M_BC // 2 + g * M_STATE:M_BC // 2 + (g + 1) * M_STATE]
            new = st * edec_b[gs, ls] + xdt_b[gs, ls] * brow
            so_ref[i, hs] = new.reshape(M_HPG, M_HEADDIM, M_STATE)
            yt_ref[gs, i:i + 1] = jnp.sum(new * crow, axis=-1, keepdims=True)
    y = yt_ref[...].T[0:STEP_BT, :]
    xs = xs_ref[...]
    y = (y + dsk_ref[...] * xs) * _silu(z_ref[...])
    for g in range(M_GROUPS):
        gs = slice(g * M_GW, (g + 1) * M_GW)
        y_ref[:, gs] = _rms(y[:, gs]) * nw_ref[:, gs]


def _ssd_step(state, layer, prev, xdt_c, edec_c, bc, xs, proj_s, p):
    nb = xs.shape[0]
    bt = STEP_BT
    full = lambda a: pl.BlockSpec(a.shape, lambda j: (0,) * a.ndim)
    sspec = pl.BlockSpec((None, bt, M_HEADS, M_HEADDIM, M_STATE), lambda j: (layer, j, 0, 0, 0))
    inputs = [state, xdt_c, edec_c, bc, xs, proj_s, p["d_skip"], p["m_norm_w"]]
    body, x_specs, x_in, aliases = _alias_prev(_ssd_step_kernel, len(inputs),
                                               None if prev is None else [prev], (0,))
    return pl.pallas_call(
        body,
        grid=(nb // bt,),
        in_specs=[sspec, full(xdt_c), full(edec_c),
                  pl.BlockSpec((bt, M_BC), lambda j: (j, 0)),
                  pl.BlockSpec((bt, M_INNER), lambda j: (j, 0)),
                  pl.BlockSpec((bt, D_MODEL), lambda j: (j, COL_Z)),
                  full(p["d_skip"]), full(p["m_norm_w"])] + x_specs,
        out_specs=[sspec, pl.BlockSpec((bt, M_INNER), lambda j: (j, 0))],
        out_shape=[jax.ShapeDtypeStruct(state.shape, F32),
                   jax.ShapeDtypeStruct((nb, M_INNER), F32)],
        scratch_shapes=[pltpu.VMEM((M_INNER, LANES), F32)] + [pltpu.VMEM((M_INNER, 2 * LANES), F32)] * 2,
        input_output_aliases=aliases,
        compiler_params=_cparams(("parallel",)),
        name="ssd_step",
    )(*inputs, *x_in)


def _cols(x):
    nb, w = x.shape
    return jnp.transpose(x.reshape(nb // STEP_BT, STEP_BT, w), (0, 2, 1))


def _hgrn_step_kernel(s_ref, q_ref, ef_ref, k_ref, v_ref, g_ref, nw_ref, so_ref, y_ref, ef_b, k_b):
    for i in range(STEP_BT):
        ls = slice((i % 2) * LANES, (i % 2 + 1) * LANES)
        if i % 2 == 0:
            pick = _pick_col_pair(pl.program_id(0) * STEP_BT + i)
            ef_b[...] = _dot(ef_ref[...], pick)
            k_b[...] = _dot(k_ref[...], pick)
        for h in range(H_HEADS):
            cs = slice(h * H_KDIM, (h + 1) * H_KDIM)
            new = s_ref[i, h] * ef_b[cs, ls] + k_b[cs, ls] * v_ref[i:i + 1, cs]
            so_ref[i, h] = new
            y_ref[i:i + 1, cs] = jnp.sum(new * q_ref[cs, i:i + 1], axis=0, keepdims=True)
    for h in range(H_HEADS):
        cs = slice(h * H_KDIM, (h + 1) * H_KDIM)
        y_ref[:, cs] = _rms(y_ref[:, cs]) * nw_ref[:, cs] * _sigmoid(g_ref[:, cs])


def _hgrn_step(state, layer, prev, q_c, ef_c, k_c, proj_s, nw):
    nb = state.shape[1]
    bt = STEP_BT
    cspec = pl.BlockSpec(ef_c.shape, lambda j: (0, 0))
    qspec = pl.BlockSpec((None, D_MODEL, bt), lambda j: (j, 0, 0))
    sspec = pl.BlockSpec((None, bt, H_HEADS, H_KDIM, H_VDIM), lambda j: (layer, j, 0, 0, 0))
    inputs = [state, q_c, ef_c, k_c, proj_s, proj_s, nw]
    body, x_specs, x_in, aliases = _alias_prev(_hgrn_step_kernel, len(inputs),
                                               None if prev is None else [prev], (0,))
    return pl.pallas_call(
        body,
        grid=(nb // bt,),
        in_specs=[sspec, qspec, cspec, cspec,
                  pl.BlockSpec((bt, D_MODEL), lambda j: (j, COL_HI)),
                  pl.BlockSpec((bt, D_MODEL), lambda j: (j, COL_HG)),
                  pl.BlockSpec(nw.shape, lambda j: (0, 0))] + x_specs,
        out_specs=[sspec, pl.BlockSpec((bt, D_MODEL), lambda j: (j, 0))],
        out_shape=[jax.ShapeDtypeStruct(state.shape, F32),
                   jax.ShapeDtypeStruct((nb, D_MODEL), F32)],
        scratch_shapes=[pltpu.VMEM((D_MODEL, 2 * LANES), F32)] * 2,
        input_output_aliases=aliases,
        compiler_params=_cparams(("parallel",)),
        name="hgrn_step",
    )(*inputs, *x_in)


def _ret_step_kernel(*refs):
    _ret_token_update(pl.program_id(0), *refs)


def _ret_step(state, layer, prev, q_c, k_c, v3, g3, gam):
    nb = state.shape[1]
    full = lambda a: pl.BlockSpec(a.shape, lambda n: (0,) * a.ndim)
    seq_row = lambda a: pl.BlockSpec((None,) + a.shape[1:], lambda n: (n, 0, 0))
    sspec = pl.BlockSpec((None, None, R_HEADS, R_KDIM, R_VDIM), lambda n: (layer, n, 0, 0, 0))
    inputs = [state, q_c, k_c, v3, g3, gam]
    body, x_specs, x_in, aliases = _alias_prev(_ret_step_kernel, len(inputs),
                                               None if prev is None else [prev], (0,))
    return pl.pallas_call(
        body,
        grid=(nb,),
        in_specs=[sspec, full(q_c), full(k_c), seq_row(v3), seq_row(g3), full(gam)] + x_specs,
        out_specs=[sspec, seq_row(v3)],
        out_shape=[jax.ShapeDtypeStruct(state.shape, F32), jax.ShapeDtypeStruct(v3.shape, F32)],
        input_output_aliases=aliases,
        compiler_params=_cparams(("parallel",)),
        name="ret_step",
    )(*inputs, *x_in)


def _rope_tables(positions):
    inv_freq = 1.0 / (ROPE_BASE ** jnp.linspace(0.0, 1.0, R_HALF, dtype=F32))
    ang = positions[:, None] * inv_freq[None, :]
    return jnp.cos(ang), jnp.sin(ang)


def _per_channel(v):
    return jnp.repeat(v.astype(F32), M_HEADDIM).reshape(1, M_INNER)


def _pad_lanes(v):
    return jnp.pad(v.astype(F32), (0, LANES - v.shape[0])).reshape(1, LANES)


def kernel(x_prompt, x_sample, state_ssm, state_conv, state_hgrn, state_ret, meta_tokens, ln_in_g, ln_in_b,
           w_in, conv_w, conv_b, dt_bias, a_log, d_skip, m_norm_w, hgrn_lb_logits, h_norm_w, w_br_m, w_br_h,
           w_br_r, w_out, ln1_g, ln1_b, w_ffn_in, w_ffn_out, ln2_g, ln2_b):
    bp, sp = x_prompt.shape[0], x_prompt.shape[1]
    nb = x_sample.shape[0]
    assert x_sample.shape[1] == 1 and nb == SMALL_ROWS - SEQ_BLOCK and nb % STEP_BT == 0
    assert sp % RET_BLOCK == 0 and meta_tokens.shape[0] == N_META

    wt = jnp.swapaxes(w_in, 1, 2).astype(BF16)
    wm_b, wh_b, wr_b, wo_b = (w.astype(BF16) for w in (w_br_m, w_br_h, w_br_r, w_out))
    wfi_b, wfo_b = w_ffn_in.astype(BF16), w_ffn_out.astype(BF16)
    ln1 = (ln1_g.reshape(DEPTH, 1, D_MODEL), ln1_b.reshape(DEPTH, 1, D_MODEL))
    ln2 = (ln2_g.reshape(DEPTH, 1, D_MODEL), ln2_b.reshape(DEPTH, 1, D_MODEL))
    lb_cum = jnp.cumsum(jax.nn.softmax(hgrn_lb_logits.astype(F32), axis=0), axis=0)
    lbs = lb_cum - lb_cum[0]
    expand = (np.arange(LANES)[:, None] == (np.arange(M_INNER)[None, :] // M_HEADDIM)).astype(np.float32)
    expand = jnp.asarray(np.concatenate([expand] * 3, 0), BF16)
    h_sums, h_pair = _hgrn_tables()
    h_tables = (jnp.asarray(h_sums, BF16), jnp.asarray(h_pair, F32))
    gam = jnp.asarray(np.broadcast_to(
        np.array([1.0 - 2.0 ** (-5.0 - h) for h in range(R_HEADS)], np.float32)[:, None, None],
        (R_HEADS, 1, R_VDIM)))

    pos_real = jnp.arange(N_META, N_META + sp, dtype=F32)
    pos_meta = jnp.maximum(jnp.arange(SEQ_BLOCK, dtype=F32) - META_LEAD, 0.0)
    pos_samp = jnp.full((nb,), float(PAST_LEN), F32)
    cos_r, sin_r = _rope_tables(pos_real)
    cos_m, sin_m = _rope_tables(pos_meta)
    cos_s, sin_s = _rope_tables(pos_samp)

    x_real = _layer_norm_rows(x_prompt.reshape(bp * sp, D_MODEL), ln_in_g, ln_in_b, 512)
    small_in = jnp.concatenate([x_sample.reshape(nb, D_MODEL),
                                jnp.zeros((META_LEAD, D_MODEL), F32), meta_tokens.astype(F32)], axis=0)
    x_small = _layer_norm_rows(small_in, ln_in_g, ln_in_b, SMALL_ROWS)

    z_ssm = jnp.zeros((1, M_HEADS, M_HEADDIM, M_STATE), F32)
    z_conv = jnp.zeros((1, 8, M_CONV_DIM), F32)
    z_hgrn = jnp.zeros((1, H_HEADS, H_KDIM, H_VDIM), F32)
    z_ret = jnp.zeros((1, R_HEADS, R_KDIM, R_VDIM), F32)
    ssm_p = conv_p = hgrn_p = ret_p = ssm_s = hgrn_s = ret_s = None
    conv_s = []
    for l in range(DEPTH):
        p = dict(conv_w=conv_w[l], conv_b=conv_b[l].reshape(1, -1), dt_bias=_pad_lanes(dt_bias[l]),
                 a_neg=_pad_lanes(-jnp.exp(a_log[l].astype(F32))), d_skip=_per_channel(d_skip[l]),
                 m_norm_w=m_norm_w[l].reshape(1, -1), expand=expand, lb=lbs[l].reshape(1, -1))
        hnw = h_norm_w[l].reshape(1, -1)

        proj_s = _proj(x_small, wt, l, SMALL_ROWS, "proj_small")
        proj_s3, x_small3 = proj_s.reshape(1, SMALL_ROWS, PROJ_COLS), x_small.reshape(1, SMALL_ROWS, D_MODEL)
        conv_flat = state_conv[l].reshape(nb, (M_CONV - 1) * M_CONV_DIM)
        (xs_s, bc_s, xdt_s, edec_s, conv_new, hq_s, hef_s, hk_s, rq_s, rk_s) = _step_prep(
            proj_s, x_small, wt, l, conv_flat, p, cos_s, sin_s)
        conv_s.append(conv_new.reshape(nb, M_CONV - 1, M_CONV_DIM))
        wide = R_HEADS * R_VDIM
        rv3 = proj_s[:nb, 0:wide].reshape(nb, 1, wide)
        rg3 = proj_s[:nb, wide:2 * wide].reshape(nb, 1, wide)

        if _proj_steps(bp * sp, PROJ_TM) == nb:
            proj_r, ret_s, yr_s = _proj(x_real, wt, l, PROJ_TM, "proj_real",
                                        ride=(state_ret, ret_s, rq_s, rk_s, rv3, rg3, gam))
        else:
            proj_r = _proj(x_real, wt, l, PROJ_TM, "proj_real")
            ret_s, yr_s = _ret_step(state_ret, l, ret_s, rq_s, rk_s, rv3, rg3, gam)
        yr_s = yr_s.reshape(nb, wide)
        proj_r3, x_real3 = proj_r.reshape(bp, sp, PROJ_COLS), x_real.reshape(bp, sp, D_MODEL)

        ssm_s, ym_s = _ssd_step(state_ssm, l, ssm_s, xdt_s, edec_s, bc_s, xs_s, proj_s, p)
        hgrn_s, yh_s = _hgrn_step(state_hgrn, l, hgrn_s, _cols(hq_s), hef_s, hk_s, proj_s, hnw)

        ym_m, ssm_m, conv_m = _ssd_seq(proj_s3, x_small3, wt, l, 1, 1, META_LEAD, p, z_ssm, z_conv, F32)
        yh_m, hgrn_m = _hgrn_seq(proj_s3, 1, 1, META_LEAD, p["lb"], l == 0, hnw, h_tables, z_hgrn, F32)
        yr_m, ret_m = _ret_seq(proj_s3, SEQ_BLOCK, 1, 1, META_LEAD, cos_m, sin_m, z_ret, F32)

        conv0 = jnp.pad(conv_m, ((0, 0), (8 - (M_CONV - 1), 0), (0, 0)))
        ym_r, ssm_p, conv_p = _ssd_seq(proj_r3, x_real3, wt, l, 0, sp // SEQ_BLOCK, 0, p, ssm_m, conv0, BF16,
                                       stack=(l, None if l == 0 else [ssm_p, conv_p]))
        yh_r, hgrn_p = _hgrn_seq(proj_r3, 0, sp // SEQ_BLOCK, 0, p["lb"], l == 0, hnw, h_tables, hgrn_m, BF16,
                                 stack=(l, None if l == 0 else [hgrn_p]))
        yr_r, ret_p = _ret_seq(proj_r3, RET_BLOCK, 0, sp // RET_BLOCK, 0, cos_r, sin_r, ret_m, BF16,
                               stack=(l, None if l == 0 else [ret_p]))

        ym_small = jnp.concatenate([ym_s, ym_m[0]], axis=0)
        yh_small = jnp.concatenate([yh_s, yh_m[0]], axis=0)
        yr_small = jnp.concatenate([yr_s, yr_m[0]], axis=0)
        x_real = _mix(x_real, ym_r.reshape(bp * sp, -1), yh_r.reshape(bp * sp, -1), yr_r.reshape(bp * sp, -1),
                      proj_r, l, wm_b, wh_b, wr_b, wo_b, *ln1, 512)
        x_small = _mix(x_small, ym_small, yh_small, yr_small, proj_s, l, wm_b, wh_b, wr_b, wo_b, *ln1, SMALL_ROWS)
        x_real = _ffn(x_real, l, wfi_b, wfo_b, *ln2, 1024)
        x_small = _ffn(x_small, l, wfi_b, wfo_b, *ln2, SMALL_ROWS)

    return (x_real.reshape(bp, sp, D_MODEL), x_small[:nb].reshape(nb, 1, D_MODEL),
            ssm_p, conv_p, hgrn_p, ret_p, ssm_s, jnp.stack(conv_s), hgrn_s, ret_s)
```

```python
import functools
import math

import numpy as np
import jax
import jax.numpy as jnp
from jax import lax
from jax.experimental import pallas as pl
from jax.experimental.pallas import tpu as pltpu

F32 = jnp.float32
BF16 = jnp.bfloat16

D_MODEL = 1024
DEPTH = 2
N_META = 16
M_INNER = D_MODEL
M_HEADDIM = 64
M_HEADS = M_INNER // M_HEADDIM
M_GROUPS = 4
M_HPG = M_HEADS // M_GROUPS
M_STATE = 128
M_CONV = 4
M_BC = 2 * M_GROUPS * M_STATE
M_CONV_DIM = M_INNER + M_BC
M_GW = M_INNER // M_GROUPS
H_KDIM = 128
H_HEADS = D_MODEL // H_KDIM
H_VDIM = 128
R_HEADS = 4
R_KDIM = D_MODEL // R_HEADS
R_VDIM = 2 * R_KDIM
R_HALF = R_KDIM // 2
ROPE_BASE = 10000.0
D_FF = ((8 * D_MODEL // 3 + 255) // 256) * 256
FF_CHUNK = 256
DN_ALPHA = (2 * DEPTH) ** 0.25
PAST_LEN = 16384

LANES = 128
SEQ_BLOCK = 128
RET_BLOCK = 256
SMALL_ROWS = 256
META_LEAD = SEQ_BLOCK - N_META
STEP_BT = 8
VMEM_BYTES = 64 * 1024 * 1024
VMEM_LIMIT = VMEM_BYTES - 4 * 1024 * 1024

COL_RV, COL_RG = 0, 1
COL_Z, COL_XS, COL_BC, COL_HQ, COL_HF, COL_HI, COL_HG, COL_RQ, COL_RK, COL_GM, COL_GH, COL_GR = range(4, 16)
PROJ_COLS = 16 * 1024

_ORIG_SPLITS = (M_INNER, M_CONV_DIM, M_HEADS, 1024, 1024, 1024, 1024, 1024, 1024, 2048, 2048, 3072)
_ORIG_OFF = np.concatenate([[0], np.cumsum(_ORIG_SPLITS)]).tolist()


def _cparams(sem):
    return pltpu.CompilerParams(dimension_semantics=sem, vmem_limit_bytes=VMEM_LIMIT)


def _sigmoid(x):
    return 1.0 / (1.0 + jnp.exp(-x))


def _silu(x):
    return x * _sigmoid(x)


def _softplus(x):
    return jnp.maximum(x, 0.0) + jnp.log1p(jnp.exp(-jnp.abs(x)))


def _layer_norm(x, g, b):
    mu = jnp.mean(x, axis=-1, keepdims=True)
    xc = x - mu
    var = jnp.mean(xc * xc, axis=-1, keepdims=True)
    return xc * lax.rsqrt(var + 1e-5) * g + b


def _rms(x):
    return x * lax.rsqrt(jnp.mean(x * x, axis=-1, keepdims=True) + 1e-6)


def _split3(x):
    hi = x.astype(BF16)
    r = x - hi.astype(F32)
    mid = r.astype(BF16)
    lo = (r - mid.astype(F32)).astype(BF16)
    return hi, mid, lo


def _dot(a, b):
    return jnp.dot(a, b, preferred_element_type=F32)


def _dot_nt(a, b):
    return lax.dot_general(a, b, (((1,), (1,)), ((), ())), preferred_element_type=F32)


def _sel_right(x, m3):
    return _dot(jnp.concatenate(_split3(x), axis=1), m3)


def _sel_left(m3, x):
    return _dot(m3, jnp.concatenate(_split3(x), axis=0))


def _iota(shape, dim):
    return lax.broadcasted_iota(jnp.int32, shape, dim)


def _ln_kernel(x_ref, g_ref, b_ref, o_ref):
    o_ref[...] = _layer_norm(x_ref[...], g_ref[...], b_ref[...])


def _layer_norm_rows(x, g, b, tm):
    m = x.shape[0]
    tm = min(tm, m)
    return pl.pallas_call(
        _ln_kernel,
        grid=(m // tm,),
        in_specs=[pl.BlockSpec((tm, D_MODEL), lambda i: (i, 0)),
                  pl.BlockSpec((1, D_MODEL), lambda i: (0, 0)),
                  pl.BlockSpec((1, D_MODEL), lambda i: (0, 0))],
        out_specs=pl.BlockSpec((tm, D_MODEL), lambda i: (i, 0)),
        out_shape=jax.ShapeDtypeStruct((m, D_MODEL), F32),
        compiler_params=_cparams(("parallel",)),
        name="ln_in",
    )(x, g.reshape(1, -1), b.reshape(1, -1))


def _proj_kernel(ride, x_ref, w_ref, *refs):
    if ride:
        ride_in, (o_ref, so_ref, ry_ref, xb_ref) = refs[:6], refs[6:]
    else:
        o_ref, xb_ref = refs

    @pl.when(pl.program_id(1) == 0)
    def _():
        xb_ref[...] = x_ref[...].astype(BF16)

    o_ref[...] = _dot_nt(xb_ref[...], w_ref[0])
    if ride:
        _ret_token_update(pl.program_id(0) * pl.num_programs(1) + pl.program_id(1), *ride_in, so_ref, ry_ref)


def _layer_block(a, layer):
    return pl.BlockSpec((None,) + a.shape[1:], lambda *_: (layer,) + (0,) * (a.ndim - 1),
                        pipeline_mode=pl.Buffered(1))


_W_RUNS = ((0, _ORIG_OFF[9]), (4096, _ORIG_OFF[0]), (7168, _ORIG_OFF[3]), (13312, _ORIG_OFF[11]))
PROJ_TN = 1024
PROJ_TM = 2048


def _orig_col(j):
    c = j * PROJ_TN
    off = c - _W_RUNS[0][0] + _W_RUNS[0][1]
    for new0, orig0 in _W_RUNS[1:]:
        off = jnp.where(c >= new0, c - new0 + orig0, off)
    return pl.multiple_of(off, M_HEADS)


def _proj_steps(m, tm):
    return (m // min(tm, m)) * (PROJ_COLS // PROJ_TN)


def _proj(x, wt, layer, tm, name, ride=None):
    m = x.shape[0]
    tm = min(tm, m)
    nj = PROJ_COLS // PROJ_TN
    inputs = [x, wt]
    in_specs = [pl.BlockSpec((tm, D_MODEL), lambda i, j: (i, 0)),
                pl.BlockSpec((pl.Element(1), pl.Element(PROJ_TN), pl.Element(D_MODEL)),
                             lambda i, j: (layer, _orig_col(j), 0))]
    out_specs = [pl.BlockSpec((tm, PROJ_TN), lambda i, j: (i, j))]
    out_shape = [jax.ShapeDtypeStruct((m, PROJ_COLS), F32)]
    prev = None
    if ride:
        state, prev, q_c, k_c, v3, g3, gam = ride
        assert _proj_steps(m, tm) == state.shape[1]
        full = lambda a: pl.BlockSpec(a.shape, lambda i, j: (0,) * a.ndim)
        seq_row = lambda a: pl.BlockSpec((None,) + a.shape[1:], lambda i, j: (i * nj + j, 0, 0))
        sspec = pl.BlockSpec((None, None, R_HEADS, R_KDIM, R_VDIM), lambda i, j: (layer, i * nj + j, 0, 0, 0))
        inputs += [state, q_c, k_c, v3, g3, gam]
        in_specs[0] = pl.BlockSpec((tm, D_MODEL), lambda i, j: (i, 0), pipeline_mode=pl.Buffered(1))
        in_specs += [sspec, full(q_c), full(k_c), seq_row(v3), seq_row(g3), full(gam)]
        out_specs += [sspec, seq_row(v3)]
        out_shape += [jax.ShapeDtypeStruct(state.shape, F32), jax.ShapeDtypeStruct(v3.shape, F32)]
    body, x_specs, x_in, aliases = _alias_prev(functools.partial(_proj_kernel, bool(ride)), len(inputs),
                                               None if prev is None else [prev], (1,))
    res = pl.pallas_call(
        body,
        grid=(m // tm, nj),
        in_specs=in_specs + x_specs,
        out_specs=out_specs,
        out_shape=out_shape,
        scratch_shapes=[pltpu.VMEM((tm, D_MODEL), BF16)],
        input_output_aliases=aliases,
        compiler_params=_cparams(("parallel", "arbitrary")),
        name=name,
    )(*inputs, *x_in)
    return res if ride else res[0]


def _dt_weight_spec(layer):
    assert _ORIG_OFF[2] % LANES == 0
    return pl.BlockSpec((None, LANES, D_MODEL), lambda *_: (layer, _ORIG_OFF[2] // LANES, 0))


def _mix_kernel(x_ref, ym_ref, yh_ref, yr_ref, gm_ref, gh_ref, gr_ref,
                wm_ref, wh_ref, wr_ref, wo_ref, g_ref, b_ref, o_ref):
    mixed = _sigmoid(gm_ref[...]) * _dot(ym_ref[...].astype(BF16), wm_ref[...])
    mixed += _sigmoid(gh_ref[...]) * _dot(yh_ref[...].astype(BF16), wh_ref[...])
    mixed += _sigmoid(gr_ref[...]) * _dot(yr_ref[...].astype(BF16), wr_ref[...])
    h = _dot(mixed.astype(BF16), wo_ref[...])
    o_ref[...] = _layer_norm(DN_ALPHA * x_ref[...] + h, g_ref[...], b_ref[...])


def _mix(x, ym, yh, yr, proj, layer, wm, wh, wr, wo, g2, b2, tm):
    m = x.shape[0]
    tm = min(tm, m)
    row = lambda w: pl.BlockSpec((tm, w), lambda i: (i, 0))
    col = lambda c: pl.BlockSpec((tm, D_MODEL), lambda i, c=c: (i, c))
    full = lambda a: _layer_block(a, layer)
    return pl.pallas_call(
        _mix_kernel,
        grid=(m // tm,),
        in_specs=[row(D_MODEL), row(M_INNER), row(D_MODEL), row(R_HEADS * R_VDIM),
                  col(COL_GM), col(COL_GH), col(COL_GR),
                  full(wm), full(wh), full(wr), full(wo), full(g2), full(b2)],
        out_specs=row(D_MODEL),
        out_shape=jax.ShapeDtypeStruct((m, D_MODEL), F32),
        compiler_params=_cparams(("parallel",)),
        name="mix",
    )(x, ym, yh, yr, proj, proj, proj, wm, wh, wr, wo, g2, b2)


def _ffn_kernel(x_ref, wi_ref, wo_ref, g_ref, b_ref, o_ref):
    x = x_ref[...]
    xb = x.astype(BF16)
    acc = jnp.zeros(x.shape, F32)
    for j in range(0, D_FF, FF_CHUNK):
        hg = _dot(xb, wi_ref[:, j:j + FF_CHUNK])
        hu = _dot(xb, wi_ref[:, D_FF + j:D_FF + j + FF_CHUNK])
        acc += _dot((_silu(hg) * hu).astype(BF16), wo_ref[j:j + FF_CHUNK, :])
    o_ref[...] = _layer_norm(DN_ALPHA * x + acc, g_ref[...], b_ref[...])


def _ffn(x, layer, wi, wo, g2, b2, tm):
    m = x.shape[0]
    tm = min(tm, m)
    full = lambda a: _layer_block(a, layer)
    return pl.pallas_call(
        _ffn_kernel,
        grid=(m // tm,),
        in_specs=[pl.BlockSpec((tm, D_MODEL), lambda i: (i, 0)), full(wi), full(wo), full(g2), full(b2)],
        out_specs=pl.BlockSpec((tm, D_MODEL), lambda i: (i, 0)),
        out_shape=jax.ShapeDtypeStruct((m, D_MODEL), F32),
        compiler_params=_cparams(("parallel",)),
        name="ffn",
    )(x, wi, wo, g2, b2)


SSD_PREP_BLOCKS = 8


def _ssd_prep_kernel(lead, nblk, x_ref, wdt_ref, dtb_ref, a_ref, cum_ref, ecum_ref, tailw_ref, cumt_ref, dtt_ref):
    t = SEQ_BLOCK
    dt = _softplus(_dot_nt(x_ref[...].astype(BF16), wdt_ref[...]) + dtb_ref[...])
    if lead:
        dt = jnp.where(_iota((nblk * t, 1), 0) >= lead, dt, 0.0)
    a = dt * a_ref[...]
    tri = jnp.where(_iota((t, t), 1) <= _iota((t, t), 0), 1.0, 0.0).astype(BF16)
    tri3 = jnp.concatenate([tri] * 3, axis=1)
    for i in range(nblk):
        rs = slice(i * t, (i + 1) * t)
        cum = _sel_left(tri3, a[rs])
        cum_ref[rs, :] = cum
        ecum_ref[rs, :] = jnp.exp(cum)
        tailw_ref[rs, :] = jnp.exp(cum[t - 1:t, :] - cum) * dt[rs]
        cumt_ref[i] = cum.T
        dtt_ref[i] = dt[rs].T


def _ssd_prep(x3, wt, layer, row_blk0, nblk, lead, p):
    bsz = x3.shape[0]
    t = SEQ_BLOCK
    g = math.gcd(nblk, SSD_PREP_BLOCKS)
    assert row_blk0 % g == 0
    full = lambda a: pl.BlockSpec(a.shape, lambda b, c: (0,) * a.ndim)
    rows = pl.BlockSpec((None, g * t, LANES), lambda b, c: (b, c, 0))
    cols = pl.BlockSpec((None, g, LANES, t), lambda b, c: (b, c, 0, 0))
    row_shape = jax.ShapeDtypeStruct((bsz, nblk * t, LANES), F32)
    col_shape = jax.ShapeDtypeStruct((bsz, nblk, LANES, t), F32)
    return pl.pallas_call(
        functools.partial(_ssd_prep_kernel, lead, g),
        grid=(bsz, nblk // g),
        in_specs=[pl.BlockSpec((None, g * t, D_MODEL), lambda b, c: (b, row_blk0 // g + c, 0)),
                  _dt_weight_spec(layer), full(p["dt_bias"]), full(p["a_neg"])],
        out_specs=[rows, rows, rows, cols, cols],
        out_shape=[row_shape, row_shape, row_shape, col_shape, col_shape],
        compiler_params=_cparams(("parallel", "parallel")),
        name="ssd_prep",
    )(x3, wt, p["dt_bias"], p["a_neg"])


def _ssd_seq_kernel(lead, z_ref, xs_ref, bc_ref, cum_ref, ecum_ref, tailw_ref, cumt_ref, dtt_ref,
                    cw_ref, cb_ref, dsk_ref, nw_ref, ex_ref, s0_ref, c0_ref,
                    y_ref, sout_ref, cout_ref, ext_ref, st_ref, xbc_ref):
    t = SEQ_BLOCK
    c = pl.program_id(1)

    @pl.when(c == 0)
    def _():
        ext_ref[0:8, :] = c0_ref[...]
        for g in range(M_GROUPS):
            st_ref[g] = s0_ref[M_HPG * g:M_HPG * (g + 1)].reshape(M_GW, M_STATE).T

    rows = _iota((t, 1), 0)

    def conv_chunk(q):
        w = M_CONV_DIM // 4
        cs = slice(q * w, (q + 1) * w)
        src = xs_ref if q < 2 else bc_ref
        raw = src[:, (q % 2) * w:(q % 2 + 1) * w]
        if lead:
            raw = jnp.where(rows >= lead, raw, 0.0)
        ext_ref[8:8 + t, cs] = raw
        acc = cb_ref[:, cs] + cw_ref[3:4, cs] * raw
        for k in range(M_CONV - 1):
            acc += cw_ref[k:k + 1, cs] * ext_ref[5 + k:5 + k + t, cs]
        ext_ref[5:8, cs] = ext_ref[t + 5:t + 8, cs]
        xbc_ref[:, cs] = _silu(acc)

    cum, cum_t, dt_t = cum_ref[...], cumt_ref[...], dtt_ref[...]
    tril = _iota((t, t), 1) <= _iota((t, t), 0)
    ex = ex_ref[...]
    ecum_full = _sel_right(ecum_ref[...], ex)
    tailw_full = _sel_right(tailw_ref[...], ex)
    for q in range(4):
        conv_chunk(q)

    xs = xbc_ref[:, 0:M_INNER]
    xw = (xs * tailw_full).astype(BF16)
    xs_b = xs.astype(BF16)
    head_of_lane = jnp.right_shift(_iota((1, M_GW), 1), int(math.log2(M_HEADDIM)))

    for g in range(M_GROUPS):
        gs = slice(g * M_GW, (g + 1) * M_GW)
        bm_f = xbc_ref[:, M_INNER + g * M_STATE:M_INNER + (g + 1) * M_STATE]
        bm = bm_f.astype(BF16)
        cm = xbc_ref[:, M_INNER + M_BC // 2 + g * M_STATE:M_INNER + M_BC // 2 + (g + 1) * M_STATE].astype(BF16)
        cb = _dot_nt(cm, bm)
        st = st_ref[g]
        y_g = _dot(cm, st.astype(BF16)) * ecum_full[:, gs]
        for hh in range(M_HPG):
            h = g * M_HPG + hh
            diff = cum[:, h:h + 1] - cum_t[h:h + 1, :]
            w = cb * jnp.exp(jnp.where(tril, diff, -1e30)) * dt_t[h:h + 1, :]
            x_h = jnp.where(head_of_lane == hh, xs_b[:, gs], jnp.zeros((), BF16))
            y_g = y_g + _dot(w.astype(BF16), x_h)
        st_ref[g] = st * ecum_full[t - 1:t, gs] + _dot(bm_f.T.astype(BF16), xw[:, gs])
        y_g = (y_g + dsk_ref[:, gs] * xs[:, gs]) * _silu(z_ref[:, gs])
        y_ref[:, gs] = (_rms(y_g) * nw_ref[:, gs]).astype(y_ref.dtype)

    @pl.when(c == pl.num_programs(1) - 1)
    def _():
        cout_ref[...] = ext_ref[5:8, :]
        for g in range(M_GROUPS):
            sout_ref[M_HPG * g:M_HPG * (g + 1)] = st_ref[g].T.reshape(M_HPG, M_HEADDIM, M_STATE)


def _alias_prev(body, n_in, prevs, out_ids):
    if not prevs:
        return body, [], [], {}
    k = len(prevs)
    wrapped = lambda *refs: body(*refs[:n_in], *refs[n_in + k:])
    return (wrapped, [pl.BlockSpec(memory_space=pl.ANY)] * k, list(prevs),
            {n_in + i: o for i, o in enumerate(out_ids)})


def _seq_state_out(stack, bsz, dims):
    zeros = (0,) * len(dims)
    if stack is None:
        return (jax.ShapeDtypeStruct((bsz,) + dims, F32),
                pl.BlockSpec((None,) + dims, lambda b, c: (b,) + zeros))
    layer = stack[0]
    return (jax.ShapeDtypeStruct((DEPTH, bsz) + dims, F32),
            pl.BlockSpec((None, None) + dims, lambda b, c: (layer, b) + zeros))


def _ssd_seq(proj3, x3, wt, layer, row_blk0, nblk, lead, p, s0, c0, out_dtype, stack=None):
    bsz = proj3.shape[0]
    t = SEQ_BLOCK
    assert not lead or nblk == 1
    colspec = lambda cblk: pl.BlockSpec((None, t, D_MODEL), lambda b, c, cblk=cblk: (b, row_blk0 + c, cblk))
    full = lambda a: pl.BlockSpec(a.shape, lambda b, c: (0,) * a.ndim)
    bcast = lambda a: pl.BlockSpec((None,) + a.shape[1:], lambda b, c: (0,) * a.ndim)
    prep = _ssd_prep(x3, wt, layer, row_blk0, nblk, lead, p)
    prep_rows = pl.BlockSpec((None, t, LANES), lambda b, c: (b, c, 0))
    prep_cols = pl.BlockSpec((None, None, LANES, t), lambda b, c: (b, c, 0, 0))
    params = [p["conv_w"], p["conv_b"], p["d_skip"], p["m_norm_w"], p["expand"]]
    inputs = [proj3, proj3, proj3, *prep, *params, s0, c0]
    st_shape, st_spec = _seq_state_out(stack, bsz, (M_HEADS, M_HEADDIM, M_STATE))
    cv_shape, cv_spec = _seq_state_out(stack, bsz, (M_CONV - 1, M_CONV_DIM))
    body, x_specs, x_in, aliases = _alias_prev(functools.partial(_ssd_seq_kernel, lead), len(inputs),
                                               stack and stack[1], (1, 2))
    return pl.pallas_call(
        body,
        grid=(bsz, nblk),
        in_specs=[colspec(COL_Z), colspec(COL_XS), colspec(COL_BC),
                  prep_rows, prep_rows, prep_rows, prep_cols, prep_cols]
                 + [full(a) for a in params] + [bcast(s0), bcast(c0)] + x_specs,
        out_specs=[pl.BlockSpec((None, t, M_INNER), lambda b, c: (b, c, 0)), st_spec, cv_spec],
        out_shape=[jax.ShapeDtypeStruct((bsz, nblk * t, M_INNER), out_dtype), st_shape, cv_shape],
        scratch_shapes=[pltpu.VMEM((t + 8, M_CONV_DIM), F32),
                        pltpu.VMEM((M_GROUPS, M_STATE, M_GW), F32),
                        pltpu.VMEM((t, M_CONV_DIM), F32)],
        input_output_aliases=aliases,
        compiler_params=_cparams(("parallel", "arbitrary")),
        name="ssd_seq",
    )(*inputs, *x_in)


def _hgrn_gates(fz, lb, lb_is_zero):
    e = jnp.exp(-jnp.abs(fz))
    r = 1.0 / (1.0 + e)
    pos = fz >= 0.0
    sig_neg = jnp.where(pos, e * r, r)
    log_sig = jnp.minimum(fz, 0.0) - jnp.log(1.0 + e)
    if lb_is_zero:
        return log_sig, sig_neg
    sig_pos = jnp.where(pos, r, e * r)
    logf = jnp.where(lb > 0.0, jnp.log(lb + (1.0 - lb) * sig_pos), log_sig)
    return logf, (1.0 - lb) * sig_neg


H_LEVELS = int(math.log2(SEQ_BLOCK))


def _hgrn_tables():
    n = SEQ_BLOCK
    t = np.arange(n)[:, None]
    u = np.arange(n)[None, :]
    pair = []
    for b in range(H_LEVELS):
        bit = ((t >> b) & 1) == 1
        pair.append(((t >> (b + 1)) == (u >> (b + 1))) & bit & (((u >> b) & 1) == 0))
    m1 = ((t >> 1) | 1) << 1
    lvl1 = np.where(((t >> 1) & 1) == 1, (u >= m1) & (u <= t), (u > t) & (u < m1))
    sums = np.concatenate([u <= t, lvl1], 0).astype(np.float32)
    return np.concatenate([sums] * 3, 1), np.concatenate(pair, 0).astype(np.float32)


def _exp_neg_abs(d):
    return jnp.exp2(jnp.abs(d) * (-1.0 / math.log(2.0)))


def _ret_token_update(n, s_ref, q_ref, k_ref, v_ref, g_ref, gam_ref, so_ref, y_ref):
    pick = _pick_col(n)
    for h in range(R_HEADS):
        rows = slice(h * R_KDIM, (h + 1) * R_KDIM)
        q_b = _dot(q_ref[rows, :], pick)
        k_b = _dot(k_ref[rows, :], pick)
        outs = []
        for c0 in range(0, R_VDIM, LANES):
            cs = slice(c0, c0 + LANES)
            new = s_ref[h, :, cs] * gam_ref[h, :, cs] + k_b * v_ref[:, h * R_VDIM + c0:h * R_VDIM + c0 + LANES]
            so_ref[h, :, cs] = new
            outs.append(jnp.sum(new * q_b, axis=0, keepdims=True))
        vs = slice(h * R_VDIM, (h + 1) * R_VDIM)
        y_ref[:, vs] = _rms(jnp.concatenate(outs, axis=1)) * _silu(g_ref[:, vs])


def _hgrn_seq_kernel(lead, lb_is_zero, q_ref, f_ref, i_ref, g_ref, lb_ref, nw_ref, sums_ref, pair_ref, s0_ref,
                     y_ref, sout_ref, st_ref, ex_ref, q_s, k_s, z_ref, qd_ref, kd_ref, sc_ref):
    t = SEQ_BLOCK
    c = pl.program_id(1)

    @pl.when(c == 0)
    def _():
        for h in range(H_HEADS):
            st_ref[h] = s0_ref[h].T

    rows = _iota((t, 1), 0)
    logf, k = _hgrn_gates(f_ref[...], lb_ref[...], lb_is_zero)
    if lead:
        logf = jnp.where(rows >= lead, logf, 0.0)
        k = jnp.where(rows >= lead, k, 0.0)
    q = q_ref[...] * (H_KDIM ** -0.5)
    q_s[...] = q
    k_s[...] = k
    ex_ref[...] = _dot(sums_ref[...], jnp.concatenate(_split3(logf), axis=0))

    def side_of(b):
        return (jnp.right_shift(rows, b) & 1) == 1

    z_ref[0] = jnp.where(side_of(0), q * jnp.exp(logf), k).astype(BF16)
    z_ref[1] = (jnp.exp(ex_ref[t:2 * t, :]) * jnp.where(side_of(1), q_s[...], k_s[...])).astype(BF16)
    for b in range(2, H_LEVELS):
        half = 1 << b
        groups = range(0, t, 2 * half)
        d = jnp.concatenate([ex_ref[g0:g0 + 2 * half, :] - ex_ref[g0 + half - 1:g0 + half, :]
                             for g0 in groups], axis=0)
        if half >= 8:
            qk = jnp.concatenate([ref[g0 + o:g0 + o + half, :] for g0 in groups
                                  for ref, o in ((k_s, 0), (q_s, half))], axis=0)
        else:
            qk = jnp.where(side_of(b), q_s[...], k_s[...])
        z_ref[b] = (_exp_neg_abs(d) * qk).astype(BF16)
    cum = ex_ref[0:t, :]
    qd_ref[...] = (q_s[...] * jnp.exp(cum)).astype(BF16)
    kd_ref[...] = (k_s[...] * jnp.exp(ex_ref[t - 1:t, :] - cum)).astype(BF16)

    for h in range(H_HEADS):
        cs = slice(h * H_KDIM, (h + 1) * H_KDIM)
        scores = None
        for b in range(H_LEVELS):
            z = z_ref[b, :, cs]
            p = _dot_nt(z, z) * pair_ref[b * t:(b + 1) * t, :]
            scores = p if scores is None else scores + p
        sc_ref[h] = scores.astype(BF16)

    for h in range(H_HEADS):
        cs = slice(h * H_KDIM, (h + 1) * H_KDIM)
        v = i_ref[:, cs]
        st = st_ref[h]
        o = (_dot(sc_ref[h], v.astype(BF16))
             + jnp.sum(q_s[:, cs] * k_s[:, cs], axis=-1, keepdims=True) * v
             + _dot_nt(qd_ref[:, cs], st.astype(BF16)))
        y = _rms(o) * nw_ref[:, cs] * _sigmoid(g_ref[:, cs])
        y_ref[:, cs] = y.astype(y_ref.dtype)
        st_ref[h] = st * jnp.exp(ex_ref[t - 1:t, cs]) + _dot(v.T.astype(BF16), kd_ref[:, cs])

    @pl.when(c == pl.num_programs(1) - 1)
    def _():
        for h in range(H_HEADS):
            sout_ref[h] = st_ref[h].T


def _hgrn_seq(proj3, row_blk0, nblk, lead, lb, lb_is_zero, nw, tables, s0, out_dtype, stack=None):
    bsz = proj3.shape[0]
    t = SEQ_BLOCK
    colspec = lambda cblk: pl.BlockSpec((None, t, D_MODEL), lambda b, c, cblk=cblk: (b, row_blk0 + c, cblk))
    full = lambda a: pl.BlockSpec(a.shape, lambda b, c: (0,) * a.ndim)
    bcast = lambda a: pl.BlockSpec((None,) + a.shape[1:], lambda b, c: (0,) * a.ndim)
    inputs = [proj3, proj3, proj3, proj3, lb, nw, *tables, s0]
    st_shape, st_spec = _seq_state_out(stack, bsz, (H_HEADS, H_KDIM, H_VDIM))
    body, x_specs, x_in, aliases = _alias_prev(functools.partial(_hgrn_seq_kernel, lead, lb_is_zero), len(inputs),
                                               stack and stack[1], (1,))
    return pl.pallas_call(
        body,
        grid=(bsz, nblk),
        in_specs=[colspec(COL_HQ), colspec(COL_HF), colspec(COL_HI), colspec(COL_HG),
                  full(lb), full(nw), full(tables[0]), full(tables[1]), bcast(s0)] + x_specs,
        out_specs=[pl.BlockSpec((None, t, D_MODEL), lambda b, c: (b, c, 0)), st_spec],
        out_shape=[jax.ShapeDtypeStruct((bsz, nblk * t, D_MODEL), out_dtype), st_shape],
        input_output_aliases=aliases,
        scratch_shapes=[pltpu.VMEM((H_HEADS, H_VDIM, H_KDIM), F32),
                        pltpu.VMEM((2 * t, D_MODEL), F32),
                        pltpu.VMEM((t, D_MODEL), F32), pltpu.VMEM((t, D_MODEL), F32),
                        pltpu.VMEM((H_LEVELS, t, D_MODEL), BF16),
                        pltpu.VMEM((t, D_MODEL), BF16), pltpu.VMEM((t, D_MODEL), BF16),
                        pltpu.VMEM((H_HEADS, t, t), BF16)],
        compiler_params=_cparams(("parallel", "arbitrary")),
        name="hgrn_seq",
    )(*inputs, *x_in)


def _log_gamma(h):
    return math.log(1.0 - 2.0 ** (-5.0 - h))


def _rotary(x, cos, sin):
    x1, x2 = x[:, :R_HALF], x[:, R_HALF:]
    return jnp.concatenate([x1 * cos - x2 * sin, x2 * cos + x1 * sin], axis=1)


def _ret_decay(t):
    d = np.arange(t)[:, None] - np.arange(t)[None, :]
    return np.stack([np.where(d >= 0, np.exp(np.maximum(d, 0) * _log_gamma(h)), 0.0)
                     for h in range(R_HEADS)]).astype(np.float32)


def _ret_seq_kernel(lead, t, q_ref, k_ref, v_ref, g_ref, cos_ref, sin_ref, dec_ref, s0_ref,
                    y_ref, sout_ref, st_ref):
    c = pl.program_id(1)

    @pl.when(c == 0)
    def _():
        st_ref[...] = s0_ref[...]

    cos, sin = cos_ref[...], sin_ref[...]
    tcol = _iota((t, 1), 0).astype(F32)
    for h in range(R_HEADS):
        lg = _log_gamma(h)
        ks = slice(h * R_KDIM, (h + 1) * R_KDIM)
        vs = slice(h * R_VDIM, (h + 1) * R_VDIM)
        qh = _rotary(q_ref[:, ks], cos, sin)
        kh = _rotary(k_ref[:, ks], cos, sin) * (R_KDIM ** -0.5)
        if lead:
            kh = jnp.where(_iota((t, 1), 0) >= lead, kh, 0.0)
        qb, kb, vb = qh.astype(BF16), kh.astype(BF16), v_ref[:, vs].astype(BF16)
        scores = _dot_nt(qb, kb) * dec_ref[h]
        st = st_ref[h]
        o = _dot(scores.astype(BF16), vb) + _dot(qb, st.astype(BF16)) * jnp.exp((tcol + 1.0) * lg)
        kdec = (kh * jnp.exp((t - 1.0 - tcol) * lg)).T.astype(BF16)
        st_ref[h] = st * math.exp(t * lg) + _dot(kdec, vb)
        y_ref[:, vs] = (_rms(o) * _silu(g_ref[:, vs])).astype(y_ref.dtype)

    @pl.when(c == pl.num_programs(1) - 1)
    def _():
        sout_ref[...] = st_ref[...]


def _ret_seq(proj3, t, row_blk0, nblk, lead, cos, sin, s0, out_dtype, stack=None):
    bsz = proj3.shape[0]
    wide = R_HEADS * R_VDIM
    bcast = lambda a: pl.BlockSpec((None,) + a.shape[1:], lambda b, c: (0,) * a.ndim)
    decay = jnp.asarray(_ret_decay(t))
    inputs = [proj3, proj3, proj3, proj3, cos, sin, decay, s0]
    st_shape, st_spec = _seq_state_out(stack, bsz, (R_HEADS, R_KDIM, R_VDIM))
    body, x_specs, x_in, aliases = _alias_prev(functools.partial(_ret_seq_kernel, lead, t), len(inputs),
                                               stack and stack[1], (1,))
    return pl.pallas_call(
        body,
        grid=(bsz, nblk),
        in_specs=[pl.BlockSpec((None, t, D_MODEL), lambda b, c: (b, row_blk0 + c, COL_RQ)),
                  pl.BlockSpec((None, t, D_MODEL), lambda b, c: (b, row_blk0 + c, COL_RK)),
                  pl.BlockSpec((None, t, wide), lambda b, c: (b, row_blk0 + c, COL_RV)),
                  pl.BlockSpec((None, t, wide), lambda b, c: (b, row_blk0 + c, COL_RG)),
                  pl.BlockSpec((t, R_HALF), lambda b, c: (c, 0)),
                  pl.BlockSpec((t, R_HALF), lambda b, c: (c, 0)),
                  pl.BlockSpec(decay.shape, lambda b, c: (0, 0, 0)),
                  bcast(s0)] + x_specs,
        out_specs=[pl.BlockSpec((None, t, wide), lambda b, c: (b, c, 0)), st_spec],
        out_shape=[jax.ShapeDtypeStruct((bsz, nblk * t, wide), out_dtype), st_shape],
        scratch_shapes=[pltpu.VMEM((R_HEADS, R_KDIM, R_VDIM), F32)],
        input_output_aliases=aliases,
        compiler_params=_cparams(("parallel", "arbitrary")),
        name="ret_seq",
    )(*inputs, *x_in)


def _col_pieces(x):
    return jnp.concatenate(_split3(x.T), axis=1)


def _pick_col(n):
    r = _iota((3 * LANES, LANES), 0) & (LANES - 1)
    return jnp.where(r == n, 1.0, 0.0).astype(BF16)


def _pick_col_pair(n):
    r = _iota((3 * LANES, 2 * LANES), 0) & (LANES - 1)
    want = n + jnp.right_shift(_iota((3 * LANES, 2 * LANES), 1), int(math.log2(LANES)))
    return jnp.where(r == want, 1.0, 0.0).astype(BF16)


def _step_prep_kernel(lb_is_zero, xs_ref, bc_ref, x_ref, wdt_ref, hq_ref, hf_ref, rq_ref, rk_ref, conv_ref,
                      cw_ref, cb_ref, dtb_ref, a_ref, ex_ref, lb_ref, cos_ref, sin_ref,
                      xs_o, bc_o, xdt_o, edec_o, conv_o, hq_o, hef_o, hk_o, rq_o, rk_o):
    raw = jnp.concatenate([xs_ref[...], bc_ref[...]], axis=1)
    buf = conv_ref[...]
    acc = cb_ref[...] + cw_ref[3:4, :] * raw
    for k in range(M_CONV - 1):
        acc += cw_ref[k:k + 1, :] * buf[:, k * M_CONV_DIM:(k + 1) * M_CONV_DIM]
    conv_o[:, 0:2 * M_CONV_DIM] = buf[:, M_CONV_DIM:]
    conv_o[:, 2 * M_CONV_DIM:] = raw
    xbc = _silu(acc)
    xs = xbc[:, :M_INNER]
    xs_o[...] = xs
    bc_o[...] = xbc[:, M_INNER:]
    dt = _softplus(_dot_nt(x_ref[...].astype(BF16), wdt_ref[...]) + dtb_ref[...])
    ex = ex_ref[...]
    xdt_o[...] = _col_pieces(xs * _sel_right(dt, ex))
    edec_o[...] = _col_pieces(_sel_right(jnp.exp(dt * a_ref[...]), ex))
    logf, k = _hgrn_gates(hf_ref[...], lb_ref[...], lb_is_zero)
    hq_o[...] = hq_ref[...] * (H_KDIM ** -0.5)
    hef_o[...] = _col_pieces(jnp.exp(logf))
    hk_o[...] = _col_pieces(k)
    cos, sin = cos_ref[...], sin_ref[...]
    rq = jnp.concatenate([_rotary(rq_ref[:, h * R_KDIM:(h + 1) * R_KDIM], cos, sin) for h in range(R_HEADS)], axis=1)
    rk = jnp.concatenate([_rotary(rk_ref[:, h * R_KDIM:(h + 1) * R_KDIM], cos, sin) for h in range(R_HEADS)], axis=1)
    rq_o[...] = _col_pieces(rq)
    rk_o[...] = _col_pieces(rk * (R_KDIM ** -0.5))


def _step_prep(proj_s, x_s, wt, layer, conv_flat, p, cos, sin):
    nb = conv_flat.shape[0]
    col = lambda cblk: pl.BlockSpec((nb, D_MODEL), lambda i, cblk=cblk: (0, cblk))
    full = lambda a: pl.BlockSpec(a.shape, lambda i: (0,) * a.ndim)
    params = [p["conv_w"], p["conv_b"], p["dt_bias"], p["a_neg"], p["expand"], p["lb"], cos, sin]
    assert nb == LANES
    rows = lambda w: jax.ShapeDtypeStruct((nb, w), F32)
    cols = lambda w: jax.ShapeDtypeStruct((w, 3 * nb), BF16)
    shapes = [rows(M_INNER), rows(M_BC), cols(M_INNER), cols(M_INNER), rows((M_CONV - 1) * M_CONV_DIM),
              rows(D_MODEL), cols(D_MODEL), cols(D_MODEL), cols(D_MODEL), cols(D_MODEL)]
    return pl.pallas_call(
        functools.partial(_step_prep_kernel, layer == 0),
        grid=(1,),
        in_specs=[col(COL_XS), col(COL_BC), pl.BlockSpec((nb, D_MODEL), lambda i: (0, 0)), _dt_weight_spec(layer),
                  col(COL_HQ), col(COL_HF), col(COL_RQ), col(COL_RK), full(conv_flat)]
                 + [full(a) for a in params],
        out_specs=[pl.BlockSpec(s.shape, lambda i: (0, 0)) for s in shapes],
        out_shape=shapes,
        compiler_params=_cparams(("arbitrary",)),
        name="step_prep",
    )(proj_s, proj_s, x_s, wt, proj_s, proj_s, proj_s, proj_s, conv_flat, *params)


def _ssd_step_kernel(s_ref, xdt_ref, edec_ref, bc_ref, xs_ref, z_ref, dsk_ref, nw_ref, so_ref, y_ref,
                     yt_ref, xdt_b, edec_b):
    yt_ref[...] = jnp.zeros(yt_ref.shape, F32)
    for i in range(STEP_BT):
        ls = slice((i % 2) * LANES, (i % 2 + 1) * LANES)
        if i % 2 == 0:
            pick = _pick_col_pair(pl.program_id(0) * STEP_BT + i)
            xdt_b[...] = _dot(xdt_ref[...], pick)
            edec_b[...] = _dot(edec_ref[...], pick)
        for g in range(M_GROUPS):
            gs = slice(g * M_GW, (g + 1) * M_GW)
            hs = slice(M_HPG * g, M_HPG * (g + 1))
            st = s_ref[i, hs].reshape(M_GW, M_STATE)
            brow = bc_ref[i:i + 1, g * M_STATE:(g + 1) * M_STATE]
            crow = bc_ref[i:i + 1, M_BC // 2 + g * M_STATE:M_BC // 2 + (g + 1) * M_STATE]
            new = st * edec_b[gs, ls] + xdt_b[gs, ls] * brow
            so_ref[i, hs] = new.reshape(M_HPG, M_HEADDIM, M_STATE)
            yt_ref[gs, i:i + 1] = jnp.sum(new * crow, axis=-1, keepdims=True)
    y = yt_ref[...].T[0:STEP_BT, :]
    xs = xs_ref[...]
    y = (y + dsk_ref[...] * xs) * _silu(z_ref[...])
    for g in range(M_GROUPS):
        gs = slice(g * M_GW, (g + 1) * M_GW)
        y_ref[:, gs] = _rms(y[:, gs]) * nw_ref[:, gs]


def _ssd_step(state, layer, prev, xdt_c, edec_c, bc, xs, proj_s, p):
    nb = xs.shape[0]
    bt = STEP_BT
    full = lambda a: pl.BlockSpec(a.shape, lambda j: (0,) * a.ndim)
    sspec = pl.BlockSpec((None, bt, M_HEADS, M_HEADDIM, M_STATE), lambda j: (layer, j, 0, 0, 0))
    inputs = [state, xdt_c, edec_c, bc, xs, proj_s, p["d_skip"], p["m_norm_w"]]
    body, x_specs, x_in, aliases = _alias_prev(_ssd_step_kernel, len(inputs),
                                               None if prev is None else [prev], (0,))
    return pl.pallas_call(
        body,
        grid=(nb // bt,),
        in_specs=[sspec, full(xdt_c), full(edec_c),
                  pl.BlockSpec((bt, M_BC), lambda j: (j, 0)),
                  pl.BlockSpec((bt, M_INNER), lambda j: (j, 0)),
                  pl.BlockSpec((bt, D_MODEL), lambda j: (j, COL_Z)),
                  full(p["d_skip"]), full(p["m_norm_w"])] + x_specs,
        out_specs=[sspec, pl.BlockSpec((bt, M_INNER), lambda j: (j, 0))],
        out_shape=[jax.ShapeDtypeStruct(state.shape, F32),
                   jax.ShapeDtypeStruct((nb, M_INNER), F32)],
        scratch_shapes=[pltpu.VMEM((M_INNER, LANES), F32)] + [pltpu.VMEM((M_INNER, 2 * LANES), F32)] * 2,
        input_output_aliases=aliases,
        compiler_params=_cparams(("parallel",)),
        name="ssd_step",
    )(*inputs, *x_in)


def _cols(x):
    nb, w = x.shape
    return jnp.transpose(x.reshape(nb // STEP_BT, STEP_BT, w), (0, 2, 1))


def _hgrn_step_kernel(s_ref, q_ref, ef_ref, k_ref, v_ref, g_ref, nw_ref, so_ref, y_ref, ef_b, k_b):
    for i in range(STEP_BT):
        ls = slice((i % 2) * LANES, (i % 2 + 1) * LANES)
        if i % 2 == 0:
            pick = _pick_col_pair(pl.program_id(0) * STEP_BT + i)
            ef_b[...] = _dot(ef_ref[...], pick)
            k_b[...] = _dot(k_ref[...], pick)
        for h in range(H_HEADS):
            cs = slice(h * H_KDIM, (h + 1) * H_KDIM)
            new = s_ref[i, h] * ef_b[cs, ls] + k_b[cs, ls] * v_ref[i:i + 1, cs]
            so_ref[i, h] = new
            y_ref[i:i + 1, cs] = jnp.sum(new * q_ref[cs, i:i + 1], axis=0, keepdims=True)
    for h in range(H_HEADS):
        cs = slice(h * H_KDIM, (h + 1) * H_KDIM)
        y_ref[:, cs] = _rms(y_ref[:, cs]) * nw_ref[:, cs] * _sigmoid(g_ref[:, cs])


def _hgrn_step(state, layer, prev, q_c, ef_c, k_c, proj_s, nw):
    nb = state.shape[1]
    bt = STEP_BT
    cspec = pl.BlockSpec(ef_c.shape, lambda j: (0, 0))
    qspec = pl.BlockSpec((None, D_MODEL, bt), lambda j: (j, 0, 0))
    sspec = pl.BlockSpec((None, bt, H_HEADS, H_KDIM, H_VDIM), lambda j: (layer, j, 0, 0, 0))
    inputs = [state, q_c, ef_c, k_c, proj_s, proj_s, nw]
    body, x_specs, x_in, aliases = _alias_prev(_hgrn_step_kernel, len(inputs),
                                               None if prev is None else [prev], (0,))
    return pl.pallas_call(
        body,
        grid=(nb // bt,),
        in_specs=[sspec, qspec, cspec, cspec,
                  pl.BlockSpec((bt, D_MODEL), lambda j: (j, COL_HI)),
                  pl.BlockSpec((bt, D_MODEL), lambda j: (j, COL_HG)),
                  pl.BlockSpec(nw.shape, lambda j: (0, 0))] + x_specs,
        out_specs=[sspec, pl.BlockSpec((bt, D_MODEL), lambda j: (j, 0))],
        out_shape=[jax.ShapeDtypeStruct(state.shape, F32),
                   jax.ShapeDtypeStruct((nb, D_MODEL), F32)],
        scratch_shapes=[pltpu.VMEM((D_MODEL, 2 * LANES), F32)] * 2,
        input_output_aliases=aliases,
        compiler_params=_cparams(("parallel",)),
        name="hgrn_step",
    )(*inputs, *x_in)


def _ret_step_kernel(*refs):
    _ret_token_update(pl.program_id(0), *refs)


def _ret_step(state, layer, prev, q_c, k_c, v3, g3, gam):
    nb = state.shape[1]
    full = lambda a: pl.BlockSpec(a.shape, lambda n: (0,) * a.ndim)
    seq_row = lambda a: pl.BlockSpec((None,) + a.shape[1:], lambda n: (n, 0, 0))
    sspec = pl.BlockSpec((None, None, R_HEADS, R_KDIM, R_VDIM), lambda n: (layer, n, 0, 0, 0))
    inputs = [state, q_c, k_c, v3, g3, gam]
    body, x_specs, x_in, aliases = _alias_prev(_ret_step_kernel, len(inputs),
                                               None if prev is None else [prev], (0,))
    return pl.pallas_call(
        body,
        grid=(nb,),
        in_specs=[sspec, full(q_c), full(k_c), seq_row(v3), seq_row(g3), full(gam)] + x_specs,
        out_specs=[sspec, seq_row(v3)],
        out_shape=[jax.ShapeDtypeStruct(state.shape, F32), jax.ShapeDtypeStruct(v3.shape, F32)],
        input_output_aliases=aliases,
        compiler_params=_cparams(("parallel",)),
        name="ret_step",
    )(*inputs, *x_in)


def _rope_tables(positions):
    inv_freq = 1.0 / (ROPE_BASE ** jnp.linspace(0.0, 1.0, R_HALF, dtype=F32))
    ang = positions[:, None] * inv_freq[None, :]
    return jnp.cos(ang), jnp.sin(ang)


def _per_channel(v):
    return jnp.repeat(v.astype(F32), M_HEADDIM).reshape(1, M_INNER)


def _pad_lanes(v):
    return jnp.pad(v.astype(F32), (0, LANES - v.shape[0])).reshape(1, LANES)


def kernel(x_prompt, x_sample, state_ssm, state_conv, state_hgrn, state_ret, meta_tokens, ln_in_g, ln_in_b,
           w_in, conv_w, conv_b, dt_bias, a_log, d_skip, m_norm_w, hgrn_lb_logits, h_norm_w, w_br_m, w_br_h,
           w_br_r, w_out, ln1_g, ln1_b, w_ffn_in, w_ffn_out, ln2_g, ln2_b):
    bp, sp = x_prompt.shape[0], x_prompt.shape[1]
    nb = x_sample.shape[0]
    assert x_sample.shape[1] == 1 and nb == SMALL_ROWS - SEQ_BLOCK and nb % STEP_BT == 0
    assert sp % RET_BLOCK == 0 and meta_tokens.shape[0] == N_META

    wt = jnp.swapaxes(w_in, 1, 2).astype(BF16)
    wm_b, wh_b, wr_b, wo_b = (w.astype(BF16) for w in (w_br_m, w_br_h, w_br_r, w_out))
    wfi_b, wfo_b = w_ffn_in.astype(BF16), w_ffn_out.astype(BF16)
    ln1 = (ln1_g.reshape(DEPTH, 1, D_MODEL), ln1_b.reshape(DEPTH, 1, D_MODEL))
    ln2 = (ln2_g.reshape(DEPTH, 1, D_MODEL), ln2_b.reshape(DEPTH, 1, D_MODEL))
    lb_cum = jnp.cumsum(jax.nn.softmax(hgrn_lb_logits.astype(F32), axis=0), axis=0)
    lbs = lb_cum - lb_cum[0]
    expand = (np.arange(LANES)[:, None] == (np.arange(M_INNER)[None, :] // M_HEADDIM)).astype(np.float32)
    expand = jnp.asarray(np.concatenate([expand] * 3, 0), BF16)
    h_sums, h_pair = _hgrn_tables()
    h_tables = (jnp.asarray(h_sums, BF16), jnp.asarray(h_pair, F32))
    gam = jnp.asarray(np.broadcast_to(
        np.array([1.0 - 2.0 ** (-5.0 - h) for h in range(R_HEADS)], np.float32)[:, None, None],
        (R_HEADS, 1, R_VDIM)))

    pos_real = jnp.arange(N_META, N_META + sp, dtype=F32)
    pos_meta = jnp.maximum(jnp.arange(SEQ_BLOCK, dtype=F32) - META_LEAD, 0.0)
    pos_samp = jnp.full((nb,), float(PAST_LEN), F32)
    cos_r, sin_r = _rope_tables(pos_real)
    cos_m, sin_m = _rope_tables(pos_meta)
    cos_s, sin_s = _rope_tables(pos_samp)

    x_real = _layer_norm_rows(x_prompt.reshape(bp * sp, D_MODEL), ln_in_g, ln_in_b, 512)
    small_in = jnp.concatenate([x_sample.reshape(nb, D_MODEL),
                                jnp.zeros((META_LEAD, D_MODEL), F32), meta_tokens.astype(F32)], axis=0)
    x_small = _layer_norm_rows(small_in, ln_in_g, ln_in_b, SMALL_ROWS)

    z_ssm = jnp.zeros((1, M_HEADS, M_HEADDIM, M_STATE), F32)
    z_conv = jnp.zeros((1, 8, M_CONV_DIM), F32)
    z_hgrn = jnp.zeros((1, H_HEADS, H_KDIM, H_VDIM), F32)
    z_ret = jnp.zeros((1, R_HEADS, R_KDIM, R_VDIM), F32)
    ssm_p = conv_p = hgrn_p = ret_p = ssm_s = hgrn_s = ret_s = None
    conv_s = []
    for l in range(DEPTH):
        p = dict(conv_w=conv_w[l], conv_b=conv_b[l].reshape(1, -1), dt_bias=_pad_lanes(dt_bias[l]),
                 a_neg=_pad_lanes(-jnp.exp(a_log[l].astype(F32))), d_skip=_per_channel(d_skip[l]),
                 m_norm_w=m_norm_w[l].reshape(1, -1), expand=expand, lb=lbs[l].reshape(1, -1))
        hnw = h_norm_w[l].reshape(1, -1)

        proj_s = _proj(x_small, wt, l, SMALL_ROWS, "proj_small")
        proj_s3, x_small3 = proj_s.reshape(1, SMALL_ROWS, PROJ_COLS), x_small.reshape(1, SMALL_ROWS, D_MODEL)
        conv_flat = state_conv[l].reshape(nb, (M_CONV - 1) * M_CONV_DIM)
        (xs_s, bc_s, xdt_s, edec_s, conv_new, hq_s, hef_s, hk_s, rq_s, rk_s) = _step_prep(
            proj_s, x_small, wt, l, conv_flat, p, cos_s, sin_s)
        conv_s.append(conv_new.reshape(nb, M_CONV - 1, M_CONV_DIM))
        wide = R_HEADS * R_VDIM
        rv3 = proj_s[:nb, 0:wide].reshape(nb, 1, wide)
        rg3 = proj_s[:nb, wide:2 * wide].reshape(nb, 1, wide)

        if _proj_steps(bp * sp, PROJ_TM) == nb:
            proj_r, ret_s, yr_s = _proj(x_real, wt, l, PROJ_TM, "proj_real",
                                        ride=(state_ret, ret_s, rq_s, rk_s, rv3, rg3, gam))
        else:
            proj_r = _proj(x_real, wt, l, PROJ_TM, "proj_real")
            ret_s, yr_s = _ret_step(state_ret, l, ret_s, rq_s, rk_s, rv3, rg3, gam)
        yr_s = yr_s.reshape(nb, wide)
        proj_r3, x_real3 = proj_r.reshape(bp, sp, PROJ_COLS), x_real.reshape(bp, sp, D_MODEL)

        ssm_s, ym_s = _ssd_step(state_ssm, l, ssm_s, xdt_s, edec_s, bc_s, xs_s, proj_s, p)
        hgrn_s, yh_s = _hgrn_step(state_hgrn, l, hgrn_s, _cols(hq_s), hef_s, hk_s, proj_s, hnw)

        ym_m, ssm_m, conv_m = _ssd_seq(proj_s3, x_small3, wt, l, 1, 1, META_LEAD, p, z_ssm, z_conv, F32)
        yh_m, hgrn_m = _hgrn_seq(proj_s3, 1, 1, META_LEAD, p["lb"], l == 0, hnw, h_tables, z_hgrn, F32)
        yr_m, ret_m = _ret_seq(proj_s3, SEQ_BLOCK, 1, 1, META_LEAD, cos_m, sin_m, z_ret, F32)

        conv0 = jnp.pad(conv_m, ((0, 0), (8 - (M_CONV - 1), 0), (0, 0)))
        ym_r, ssm_p, conv_p = _ssd_seq(proj_r3, x_real3, wt, l, 0, sp // SEQ_BLOCK, 0, p, ssm_m, conv0, BF16,
                                       stack=(l, None if l == 0 else [ssm_p, conv_p]))
        yh_r, hgrn_p = _hgrn_seq(proj_r3, 0, sp // SEQ_BLOCK, 0, p["lb"], l == 0, hnw, h_tables, hgrn_m, BF16,
                                 stack=(l, None if l == 0 else [hgrn_p]))
        yr_r, ret_p = _ret_seq(proj_r3, RET_BLOCK, 0, sp // RET_BLOCK, 0, cos_r, sin_r, ret_m, BF16,
                               stack=(l, None if l == 0 else [ret_p]))

        ym_small = jnp.concatenate([ym_s, ym_m[0]], axis=0)
        yh_small = jnp.concatenate([yh_s, yh_m[0]], axis=0)
        yr_small = jnp.concatenate([yr_s, yr_m[0]], axis=0)
        x_real = _mix(x_real, ym_r.reshape(bp * sp, -1), yh_r.reshape(bp * sp, -1), yr_r.reshape(bp * sp, -1),
                      proj_r, l, wm_b, wh_b, wr_b, wo_b, *ln1, 512)
        x_small = _mix(x_small, ym_small, yh_small, yr_small, proj_s, l, wm_b, wh_b, wr_b, wo_b, *ln1, SMALL_ROWS)
        x_real = _ffn(x_real, l, wfi_b, wfo_b, *ln2, 1024)
        x_small = _ffn(x_small, l, wfi_b, wfo_b, *ln2, SMALL_ROWS)

    return (x_real.reshape(bp, sp, D_MODEL), x_small[:nb].reshape(nb, 1, D_MODEL),
            ssm_p, conv_p, hgrn_p, ret_p, ssm_s, jnp.stack(conv_s), hgrn_s, ret_s)
```

```python
import functools
import math

import numpy as np
import jax
import jax.numpy as jnp
from jax import lax
from jax.experimental import pallas as pl
from jax.experimental.pallas import tpu as pltpu

F32 = jnp.float32
BF16 = jnp.bfloat16

D_MODEL = 1024
DEPTH = 2
N_META = 16
M_INNER = D_MODEL
M_HEADDIM = 64
M_HEADS = M_INNER // M_HEADDIM
M_GROUPS = 4
M_HPG = M_HEADS // M_GROUPS
M_STATE = 128
M_CONV = 4
M_BC = 2 * M_GROUPS * M_STATE
M_CONV_DIM = M_INNER + M_BC
M_GW = M_INNER // M_GROUPS
H_KDIM = 128
H_HEADS = D_MODEL // H_KDIM
H_VDIM = 128
R_HEADS = 4
R_KDIM = D_MODEL // R_HEADS
R_VDIM = 2 * R_KDIM
R_HALF = R_KDIM // 2
ROPE_BASE = 10000.0
D_FF = ((8 * D_MODEL // 3 + 255) // 256) * 256
FF_CHUNK = 256
DN_ALPHA = (2 * DEPTH) ** 0.25
PAST_LEN = 16384

LANES = 128
SEQ_BLOCK = 128
RET_BLOCK = 256
SMALL_ROWS = 256
META_LEAD = SEQ_BLOCK - N_META
STEP_BT = 8
VMEM_BYTES = 64 * 1024 * 1024
VMEM_LIMIT = VMEM_BYTES - 4 * 1024 * 1024

COL_RV = 0
COL_Z, COL_XS, COL_BC, COL_HQ, COL_HF, COL_HI, COL_HG, COL_RQ, COL_RK = range(2, 11)
PROJ_COLS = 11 * 1024
COL_RG = 0
COL_GM, COL_GH, COL_GR = range(2, 5)
GATE_COLS = 5 * 1024

_ORIG_SPLITS = (M_INNER, M_CONV_DIM, M_HEADS, 1024, 1024, 1024, 1024, 1024, 1024, 2048, 2048, 3072)
_ORIG_OFF = np.concatenate([[0], np.cumsum(_ORIG_SPLITS)]).tolist()


def _cparams(sem):
    return pltpu.CompilerParams(dimension_semantics=sem, vmem_limit_bytes=VMEM_LIMIT)


def _sigmoid(x):
    return 1.0 / (1.0 + jnp.exp(-x))


def _silu(x):
    return x * _sigmoid(x)


def _softplus(x):
    return jnp.maximum(x, 0.0) + jnp.log1p(jnp.exp(-jnp.abs(x)))


def _layer_norm(x, g, b):
    mu = jnp.mean(x, axis=-1, keepdims=True)
    xc = x - mu
    var = jnp.mean(xc * xc, axis=-1, keepdims=True)
    return xc * lax.rsqrt(var + 1e-5) * g + b


def _rms(x):
    return x * lax.rsqrt(jnp.mean(x * x, axis=-1, keepdims=True) + 1e-6)


def _split3(x):
    hi = x.astype(BF16)
    r = x - hi.astype(F32)
    mid = r.astype(BF16)
    lo = (r - mid.astype(F32)).astype(BF16)
    return hi, mid, lo


def _dot(a, b):
    return jnp.dot(a, b, preferred_element_type=F32)


def _dot_nt(a, b):
    return lax.dot_general(a, b, (((1,), (1,)), ((), ())), preferred_element_type=F32)


def _sel_right(x, m3):
    return _dot(jnp.concatenate(_split3(x), axis=1), m3)


def _sel_left(m3, x):
    return _dot(m3, jnp.concatenate(_split3(x), axis=0))


def _iota(shape, dim):
    return lax.broadcasted_iota(jnp.int32, shape, dim)


def _ln_kernel(x_ref, g_ref, b_ref, wdt_ref, o_ref, dt_ref):
    y = _layer_norm(x_ref[...], g_ref[...], b_ref[...])
    o_ref[...] = y
    dt_ref[...] = _dot_nt(y.astype(BF16), wdt_ref[...])


def _layer_norm_rows(x, g, b, wt, tm):
    m = x.shape[0]
    tm = min(tm, m)
    return pl.pallas_call(
        _ln_kernel,
        grid=(m // tm,),
        in_specs=[pl.BlockSpec((tm, D_MODEL), lambda i: (i, 0)),
                  pl.BlockSpec((1, D_MODEL), lambda i: (0, 0)),
                  pl.BlockSpec((1, D_MODEL), lambda i: (0, 0)),
                  _dt_weight_spec(0)],
        out_specs=[pl.BlockSpec((tm, D_MODEL), lambda i: (i, 0)), pl.BlockSpec((tm, LANES), lambda i: (i, 0))],
        out_shape=[jax.ShapeDtypeStruct((m, D_MODEL), F32), jax.ShapeDtypeStruct((m, LANES), F32)],
        compiler_params=_cparams(("parallel",)),
        name="ln_in",
    )(x, g.reshape(1, -1), b.reshape(1, -1), wt)


def _proj_kernel(ride_seq0, x_ref, w_ref, *refs):
    ride = ride_seq0 is not None
    if ride:
        ride_in, (o_ref, so_ref, ry_ref, xb_ref) = refs[:6], refs[6:]
    else:
        o_ref, xb_ref = refs

    @pl.when(pl.program_id(1) == 0)
    def _():
        xb_ref[...] = x_ref[...].astype(BF16)

    o_ref[...] = _dot_nt(xb_ref[...], w_ref[0]).astype(o_ref.dtype)
    if ride:
        n = ride_seq0 + pl.program_id(0) * pl.num_programs(1) + pl.program_id(1)
        _ret_token_update(n, *ride_in, so_ref, ry_ref)


def _layer_block(a, layer):
    return pl.BlockSpec((None,) + a.shape[1:], lambda *_: (layer,) + (0,) * (a.ndim - 1),
                        pipeline_mode=pl.Buffered(1))


_MAIN_RUNS = ((0, _ORIG_OFF[9]), (2, _ORIG_OFF[0]), (5, _ORIG_OFF[3]))
_GATE_RUNS = ((0, _ORIG_OFF[10]),)
PROJ_TN = 1024
PROJ_TM = 2048


def _orig_col(j, runs):
    off = (j - runs[0][0]) * PROJ_TN + runs[0][1]
    for blk0, orig0 in runs[1:]:
        off = jnp.where(j >= blk0, (j - blk0) * PROJ_TN + orig0, off)
    return pl.multiple_of(off, M_HEADS)


def _proj_steps(m, tm, ncols):
    return (m // min(tm, m)) * (ncols // PROJ_TN)


def _proj(x, wt, layer, tm, name, runs, ncols, out_dtype, ride=None):
    m = x.shape[0]
    tm = min(tm, m)
    nj = ncols // PROJ_TN
    inputs = [x, wt]
    in_specs = [pl.BlockSpec((tm, D_MODEL), lambda i, j: (i, 0)),
                pl.BlockSpec((pl.Element(1), pl.Element(PROJ_TN), pl.Element(D_MODEL)),
                             lambda i, j: (layer, _orig_col(j, runs), 0))]
    out_specs = [pl.BlockSpec((tm, PROJ_TN), lambda i, j: (i, j))]
    out_shape = [jax.ShapeDtypeStruct((m, ncols), out_dtype)]
    prevs, prev_outs, seq0 = [], [], None
    if ride:
        seq0, state, prev_state, prev_y, q_c, k_c, v3, g3, gam = ride
        full = lambda a: pl.BlockSpec(a.shape, lambda i, j: (0,) * a.ndim)
        seq_row = lambda a: pl.BlockSpec((None,) + a.shape[1:], lambda i, j: (seq0 + i * nj + j, 0, 0))
        sspec = pl.BlockSpec((None, None, R_HEADS, R_KDIM, R_VDIM),
                             lambda i, j: (layer, seq0 + i * nj + j, 0, 0, 0))
        inputs += [state, q_c, k_c, v3, g3, gam]
        in_specs[0] = pl.BlockSpec((tm, D_MODEL), lambda i, j: (i, 0), pipeline_mode=pl.Buffered(1))
        in_specs += [sspec, full(q_c), full(k_c), seq_row(v3), seq_row(g3), full(gam)]
        out_specs += [sspec, seq_row(v3)]
        out_shape += [jax.ShapeDtypeStruct(state.shape, F32), jax.ShapeDtypeStruct(v3.shape, F32)]
        for arr, out_id in ((prev_state, 1), (prev_y, 2)):
            if arr is not None:
                prevs.append(arr)
                prev_outs.append(out_id)
    body, x_specs, x_in, aliases = _alias_prev(functools.partial(_proj_kernel, seq0), len(inputs), prevs, prev_outs)
    res = pl.pallas_call(
        body,
        grid=(m // tm, nj),
        in_specs=in_specs + x_specs,
        out_specs=out_specs,
        out_shape=out_shape,
        scratch_shapes=[pltpu.VMEM((tm, D_MODEL), BF16)],
        input_output_aliases=aliases,
        compiler_params=_cparams(("parallel", "arbitrary")),
        name=name,
    )(*inputs, *x_in)
    return res if ride else res[0]


def _dt_weight_spec(layer):
    assert _ORIG_OFF[2] % LANES == 0
    return pl.BlockSpec((None, LANES, D_MODEL), lambda *_: (layer, _ORIG_OFF[2] // LANES, 0))


def _mix_kernel(x_ref, ym_ref, yh_ref, yr_ref, gm_ref, gh_ref, gr_ref,
                wm_ref, wh_ref, wr_ref, wo_ref, g_ref, b_ref, o_ref):
    mixed = _sigmoid(gm_ref[...].astype(F32)) * _dot(ym_ref[...].astype(BF16), wm_ref[...])
    mixed += _sigmoid(gh_ref[...].astype(F32)) * _dot(yh_ref[...].astype(BF16), wh_ref[...])
    mixed += _sigmoid(gr_ref[...].astype(F32)) * _dot(yr_ref[...].astype(BF16), wr_ref[...])
    h = _dot(mixed.astype(BF16), wo_ref[...])
    o_ref[...] = _layer_norm(DN_ALPHA * x_ref[...] + h, g_ref[...], b_ref[...])


def _mix(x, ym, yh, yr, gates, layer, wm, wh, wr, wo, g2, b2, tm):
    m = x.shape[0]
    tm = min(tm, m)
    row = lambda w: pl.BlockSpec((tm, w), lambda i: (i, 0))
    col = lambda c: pl.BlockSpec((tm, D_MODEL), lambda i, c=c: (i, c))
    full = lambda a: _layer_block(a, layer)
    return pl.pallas_call(
        _mix_kernel,
        grid=(m // tm,),
        in_specs=[row(D_MODEL), row(M_INNER), row(D_MODEL), row(R_HEADS * R_VDIM),
                  col(COL_GM), col(COL_GH), col(COL_GR),
                  full(wm), full(wh), full(wr), full(wo), full(g2), full(b2)],
        out_specs=row(D_MODEL),
        out_shape=jax.ShapeDtypeStruct((m, D_MODEL), F32),
        compiler_params=_cparams(("parallel",)),
        name="mix",
    )(x, ym, yh, yr, gates, gates, gates, wm, wh, wr, wo, g2, b2)


def _ffn_kernel(with_dt, x_ref, wi_ref, wo_ref, g_ref, b_ref, *refs):
    x = x_ref[...]
    xb = x.astype(BF16)
    acc = jnp.zeros(x.shape, F32)
    for j in range(0, D_FF, FF_CHUNK):
        hg = _dot(xb, wi_ref[:, j:j + FF_CHUNK])
        hu = _dot(xb, wi_ref[:, D_FF + j:D_FF + j + FF_CHUNK])
        acc += _dot((_silu(hg) * hu).astype(BF16), wo_ref[j:j + FF_CHUNK, :])
    y = _layer_norm(DN_ALPHA * x + acc, g_ref[...], b_ref[...])
    if with_dt:
        wdt_ref, o_ref, dt_ref = refs
        dt_ref[...] = _dot_nt(y.astype(BF16), wdt_ref[...])
    else:
        o_ref, = refs
    o_ref[...] = y


def _ffn(x, layer, wi, wo, g2, b2, tm, wt=None):
    m = x.shape[0]
    tm = min(tm, m)
    full = lambda a: _layer_block(a, layer)
    row = lambda w: pl.BlockSpec((tm, w), lambda i: (i, 0))
    with_dt = wt is not None
    res = pl.pallas_call(
        functools.partial(_ffn_kernel, with_dt),
        grid=(m // tm,),
        in_specs=[row(D_MODEL), full(wi), full(wo), full(g2), full(b2)]
                 + ([_dt_weight_spec(layer + 1)] if with_dt else []),
        out_specs=[row(D_MODEL)] + ([row(LANES)] if with_dt else []),
        out_shape=[jax.ShapeDtypeStruct((m, D_MODEL), F32)]
                  + ([jax.ShapeDtypeStruct((m, LANES), F32)] if with_dt else []),
        compiler_params=_cparams(("parallel",)),
        name="ffn",
    )(x, wi, wo, g2, b2, *([wt] if with_dt else []))
    return res if with_dt else (res[0], None)


SSD_PREP_BLOCKS = 8


def _ssd_prep_kernel(lead, nblk, dtr_ref, dtb_ref, a_ref, cum_ref, ecum_ref, tailw_ref, cumt_ref, dtt_ref):
    t = SEQ_BLOCK
    dt = _softplus(dtr_ref[...] + dtb_ref[...])
    if lead:
        dt = jnp.where(_iota((nblk * t, 1), 0) >= lead, dt, 0.0)
    a = dt * a_ref[...]
    tri = jnp.where(_iota((t, t), 1) <= _iota((t, t), 0), 1.0, 0.0).astype(BF16)
    tri3 = jnp.concatenate([tri] * 3, axis=1)
    for i in range(nblk):
        rs = slice(i * t, (i + 1) * t)
        cum = _sel_left(tri3, a[rs])
        cum_ref[rs, :] = cum
        ecum_ref[rs, :] = jnp.exp(cum)
        tailw_ref[rs, :] = jnp.exp(cum[t - 1:t, :] - cum) * dt[rs]
        cumt_ref[i] = cum.T
        dtt_ref[i] = dt[rs].T


def _ssd_prep(dt3, row_blk0, nblk, lead, p):
    bsz = dt3.shape[0]
    t = SEQ_BLOCK
    g = math.gcd(nblk, SSD_PREP_BLOCKS)
    assert row_blk0 % g == 0
    full = lambda a: pl.BlockSpec(a.shape, lambda b, c: (0,) * a.ndim)
    rows = pl.BlockSpec((None, g * t, LANES), lambda b, c: (b, c, 0))
    cols = pl.BlockSpec((None, g, LANES, t), lambda b, c: (b, c, 0, 0))
    row_shape = jax.ShapeDtypeStruct((bsz, nblk * t, LANES), F32)
    col_shape = jax.ShapeDtypeStruct((bsz, nblk, LANES, t), F32)
    return pl.pallas_call(
        functools.partial(_ssd_prep_kernel, lead, g),
        grid=(bsz, nblk // g),
        in_specs=[pl.BlockSpec((None, g * t, LANES), lambda b, c: (b, row_blk0 // g + c, 0)),
                  full(p["dt_bias"]), full(p["a_neg"])],
        out_specs=[rows, rows, rows, cols, cols],
        out_shape=[row_shape, row_shape, row_shape, col_shape, col_shape],
        compiler_params=_cparams(("parallel", "parallel")),
        name="ssd_prep",
    )(dt3, p["dt_bias"], p["a_neg"])


def _ssd_seq_kernel(lead, z_ref, xs_ref, bc_ref, cum_ref, ecum_ref, tailw_ref, cumt_ref, dtt_ref,
                    cw_ref, cb_ref, dsk_ref, nw_ref, ex_ref, s0_ref, c0_ref,
                    y_ref, sout_ref, cout_ref, ext_ref, st_ref, xbc_ref):
    t = SEQ_BLOCK
    c = pl.program_id(1)

    @pl.when(c == 0)
    def _():
        ext_ref[0:8, :] = c0_ref[...]
        for g in range(M_GROUPS):
            st_ref[g] = s0_ref[M_HPG * g:M_HPG * (g + 1)].reshape(M_GW, M_STATE).T

    rows = _iota((t, 1), 0)

    def conv_chunk(q):
        w = M_CONV_DIM // 4
        cs = slice(q * w, (q + 1) * w)
        src = xs_ref if q < 2 else bc_ref
        raw = src[:, (q % 2) * w:(q % 2 + 1) * w]
        if lead:
            raw = jnp.where(rows >= lead, raw, 0.0)
        ext_ref[8:8 + t, cs] = raw
        acc = cb_ref[:, cs] + cw_ref[3:4, cs] * raw
        for k in range(M_CONV - 1):
            acc += cw_ref[k:k + 1, cs] * ext_ref[5 + k:5 + k + t, cs]
        ext_ref[5:8, cs] = ext_ref[t + 5:t + 8, cs]
        xbc_ref[:, cs] = _silu(acc)

    cum, cum_t, dt_t = cum_ref[...], cumt_ref[...], dtt_ref[...]
    tril = _iota((t, t), 1) <= _iota((t, t), 0)
    ex = ex_ref[...]
    ecum_full = _sel_right(ecum_ref[...], ex)
    tailw_full = _sel_right(tailw_ref[...], ex)
    for q in range(4):
        conv_chunk(q)

    xs = xbc_ref[:, 0:M_INNER]
    xw = (xs * tailw_full).astype(BF16)
    xs_b = xs.astype(BF16)
    head_of_lane = jnp.right_shift(_iota((1, M_GW), 1), int(math.log2(M_HEADDIM)))

    for g in range(M_GROUPS):
        gs = slice(g * M_GW, (g + 1) * M_GW)
        bm_f = xbc_ref[:, M_INNER + g * M_STATE:M_INNER + (g + 1) * M_STATE]
        bm = bm_f.astype(BF16)
        cm = xbc_ref[:, M_INNER + M_BC // 2 + g * M_STATE:M_INNER + M_BC // 2 + (g + 1) * M_STATE].astype(BF16)
        cb = _dot_nt(cm, bm)
        st = st_ref[g]
        y_g = _dot(cm, st.astype(BF16)) * ecum_full[:, gs]
        for hh in range(M_HPG):
            h = g * M_HPG + hh
            diff = cum[:, h:h + 1] - cum_t[h:h + 1, :]
            w = cb * jnp.exp(jnp.where(tril, diff, -1e30)) * dt_t[h:h + 1, :]
            x_h = jnp.where(head_of_lane == hh, xs_b[:, gs], jnp.zeros((), BF16))
            y_g = y_g + _dot(w.astype(BF16), x_h)
        st_ref[g] = st * ecum_full[t - 1:t, gs] + _dot(bm_f.T.astype(BF16), xw[:, gs])
        y_g = (y_g + dsk_ref[:, gs] * xs[:, gs]) * _silu(z_ref[:, gs])
        y_ref[:, gs] = (_rms(y_g) * nw_ref[:, gs]).astype(y_ref.dtype)

    @pl.when(c == pl.num_programs(1) - 1)
    def _():
        cout_ref[...] = ext_ref[5:8, :]
        for g in range(M_GROUPS):
            sout_ref[M_HPG * g:M_HPG * (g + 1)] = st_ref[g].T.reshape(M_HPG, M_HEADDIM, M_STATE)


def _alias_prev(body, n_in, prevs, out_ids):
    if not prevs:
        return body, [], [], {}
    k = len(prevs)
    wrapped = lambda *refs: body(*refs[:n_in], *refs[n_in + k:])
    return (wrapped, [pl.BlockSpec(memory_space=pl.ANY)] * k, list(prevs),
            {n_in + i: o for i, o in enumerate(out_ids)})


def _seq_state_out(stack, bsz, dims):
    zeros = (0,) * len(dims)
    if stack is None:
        return (jax.ShapeDtypeStruct((bsz,) + dims, F32),
                pl.BlockSpec((None,) + dims, lambda b, c: (b,) + zeros))
    layer = stack[0]
    return (jax.ShapeDtypeStruct((DEPTH, bsz) + dims, F32),
            pl.BlockSpec((None, None) + dims, lambda b, c: (layer, b) + zeros))


def _ssd_seq(proj3, dt3, row_blk0, nblk, lead, p, s0, c0, out_dtype, stack=None):
    bsz = proj3.shape[0]
    t = SEQ_BLOCK
    assert not lead or nblk == 1
    colspec = lambda cblk: pl.BlockSpec((None, t, D_MODEL), lambda b, c, cblk=cblk: (b, row_blk0 + c, cblk))
    full = lambda a: pl.BlockSpec(a.shape, lambda b, c: (0,) * a.ndim)
    bcast = lambda a: pl.BlockSpec((None,) + a.shape[1:], lambda b, c: (0,) * a.ndim)
    prep = _ssd_prep(dt3, row_blk0, nblk, lead, p)
    prep_rows = pl.BlockSpec((None, t, LANES), lambda b, c: (b, c, 0))
    prep_cols = pl.BlockSpec((None, None, LANES, t), lambda b, c: (b, c, 0, 0))
    params = [p["conv_w"], p["conv_b"], p["d_skip"], p["m_norm_w"], p["expand"]]
    inputs = [proj3, proj3, proj3, *prep, *params, s0, c0]
    st_shape, st_spec = _seq_state_out(stack, bsz, (M_HEADS, M_HEADDIM, M_STATE))
    cv_shape, cv_spec = _seq_state_out(stack, bsz, (M_CONV - 1, M_CONV_DIM))
    body, x_specs, x_in, aliases = _alias_prev(functools.partial(_ssd_seq_kernel, lead), len(inputs),
                                               stack and stack[1], (1, 2))
    return pl.pallas_call(
        body,
        grid=(bsz, nblk),
        in_specs=[colspec(COL_Z), colspec(COL_XS), colspec(COL_BC),
                  prep_rows, prep_rows, prep_rows, prep_cols, prep_cols]
                 + [full(a) for a in params] + [bcast(s0), bcast(c0)] + x_specs,
        out_specs=[pl.BlockSpec((None, t, M_INNER), lambda b, c: (b, c, 0)), st_spec, cv_spec],
        out_shape=[jax.ShapeDtypeStruct((bsz, nblk * t, M_INNER), out_dtype), st_shape, cv_shape],
        scratch_shapes=[pltpu.VMEM((t + 8, M_CONV_DIM), F32),
                        pltpu.VMEM((M_GROUPS, M_STATE, M_GW), F32),
                        pltpu.VMEM((t, M_CONV_DIM), F32)],
        input_output_aliases=aliases,
        compiler_params=_cparams(("parallel", "arbitrary")),
        name="ssd_seq",
    )(*inputs, *x_in)


def _hgrn_gates(fz, lb, lb_is_zero):
    e = jnp.exp(-jnp.abs(fz))
    r = 1.0 / (1.0 + e)
    pos = fz >= 0.0
    sig_neg = jnp.where(pos, e * r, r)
    log_sig = jnp.minimum(fz, 0.0) - jnp.log(1.0 + e)
    if lb_is_zero:
        return log_sig, sig_neg
    sig_pos = jnp.where(pos, r, e * r)
    logf = jnp.where(lb > 0.0, jnp.log(lb + (1.0 - lb) * sig_pos), log_sig)
    return logf, (1.0 - lb) * sig_neg


H_LEVELS = int(math.log2(SEQ_BLOCK))


def _hgrn_tables():
    n = SEQ_BLOCK
    t = np.arange(n)[:, None]
    u = np.arange(n)[None, :]
    pair = []
    for b in range(H_LEVELS):
        bit = ((t >> b) & 1) == 1
        pair.append(((t >> (b + 1)) == (u >> (b + 1))) & bit & (((u >> b) & 1) == 0))
    m1 = ((t >> 1) | 1) << 1
    lvl1 = np.where(((t >> 1) & 1) == 1, (u >= m1) & (u <= t), (u > t) & (u < m1))
    sums = np.concatenate([u <= t, lvl1], 0).astype(np.float32)
    return np.concatenate([sums] * 3, 1), np.concatenate(pair, 0).astype(np.float32)


def _exp_neg_abs(d):
    return jnp.exp2(jnp.abs(d) * (-1.0 / math.log(2.0)))


def _ret_token_update(n, s_ref, q_ref, k_ref, v_ref, g_ref, gam_ref, so_ref, y_ref):
    pick = _pick_col(n)
    for h in range(R_HEADS):
        rows = slice(h * R_KDIM, (h + 1) * R_KDIM)
        q_b = _dot(q_ref[rows, :], pick)
        k_b = _dot(k_ref[rows, :], pick)
        outs = []
        for c0 in range(0, R_VDIM, LANES):
            cs = slice(c0, c0 + LANES)
            new = s_ref[h, :, cs] * gam_ref[h, :, cs] + k_b * v_ref[:, h * R_VDIM + c0:h * R_VDIM + c0 + LANES]
            so_ref[h, :, cs] = new
            outs.append(jnp.sum(new * q_b, axis=0, keepdims=True))
        vs = slice(h * R_VDIM, (h + 1) * R_VDIM)
        y_ref[:, vs] = _rms(jnp.concatenate(outs, axis=1)) * _silu(g_ref[:, vs].astype(F32))


def _hgrn_seq_kernel(lead, lb_is_zero, q_ref, f_ref, i_ref, g_ref, lb_ref, nw_ref, sums_ref, pair_ref, s0_ref,
                     y_ref, sout_ref, st_ref, ex_ref, q_s, k_s, z_ref, qd_ref, kd_ref, sc_ref):
    t = SEQ_BLOCK
    c = pl.program_id(1)

    @pl.when(c == 0)
    def _():
        for h in range(H_HEADS):
            st_ref[h] = s0_ref[h].T

    rows = _iota((t, 1), 0)
    logf, k = _hgrn_gates(f_ref[...], lb_ref[...], lb_is_zero)
    if lead:
        logf = jnp.where(rows >= lead, logf, 0.0)
        k = jnp.where(rows >= lead, k, 0.0)
    q = q_ref[...] * (H_KDIM ** -0.5)
    q_s[...] = q
    k_s[...] = k
    ex_ref[...] = _dot(sums_ref[...], jnp.concatenate(_split3(logf), axis=0))

    def side_of(b):
        return (jnp.right_shift(rows, b) & 1) == 1

    z_ref[0] = jnp.where(side_of(0), q * jnp.exp(logf), k).astype(BF16)
    z_ref[1] = (jnp.exp(ex_ref[t:2 * t, :]) * jnp.where(side_of(1), q_s[...], k_s[...])).astype(BF16)
    for b in range(2, H_LEVELS):
        half = 1 << b
        groups = range(0, t, 2 * half)
        d = jnp.concatenate([ex_ref[g0:g0 + 2 * half, :] - ex_ref[g0 + half - 1:g0 + half, :]
                             for g0 in groups], axis=0)
        if half >= 8:
            qk = jnp.concatenate([ref[g0 + o:g0 + o + half, :] for g0 in groups
                                  for ref, o in ((k_s, 0), (q_s, half))], axis=0)
        else:
            qk = jnp.where(side_of(b), q_s[...], k_s[...])
        z_ref[b] = (_exp_neg_abs(d) * qk).astype(BF16)
    cum = ex_ref[0:t, :]
    qd_ref[...] = (q_s[...] * jnp.exp(cum)).astype(BF16)
    kd_ref[...] = (k_s[...] * jnp.exp(ex_ref[t - 1:t, :] - cum)).astype(BF16)

    for h in range(H_HEADS):
        cs = slice(h * H_KDIM, (h + 1) * H_KDIM)
        scores = None
        for b in range(H_LEVELS):
            z = z_ref[b, :, cs]
            p = _dot_nt(z, z) * pair_ref[b * t:(b + 1) * t, :]
            scores = p if scores is None else scores + p
        sc_ref[h] = scores.astype(BF16)

    for h in range(H_HEADS):
        cs = slice(h * H_KDIM, (h + 1) * H_KDIM)
        v = i_ref[:, cs]
        st = st_ref[h]
        o = (_dot(sc_ref[h], v.astype(BF16))
             + jnp.sum(q_s[:, cs] * k_s[:, cs], axis=-1, keepdims=True) * v
             + _dot_nt(qd_ref[:, cs], st.astype(BF16)))
        y = _rms(o) * nw_ref[:, cs] * _sigmoid(g_ref[:, cs])
        y_ref[:, cs] = y.astype(y_ref.dtype)
        st_ref[h] = st * jnp.exp(ex_ref[t - 1:t, cs]) + _dot(v.T.astype(BF16), kd_ref[:, cs])

    @pl.when(c == pl.num_programs(1) - 1)
    def _():
        for h in range(H_HEADS):
            sout_ref[h] = st_ref[h].T


def _hgrn_seq(proj3, row_blk0, nblk, lead, lb, lb_is_zero, nw, tables, s0, out_dtype, stack=None):
    bsz = proj3.shape[0]
    t = SEQ_BLOCK
    colspec = lambda cblk: pl.BlockSpec((None, t, D_MODEL), lambda b, c, cblk=cblk: (b, row_blk0 + c, cblk))
    full = lambda a: pl.BlockSpec(a.shape, lambda b, c: (0,) * a.ndim)
    bcast = lambda a: pl.BlockSpec((None,) + a.shape[1:], lambda b, c: (0,) * a.ndim)
    inputs = [proj3, proj3, proj3, proj3, lb, nw, *tables, s0]
    st_shape, st_spec = _seq_state_out(stack, bsz, (H_HEADS, H_KDIM, H_VDIM))
    body, x_specs, x_in, aliases = _alias_prev(functools.partial(_hgrn_seq_kernel, lead, lb_is_zero), len(inputs),
                                               stack and stack[1], (1,))
    return pl.pallas_call(
        body,
        grid=(bsz, nblk),
        in_specs=[colspec(COL_HQ), colspec(COL_HF), colspec(COL_HI), colspec(COL_HG),
                  full(lb), full(nw), full(tables[0]), full(tables[1]), bcast(s0)] + x_specs,
        out_specs=[pl.BlockSpec((None, t, D_MODEL), lambda b, c: (b, c, 0)), st_spec],
        out_shape=[jax.ShapeDtypeStruct((bsz, nblk * t, D_MODEL), out_dtype), st_shape],
        input_output_aliases=aliases,
        scratch_shapes=[pltpu.VMEM((H_HEADS, H_VDIM, H_KDIM), F32),
                        pltpu.VMEM((2 * t, D_MODEL), F32),
                        pltpu.VMEM((t, D_MODEL), F32), pltpu.VMEM((t, D_MODEL), F32),
                        pltpu.VMEM((H_LEVELS, t, D_MODEL), BF16),
                        pltpu.VMEM((t, D_MODEL), BF16), pltpu.VMEM((t, D_MODEL), BF16),
                        pltpu.VMEM((H_HEADS, t, t), BF16)],
        compiler_params=_cparams(("parallel", "arbitrary")),
        name="hgrn_seq",
    )(*inputs, *x_in)


def _log_gamma(h):
    return math.log(1.0 - 2.0 ** (-5.0 - h))


def _rotary(x, cos, sin):
    x1, x2 = x[:, :R_HALF], x[:, R_HALF:]
    return jnp.concatenate([x1 * cos - x2 * sin, x2 * cos + x1 * sin], axis=1)


def _ret_decay(t):
    d = np.arange(t)[:, None] - np.arange(t)[None, :]
    return np.stack([np.where(d >= 0, np.exp(np.maximum(d, 0) * _log_gamma(h)), 0.0)
                     for h in range(R_HEADS)]).astype(np.float32)


def _ret_seq_kernel(lead, t, q_ref, k_ref, v_ref, g_ref, cos_ref, sin_ref, dec_ref, s0_ref,
                    y_ref, sout_ref, st_ref):
    c = pl.program_id(1)

    @pl.when(c == 0)
    def _():
        st_ref[...] = s0_ref[...]

    cos, sin = cos_ref[...], sin_ref[...]
    tcol = _iota((t, 1), 0).astype(F32)
    for h in range(R_HEADS):
        lg = _log_gamma(h)
        ks = slice(h * R_KDIM, (h + 1) * R_KDIM)
        vs = slice(h * R_VDIM, (h + 1) * R_VDIM)
        qh = _rotary(q_ref[:, ks], cos, sin)
        kh = _rotary(k_ref[:, ks], cos, sin) * (R_KDIM ** -0.5)
        if lead:
            kh = jnp.where(_iota((t, 1), 0) >= lead, kh, 0.0)
        qb, kb, vb = qh.astype(BF16), kh.astype(BF16), v_ref[:, vs].astype(BF16)
        scores = _dot_nt(qb, kb) * dec_ref[h]
        st = st_ref[h]
        o = _dot(scores.astype(BF16), vb) + _dot(qb, st.astype(BF16)) * jnp.exp((tcol + 1.0) * lg)
        kdec = (kh * jnp.exp((t - 1.0 - tcol) * lg)).T.astype(BF16)
        st_ref[h] = st * math.exp(t * lg) + _dot(kdec, vb)
        y_ref[:, vs] = (_rms(o) * _silu(g_ref[:, vs].astype(F32))).astype(y_ref.dtype)

    @pl.when(c == pl.num_programs(1) - 1)
    def _():
        sout_ref[...] = st_ref[...]


def _ret_seq(proj3, gate3, t, row_blk0, nblk, lead, cos, sin, s0, out_dtype, stack=None):
    bsz = proj3.shape[0]
    wide = R_HEADS * R_VDIM
    bcast = lambda a: pl.BlockSpec((None,) + a.shape[1:], lambda b, c: (0,) * a.ndim)
    decay = jnp.asarray(_ret_decay(t))
    inputs = [proj3, proj3, proj3, gate3, cos, sin, decay, s0]
    st_shape, st_spec = _seq_state_out(stack, bsz, (R_HEADS, R_KDIM, R_VDIM))
    body, x_specs, x_in, aliases = _alias_prev(functools.partial(_ret_seq_kernel, lead, t), len(inputs),
                                               stack and stack[1], (1,))
    return pl.pallas_call(
        body,
        grid=(bsz, nblk),
        in_specs=[pl.BlockSpec((None, t, D_MODEL), lambda b, c: (b, row_blk0 + c, COL_RQ)),
                  pl.BlockSpec((None, t, D_MODEL), lambda b, c: (b, row_blk0 + c, COL_RK)),
                  pl.BlockSpec((None, t, wide), lambda b, c: (b, row_blk0 + c, COL_RV)),
                  pl.BlockSpec((None, t, wide), lambda b, c: (b, row_blk0 + c, COL_RG)),
                  pl.BlockSpec((t, R_HALF), lambda b, c: (c, 0)),
                  pl.BlockSpec((t, R_HALF), lambda b, c: (c, 0)),
                  pl.BlockSpec(decay.shape, lambda b, c: (0, 0, 0)),
                  bcast(s0)] + x_specs,
        out_specs=[pl.BlockSpec((None, t, wide), lambda b, c: (b, c, 0)), st_spec],
        out_shape=[jax.ShapeDtypeStruct((bsz, nblk * t, wide), out_dtype), st_shape],
        scratch_shapes=[pltpu.VMEM((R_HEADS, R_KDIM, R_VDIM), F32)],
        input_output_aliases=aliases,
        compiler_params=_cparams(("parallel", "arbitrary")),
        name="ret_seq",
    )(*inputs, *x_in)


def _col_pieces(x):
    return jnp.concatenate(_split3(x.T), axis=1)


def _pick_col(n):
    r = _iota((3 * LANES, LANES), 0) & (LANES - 1)
    return jnp.where(r == n, 1.0, 0.0).astype(BF16)


def _pick_col_pair(n):
    r = _iota((3 * LANES, 2 * LANES), 0) & (LANES - 1)
    want = n + jnp.right_shift(_iota((3 * LANES, 2 * LANES), 1), int(math.log2(LANES)))
    return jnp.where(r == want, 1.0, 0.0).astype(BF16)


def _step_prep_kernel(lb_is_zero, xs_ref, bc_ref, dtr_ref, hq_ref, hf_ref, rq_ref, rk_ref, conv_ref,
                      cw_ref, cb_ref, dtb_ref, a_ref, ex_ref, lb_ref, cos_ref, sin_ref,
                      xs_o, bc_o, xdt_o, edec_o, conv_o, hq_o, hef_o, hk_o, rq_o, rk_o):
    raw = jnp.concatenate([xs_ref[...], bc_ref[...]], axis=1)
    buf = conv_ref[...]
    acc = cb_ref[...] + cw_ref[3:4, :] * raw
    for k in range(M_CONV - 1):
        acc += cw_ref[k:k + 1, :] * buf[:, k * M_CONV_DIM:(k + 1) * M_CONV_DIM]
    conv_o[:, 0:2 * M_CONV_DIM] = buf[:, M_CONV_DIM:]
    conv_o[:, 2 * M_CONV_DIM:] = raw
    xbc = _silu(acc)
    xs = xbc[:, :M_INNER]
    xs_o[...] = xs
    bc_o[...] = xbc[:, M_INNER:]
    dt = _softplus(dtr_ref[...] + dtb_ref[...])
    ex = ex_ref[...]
    xdt_o[...] = _col_pieces(xs * _sel_right(dt, ex))
    edec_o[...] = _col_pieces(_sel_right(jnp.exp(dt * a_ref[...]), ex))
    logf, k = _hgrn_gates(hf_ref[...], lb_ref[...], lb_is_zero)
    hq_o[...] = hq_ref[...] * (H_KDIM ** -0.5)
    hef_o[...] = _col_pieces(jnp.exp(logf))
    hk_o[...] = _col_pieces(k)
    cos, sin = cos_ref[...], sin_ref[...]
    rq = jnp.concatenate([_rotary(rq_ref[:, h * R_KDIM:(h + 1) * R_KDIM], cos, sin) for h in range(R_HEADS)], axis=1)
    rk = jnp.concatenate([_rotary(rk_ref[:, h * R_KDIM:(h + 1) * R_KDIM], cos, sin) for h in range(R_HEADS)], axis=1)
    rq_o[...] = _col_pieces(rq)
    rk_o[...] = _col_pieces(rk * (R_KDIM ** -0.5))


def _step_prep(proj_s, dt_s, layer, conv_flat, p, cos, sin):
    nb = conv_flat.shape[0]
    col = lambda cblk: pl.BlockSpec((nb, D_MODEL), lambda i, cblk=cblk: (0, cblk))
    full = lambda a: pl.BlockSpec(a.shape, lambda i: (0,) * a.ndim)
    params = [p["conv_w"], p["conv_b"], p["dt_bias"], p["a_neg"], p["expand"], p["lb"], cos, sin]
    assert nb == LANES
    rows = lambda w: jax.ShapeDtypeStruct((nb, w), F32)
    cols = lambda w: jax.ShapeDtypeStruct((w, 3 * nb), BF16)
    shapes = [rows(M_INNER), rows(M_BC), cols(M_INNER), cols(M_INNER), rows((M_CONV - 1) * M_CONV_DIM),
              rows(D_MODEL), cols(D_MODEL), cols(D_MODEL), cols(D_MODEL), cols(D_MODEL)]
    return pl.pallas_call(
        functools.partial(_step_prep_kernel, layer == 0),
        grid=(1,),
        in_specs=[col(COL_XS), col(COL_BC), pl.BlockSpec((nb, LANES), lambda i: (0, 0)),
                  col(COL_HQ), col(COL_HF), col(COL_RQ), col(COL_RK), full(conv_flat)]
                 + [full(a) for a in params],
        out_specs=[pl.BlockSpec(s.shape, lambda i: (0, 0)) for s in shapes],
        out_shape=shapes,
        compiler_params=_cparams(("arbitrary",)),
        name="step_prep",
    )(proj_s, proj_s, dt_s, proj_s, proj_s, proj_s, proj_s, conv_flat, *params)


def _ssd_step_kernel(s_ref, xdt_ref, edec_ref, bc_ref, xs_ref, z_ref, dsk_ref, nw_ref, so_ref, y_ref,
                     yt_ref, xdt_b, edec_b):
    yt_ref[...] = jnp.zeros(yt_ref.shape, F32)
    for i in range(STEP_BT):
        ls = slice((i % 2) * LANES, (i % 2 + 1) * LANES)
        if i % 2 == 0:
            pick = _pick_col_pair(pl.program_id(0) * STEP_BT + i)
            xdt_b[...] = _dot(xdt_ref[...], pick)
            edec_b[...] = _dot(edec_ref[...], pick)
        for g in range(M_GROUPS):
            gs = slice(g * M_GW, (g + 1) * M_GW)
            hs = slice(M_HPG * g, M_HPG * (g + 1))
            st = s_ref[i, hs].reshape(M_GW, M_STATE)
            brow = bc_ref[i:i + 1, g * M_STATE:(g + 1) * M_STATE]
            crow = bc_ref[i:i + 1, M_BC // 2 + g * M_STATE:M_BC // 2 + (g + 1) * M_STATE]
            new = st * edec_b[gs, ls] + xdt_b[gs, ls] * brow
            so_ref[i, hs] = new.reshape(M_HPG, M_HEADDIM, M_STATE)
            yt_ref[gs, i:i + 1] = jnp.sum(new * crow, axis=-1, keepdims=True)
    y = yt_ref[...].T[0:STEP_BT, :]
    xs = xs_ref[...]
    y = (y + dsk_ref[...] * xs) * _silu(z_ref[...])
    for g in range(M_GROUPS):
        gs = slice(g * M_GW, (g + 1) * M_GW)
        y_ref[:, gs] = _rms(y[:, gs]) * nw_ref[:, gs]


def _ssd_step(state, layer, prev, xdt_c, edec_c, bc, xs, proj_s, p):
    nb = xs.shape[0]
    bt = STEP_BT
    full = lambda a: pl.BlockSpec(a.shape, lambda j: (0,) * a.ndim)
    sspec = pl.BlockSpec((None, bt, M_HEADS, M_HEADDIM, M_STATE), lambda j: (layer, j, 0, 0, 0))
    inputs = [state, xdt_c, edec_c, bc, xs, proj_s, p["d_skip"], p["m_norm_w"]]
    body, x_specs, x_in, aliases = _alias_prev(_ssd_step_kernel, len(inputs),
                                               None if prev is None else [prev], (0,))
    return pl.pallas_call(
        body,
        grid=(nb // bt,),
        in_specs=[sspec, full(xdt_c), full(edec_c),
                  pl.BlockSpec((bt, M_BC), lambda j: (j, 0)),
                  pl.BlockSpec((bt, M_INNER), lambda j: (j, 0)),
                  pl.BlockSpec((bt, D_MODEL), lambda j: (j, COL_Z)),
                  full(p["d_skip"]), full(p["m_norm_w"])] + x_specs,
        out_specs=[sspec, pl.BlockSpec((bt, M_INNER), lambda j: (j, 0))],
        out_shape=[jax.ShapeDtypeStruct(state.shape, F32),
                   jax.ShapeDtypeStruct((nb, M_INNER), F32)],
        scratch_shapes=[pltpu.VMEM((M_INNER, LANES), F32)] + [pltpu.VMEM((M_INNER, 2 * LANES), F32)] * 2,
        input_output_aliases=aliases,
        compiler_params=_cparams(("parallel",)),
        name="ssd_step",
    )(*inputs, *x_in)


def _cols(x):
    nb, w = x.shape
    return jnp.transpose(x.reshape(nb // STEP_BT, STEP_BT, w), (0, 2, 1))


def _hgrn_step_kernel(s_ref, q_ref, ef_ref, k_ref, v_ref, g_ref, nw_ref, so_ref, y_ref, ef_b, k_b):
    for i in range(STEP_BT):
        ls = slice((i % 2) * LANES, (i % 2 + 1) * LANES)
        if i % 2 == 0:
            pick = _pick_col_pair(pl.program_id(0) * STEP_BT + i)
            ef_b[...] = _dot(ef_ref[...], pick)
            k_b[...] = _dot(k_ref[...], pick)
        for h in range(H_HEADS):
            cs = slice(h * H_KDIM, (h + 1) * H_KDIM)
            new = s_ref[i, h] * ef_b[cs, ls] + k_b[cs, ls] * v_ref[i:i + 1, cs]
            so_ref[i, h] = new
            y_ref[i:i + 1, cs] = jnp.sum(new * q_ref[cs, i:i + 1], axis=0, keepdims=True)
    for h in range(H_HEADS):
        cs = slice(h * H_KDIM, (h + 1) * H_KDIM)
        y_ref[:, cs] = _rms(y_ref[:, cs]) * nw_ref[:, cs] * _sigmoid(g_ref[:, cs])


def _hgrn_step(state, layer, prev, q_c, ef_c, k_c, proj_s, nw):
    nb = state.shape[1]
    bt = STEP_BT
    cspec = pl.BlockSpec(ef_c.shape, lambda j: (0, 0))
    qspec = pl.BlockSpec((None, D_MODEL, bt), lambda j: (j, 0, 0))
    sspec = pl.BlockSpec((None, bt, H_HEADS, H_KDIM, H_VDIM), lambda j: (layer, j, 0, 0, 0))
    inputs = [state, q_c, ef_c, k_c, proj_s, proj_s, nw]
    body, x_specs, x_in, aliases = _alias_prev(_hgrn_step_kernel, len(inputs),
                                               None if prev is None else [prev], (0,))
    return pl.pallas_call(
        body,
        grid=(nb // bt,),
        in_specs=[sspec, qspec, cspec, cspec,
                  pl.BlockSpec((bt, D_MODEL), lambda j: (j, COL_HI)),
                  pl.BlockSpec((bt, D_MODEL), lambda j: (j, COL_HG)),
                  pl.BlockSpec(nw.shape, lambda j: (0, 0))] + x_specs,
        out_specs=[sspec, pl.BlockSpec((bt, D_MODEL), lambda j: (j, 0))],
        out_shape=[jax.ShapeDtypeStruct(state.shape, F32),
                   jax.ShapeDtypeStruct((nb, D_MODEL), F32)],
        scratch_shapes=[pltpu.VMEM((D_MODEL, 2 * LANES), F32)] * 2,
        input_output_aliases=aliases,
        compiler_params=_cparams(("parallel",)),
        name="hgrn_step",
    )(*inputs, *x_in)


def _ret_step_kernel(*refs):
    _ret_token_update(pl.program_id(0), *refs)


def _ret_step(state, layer, prev, q_c, k_c, v3, g3, gam):
    nb = state.shape[1]
    full = lambda a: pl.BlockSpec(a.shape, lambda n: (0,) * a.ndim)
    seq_row = lambda a: pl.BlockSpec((None,) + a.shape[1:], lambda n: (n, 0, 0))
    sspec = pl.BlockSpec((None, None, R_HEADS, R_KDIM, R_VDIM), lambda n: (layer, n, 0, 0, 0))
    inputs = [state, q_c, k_c, v3, g3, gam]
    body, x_specs, x_in, aliases = _alias_prev(_ret_step_kernel, len(inputs),
                                               None if prev is None else [prev], (0,))
    return pl.pallas_call(
        body,
        grid=(nb,),
        in_specs=[sspec, full(q_c), full(k_c), seq_row(v3), seq_row(g3), full(gam)] + x_specs,
        out_specs=[sspec, seq_row(v3)],
        out_shape=[jax.ShapeDtypeStruct(state.shape, F32), jax.ShapeDtypeStruct(v3.shape, F32)],
        input_output_aliases=aliases,
        compiler_params=_cparams(("parallel",)),
        name="ret_step",
    )(*inputs, *x_in)


def _rope_tables(positions):
    inv_freq = 1.0 / (ROPE_BASE ** jnp.linspace(0.0, 1.0, R_HALF, dtype=F32))
    ang = positions[:, None] * inv_freq[None, :]
    return jnp.cos(ang), jnp.sin(ang)


def _per_channel(v):
    return jnp.repeat(v.astype(F32), M_HEADDIM).reshape(1, M_INNER)


def _pad_lanes(v):
    return jnp.pad(v.astype(F32), (0, LANES - v.shape[0])).reshape(1, LANES)


def kernel(x_prompt, x_sample, state_ssm, state_conv, state_hgrn, state_ret, meta_tokens, ln_in_g, ln_in_b,
           w_in, conv_w, conv_b, dt_bias, a_log, d_skip, m_norm_w, hgrn_lb_logits, h_norm_w, w_br_m, w_br_h,
           w_br_r, w_out, ln1_g, ln1_b, w_ffn_in, w_ffn_out, ln2_g, ln2_b):
    bp, sp = x_prompt.shape[0], x_prompt.shape[1]
    nb = x_sample.shape[0]
    assert x_sample.shape[1] == 1 and nb == SMALL_ROWS - SEQ_BLOCK and nb % STEP_BT == 0
    assert sp % RET_BLOCK == 0 and meta_tokens.shape[0] == N_META

    wt = jnp.swapaxes(w_in, 1, 2).astype(BF16)
    wm_b, wh_b, wr_b, wo_b = (w.astype(BF16) for w in (w_br_m, w_br_h, w_br_r, w_out))
    wfi_b, wfo_b = w_ffn_in.astype(BF16), w_ffn_out.astype(BF16)
    ln1 = (ln1_g.reshape(DEPTH, 1, D_MODEL), ln1_b.reshape(DEPTH, 1, D_MODEL))
    ln2 = (ln2_g.reshape(DEPTH, 1, D_MODEL), ln2_b.reshape(DEPTH, 1, D_MODEL))
    lb_cum = jnp.cumsum(jax.nn.softmax(hgrn_lb_logits.astype(F32), axis=0), axis=0)
    lbs = lb_cum - lb_cum[0]
    expand = (np.arange(LANES)[:, None] == (np.arange(M_INNER)[None, :] // M_HEADDIM)).astype(np.float32)
    expand = jnp.asarray(np.concatenate([expand] * 3, 0), BF16)
    h_sums, h_pair = _hgrn_tables()
    h_tables = (jnp.asarray(h_sums, BF16), jnp.asarray(h_pair, F32))
    gam = jnp.asarray(np.broadcast_to(
        np.array([1.0 - 2.0 ** (-5.0 - h) for h in range(R_HEADS)], np.float32)[:, None, None],
        (R_HEADS, 1, R_VDIM)))

    pos_real = jnp.arange(N_META, N_META + sp, dtype=F32)
    pos_meta = jnp.maximum(jnp.arange(SEQ_BLOCK, dtype=F32) - META_LEAD, 0.0)
    pos_samp = jnp.full((nb,), float(PAST_LEN), F32)
    cos_r, sin_r = _rope_tables(pos_real)
    cos_m, sin_m = _rope_tables(pos_meta)
    cos_s, sin_s = _rope_tables(pos_samp)

    x_real, dt_real = _layer_norm_rows(x_prompt.reshape(bp * sp, D_MODEL), ln_in_g, ln_in_b, wt, 512)
    small_in = jnp.concatenate([x_sample.reshape(nb, D_MODEL),
                                jnp.zeros((META_LEAD, D_MODEL), F32), meta_tokens.astype(F32)], axis=0)
    x_small, dt_small = _layer_norm_rows(small_in, ln_in_g, ln_in_b, wt, SMALL_ROWS)

    z_ssm = jnp.zeros((1, M_HEADS, M_HEADDIM, M_STATE), F32)
    z_conv = jnp.zeros((1, 8, M_CONV_DIM), F32)
    z_hgrn = jnp.zeros((1, H_HEADS, H_KDIM, H_VDIM), F32)
    z_ret = jnp.zeros((1, R_HEADS, R_KDIM, R_VDIM), F32)
    ssm_p = conv_p = hgrn_p = ret_p = ssm_s = hgrn_s = ret_s = None
    conv_s = []
    for l in range(DEPTH):
        p = dict(conv_w=conv_w[l], conv_b=conv_b[l].reshape(1, -1), dt_bias=_pad_lanes(dt_bias[l]),
                 a_neg=_pad_lanes(-jnp.exp(a_log[l].astype(F32))), d_skip=_per_channel(d_skip[l]),
                 m_norm_w=m_norm_w[l].reshape(1, -1), expand=expand, lb=lbs[l].reshape(1, -1))
        hnw = h_norm_w[l].reshape(1, -1)

        main = (_MAIN_RUNS, PROJ_COLS, F32)
        gate = (_GATE_RUNS, GATE_COLS, BF16)
        proj_s = _proj(x_small, wt, l, SMALL_ROWS, "proj_small", *main)
        gate_s = _proj(x_small, wt, l, SMALL_ROWS, "gate_small", *gate)
        proj_s3, dt_small3 = proj_s.reshape(1, SMALL_ROWS, PROJ_COLS), dt_small.reshape(1, SMALL_ROWS, LANES)
        conv_flat = state_conv[l].reshape(nb, (M_CONV - 1) * M_CONV_DIM)
        (xs_s, bc_s, xdt_s, edec_s, conv_new, hq_s, hef_s, hk_s, rq_s, rk_s) = _step_prep(
            proj_s, dt_small, l, conv_flat, p, cos_s, sin_s)
        conv_s.append(conv_new.reshape(nb, M_CONV - 1, M_CONV_DIM))
        wide = R_HEADS * R_VDIM
        rv3 = proj_s[:nb, 0:wide].reshape(nb, 1, wide)
        rg3 = gate_s[:nb, 0:wide].reshape(nb, 1, wide)

        n_main, n_gate = _proj_steps(bp * sp, PROJ_TM, PROJ_COLS), _proj_steps(bp * sp, PROJ_TM, GATE_COLS)
        if n_main + n_gate == nb:
            ride = (state_ret, rq_s, rk_s, rv3, rg3, gam)
            proj_r, ret_half, yr_half = _proj(x_real, wt, l, PROJ_TM, "proj_real", *main,
                                              ride=(0, ride[0], ret_s, None) + ride[1:])
            gate_r, ret_s, yr_s = _proj(x_real, wt, l, PROJ_TM, "gate_real", *gate,
                                        ride=(n_main, ride[0], ret_half, yr_half) + ride[1:])
        else:
            proj_r = _proj(x_real, wt, l, PROJ_TM, "proj_real", *main)
            gate_r = _proj(x_real, wt, l, PROJ_TM, "gate_real", *gate)
            ret_s, yr_s = _ret_step(state_ret, l, ret_s, rq_s, rk_s, rv3, rg3, gam)
        yr_s = yr_s.reshape(nb, wide)
        proj_r3, dt_real3 = proj_r.reshape(bp, sp, PROJ_COLS), dt_real.reshape(bp, sp, LANES)

        ssm_s, ym_s = _ssd_step(state_ssm, l, ssm_s, xdt_s, edec_s, bc_s, xs_s, proj_s, p)
        hgrn_s, yh_s = _hgrn_step(state_hgrn, l, hgrn_s, _cols(hq_s), hef_s, hk_s, proj_s, hnw)

        ym_m, ssm_m, conv_m = _ssd_seq(proj_s3, dt_small3, 1, 1, META_LEAD, p, z_ssm, z_conv, F32)
        yh_m, hgrn_m = _hgrn_seq(proj_s3, 1, 1, META_LEAD, p["lb"], l == 0, hnw, h_tables, z_hgrn, F32)
        yr_m, ret_m = _ret_seq(proj_s3, gate_s.reshape(1, SMALL_ROWS, GATE_COLS), SEQ_BLOCK, 1, 1, META_LEAD,
                               cos_m, sin_m, z_ret, F32)

        conv0 = jnp.pad(conv_m, ((0, 0), (8 - (M_CONV - 1), 0), (0, 0)))
        ym_r, ssm_p, conv_p = _ssd_seq(proj_r3, dt_real3, 0, sp // SEQ_BLOCK, 0, p, ssm_m, conv0, BF16,
                                       stack=(l, None if l == 0 else [ssm_p, conv_p]))
        yh_r, hgrn_p = _hgrn_seq(proj_r3, 0, sp // SEQ_BLOCK, 0, p["lb"], l == 0, hnw, h_tables, hgrn_m, BF16,
                                 stack=(l, None if l == 0 else [hgrn_p]))
        yr_r, ret_p = _ret_seq(proj_r3, gate_r.reshape(bp, sp, GATE_COLS), RET_BLOCK, 0, sp // RET_BLOCK, 0,
                               cos_r, sin_r, ret_m, BF16, stack=(l, None if l == 0 else [ret_p]))

        ym_small = jnp.concatenate([ym_s, ym_m[0]], axis=0)
        yh_small = jnp.concatenate([yh_s, yh_m[0]], axis=0)
        yr_small = jnp.concatenate([yr_s, yr_m[0]], axis=0)
        x_real = _mix(x_real, ym_r.reshape(bp * sp, -1), yh_r.reshape(bp * sp, -1), yr_r.reshape(bp * sp, -1),
                      gate_r, l, wm_b, wh_b, wr_b, wo_b, *ln1, 512)
        x_small = _mix(x_small, ym_small, yh_small, yr_small, gate_s, l, wm_b, wh_b, wr_b, wo_b, *ln1, SMALL_ROWS)
        wt_next = wt if l + 1 < DEPTH else None
        x_real, dt_real = _ffn(x_real, l, wfi_b, wfo_b, *ln2, 1024, wt=wt_next)
        x_small, dt_small = _ffn(x_small, l, wfi_b, wfo_b, *ln2, SMALL_ROWS, wt=wt_next)

    return (x_real.reshape(bp, sp, D_MODEL), x_small[:nb].reshape(nb, 1, D_MODEL),
            ssm_p, conv_p, hgrn_p, ret_p, ssm_s, jnp.stack(conv_s), hgrn_s, ret_s)
```
